```python
import jax, jax.numpy as jnp
from jax import lax
import numpy as np

D_MODEL = 2048
BATCH = 8
SEQ = 8192
DEPTH = 4

N_MIXERS = 2
N_CONV_LAYERS = (DEPTH + 1) // 2
N_GLA_LAYERS = DEPTH // 2
CONV_WIDTH = 31
GLA_HEADS = 4
GLA_DK = D_MODEL // 2
GLA_DV = D_MODEL
GLA_HEAD_K = GLA_DK // GLA_HEADS
GLA_HEAD_V = GLA_DV // GLA_HEADS
GLA_GATE_RANK = 16
GLA_GATE_TAU = 16.0
GLA_CHUNK = 64
GLA_IN_WIDTH = 2 * GLA_DK + 2 * GLA_DV + GLA_GATE_RANK
D_FF = -(-8 * D_MODEL // (3 * 256)) * 256
N_MOD = 6
EPS = 1e-6

kernel_name = "hybrid_conformerconv_gla_sandwich_adaln"


def rms_norm(x, gain):
    xf = x.astype(jnp.float32)
    y = xf * lax.rsqrt(jnp.mean(xf * xf, axis=-1, keepdims=True) + EPS)
    return (y * gain.astype(jnp.float32)).astype(x.dtype)


def layer_norm(x, gain, bias):
    xf = x.astype(jnp.float32)
    mu = jnp.mean(xf, axis=-1, keepdims=True)
    xc = xf - mu
    y = xc * lax.rsqrt(jnp.mean(xc * xc, axis=-1, keepdims=True) + EPS)
    return (y * gain.astype(jnp.float32) + bias.astype(jnp.float32)).astype(x.dtype)


def conformer_conv(h, w_pw1, b_pw1, w_dw, b_dw, ln_g, ln_b, w_pw2, b_pw2):
    u = h @ w_pw1 + b_pw1
    a, g = jnp.split(u, 2, axis=-1)
    u = a * jax.nn.sigmoid(g)
    u = lax.conv_general_dilated(
        u, w_dw[:, None, :], window_strides=(1,),
        padding=((CONV_WIDTH - 1, 0),),
        dimension_numbers=("NWC", "WIO", "NWC"),
        feature_group_count=D_MODEL) + b_dw
    u = jax.nn.silu(layer_norm(u, ln_g, ln_b))
    return u @ w_pw2 + b_pw2


def gla_mixer(h, w_in, w_gate_up, b_gate, norm_g, w_out):
    bsz, L, _ = h.shape
    proj = h @ w_in
    q, k, v, r, a = jnp.split(
        proj, [GLA_DK, 2 * GLA_DK, 2 * GLA_DK + GLA_DV, 2 * GLA_DK + 2 * GLA_DV], axis=-1)
    g = jax.nn.log_sigmoid((a @ w_gate_up + b_gate).astype(jnp.float32)) / GLA_GATE_TAU
    nc = L // GLA_CHUNK

    def to_chunks(t, dh):
        t = t.astype(jnp.float32).reshape(bsz, nc, GLA_CHUNK, GLA_HEADS, dh)
        return t.transpose(1, 0, 3, 2, 4)

    qc = to_chunks(q * (GLA_HEAD_K ** -0.5), GLA_HEAD_K)
    kc = to_chunks(k, GLA_HEAD_K)
    vc = to_chunks(v, GLA_HEAD_V)
    gc = to_chunks(g, GLA_HEAD_K)
    causal = jnp.tril(jnp.ones((GLA_CHUNK, GLA_CHUNK), dtype=bool))[:, :, None]

    def step(S, inp):
        qb, kb, vb, gb = inp
        b = jnp.cumsum(gb, axis=2)
        o_inter = jnp.einsum("bhcd,bhde->bhce", qb * jnp.exp(b), S)
        diff = jnp.where(causal, b[:, :, :, None, :] - b[:, :, None, :, :], -jnp.inf)
        scores = jnp.einsum("bhid,bhjd,bhijd->bhij", qb, kb, jnp.exp(diff))
        o_intra = jnp.einsum("bhij,bhje->bhie", scores, vb)
        b_last = b[:, :, -1:, :]
        S_new = S * jnp.exp(b_last[:, :, 0, :])[..., None] + jnp.einsum(
            "bhcd,bhce->bhde", kb * jnp.exp(b_last - b), vb)
        return S_new, o_inter + o_intra

    S0 = jnp.zeros((bsz, GLA_HEADS, GLA_HEAD_K, GLA_HEAD_V), jnp.float32)
    _, o = lax.scan(step, S0, (qc, kc, vc, gc))
    o = o.transpose(1, 0, 3, 2, 4).reshape(bsz, L, GLA_HEADS, GLA_HEAD_V)
    o = rms_norm(o, norm_g.reshape(GLA_HEADS, GLA_HEAD_V))
    o = o.reshape(bsz, L, GLA_DV).astype(h.dtype) * jax.nn.silu(r)
    return o @ w_out


def swiglu_ffn(h, w_in, w_out):
    gate, up = jnp.split(h @ w_in, 2, axis=-1)
    return (jax.nn.silu(gate) * up) @ w_out


def _fwd_setup_inputs(seed: int = 0) -> dict:
    key = jax.random.key(seed)
    ks = iter(jax.random.split(key, 32))
    D = D_MODEL

    def nrm(shape, scale):
        return jax.random.normal(next(ks), shape, jnp.float32) * scale

    def gain(shape):
        return 1.0 + nrm(shape, 0.02)

    return {
        "x": nrm((BATCH, SEQ, D), 1.0),
        "c": nrm((BATCH, D), 1.0),
        "w_ada": nrm((DEPTH, D, N_MOD * D), D ** -0.5),
        "b_ada": nrm((DEPTH, N_MOD * D), 0.02),
        "pre_mix_g": gain((DEPTH, D)),
        "post_mix_g": gain((DEPTH, D)),
        "pre_ffn_g": gain((DEPTH, D)),
        "post_ffn_g": gain((DEPTH, D)),
        "conv_w_pw1": nrm((N_CONV_LAYERS, D, 2 * D), D ** -0.5),
        "conv_b_pw1": nrm((N_CONV_LAYERS, 2 * D), 0.02),
        "conv_w_dw": nrm((N_CONV_LAYERS, CONV_WIDTH, D), CONV_WIDTH ** -0.5),
        "conv_b_dw": nrm((N_CONV_LAYERS, D), 0.02),
        "conv_ln_g": gain((N_CONV_LAYERS, D)),
        "conv_ln_b": nrm((N_CONV_LAYERS, D), 0.02),
        "conv_w_pw2": nrm((N_CONV_LAYERS, D, D), D ** -0.5),
        "conv_b_pw2": nrm((N_CONV_LAYERS, D), 0.02),
        "gla_w_in": nrm((N_GLA_LAYERS, D, GLA_IN_WIDTH), D ** -0.5),
        "gla_w_gate_up": nrm((N_GLA_LAYERS, GLA_GATE_RANK, GLA_DK), GLA_GATE_RANK ** -0.5),
        "gla_b_gate": nrm((N_GLA_LAYERS, GLA_DK), 0.02),
        "gla_norm_g": gain((N_GLA_LAYERS, GLA_DV)),
        "gla_w_out": nrm((N_GLA_LAYERS, GLA_DV, D), GLA_DV ** -0.5),
        "ffn_w_in": nrm((DEPTH, D, 2 * D_FF), D ** -0.5),
        "ffn_w_out": nrm((DEPTH, D_FF, D), D_FF ** -0.5),
    }


def _fwd_reference(x, c, w_ada, b_ada, pre_mix_g, post_mix_g, pre_ffn_g, post_ffn_g,
              conv_w_pw1, conv_b_pw1, conv_w_dw, conv_b_dw, conv_ln_g, conv_ln_b,
              conv_w_pw2, conv_b_pw2,
              gla_w_in, gla_w_gate_up, gla_b_gate, gla_norm_g, gla_w_out,
              ffn_w_in, ffn_w_out):
    c_act = jax.nn.silu(c)
    for i in range(DEPTH):
        mod = c_act @ w_ada[i] + b_ada[i]
        sh1, sc1, gt1, sh2, sc2, gt2 = jnp.split(mod[:, None, :], N_MOD, axis=-1)

        h = rms_norm(x, pre_mix_g[i]) * (1.0 + sc1) + sh1
        j = i // N_MIXERS
        if i % N_MIXERS == 0:
            y = conformer_conv(h, conv_w_pw1[j], conv_b_pw1[j], conv_w_dw[j], conv_b_dw[j],
                               conv_ln_g[j], conv_ln_b[j], conv_w_pw2[j], conv_b_pw2[j])
        else:
            y = gla_mixer(h, gla_w_in[j], gla_w_gate_up[j], gla_b_gate[j],
                          gla_norm_g[j], gla_w_out[j])
        x = x + gt1 * rms_norm(y, post_mix_g[i])

        h = rms_norm(x, pre_ffn_g[i]) * (1.0 + sc2) + sh2
        y = swiglu_ffn(h, ffn_w_in[i], ffn_w_out[i])
        x = x + gt2 * rms_norm(y, post_ffn_g[i])
    return x


import jax as _jax
import jax.numpy as _jnp

TWIN_FORMAT = 'train_step'
FWD_PARAMS = ['x', 'c', 'w_ada', 'b_ada', 'pre_mix_g', 'post_mix_g', 'pre_ffn_g', 'post_ffn_g', 'conv_w_pw1', 'conv_b_pw1', 'conv_w_dw', 'conv_b_dw', 'conv_ln_g', 'conv_ln_b', 'conv_w_pw2', 'conv_b_pw2', 'gla_w_in', 'gla_w_gate_up', 'gla_b_gate', 'gla_norm_g', 'gla_w_out', 'ffn_w_in', 'ffn_w_out']
TWIN_WEIGHTS = ['w_ada', 'b_ada', 'pre_mix_g', 'post_mix_g', 'pre_ffn_g', 'post_ffn_g', 'conv_w_pw1', 'conv_b_pw1', 'conv_w_dw', 'conv_b_dw', 'conv_ln_g', 'conv_ln_b', 'conv_w_pw2', 'conv_b_pw2', 'gla_w_in', 'gla_w_gate_up', 'gla_b_gate', 'gla_norm_g', 'gla_w_out', 'ffn_w_in', 'ffn_w_out']
TWIN_DIFF_INPUT = 'x'
TWIN_INPUTS = ['x', 'c', 'w_ada', 'b_ada', 'pre_mix_g', 'post_mix_g', 'pre_ffn_g', 'post_ffn_g', 'conv_w_pw1', 'conv_b_pw1', 'conv_w_dw', 'conv_b_dw', 'conv_ln_g', 'conv_ln_b', 'conv_w_pw2', 'conv_b_pw2', 'gla_w_in', 'gla_w_gate_up', 'gla_b_gate', 'gla_norm_g', 'gla_w_out', 'ffn_w_in', 'ffn_w_out', 'loss_target', 'm_w_ada', 'm_b_ada', 'm_pre_mix_g', 'm_post_mix_g', 'm_pre_ffn_g', 'm_post_ffn_g', 'm_conv_w_pw1', 'm_conv_b_pw1', 'm_conv_w_dw', 'm_conv_b_dw', 'm_conv_ln_g', 'm_conv_ln_b', 'm_conv_w_pw2', 'm_conv_b_pw2', 'm_gla_w_in', 'm_gla_w_gate_up', 'm_gla_b_gate', 'm_gla_norm_g', 'm_gla_w_out', 'm_ffn_w_in', 'm_ffn_w_out', 'v_w_ada', 'v_b_ada', 'v_pre_mix_g', 'v_post_mix_g', 'v_pre_ffn_g', 'v_post_ffn_g', 'v_conv_w_pw1', 'v_conv_b_pw1', 'v_conv_w_dw', 'v_conv_b_dw', 'v_conv_ln_g', 'v_conv_ln_b', 'v_conv_w_pw2', 'v_conv_b_pw2', 'v_gla_w_in', 'v_gla_w_gate_up', 'v_gla_b_gate', 'v_gla_norm_g', 'v_gla_w_out', 'v_ffn_w_in', 'v_ffn_w_out']
TWIN_OUTPUTS = ['loss', 'grad_x', 'grad_w_ada', 'grad_b_ada', 'grad_pre_mix_g', 'grad_post_mix_g', 'grad_pre_ffn_g', 'grad_post_ffn_g', 'grad_conv_w_pw1', 'grad_conv_b_pw1', 'grad_conv_w_dw', 'grad_conv_b_dw', 'grad_conv_ln_g', 'grad_conv_ln_b', 'grad_conv_w_pw2', 'grad_conv_b_pw2', 'grad_gla_w_in', 'grad_gla_w_gate_up', 'grad_gla_b_gate', 'grad_gla_norm_g', 'grad_gla_w_out', 'grad_ffn_w_in', 'grad_ffn_w_out', 'delta_w_ada', 'delta_b_ada', 'delta_pre_mix_g', 'delta_post_mix_g', 'delta_pre_ffn_g', 'delta_post_ffn_g', 'delta_conv_w_pw1', 'delta_conv_b_pw1', 'delta_conv_w_dw', 'delta_conv_b_dw', 'delta_conv_ln_g', 'delta_conv_ln_b', 'delta_conv_w_pw2', 'delta_conv_b_pw2', 'delta_gla_w_in', 'delta_gla_w_gate_up', 'delta_gla_b_gate', 'delta_gla_norm_g', 'delta_gla_w_out', 'delta_ffn_w_in', 'delta_ffn_w_out', 'new_m_w_ada', 'new_m_b_ada', 'new_m_pre_mix_g', 'new_m_post_mix_g', 'new_m_pre_ffn_g', 'new_m_post_ffn_g', 'new_m_conv_w_pw1', 'new_m_conv_b_pw1', 'new_m_conv_w_dw', 'new_m_conv_b_dw', 'new_m_conv_ln_g', 'new_m_conv_ln_b', 'new_m_conv_w_pw2', 'new_m_conv_b_pw2', 'new_m_gla_w_in', 'new_m_gla_w_gate_up', 'new_m_gla_b_gate', 'new_m_gla_norm_g', 'new_m_gla_w_out', 'new_m_ffn_w_in', 'new_m_ffn_w_out', 'new_v_w_ada', 'new_v_b_ada', 'new_v_pre_mix_g', 'new_v_post_mix_g', 'new_v_pre_ffn_g', 'new_v_post_ffn_g', 'new_v_conv_w_pw1', 'new_v_conv_b_pw1', 'new_v_conv_w_dw', 'new_v_conv_b_dw', 'new_v_conv_ln_g', 'new_v_conv_ln_b', 'new_v_conv_w_pw2', 'new_v_conv_b_pw2', 'new_v_gla_w_in', 'new_v_gla_w_gate_up', 'new_v_gla_b_gate', 'new_v_gla_norm_g', 'new_v_gla_w_out', 'new_v_ffn_w_in', 'new_v_ffn_w_out']
TWIN_LEAF_KINDS = {'loss': 'loss', 'grad_x': 'grad_x', 'grad_w_ada': 'grad_w', 'grad_b_ada': 'grad_w', 'grad_pre_mix_g': 'grad_w', 'grad_post_mix_g': 'grad_w', 'grad_pre_ffn_g': 'grad_w', 'grad_post_ffn_g': 'grad_w', 'grad_conv_w_pw1': 'grad_w', 'grad_conv_b_pw1': 'grad_w', 'grad_conv_w_dw': 'grad_w', 'grad_conv_b_dw': 'grad_w', 'grad_conv_ln_g': 'grad_w', 'grad_conv_ln_b': 'grad_w', 'grad_conv_w_pw2': 'grad_w', 'grad_conv_b_pw2': 'grad_w', 'grad_gla_w_in': 'grad_w', 'grad_gla_w_gate_up': 'grad_w', 'grad_gla_b_gate': 'grad_w', 'grad_gla_norm_g': 'grad_w', 'grad_gla_w_out': 'grad_w', 'grad_ffn_w_in': 'grad_w', 'grad_ffn_w_out': 'grad_w', 'delta_w_ada': 'delta_w', 'delta_b_ada': 'delta_w', 'delta_pre_mix_g': 'delta_w', 'delta_post_mix_g': 'delta_w', 'delta_pre_ffn_g': 'delta_w', 'delta_post_ffn_g': 'delta_w', 'delta_conv_w_pw1': 'delta_w', 'delta_conv_b_pw1': 'delta_w', 'delta_conv_w_dw': 'delta_w', 'delta_conv_b_dw': 'delta_w', 'delta_conv_ln_g': 'delta_w', 'delta_conv_ln_b': 'delta_w', 'delta_conv_w_pw2': 'delta_w', 'delta_conv_b_pw2': 'delta_w', 'delta_gla_w_in': 'delta_w', 'delta_gla_w_gate_up': 'delta_w', 'delta_gla_b_gate': 'delta_w', 'delta_gla_norm_g': 'delta_w', 'delta_gla_w_out': 'delta_w', 'delta_ffn_w_in': 'delta_w', 'delta_ffn_w_out': 'delta_w', 'new_m_w_ada': 'new_m', 'new_m_b_ada': 'new_m', 'new_m_pre_mix_g': 'new_m', 'new_m_post_mix_g': 'new_m', 'new_m_pre_ffn_g': 'new_m', 'new_m_post_ffn_g': 'new_m', 'new_m_conv_w_pw1': 'new_m', 'new_m_conv_b_pw1': 'new_m', 'new_m_conv_w_dw': 'new_m', 'new_m_conv_b_dw': 'new_m', 'new_m_conv_ln_g': 'new_m', 'new_m_conv_ln_b': 'new_m', 'new_m_conv_w_pw2': 'new_m', 'new_m_conv_b_pw2': 'new_m', 'new_m_gla_w_in': 'new_m', 'new_m_gla_w_gate_up': 'new_m', 'new_m_gla_b_gate': 'new_m', 'new_m_gla_norm_g': 'new_m', 'new_m_gla_w_out': 'new_m', 'new_m_ffn_w_in': 'new_m', 'new_m_ffn_w_out': 'new_m', 'new_v_w_ada': 'new_v', 'new_v_b_ada': 'new_v', 'new_v_pre_mix_g': 'new_v', 'new_v_post_mix_g': 'new_v', 'new_v_pre_ffn_g': 'new_v', 'new_v_post_ffn_g': 'new_v', 'new_v_conv_w_pw1': 'new_v', 'new_v_conv_b_pw1': 'new_v', 'new_v_conv_w_dw': 'new_v', 'new_v_conv_b_dw': 'new_v', 'new_v_conv_ln_g': 'new_v', 'new_v_conv_ln_b': 'new_v', 'new_v_conv_w_pw2': 'new_v', 'new_v_conv_b_pw2': 'new_v', 'new_v_gla_w_in': 'new_v', 'new_v_gla_w_gate_up': 'new_v', 'new_v_gla_b_gate': 'new_v', 'new_v_gla_norm_g': 'new_v', 'new_v_gla_w_out': 'new_v', 'new_v_ffn_w_in': 'new_v', 'new_v_ffn_w_out': 'new_v'}


def _forward(args):
    return _fwd_reference(*[args[k] for k in FWD_PARAMS])


def _output_shape():
    def fwd():
        inp = _fwd_setup_inputs(0)
        return _fwd_reference(*[inp[k] for k in FWD_PARAMS])
    out = _jax.eval_shape(fwd)
    return out.shape, out.dtype

N_MICROBATCH = 1
ADAM_LR = 0.001
ADAM_B1 = 0.9
ADAM_B2 = 0.999
ADAM_EPS = 1e-08
ADAM_WD = 0.01
ADAM_STEP = 10
PER_EXAMPLE_BATCH_AXIS = {'x': 0, 'c': 0, 'loss_target': 0}
SHARED_INPUTS = []
_WEIGHT_DTYPES = {'w_ada': _jnp.float32, 'b_ada': _jnp.float32, 'pre_mix_g': _jnp.float32, 'post_mix_g': _jnp.float32, 'pre_ffn_g': _jnp.float32, 'post_ffn_g': _jnp.float32, 'conv_w_pw1': _jnp.float32, 'conv_b_pw1': _jnp.float32, 'conv_w_dw': _jnp.float32, 'conv_b_dw': _jnp.float32, 'conv_ln_g': _jnp.float32, 'conv_ln_b': _jnp.float32, 'conv_w_pw2': _jnp.float32, 'conv_b_pw2': _jnp.float32, 'gla_w_in': _jnp.float32, 'gla_w_gate_up': _jnp.float32, 'gla_b_gate': _jnp.float32, 'gla_norm_g': _jnp.float32, 'gla_w_out': _jnp.float32, 'ffn_w_in': _jnp.float32, 'ffn_w_out': _jnp.float32}
MOMENT_SCALE = {'w_ada': 2.703809e+00, 'b_ada': 5.561716e+00, 'pre_mix_g': 7.335867e-01, 'post_mix_g': 1.331007e+01, 'pre_ffn_g': 7.262559e-01, 'post_ffn_g': 1.301457e+01, 'conv_w_pw1': 1.325868e+00, 'conv_b_pw1': 2.496946e+00, 'conv_w_dw': 1.978964e+00, 'conv_b_dw': 5.930501e+00, 'conv_ln_g': 3.651277e+00, 'conv_ln_b': 4.700951e+00, 'conv_w_pw2': 2.584128e+00, 'conv_b_pw2': 8.933965e+00, 'gla_w_in': 9.501280e-01, 'gla_w_gate_up': 6.656423e-01, 'gla_b_gate': 8.844145e-01, 'gla_norm_g': 1.035495e+00, 'gla_w_out': 1.053717e+00, 'ffn_w_in': 6.411138e-01, 'ffn_w_out': 1.233761e+00}


def _to_microbatches(a, axis):
    t = _jnp.moveaxis(a, axis, 0)
    t = t.reshape((N_MICROBATCH, t.shape[0] // N_MICROBATCH) + t.shape[1:])
    return _jnp.moveaxis(t, 1, axis + 1)


def setup_inputs(seed: int = 0) -> dict:
    inp = _fwd_setup_inputs(seed)
    key = _jax.random.fold_in(_jax.random.key(seed), 7919)
    shape, _ = _output_shape()
    out = dict(inp)
    out["loss_target"] = _jax.random.normal(_jax.random.fold_in(key, 0), shape, _jnp.float32)
    for i, name in enumerate(TWIN_WEIGHTS):
        w = inp[name].astype(_jnp.float32)
        if MOMENT_SCALE is None:
            s = _jnp.sqrt(_jnp.mean(_jnp.square(w)) + 1e-30)
        else:
            s = MOMENT_SCALE[name]
        km, kv = _jax.random.split(_jax.random.fold_in(key, i + 1))
        out[name] = w
        out["m_" + name] = s * _jax.random.normal(km, w.shape, _jnp.float32)
        out["v_" + name] = (s * s) * _jax.random.uniform(kv, w.shape, _jnp.float32, 0.5, 1.5)
    if N_MICROBATCH > 1:
        for name, axis in PER_EXAMPLE_BATCH_AXIS.items():
            out[name] = _to_microbatches(out[name], axis)
    return {'x': out['x'], 'c': out['c'], 'w_ada': out['w_ada'], 'b_ada': out['b_ada'], 'pre_mix_g': out['pre_mix_g'], 'post_mix_g': out['post_mix_g'], 'pre_ffn_g': out['pre_ffn_g'], 'post_ffn_g': out['post_ffn_g'], 'conv_w_pw1': out['conv_w_pw1'], 'conv_b_pw1': out['conv_b_pw1'], 'conv_w_dw': out['conv_w_dw'], 'conv_b_dw': out['conv_b_dw'], 'conv_ln_g': out['conv_ln_g'], 'conv_ln_b': out['conv_ln_b'], 'conv_w_pw2': out['conv_w_pw2'], 'conv_b_pw2': out['conv_b_pw2'], 'gla_w_in': out['gla_w_in'], 'gla_w_gate_up': out['gla_w_gate_up'], 'gla_b_gate': out['gla_b_gate'], 'gla_norm_g': out['gla_norm_g'], 'gla_w_out': out['gla_w_out'], 'ffn_w_in': out['ffn_w_in'], 'ffn_w_out': out['ffn_w_out'], 'loss_target': out['loss_target'], 'm_w_ada': out['m_w_ada'], 'm_b_ada': out['m_b_ada'], 'm_pre_mix_g': out['m_pre_mix_g'], 'm_post_mix_g': out['m_post_mix_g'], 'm_pre_ffn_g': out['m_pre_ffn_g'], 'm_post_ffn_g': out['m_post_ffn_g'], 'm_conv_w_pw1': out['m_conv_w_pw1'], 'm_conv_b_pw1': out['m_conv_b_pw1'], 'm_conv_w_dw': out['m_conv_w_dw'], 'm_conv_b_dw': out['m_conv_b_dw'], 'm_conv_ln_g': out['m_conv_ln_g'], 'm_conv_ln_b': out['m_conv_ln_b'], 'm_conv_w_pw2': out['m_conv_w_pw2'], 'm_conv_b_pw2': out['m_conv_b_pw2'], 'm_gla_w_in': out['m_gla_w_in'], 'm_gla_w_gate_up': out['m_gla_w_gate_up'], 'm_gla_b_gate': out['m_gla_b_gate'], 'm_gla_norm_g': out['m_gla_norm_g'], 'm_gla_w_out': out['m_gla_w_out'], 'm_ffn_w_in': out['m_ffn_w_in'], 'm_ffn_w_out': out['m_ffn_w_out'], 'v_w_ada': out['v_w_ada'], 'v_b_ada': out['v_b_ada'], 'v_pre_mix_g': out['v_pre_mix_g'], 'v_post_mix_g': out['v_post_mix_g'], 'v_pre_ffn_g': out['v_pre_ffn_g'], 'v_post_ffn_g': out['v_post_ffn_g'], 'v_conv_w_pw1': out['v_conv_w_pw1'], 'v_conv_b_pw1': out['v_conv_b_pw1'], 'v_conv_w_dw': out['v_conv_w_dw'], 'v_conv_b_dw': out['v_conv_b_dw'], 'v_conv_ln_g': out['v_conv_ln_g'], 'v_conv_ln_b': out['v_conv_ln_b'], 'v_conv_w_pw2': out['v_conv_w_pw2'], 'v_conv_b_pw2': out['v_conv_b_pw2'], 'v_gla_w_in': out['v_gla_w_in'], 'v_gla_w_gate_up': out['v_gla_w_gate_up'], 'v_gla_b_gate': out['v_gla_b_gate'], 'v_gla_norm_g': out['v_gla_norm_g'], 'v_gla_w_out': out['v_gla_w_out'], 'v_ffn_w_in': out['v_ffn_w_in'], 'v_ffn_w_out': out['v_ffn_w_out']}


def _loss(weights, diff, rest, loss_target):
    with _jax.named_scope("forward"):
        args = {**rest, TWIN_DIFF_INPUT: diff, **{k: w.astype(_WEIGHT_DTYPES[k]) for k, w in weights.items()}}
        y = _forward(args)
    with _jax.named_scope("loss_head"):
        err = _jnp.square(y.astype(_jnp.float32) - loss_target)
        return 0.5 * _jnp.sum(_jnp.mean(err, axis=-1)) if err.ndim else 0.5 * err


def _adamw(w, g, m, v):
    m = ADAM_B1 * m + (1.0 - ADAM_B1) * g
    v = ADAM_B2 * v + (1.0 - ADAM_B2) * _jnp.square(g)
    m_hat = m / (1.0 - ADAM_B1 ** ADAM_STEP)
    v_hat = v / (1.0 - ADAM_B2 ** ADAM_STEP)
    delta = -ADAM_LR * (m_hat / (_jnp.sqrt(v_hat) + ADAM_EPS) + ADAM_WD * w)
    return delta, m, v


def reference(x, c, w_ada, b_ada, pre_mix_g, post_mix_g, pre_ffn_g, post_ffn_g, conv_w_pw1, conv_b_pw1, conv_w_dw, conv_b_dw, conv_ln_g, conv_ln_b, conv_w_pw2, conv_b_pw2, gla_w_in, gla_w_gate_up, gla_b_gate, gla_norm_g, gla_w_out, ffn_w_in, ffn_w_out, loss_target, m_w_ada, m_b_ada, m_pre_mix_g, m_post_mix_g, m_pre_ffn_g, m_post_ffn_g, m_conv_w_pw1, m_conv_b_pw1, m_conv_w_dw, m_conv_b_dw, m_conv_ln_g, m_conv_ln_b, m_conv_w_pw2, m_conv_b_pw2, m_gla_w_in, m_gla_w_gate_up, m_gla_b_gate, m_gla_norm_g, m_gla_w_out, m_ffn_w_in, m_ffn_w_out, v_w_ada, v_b_ada, v_pre_mix_g, v_post_mix_g, v_pre_ffn_g, v_post_ffn_g, v_conv_w_pw1, v_conv_b_pw1, v_conv_w_dw, v_conv_b_dw, v_conv_ln_g, v_conv_ln_b, v_conv_w_pw2, v_conv_b_pw2, v_gla_w_in, v_gla_w_gate_up, v_gla_b_gate, v_gla_norm_g, v_gla_w_out, v_ffn_w_in, v_ffn_w_out):
    given = dict(x=x, c=c, w_ada=w_ada, b_ada=b_ada, pre_mix_g=pre_mix_g, post_mix_g=post_mix_g, pre_ffn_g=pre_ffn_g, post_ffn_g=post_ffn_g, conv_w_pw1=conv_w_pw1, conv_b_pw1=conv_b_pw1, conv_w_dw=conv_w_dw, conv_b_dw=conv_b_dw, conv_ln_g=conv_ln_g, conv_ln_b=conv_ln_b, conv_w_pw2=conv_w_pw2, conv_b_pw2=conv_b_pw2, gla_w_in=gla_w_in, gla_w_gate_up=gla_w_gate_up, gla_b_gate=gla_b_gate, gla_norm_g=gla_norm_g, gla_w_out=gla_w_out, ffn_w_in=ffn_w_in, ffn_w_out=ffn_w_out, loss_target=loss_target, m_w_ada=m_w_ada, m_b_ada=m_b_ada, m_pre_mix_g=m_pre_mix_g, m_post_mix_g=m_post_mix_g, m_pre_ffn_g=m_pre_ffn_g, m_post_ffn_g=m_post_ffn_g, m_conv_w_pw1=m_conv_w_pw1, m_conv_b_pw1=m_conv_b_pw1, m_conv_w_dw=m_conv_w_dw, m_conv_b_dw=m_conv_b_dw, m_conv_ln_g=m_conv_ln_g, m_conv_ln_b=m_conv_ln_b, m_conv_w_pw2=m_conv_w_pw2, m_conv_b_pw2=m_conv_b_pw2, m_gla_w_in=m_gla_w_in, m_gla_w_gate_up=m_gla_w_gate_up, m_gla_b_gate=m_gla_b_gate, m_gla_norm_g=m_gla_norm_g, m_gla_w_out=m_gla_w_out, m_ffn_w_in=m_ffn_w_in, m_ffn_w_out=m_ffn_w_out, v_w_ada=v_w_ada, v_b_ada=v_b_ada, v_pre_mix_g=v_pre_mix_g, v_post_mix_g=v_post_mix_g, v_pre_ffn_g=v_pre_ffn_g, v_post_ffn_g=v_post_ffn_g, v_conv_w_pw1=v_conv_w_pw1, v_conv_b_pw1=v_conv_b_pw1, v_conv_w_dw=v_conv_w_dw, v_conv_b_dw=v_conv_b_dw, v_conv_ln_g=v_conv_ln_g, v_conv_ln_b=v_conv_ln_b, v_conv_w_pw2=v_conv_w_pw2, v_conv_b_pw2=v_conv_b_pw2, v_gla_w_in=v_gla_w_in, v_gla_w_gate_up=v_gla_w_gate_up, v_gla_b_gate=v_gla_b_gate, v_gla_norm_g=v_gla_norm_g, v_gla_w_out=v_gla_w_out, v_ffn_w_in=v_ffn_w_in, v_ffn_w_out=v_ffn_w_out)
    weights = {n: given[n] for n in TWIN_WEIGHTS}
    shared = {n: given[n] for n in SHARED_INPUTS}
    per_example = {n: given[n] for n in ['x', 'c']}
    grad_fn = _jax.value_and_grad(_loss, argnums=(0, 1))

    def one_microbatch(ex, loss_target):
        ex = dict(ex)
        diff = ex.pop(TWIN_DIFF_INPUT)
        return grad_fn(weights, diff, {**shared, **ex}, loss_target)

    if N_MICROBATCH == 1:
        loss, (grad_w, grad_x) = one_microbatch(per_example, given["loss_target"])
    else:
        def body(carry, xs):
            loss_sum, grad_sum = carry
            l_k, (gw_k, gx_k) = one_microbatch(xs[0], xs[1])
            with _jax.named_scope("update"):
                return (loss_sum + l_k, _jax.tree.map(_jnp.add, grad_sum, gw_k)), gx_k

        init = (_jnp.zeros((), _jnp.float32), _jax.tree.map(_jnp.zeros_like, weights))
        (loss, grad_w), grad_x = _jax.lax.scan(body, init, (per_example, given["loss_target"]))
    with _jax.named_scope("update"):
        delta_w, new_m, new_v = {}, {}, {}
        for n in TWIN_WEIGHTS:
            delta_w[n], new_m[n], new_v[n] = _adamw(weights[n], grad_w[n], given["m_" + n], given["v_" + n])
    return (loss, grad_x, *[grad_w[n] for n in TWIN_WEIGHTS], *[delta_w[n] for n in TWIN_WEIGHTS],
            *[new_m[n] for n in TWIN_WEIGHTS], *[new_v[n] for n in TWIN_WEIGHTS])
```

```python
import jax
import jax.numpy as jnp
from jax import lax
from jax.experimental import pallas as pl
from jax.experimental.pallas import tpu as pltpu

F32, BF16 = jnp.float32, jnp.bfloat16
MESH = pl.DeviceIdType.MESH
HBM = pl.BlockSpec(memory_space=pl.ANY)

EPS = 1e-6
VMEM_LIMIT_BYTES = 48 * 1024 * 1024
N_CHIPS = 4
N_DEVICES = 8
GLA_HEADS = 4
GLA_CHUNK = 64
GLA_TAU = 16.0
CONV_HALO = 32
ADAM_LR, ADAM_B1, ADAM_B2, ADAM_EPS, ADAM_WD, ADAM_STEP = 0.001, 0.9, 0.999, 1e-08, 0.01, 10


def _pc(body, **kw):
    return pl.pallas_call(body, **kw)


def _params(*sem):
    return pltpu.CompilerParams(dimension_semantics=sem, vmem_limit_bytes=VMEM_LIMIT_BYTES)


def _pick(n, cands):
    for c in cands:
        if c <= n and n % c == 0:
            return c
    return n


def _fold8(z):
    r, w = z.shape
    return z.reshape(r // 8, 8, w).sum(axis=0)


def _sigmoid(x):
    return 1.0 / (1.0 + jnp.exp(-x))


def _exchange(name, xs, out_shape, masks, src_of, dst_of, local=True):
    n_in = len(xs)

    def body(*refs):
        x_refs, o_ref = refs[:n_in], refs[n_in]
        send_sems, recv_sems, local_sems = refs[n_in + 1:]
        x, y, c = lax.axis_index("x"), lax.axis_index("y"), lax.axis_index("c")
        me = (x, y, c)
        peers = [(1 - x if a else x, 1 - y if b else y, 1 - c if d else c) for a, b, d in masks]
        started = []
        if local:
            for q, (s, t) in enumerate(zip(src_of(x_refs, me), dst_of(o_ref, me))):
                cp = pltpu.make_async_copy(s, t, local_sems.at[q])
                cp.start()
                started.append(cp)
        sends = []
        for k, peer in enumerate(peers):
            for q, (s, t) in enumerate(zip(src_of(x_refs, peer), dst_of(o_ref, me))):
                cp = pltpu.make_async_remote_copy(
                    src_ref=s, dst_ref=t, send_sem=send_sems.at[k, q], recv_sem=recv_sems.at[k, q],
                    device_id=peer, device_id_type=MESH)
                cp.start()
                sends.append(cp)
        for k, peer in enumerate(peers):
            for q, (s, t) in enumerate(zip(src_of(x_refs, me), dst_of(o_ref, peer))):
                pltpu.make_async_remote_copy(
                    src_ref=s, dst_ref=t, send_sem=send_sems.at[k, q], recv_sem=recv_sems.at[k, q],
                    device_id=peer, device_id_type=MESH).wait_recv()
        for cp in sends:
            cp.wait_send()
        for cp in started:
            cp.wait()

    n_q = n_in if len(xs) > 1 else 1
    return _pc(
        body, name=name, out_shape=out_shape,
        in_specs=[HBM] * n_in, out_specs=HBM,
        scratch_shapes=[pltpu.SemaphoreType.DMA((len(masks), n_q)),
                        pltpu.SemaphoreType.DMA((len(masks), n_q)),
                        pltpu.SemaphoreType.DMA((n_q,))],
    )(*xs)


_CHIP_MASKS = [(1, 0, 0), (0, 1, 0), (1, 1, 0)]
_ALL_MASKS = [(a, b, d) for a in (0, 1) for b in (0, 1) for d in (0, 1) if (a, b, d) != (0, 0, 0)]


def _chip(p):
    return 2 * p[0] + p[1]


def _dev(p):
    return 4 * p[0] + 2 * p[1] + p[2]


def allgather_chips(name, shard):
    r, c = shard.shape
    return _exchange(
        name, [shard], jax.ShapeDtypeStruct((N_CHIPS, r, c), shard.dtype), _CHIP_MASKS,
        lambda xr, peer: [xr[0]], lambda o, src: [o.at[_chip(src)]])


def scatter_chips(name, parts):
    n, r, c = parts[0].shape
    nl = len(parts)
    return _exchange(
        name, parts, jax.ShapeDtypeStruct((N_CHIPS, nl * r, c), parts[0].dtype), _CHIP_MASKS,
        lambda xr, peer: [xr[l].at[_chip(peer)] for l in range(nl)],
        lambda o, src: [o.at[_chip(src), pl.ds(l * r, r), :] for l in range(nl)])


def swap_cores(name, a):
    return _exchange(
        name, [a], jax.ShapeDtypeStruct(a.shape, a.dtype), [(0, 0, 1)],
        lambda xr, peer: [xr[0]], lambda o, src: [o], local=False)


def allgather_devices(name, v):
    r, c = v.shape
    return _exchange(
        name, [v], jax.ShapeDtypeStruct((N_DEVICES, r, c), v.dtype), _ALL_MASKS,
        lambda xr, peer: [xr[0]], lambda o, src: [o.at[_dev(src)]])


class Wt:
    def __init__(self, arr, kind, layer, rows, cols, tn=None):
        self.arr, self.kind, self.l, self.R, self.C, self.tn = arr, kind, layer, rows, cols, tn
        self.K = 4 * rows if kind == "row" else rows
        self.N = 4 * cols if kind == "col" else cols

    def spec(self, tk, tn):
        l, R, C = self.l, self.R, self.C
        if self.kind == "plain":
            off = l * (R // tk)
            return (tk, tn), lambda kb, jb: (off + kb, jb)
        if self.kind == "col":
            assert tn == self.tn
            per, off = C // tn, l * (R // tk)
            return (None, tk, tn), lambda kb, jb: (2 * (jb % 2) + (jb // 2) // per, off + kb, (jb // 2) % per)
        per = R // tk
        return (None, tk, tn), lambda kb, jb: (kb // per, l * per + kb % per, jb)

    def tile_k(self, cands):
        return _pick(self.R, cands)

    def tile_n(self, cands):
        return self.tn if self.kind == "col" else _pick(self.C, cands)


_TM = (1024, 512, 256, 128, 64, 32, 16, 8)
_TK = (2048, 1408, 1024, 896, 512, 256, 128, 64, 32, 16)
_TN = (1024, 896, 768, 512, 384, 256, 128)
_TR = (1024, 1408, 512, 256, 128, 64, 32, 16)


def _accumulate(part, acc_ref, k, nk, finish):
    if nk == 1:
        finish(part)
        return

    @pl.when(k == 0)
    def _():
        acc_ref[...] = part

    @pl.when(k > 0)
    def _():
        acc_ref[...] += part

    @pl.when(k == nk - 1)
    def _():
        finish(acc_ref[...])


def mm_nn(name, a, w, *, bias=None, out_dtype=F32):
    M, K = a.shape
    assert K == w.K
    N = w.N
    tm, tk, tn = _pick(M, _TM), w.tile_k(_TK), w.tile_n(_TN)
    nk = K // tk
    wblock, wmap = w.spec(tk, tn)
    in_specs = [pl.BlockSpec((tm, tk), lambda i, j, k: (i, k)),
                pl.BlockSpec(wblock, lambda i, j, k: wmap(k, j))]
    args = [a, w.arr]
    if bias is not None:
        in_specs.append(pl.BlockSpec((1, tn), lambda i, j, k: (0, j)))
        args.append(bias)

    def body(*refs):
        a_ref, b_ref = refs[0], refs[1]
        bias_ref = refs[2] if bias is not None else None
        o_ref = refs[3] if bias is not None else refs[2]
        acc_ref = refs[-1] if nk > 1 else None
        part = jnp.dot(a_ref[...].astype(BF16), b_ref[...].astype(BF16), preferred_element_type=F32)

        def finish(acc):
            if bias_ref is not None:
                acc = acc + bias_ref[...]
            o_ref[...] = acc.astype(o_ref.dtype)

        _accumulate(part, acc_ref, pl.program_id(2), nk, finish)

    return _pc(
        body, name=name, out_shape=jax.ShapeDtypeStruct((M, N), out_dtype),
        grid=(M // tm, N // tn, nk), in_specs=in_specs,
        out_specs=pl.BlockSpec((tm, tn), lambda i, j, k: (i, j)),
        scratch_shapes=[pltpu.VMEM((tm, tn), F32)] if nk > 1 else [],
        compiler_params=_params("parallel", "parallel", "arbitrary"),
    )(*args)


def mm_nt(name, a, w, *, out_dtype=F32):
    M, N = a.shape
    assert N == w.N
    K = w.K
    tm, tj, tn = _pick(M, _TM), w.tile_k(_TR), w.tile_n(_TK)
    nn = N // tn
    wblock, wmap = w.spec(tj, tn)

    def body(a_ref, b_ref, o_ref, *scr):
        part = lax.dot_general(a_ref[...].astype(BF16), b_ref[...].astype(BF16),
                               (((1,), (1,)), ((), ())), preferred_element_type=F32)

        def finish(acc):
            o_ref[...] = acc.astype(o_ref.dtype)

        _accumulate(part, scr[0] if nn > 1 else None, pl.program_id(2), nn, finish)

    return _pc(
        body, name=name, out_shape=jax.ShapeDtypeStruct((M, K), out_dtype),
        grid=(M // tm, K // tj, nn),
        in_specs=[pl.BlockSpec((tm, tn), lambda i, j, n: (i, n)),
                  pl.BlockSpec(wblock, lambda i, j, n: wmap(j, n))],
        out_specs=pl.BlockSpec((tm, tj), lambda i, j, n: (i, j)),
        scratch_shapes=[pltpu.VMEM((tm, tj), F32)] if nn > 1 else [],
        compiler_params=_params("parallel", "parallel", "arbitrary"),
    )(a, w.arr)


def mm_tn(name, a, b, w, *, out_dtype=BF16):
    T, K = a.shape
    N = b.shape[1]
    assert (K, N) == (w.K, w.N) and w.l == 0
    tm = w.tile_k(_TR)
    tn = w.tile_n(_TN)
    tk = _pick(T, (1024, 512, 256, 128, 64, 32, 16))
    nk = T // tk
    oblock, omap = w.spec(tm, tn)
    shape = (w.R, w.C) if w.kind == "plain" else (N_CHIPS, w.R, w.C)

    def body(a_ref, b_ref, o_ref, *scr):
        part = lax.dot_general(a_ref[...].astype(BF16), b_ref[...].astype(BF16),
                               (((0,), (0,)), ((), ())), preferred_element_type=F32)

        def finish(acc):
            o_ref[...] = acc.astype(o_ref.dtype)

        _accumulate(part, scr[0] if nk > 1 else None, pl.program_id(2), nk, finish)

    return _pc(
        body, name=name, out_shape=jax.ShapeDtypeStruct(shape, out_dtype),
        grid=(K // tm, N // tn, nk),
        in_specs=[pl.BlockSpec((tk, tm), lambda i, j, k: (k, i)),
                  pl.BlockSpec((tk, tn), lambda i, j, k: (k, j))],
        out_specs=pl.BlockSpec(oblock, lambda i, j, k: omap(i, j)),
        scratch_shapes=[pltpu.VMEM((tm, tn), F32)] if nk > 1 else [],
        compiler_params=_params("parallel", "parallel", "arbitrary"),
    )(a, b)


def _rows(name, body, rows_in, vecs_in, rows_out, accs_out, tm=256):
    T = rows_in[0].shape[0]
    tm = _pick(T, (tm, 128, 64, 32, 16, 8))
    n_r, n_v, n_o = len(rows_in), len(vecs_in), len(rows_out)

    def kern(*refs):
        r_refs, v_refs = refs[:n_r], refs[n_r:n_r + n_v]
        o_refs, a_refs = refs[n_r + n_v:n_r + n_v + n_o], refs[n_r + n_v + n_o:]

        @pl.when(pl.program_id(0) == 0)
        def _():
            for a_ref in a_refs:
                a_ref[...] = jnp.zeros(a_ref.shape, F32)

        body(r_refs, v_refs, o_refs, a_refs)

    in_specs = [pl.BlockSpec((tm, a.shape[1]), lambda i: (i, 0)) for a in rows_in]
    in_specs += [pl.BlockSpec(v.shape, lambda i: (0, 0)) for v in vecs_in]
    out_shape = [jax.ShapeDtypeStruct((T, w), dt) for w, dt in rows_out]
    out_shape += [jax.ShapeDtypeStruct((8, w), F32) for w in accs_out]
    out_specs = [pl.BlockSpec((tm, w), lambda i: (i, 0)) for w, _ in rows_out]
    out_specs += [pl.BlockSpec((8, w), lambda i: (0, 0)) for w in accs_out]
    return _pc(kern, name=name, out_shape=out_shape, grid=(T // tm,), in_specs=in_specs,
               out_specs=out_specs, compiler_params=_params("arbitrary"))(*rows_in, *vecs_in)


def _rms(x):
    return lax.rsqrt(jnp.mean(x * x, axis=-1, keepdims=True) + EPS)


def prenorm_fwd(name, x, gain, shift):
    def body(r, v, o, a):
        xv = r[0][...]
        o[0][...] = (xv * _rms(xv) * v[0][...] + v[1][...]).astype(BF16)
    return _rows(name, body, [x], [gain, shift], [(x.shape[1], BF16)], [])[0]


def postnorm_fwd(name, x, y, w):
    def body(r, v, o, a):
        yv = r[1][...]
        o[0][...] = r[0][...] + yv * _rms(yv) * v[0][...]
    return _rows(name, body, [x, y], [w], [(x.shape[1], F32)], [])[0]


def postnorm_bwd(name, dxo, y, post_g, gate):
    def body(r, v, o, a):
        d, yv = r[0][...], r[1][...]
        pg, gt = v[0][...], v[1][...]
        ry = _rms(yv)
        yn = yv * ry
        t = d * yn
        dyn = d * (gt * pg)
        dy = ry * (dyn - yn * jnp.mean(dyn * yn, axis=-1, keepdims=True))
        o[0][...] = dy.astype(BF16)
        a[0][...] += _fold8(t * pg)
        a[1][...] += _fold8(t * gt)
        a[2][...] += _fold8(dy)
    D = y.shape[1]
    return _rows(name, body, [dxo, y], [post_g, gate], [(D, BF16)], [D, D, D])


def prenorm_bwd(name, dh, x, dxo, pre_g, scale):
    def body(r, v, o, a):
        dhv, xv, d = r[0][...], r[1][...], r[2][...]
        pg, sc1 = v[0][...], 1.0 + v[1][...]
        rx = _rms(xv)
        xn = xv * rx
        t = dhv * xn
        dxn = dhv * (pg * sc1)
        o[0][...] = d + rx * (dxn - xn * jnp.mean(dxn * xn, axis=-1, keepdims=True))
        a[0][...] += _fold8(dhv)
        a[1][...] += _fold8(t * pg)
        a[2][...] += _fold8(t * sc1)
    D = x.shape[1]
    return _rows(name, body, [dh, x, dxo], [pre_g, scale], [(D, F32)], [D, D, D])


def loss_bwd(name, y, target):
    D = y.shape[1]

    def body(r, v, o, a):
        e = r[0][...] - r[1][...]
        o[0][...] = e * (1.0 / D)
        a[0][...] += _fold8(e * e * (0.5 / D))
    return _rows(name, body, [y, target], [], [(D, F32)], [D])


def ln_silu_fwd(name, v_, g, b):
    def body(r, v, o, a):
        x = r[0][...]
        xc = x - jnp.mean(x, axis=-1, keepdims=True)
        ln = xc * lax.rsqrt(jnp.mean(xc * xc, axis=-1, keepdims=True) + EPS) * v[0][...] + v[1][...]
        o[0][...] = (ln * _sigmoid(ln)).astype(BF16)
    return _rows(name, body, [v_], [g, b], [(v_.shape[1], BF16)], [])[0]


def ln_silu_bwd(name, ds, v_, g, b):
    def body(r, v, o, a):
        dsv, x = r[0][...], r[1][...]
        xc = x - jnp.mean(x, axis=-1, keepdims=True)
        rstd = lax.rsqrt(jnp.mean(xc * xc, axis=-1, keepdims=True) + EPS)
        xh = xc * rstd
        ln = xh * v[0][...] + v[1][...]
        sg = _sigmoid(ln)
        dln = dsv * (sg * (1.0 + ln * (1.0 - sg)))
        dxh = dln * v[0][...]
        o[0][...] = rstd * (dxh - jnp.mean(dxh, axis=-1, keepdims=True)
                            - xh * jnp.mean(dxh * xh, axis=-1, keepdims=True))
        a[0][...] += _fold8(dln * xh)
        a[1][...] += _fold8(dln)
    D = v_.shape[1]
    return _rows(name, body, [ds, v_], [g, b], [(D, F32)], [D, D])


def _pairs(name, body, u, others, out_w, out_dtype, tn, n_acc=0, tm=256):
    T, F2 = u.shape
    F = F2 // 2
    tm = _pick(T, (tm, 128, 64, 32, 16, 8))

    def kern(*refs):
        u_ref, o_refs = refs[0], refs[1:1 + len(others)]
        out_ref, acc_refs = refs[1 + len(others)], refs[2 + len(others):]

        @pl.when(pl.program_id(1) == 0)
        def _():
            for a_ref in acc_refs:
                a_ref[...] = jnp.zeros(a_ref.shape, F32)

        body(u_ref, o_refs, out_ref, acc_refs)

    out_shape = [jax.ShapeDtypeStruct((T, out_w * F), out_dtype)]
    out_shape += [jax.ShapeDtypeStruct((8, F2), F32)] * n_acc
    out_specs = [pl.BlockSpec((tm, out_w * tn), lambda g, i: (i, g))]
    out_specs += [pl.BlockSpec((8, 2 * tn), lambda g, i: (0, g))] * n_acc
    return _pc(kern, name=name, out_shape=out_shape, grid=(F // tn, T // tm),
               in_specs=[pl.BlockSpec((tm, 2 * tn), lambda g, i: (i, g))]
               + [pl.BlockSpec((tm, tn), lambda g, i: (i, g))] * len(others),
               out_specs=out_specs, compiler_params=_params("parallel", "arbitrary"))(u, *others)


def swiglu_fwd(name, u, tn):
    def body(u_ref, o, out, acc):
        gate, up = u_ref[:, :tn].astype(F32), u_ref[:, tn:].astype(F32)
        out[...] = (gate * _sigmoid(gate) * up).astype(BF16)
    return _pairs(name, body, u, [], 1, BF16, tn)[0]


def swiglu_bwd(name, u, da, tn):
    def body(u_ref, o, out, acc):
        gate, up = u_ref[:, :tn].astype(F32), u_ref[:, tn:].astype(F32)
        d = o[0][...].astype(F32)
        sg = _sigmoid(gate)
        out[:, :tn] = (d * up * (sg * (1.0 + gate * (1.0 - sg)))).astype(BF16)
        out[:, tn:] = (d * gate * sg).astype(BF16)
    return _pairs(name, body, u, [da], 2, BF16, tn)[0]


def glu_fwd(name, u, tn):
    def body(u_ref, o, out, acc):
        out[...] = u_ref[:, :tn].astype(F32) * _sigmoid(u_ref[:, tn:].astype(F32))
    return _pairs(name, body, u, [], 1, F32, tn)[0]


def glu_bwd(name, u, dglu, tn):
    def body(u_ref, o, out, acc):
        a, g = u_ref[:, :tn].astype(F32), u_ref[:, tn:].astype(F32)
        d = o[0][...]
        sg = _sigmoid(g)
        da, dg = d * sg, d * a * (sg * (1.0 - sg))
        out[:, :tn] = da.astype(BF16)
        out[:, tn:] = dg.astype(BF16)
        acc[0][:, :tn] += _fold8(da)
        acc[0][:, tn:] += _fold8(dg)
    return _pairs(name, body, u, [dglu], 2, BF16, tn, n_acc=1)


_CONV_ROWS, _CONV_LANES = 64, 128


def dwconv_fwd(name, x, w, b):
    T, D = x.shape
    width = w.shape[0]
    tt, cb = _pick(T, (256, 128, 64)), _pick(D, (256, 128))
    off = CONV_HALO - (width - 1)
    rows = min(_CONV_ROWS, tt)

    def body(cur_ref, prev_ref, w_ref, b_ref, o_ref, ext):
        t = pl.program_id(1)
        tail = prev_ref[pl.ds(tt - CONV_HALO, CONV_HALO), :]
        ext[pl.ds(0, CONV_HALO), :] = jnp.where(t > 0, tail, 0.0)
        ext[pl.ds(CONV_HALO, tt), :] = cur_ref[...]
        for l0 in range(0, cb, _CONV_LANES):
            ln = pl.ds(l0, _CONV_LANES)
            for r0 in range(0, tt, rows):
                acc = jnp.broadcast_to(b_ref[:, ln], (rows, _CONV_LANES))
                for k in range(width):
                    acc = acc + ext[pl.ds(r0 + off + k, rows), ln] * w_ref[pl.ds(k, 1), ln]
                o_ref[pl.ds(r0, rows), ln] = acc

    return _pc(
        body, name=name, out_shape=jax.ShapeDtypeStruct((T, D), F32), grid=(D // cb, T // tt),
        in_specs=[pl.BlockSpec((tt, cb), lambda c, t: (t, c)),
                  pl.BlockSpec((tt, cb), lambda c, t: (jnp.maximum(t - 1, 0), c)),
                  pl.BlockSpec((width, cb), lambda c, t: (0, c)),
                  pl.BlockSpec((1, cb), lambda c, t: (0, c))],
        out_specs=pl.BlockSpec((tt, cb), lambda c, t: (t, c)),
        scratch_shapes=[pltpu.VMEM((tt + CONV_HALO, cb), F32)],
        compiler_params=_params("parallel", "arbitrary"),
    )(x, x, w, b)


def dwconv_bwd(name, dv, x, w):
    T, D = x.shape
    width = w.shape[0]
    tt, cb = _pick(T, (256, 128, 64)), _pick(D, (256, 128))
    off = CONV_HALO - (width - 1)
    rows = min(_CONV_ROWS, tt)
    nt = T // tt

    def body(dv_ref, dvn_ref, x_ref, xp_ref, w_ref, dx_ref, dw_ref, db_ref, ext_d, ext_x):
        t = pl.program_id(1)

        @pl.when(t == 0)
        def _():
            dw_ref[...] = jnp.zeros(dw_ref.shape, F32)
            db_ref[...] = jnp.zeros(db_ref.shape, F32)

        ext_d[pl.ds(0, tt), :] = dv_ref[...]
        ext_d[pl.ds(tt, CONV_HALO), :] = jnp.where(t < nt - 1, dvn_ref[pl.ds(0, CONV_HALO), :], 0.0)
        ext_x[pl.ds(0, CONV_HALO), :] = jnp.where(t > 0, xp_ref[pl.ds(tt - CONV_HALO, CONV_HALO), :], 0.0)
        ext_x[pl.ds(CONV_HALO, tt), :] = x_ref[...]
        db_ref[...] += _fold8(dv_ref[...])
        for l0 in range(0, cb, _CONV_LANES):
            ln = pl.ds(l0, _CONV_LANES)
            for r0 in range(0, tt, rows):
                acc = jnp.zeros((rows, _CONV_LANES), F32)
                for k in range(width):
                    acc = acc + ext_d[pl.ds(r0 + (width - 1) - k, rows), ln] * w_ref[pl.ds(k, 1), ln]
                dx_ref[pl.ds(r0, rows), ln] = acc
            for k in range(width):
                s = jnp.zeros((8, _CONV_LANES), F32)
                for r0 in range(0, tt, rows):
                    s = s + _fold8(ext_d[pl.ds(r0, rows), ln] * ext_x[pl.ds(r0 + off + k, rows), ln])
                dw_ref[pl.ds(8 * k, 8), ln] += s

    return _pc(
        body, name=name,
        out_shape=[jax.ShapeDtypeStruct((T, D), F32), jax.ShapeDtypeStruct((8 * width, D), F32),
                   jax.ShapeDtypeStruct((8, D), F32)],
        grid=(D // cb, nt),
        in_specs=[pl.BlockSpec((tt, cb), lambda c, t: (t, c)),
                  pl.BlockSpec((tt, cb), lambda c, t: (jnp.minimum(t + 1, nt - 1), c)),
                  pl.BlockSpec((tt, cb), lambda c, t: (t, c)),
                  pl.BlockSpec((tt, cb), lambda c, t: (jnp.maximum(t - 1, 0), c)),
                  pl.BlockSpec((width, cb), lambda c, t: (0, c))],
        out_specs=[pl.BlockSpec((tt, cb), lambda c, t: (t, c)),
                   pl.BlockSpec((8 * width, cb), lambda c, t: (0, c)),
                   pl.BlockSpec((8, cb), lambda c, t: (0, c))],
        scratch_shapes=[pltpu.VMEM((tt + CONV_HALO, cb), F32), pltpu.VMEM((tt + CONV_HALO, cb), F32)],
        compiler_params=_params("parallel", "arbitrary"),
    )(dv, dv, x, x, w)


def _gla_dims(proj, wgu):
    DK = wgu.shape[1]
    DV = (proj.shape[1] - 128 - 2 * DK) // 2
    return DK, DV, DK // GLA_HEADS, DV // GLA_HEADS


def _gla_decay(a_ref, w_ref, bg_ref):
    C = GLA_CHUNK
    z = jnp.dot(a_ref[...].astype(BF16), w_ref[...].astype(BF16), preferred_element_type=F32) + bg_ref[...]
    g = (jnp.minimum(z, 0.0) - jnp.log(1.0 + jnp.exp(-jnp.abs(z)))) * (1.0 / GLA_TAU)
    row = lax.broadcasted_iota(jnp.int32, (C, C), 0)
    col = lax.broadcasted_iota(jnp.int32, (C, C), 1)
    bc = jnp.dot((row >= col).astype(F32), g, precision=lax.Precision.HIGHEST, preferred_element_type=F32)
    last = lax.broadcasted_iota(jnp.int32, bc.shape, 0) == C - 1
    b_last = jnp.sum(jnp.where(last, bc, 0.0), axis=0, keepdims=True)
    return z, bc, b_last


def _gla_in_specs(proj, wgu, DK, DV, dk, dv, cidx):
    C = GLA_CHUNK
    return [pl.BlockSpec((C, dk), lambda h, c: (cidx(c), h)),
            pl.BlockSpec((C, dk), lambda h, c: (cidx(c), GLA_HEADS + h)),
            pl.BlockSpec((C, dv), lambda h, c: (cidx(c), (2 * DK) // dv + h)),
            pl.BlockSpec((C, 128), lambda h, c: (cidx(c), (2 * DK + 2 * DV) // 128)),
            pl.BlockSpec((wgu.shape[0], dk), lambda h, c: (0, h)),
            pl.BlockSpec((1, dk), lambda h, c: (0, h))]


def _dot_nt(a, b):
    return lax.dot_general(a.astype(BF16), b.astype(BF16), (((1,), (1,)), ((), ())), preferred_element_type=F32)


def _dot_tn(a, b):
    return lax.dot_general(a.astype(BF16), b.astype(BF16), (((0,), (0,)), ((), ())), preferred_element_type=F32)


def _dot(a, b):
    return jnp.dot(a.astype(BF16), b.astype(BF16), preferred_element_type=F32)


def gla_fwd(name, proj, wgu, bg):
    T = proj.shape[0]
    C, H = GLA_CHUNK, GLA_HEADS
    DK, DV, dk, dv = _gla_dims(proj, wgu)
    nc = T // C
    scale = dk ** -0.5

    def body(q_ref, k_ref, v_ref, a_ref, w_ref, bg_ref, o_ref, s_ref, st):
        @pl.when(pl.program_id(1) == 0)
        def _():
            st[...] = jnp.zeros(st.shape, F32)

        _, bc, b_last = _gla_decay(a_ref, w_ref, bg_ref)
        q, k, v = q_ref[...], k_ref[...], v_ref[...]
        qe = q * scale * jnp.exp(bc)
        ke = k * jnp.exp(-bc)
        kd = k * jnp.exp(b_last - bc)
        sp = st[...]
        s_ref[...] = sp.astype(BF16)
        row = lax.broadcasted_iota(jnp.int32, (C, C), 0)
        col = lax.broadcasted_iota(jnp.int32, (C, C), 1)
        att = jnp.where(row >= col, _dot_nt(qe, ke), 0.0)
        o_ref[...] = _dot_nt(qe, sp) + _dot(att, v)
        st[...] = sp * jnp.exp(b_last) + _dot_tn(v, kd)

    return _pc(
        body, name=name,
        out_shape=[jax.ShapeDtypeStruct((T, DV), F32), jax.ShapeDtypeStruct((H, nc, dv, dk), BF16)],
        grid=(H, nc), in_specs=_gla_in_specs(proj, wgu, DK, DV, dk, dv, lambda c: c),
        out_specs=[pl.BlockSpec((C, dv), lambda h, c: (c, h)),
                   pl.BlockSpec((None, None, dv, dk), lambda h, c: (h, c, 0, 0))],
        scratch_shapes=[pltpu.VMEM((dv, dk), F32)],
        compiler_params=_params("parallel", "arbitrary"),
    )(proj, proj, proj, proj, wgu, bg)


def gla_bwd(name, proj, wgu, bg, states, do):
    T = proj.shape[0]
    C, H = GLA_CHUNK, GLA_HEADS
    DK, DV, dk, dv = _gla_dims(proj, wgu)
    nc = T // C
    scale = dk ** -0.5
    rev = lambda c: nc - 1 - c

    def body(q_ref, k_ref, v_ref, a_ref, w_ref, bg_ref, s_ref, do_ref, dq_ref, dk_ref, dv_ref, dz_ref, dbg_ref, dst):
        @pl.when(pl.program_id(1) == 0)
        def _():
            dst[...] = jnp.zeros(dst.shape, F32)
            dbg_ref[...] = jnp.zeros(dbg_ref.shape, F32)

        z, bc, b_last = _gla_decay(a_ref, w_ref, bg_ref)
        q, k, v, d_o = q_ref[...], k_ref[...], v_ref[...], do_ref[...]
        eb, enb, ed, el = jnp.exp(bc), jnp.exp(-bc), jnp.exp(b_last - bc), jnp.exp(b_last)
        qe, ke, kd = q * scale * eb, k * enb, k * ed
        sp = s_ref[...].astype(F32)
        ds_ = dst[...]
        row = lax.broadcasted_iota(jnp.int32, (C, C), 0)
        col = lax.broadcasted_iota(jnp.int32, (C, C), 1)
        keep = row >= col
        att = jnp.where(keep, _dot_nt(qe, ke), 0.0)
        datt = jnp.where(keep, _dot_nt(d_o, v), 0.0)
        dqe = _dot(d_o, sp) + _dot(datt, ke)
        dke = _dot_tn(datt, qe)
        dv_ref[...] = (_dot_tn(att, d_o) + _dot_nt(kd, ds_)).astype(BF16)
        dkd = _dot(v, ds_)
        dst[...] = ds_ * el + _dot_tn(d_o, qe)
        dq_ref[...] = (dqe * scale * eb).astype(BF16)
        dk_ref[...] = (dke * enb + dkd * ed).astype(BF16)
        d_el = jnp.sum(sp * ds_, axis=0, keepdims=True)
        db_last = jnp.sum(dkd * kd, axis=0, keepdims=True) + d_el * el
        last = lax.broadcasted_iota(jnp.int32, bc.shape, 0) == C - 1
        db = dqe * qe - dke * ke - dkd * kd + jnp.where(last, db_last, 0.0)
        dg = jnp.dot((row <= col).astype(F32), db, precision=lax.Precision.HIGHEST, preferred_element_type=F32)
        dz = dg * (1.0 / GLA_TAU) * _sigmoid(-z)
        dz_ref[...] = dz.astype(BF16)
        dbg_ref[...] += _fold8(dz)

    return _pc(
        body, name=name,
        out_shape=[jax.ShapeDtypeStruct((T, DK), BF16), jax.ShapeDtypeStruct((T, DK), BF16),
                   jax.ShapeDtypeStruct((T, DV), BF16), jax.ShapeDtypeStruct((T, DK), BF16),
                   jax.ShapeDtypeStruct((8, DK), F32)],
        grid=(H, nc),
        in_specs=_gla_in_specs(proj, wgu, DK, DV, dk, dv, rev)
        + [pl.BlockSpec((None, None, dv, dk), lambda h, c: (h, rev(c), 0, 0)),
           pl.BlockSpec((C, dv), lambda h, c: (rev(c), h))],
        out_specs=[pl.BlockSpec((C, dk), lambda h, c: (rev(c), h)),
                   pl.BlockSpec((C, dk), lambda h, c: (rev(c), h)),
                   pl.BlockSpec((C, dv), lambda h, c: (rev(c), h)),
                   pl.BlockSpec((C, dk), lambda h, c: (rev(c), h)),
                   pl.BlockSpec((8, dk), lambda h, c: (0, h))],
        scratch_shapes=[pltpu.VMEM((dv, dk), F32)],
        compiler_params=_params("parallel", "arbitrary"),
    )(proj, proj, proj, proj, wgu, bg, states, do)


def gla_out_fwd(name, o, proj, norm_g, r_block0):
    T, DV = o.shape
    dv = DV // GLA_HEADS
    tm = _pick(T, (512, 256, 128, 64))

    def body(o_ref, r_ref, g_ref, out_ref):
        ov, rv = o_ref[...], r_ref[...]
        out_ref[...] = (ov * _rms(ov) * g_ref[...] * (rv * _sigmoid(rv))).astype(BF16)

    return _pc(
        body, name=name, out_shape=jax.ShapeDtypeStruct((T, DV), BF16), grid=(GLA_HEADS, T // tm),
        in_specs=[pl.BlockSpec((tm, dv), lambda h, i: (i, h)),
                  pl.BlockSpec((tm, dv), lambda h, i: (i, r_block0 + h)),
                  pl.BlockSpec((1, dv), lambda h, i: (0, h))],
        out_specs=pl.BlockSpec((tm, dv), lambda h, i: (i, h)),
        compiler_params=_params("parallel", "parallel"),
    )(o, proj, norm_g)


def gla_out_bwd(name, dog, o, proj, norm_g, r_block0):
    T, DV = o.shape
    dv = DV // GLA_HEADS
    tm = _pick(T, (512, 256, 128, 64))

    def body(d_ref, o_ref, r_ref, g_ref, do_ref, dr_ref, dg_ref):
        @pl.when(pl.program_id(1) == 0)
        def _():
            dg_ref[...] = jnp.zeros(dg_ref.shape, F32)

        d, ov, rv, g = d_ref[...], o_ref[...], r_ref[...], g_ref[...]
        ro = _rms(ov)
        oh = ov * ro
        sg = _sigmoid(rv)
        dn = d * (rv * sg)
        dr_ref[...] = (d * (oh * g) * (sg * (1.0 + rv * (1.0 - sg)))).astype(BF16)
        doh = dn * g
        do_ref[...] = ro * (doh - oh * jnp.mean(doh * oh, axis=-1, keepdims=True))
        dg_ref[...] += _fold8(dn * oh)

    return _pc(
        body, name=name,
        out_shape=[jax.ShapeDtypeStruct((T, DV), F32), jax.ShapeDtypeStruct((T, DV), BF16),
                   jax.ShapeDtypeStruct((8, DV), F32)],
        grid=(GLA_HEADS, T // tm),
        in_specs=[pl.BlockSpec((tm, dv), lambda h, i: (i, h)),
                  pl.BlockSpec((tm, dv), lambda h, i: (i, h)),
                  pl.BlockSpec((tm, dv), lambda h, i: (i, r_block0 + h)),
                  pl.BlockSpec((1, dv), lambda h, i: (0, h))],
        out_specs=[pl.BlockSpec((tm, dv), lambda h, i: (i, h)),
                   pl.BlockSpec((tm, dv), lambda h, i: (i, h)),
                   pl.BlockSpec((8, dv), lambda h, i: (0, h))],
        compiler_params=_params("parallel", "arbitrary"),
    )(dog, o, proj, norm_g)


def _ew_rows(r, c):
    return _pick(r, tuple(t for t in (512, 256, 128, 64, 32, 16, 8) if t * c <= 256 * 1024) or (8,))


def sum_slots(name, p):
    n, r, c = p.shape
    tr = _ew_rows(r, c)

    def body(p_ref, o_ref):
        acc = p_ref[0].astype(F32)
        for s in range(1, n):
            acc = acc + p_ref[s].astype(F32)
        o_ref[...] = acc

    return _pc(body, name=name, out_shape=jax.ShapeDtypeStruct((r, c), F32), grid=(r // tr,),
               in_specs=[pl.BlockSpec((n, tr, c), lambda i: (0, i, 0))],
               out_specs=pl.BlockSpec((tr, c), lambda i: (i, 0)),
               compiler_params=_params("parallel"))(p)


def adamw(name, w, gs, m, v):
    r, c = w.shape
    tr = _ew_rows(r, c)
    n_g = len(gs)
    m_corr = 1.0 / (1.0 - ADAM_B1 ** ADAM_STEP)
    v_corr = 1.0 / (1.0 - ADAM_B2 ** ADAM_STEP)

    def body(*refs):
        w_ref, g_refs, m_ref, v_ref = refs[0], refs[1:1 + n_g], refs[1 + n_g], refs[2 + n_g]
        g_out, d_out, m_out, v_out = refs[3 + n_g:]
        g = g_refs[0][...]
        if n_g == 2:
            g = g + g_refs[1][...]
        mn = ADAM_B1 * m_ref[...] + (1.0 - ADAM_B1) * g
        vn = ADAM_B2 * v_ref[...] + (1.0 - ADAM_B2) * (g * g)
        g_out[...] = g
        m_out[...] = mn
        v_out[...] = vn
        d_out[...] = -ADAM_LR * ((mn * m_corr) / (jnp.sqrt(vn * v_corr) + ADAM_EPS) + ADAM_WD * w_ref[...])

    spec = pl.BlockSpec((tr, c), lambda i: (i, 0))
    return _pc(body, name=name, out_shape=[jax.ShapeDtypeStruct((r, c), F32)] * 4, grid=(r // tr,),
               in_specs=[spec] * (3 + n_g), out_specs=[spec] * 4,
               compiler_params=_params("parallel"))(w, *gs, m, v)


def _flat_pack(arrs):
    flat = jnp.concatenate([a.reshape(-1).astype(F32) for a in arrs])
    n = flat.shape[0]
    pad = (-n) % 1024
    return jnp.pad(flat, (0, pad)).reshape(-1, 128)


def _flat_unpack(packed, shapes, lead=()):
    flat = packed.reshape(lead + (-1,))
    out, pos = [], 0
    for s in shapes:
        n = 1
        for d in s:
            n *= d
        out.append(flat[..., pos:pos + n].reshape(lead + tuple(s)))
        pos += n
    return out


def _interleave_vec(b, tn):
    f = b.shape[-1] // 2
    return b.reshape(2, f // tn, tn).transpose(1, 0, 2).reshape(1, 2 * f)


def _deinterleave_vec(b, tn):
    f = b.shape[-1] // 2
    return b.reshape(f // tn, 2, tn).transpose(1, 0, 2).reshape(2 * f)


def kernel(x, c, w_ada, b_ada, pre_mix_g, post_mix_g, pre_ffn_g, post_ffn_g, conv_w_pw1, conv_b_pw1, conv_w_dw, conv_b_dw, conv_ln_g, conv_ln_b, conv_w_pw2, conv_b_pw2, gla_w_in, gla_w_gate_up, gla_b_gate, gla_norm_g, gla_w_out, ffn_w_in, ffn_w_out, loss_target, m_w_ada, m_b_ada, m_pre_mix_g, m_post_mix_g, m_pre_ffn_g, m_post_ffn_g, m_conv_w_pw1, m_conv_b_pw1, m_conv_w_dw, m_conv_b_dw, m_conv_ln_g, m_conv_ln_b, m_conv_w_pw2, m_conv_b_pw2, m_gla_w_in, m_gla_w_gate_up, m_gla_b_gate, m_gla_norm_g, m_gla_w_out, m_ffn_w_in, m_ffn_w_out, v_w_ada, v_b_ada, v_pre_mix_g, v_post_mix_g, v_pre_ffn_g, v_post_ffn_g, v_conv_w_pw1, v_conv_b_pw1, v_conv_w_dw, v_conv_b_dw, v_conv_ln_g, v_conv_ln_b, v_conv_w_pw2, v_conv_b_pw2, v_gla_w_in, v_gla_w_gate_up, v_gla_b_gate, v_gla_norm_g, v_gla_w_out, v_ffn_w_in, v_ffn_w_out):
    weights = dict(w_ada=w_ada, b_ada=b_ada, pre_mix_g=pre_mix_g, post_mix_g=post_mix_g, pre_ffn_g=pre_ffn_g, post_ffn_g=post_ffn_g, conv_w_pw1=conv_w_pw1, conv_b_pw1=conv_b_pw1, conv_w_dw=conv_w_dw, conv_b_dw=conv_b_dw, conv_ln_g=conv_ln_g, conv_ln_b=conv_ln_b, conv_w_pw2=conv_w_pw2, conv_b_pw2=conv_b_pw2, gla_w_in=gla_w_in, gla_w_gate_up=gla_w_gate_up, gla_b_gate=gla_b_gate, gla_norm_g=gla_norm_g, gla_w_out=gla_w_out, ffn_w_in=ffn_w_in, ffn_w_out=ffn_w_out)
    mom_m = dict(w_ada=m_w_ada, b_ada=m_b_ada, pre_mix_g=m_pre_mix_g, post_mix_g=m_post_mix_g, pre_ffn_g=m_pre_ffn_g, post_ffn_g=m_post_ffn_g, conv_w_pw1=m_conv_w_pw1, conv_b_pw1=m_conv_b_pw1, conv_w_dw=m_conv_w_dw, conv_b_dw=m_conv_b_dw, conv_ln_g=m_conv_ln_g, conv_ln_b=m_conv_ln_b, conv_w_pw2=m_conv_w_pw2, conv_b_pw2=m_conv_b_pw2, gla_w_in=m_gla_w_in, gla_w_gate_up=m_gla_w_gate_up, gla_b_gate=m_gla_b_gate, gla_norm_g=m_gla_norm_g, gla_w_out=m_gla_w_out, ffn_w_in=m_ffn_w_in, ffn_w_out=m_ffn_w_out)
    mom_v = dict(w_ada=v_w_ada, b_ada=v_b_ada, pre_mix_g=v_pre_mix_g, post_mix_g=v_post_mix_g, pre_ffn_g=v_pre_ffn_g, post_ffn_g=v_post_ffn_g, conv_w_pw1=v_conv_w_pw1, conv_b_pw1=v_conv_b_pw1, conv_w_dw=v_conv_w_dw, conv_b_dw=v_conv_b_dw, conv_ln_g=v_conv_ln_g, conv_ln_b=v_conv_ln_b, conv_w_pw2=v_conv_w_pw2, conv_b_pw2=v_conv_b_pw2, gla_w_in=v_gla_w_in, gla_w_gate_up=v_gla_w_gate_up, gla_b_gate=v_gla_b_gate, gla_norm_g=v_gla_norm_g, gla_w_out=v_gla_w_out, ffn_w_in=v_ffn_w_in, ffn_w_out=v_ffn_w_out)
    order = list(weights)

    ax, ay, ac = lax.axis_index("x"), lax.axis_index("y"), lax.axis_index("c")
    my_chip, my_dev = 2 * ax + ay, 4 * ax + 2 * ay + ac

    x = x[0]
    target = loss_target[0]
    T, D = x.shape
    depth = w_ada.shape[0]
    n_conv, n_gla = conv_w_pw1.shape[0], gla_w_in.shape[0]
    width = conv_w_dw.shape[1]
    F = ffn_w_out.shape[1] * N_CHIPS
    DK = gla_w_gate_up.shape[2] * N_CHIPS
    rank = gla_w_gate_up.shape[1]
    gla_cols = 2 * DK + 2 * D + rank
    P = 2 * DK + 2 * D + 128
    dvh = D // GLA_HEADS
    tn_pw1 = _pick(conv_w_pw1.shape[2], (1024, 512, 256, 128))
    tn_ffn = _pick(ffn_w_in.shape[2], (1408, 1024, 512, 384, 256, 128))

    small_sharded = ["conv_w_dw", "gla_w_gate_up", "gla_b_gate", "gla_norm_g"]
    packed = _flat_pack([weights[n] for n in small_sharded])
    got = allgather_devices("gather_small", packed)[0::2]
    parts = _flat_unpack(got, [weights[n].shape for n in small_sharded], lead=(N_CHIPS,))
    w_dw_full, wgu_full, bgate_full, normg_full = [
        jnp.concatenate([p[s] for s in range(N_CHIPS)], axis=-1) for p in parts]
    w_dw_full = w_dw_full
    wgu_pad = jnp.pad(wgu_full, ((0, 0), (0, 128 - rank), (0, 0)))

    c_act = c * _sigmoid(c)
    c_all = allgather_devices("gather_c", jnp.pad(c_act, ((0, 7), (0, 0))))[:, 0, :]
    c16 = jnp.pad(c_all, ((0, 8), (0, 0))).astype(BF16)
    ada_cols = w_ada.shape[2]
    w_ada2 = w_ada.reshape(depth * D, ada_cols)
    mod_cols = [mm_nn(f"ada_fwd_{i}", c16, Wt(w_ada2, "plain", i, D, ada_cols)) for i in range(depth)]
    mod_cols = jnp.stack(mod_cols).reshape(depth * 16, ada_cols)
    mod_all = allgather_devices("gather_mod", mod_cols)[0::2]
    mod_all = mod_all.reshape(N_CHIPS, depth, 16, ada_cols).transpose(1, 2, 0, 3).reshape(depth, 16, 6 * D)
    mod = lax.dynamic_index_in_dim(mod_all, my_dev, axis=1, keepdims=False) + b_ada
    mod = mod.reshape(depth, 6, 1, D)

    def gathered(name):
        w = weights[name]
        return allgather_chips("gather_" + name, w.reshape(-1, w.shape[-1]).astype(BF16))

    g_pw1, g_pw2 = gathered("conv_w_pw1"), gathered("conv_w_pw2")
    g_gin, g_gout = gathered("gla_w_in"), gathered("gla_w_out")
    g_fin, g_fout = gathered("ffn_w_in"), gathered("ffn_w_out")
    w_pw1 = [Wt(g_pw1, "col", j, D, conv_w_pw1.shape[2], tn_pw1) for j in range(n_conv)]
    w_pw2 = [Wt(g_pw2, "row", j, conv_w_pw2.shape[1], D) for j in range(n_conv)]
    w_gout = [Wt(g_gout, "row", j, gla_w_out.shape[1], D) for j in range(n_gla)]
    w_fin = [Wt(g_fin, "col", i, D, ffn_w_in.shape[2], tn_ffn) for i in range(depth)]
    w_fout = [Wt(g_fout, "row", i, ffn_w_out.shape[1], D) for i in range(depth)]
    gin_cols = gla_w_in.shape[2]
    w_gin = []
    for j in range(n_gla):
        full = g_gin[:, j * D:(j + 1) * D, :].transpose(1, 0, 2).reshape(D, N_CHIPS * gin_cols)
        w_gin.append(Wt(jnp.pad(full, ((0, 0), (0, P - gla_cols))), "plain", 0, D, P))

    row = lambda a: a.reshape(1, -1)
    saved = []
    xs = x
    for i in range(depth):
        j = i // 2
        sh1, sc1, gt1, sh2, sc2, gt2 = [mod[i, q] for q in range(6)]
        s = dict(x_in=xs)
        h = prenorm_fwd(f"pre_mix_{i}", xs, row(pre_mix_g[i]) * (1.0 + sc1), sh1)
        s["h_mix"] = h
        if i % 2 == 0:
            u = mm_nn(f"pw1_{i}", h, w_pw1[j], bias=_interleave_vec(row(conv_b_pw1[j]), tn_pw1), out_dtype=BF16)
            glu = glu_fwd(f"glu_{i}", u, tn_pw1)
            v = dwconv_fwd(f"dwconv_{i}", glu, w_dw_full[j], row(conv_b_dw[j]))
            sl = ln_silu_fwd(f"ln_silu_{i}", v, row(conv_ln_g[j]), row(conv_ln_b[j]))
            y = mm_nn(f"pw2_{i}", sl, w_pw2[j], bias=row(conv_b_pw2[j]))
            s.update(u=u, glu=glu, v=v, s=sl)
        else:
            proj = mm_nn(f"gla_in_{i}", h, w_gin[j])
            o, states = gla_fwd(f"gla_{i}", proj, wgu_pad[j], row(bgate_full[j]))
            og = gla_out_fwd(f"gla_out_{i}", o, proj, row(normg_full[j]), (2 * DK + D) // dvh)
            y = mm_nn(f"gla_wout_{i}", og, w_gout[j])
            s.update(proj=proj, o=o, states=states, og=og)
        s["y_mix"] = y
        xs = postnorm_fwd(f"post_mix_{i}", xs, y, gt1 * row(post_mix_g[i]))
        s["x_mid"] = xs
        h = prenorm_fwd(f"pre_ffn_{i}", xs, row(pre_ffn_g[i]) * (1.0 + sc2), sh2)
        u = mm_nn(f"ffn_in_{i}", h, w_fin[i], out_dtype=BF16)
        a = swiglu_fwd(f"swiglu_{i}", u, tn_ffn)
        y = mm_nn(f"ffn_out_{i}", a, w_fout[i])
        s.update(h_ffn=h, u_ffn=u, a_ffn=a, y_ffn=y)
        xs = postnorm_fwd(f"post_ffn_{i}", xs, y, gt2 * row(post_ffn_g[i]))
        saved.append(s)

    dx, loss_acc = loss_bwd("loss", xs, target)
    loss = lax.psum(jnp.sum(loss_acc), ("x", "y", "c"))

    fold = lambda a: jnp.sum(a, axis=0)
    small_g = {n: [None] * weights[n].shape[0] for n in order if n not in
               ("w_ada", "conv_w_pw1", "conv_w_pw2", "gla_w_in", "gla_w_out", "ffn_w_in", "ffn_w_out")}
    part = dict(conv_w_pw1=[None] * n_conv, conv_w_pw2=[None] * n_conv, gla_w_in=[None] * n_gla,
                gla_w_out=[None] * n_gla, ffn_w_in=[None] * depth, ffn_w_out=[None] * depth)
    grad_w = lambda wt: Wt(None, wt.kind, 0, wt.R, wt.C, wt.tn)
    for i in reversed(range(depth)):
        j = i // 2
        s = saved[i]
        sh1, sc1, gt1, sh2, sc2, gt2 = [mod[i, q] for q in range(6)]
        dy, d_gt2, d_pg, _ = postnorm_bwd(f"post_ffn_bwd_{i}", dx, s["y_ffn"], row(post_ffn_g[i]), gt2)
        small_g["post_ffn_g"][i] = fold(d_pg)
        da = mm_nt(f"ffn_out_dgrad_{i}", dy, w_fout[i], out_dtype=BF16)
        part["ffn_w_out"][i] = mm_tn(f"ffn_out_wgrad_{i}", s["a_ffn"], dy, grad_w(w_fout[i]))
        du = swiglu_bwd(f"swiglu_bwd_{i}", s["u_ffn"], da, tn_ffn)
        part["ffn_w_in"][i] = mm_tn(f"ffn_in_wgrad_{i}", s["h_ffn"], du, grad_w(w_fin[i]))
        dh = mm_nt(f"ffn_in_dgrad_{i}", du, w_fin[i])
        dx, d_sh2, d_sc2, d_pg = prenorm_bwd(f"pre_ffn_bwd_{i}", dh, s["x_mid"], dx, row(pre_ffn_g[i]), sc2)
        small_g["pre_ffn_g"][i] = fold(d_pg)
        dy, d_gt1, d_pg, dy_sum = postnorm_bwd(f"post_mix_bwd_{i}", dx, s["y_mix"], row(post_mix_g[i]), gt1)
        small_g["post_mix_g"][i] = fold(d_pg)
        if i % 2 == 0:
            small_g["conv_b_pw2"][j] = fold(dy_sum)
            dsl = mm_nt(f"pw2_dgrad_{i}", dy, w_pw2[j])
            part["conv_w_pw2"][j] = mm_tn(f"pw2_wgrad_{i}", s["s"], dy, grad_w(w_pw2[j]))
            dv, d_lg, d_lb = ln_silu_bwd(f"ln_silu_bwd_{i}", dsl, s["v"], row(conv_ln_g[j]), row(conv_ln_b[j]))
            small_g["conv_ln_g"][j], small_g["conv_ln_b"][j] = fold(d_lg), fold(d_lb)
            dglu, d_wdw, d_bdw = dwconv_bwd(f"dwconv_bwd_{i}", dv, s["glu"], w_dw_full[j])
            small_g["conv_w_dw"][j] = d_wdw.reshape(width, 8, D).sum(axis=1)
            small_g["conv_b_dw"][j] = fold(d_bdw)
            du, du_sum = glu_bwd(f"glu_bwd_{i}", s["u"], dglu, tn_pw1)
            small_g["conv_b_pw1"][j] = _deinterleave_vec(fold(du_sum), tn_pw1)
            part["conv_w_pw1"][j] = mm_tn(f"pw1_wgrad_{i}", s["h_mix"], du, grad_w(w_pw1[j]))
            dh = mm_nt(f"pw1_dgrad_{i}", du, w_pw1[j])
        else:
            dog = mm_nt(f"gla_wout_dgrad_{i}", dy, w_gout[j])
            part["gla_w_out"][j] = mm_tn(f"gla_wout_wgrad_{i}", s["og"], dy, grad_w(w_gout[j]))
            d_o, d_r, d_ng = gla_out_bwd(f"gla_out_bwd_{i}", dog, s["o"], s["proj"], row(normg_full[j]),
                                         (2 * DK + D) // dvh)
            small_g["gla_norm_g"][j] = fold(d_ng)
            dq, dk_, dv_, dz, d_bg = gla_bwd(f"gla_bwd_{i}", s["proj"], wgu_pad[j], row(bgate_full[j]),
                                             s["states"], d_o)
            small_g["gla_b_gate"][j] = fold(d_bg)
            wgu_w = Wt(wgu_pad[j].astype(BF16), "plain", 0, 128, DK)
            d_a = mm_nt(f"gla_gate_dgrad_{i}", dz, wgu_w, out_dtype=BF16)
            a_low = s["proj"][:, 2 * DK + 2 * D:].astype(BF16)
            d_wgu = mm_tn(f"gla_gate_wgrad_{i}", a_low, dz, Wt(None, "plain", 0, 128, DK), out_dtype=F32)
            small_g["gla_w_gate_up"][j] = d_wgu[:rank]
            dproj = jnp.concatenate([dq, dk_, dv_, d_r, d_a], axis=1)
            g_in = mm_tn(f"gla_in_wgrad_{i}", s["h_mix"], dproj, Wt(None, "plain", 0, D, P))
            part["gla_w_in"][j] = g_in[:, :gla_cols].reshape(D, N_CHIPS, gin_cols).transpose(1, 0, 2)
            dh = mm_nt(f"gla_in_dgrad_{i}", dproj, w_gin[j])
        dx, d_sh1, d_sc1, d_pg = prenorm_bwd(f"pre_mix_bwd_{i}", dh, s["x_in"], dx, row(pre_mix_g[i]), sc1)
        small_g["pre_mix_g"][i] = fold(d_pg)
        small_g["b_ada"][i] = jnp.concatenate([fold(t) for t in (d_sh1, d_sc1, d_gt1, d_sh2, d_sc2, d_gt2)])
    grad_x = dx[None]

    small_names = list(small_g)
    local_small = [jnp.stack(small_g[n]) for n in small_names]
    full_shapes = [a.shape for a in local_small]
    small_all = allgather_devices("gather_small_grads", _flat_pack(local_small))
    small_sum = sum_slots("sum_small_grads", small_all)
    small_tot = dict(zip(small_names, _flat_unpack(small_sum, full_shapes)))
    for n in small_sharded:
        cols = weights[n].shape[-1]
        small_tot[n] = lax.dynamic_slice_in_dim(small_tot[n], my_chip * cols, cols, axis=small_tot[n].ndim - 1)

    dmod_all = _flat_unpack(small_all, full_shapes, lead=(N_DEVICES,))[small_names.index("b_ada")]
    dmod_cols = lax.dynamic_slice_in_dim(
        dmod_all.reshape(N_DEVICES, depth, N_CHIPS, ada_cols), my_chip, 1, axis=2)[:, :, 0, :]
    dmod16 = jnp.pad(dmod_cols.reshape(N_DEVICES, depth * ada_cols), ((0, 8), (0, 0))).astype(BF16)
    g_ada = [mm_tn(f"ada_wgrad_{i}", c16, dmod16[:, i * ada_cols:(i + 1) * ada_cols],
                   Wt(None, "plain", 0, D, ada_cols), out_dtype=F32) for i in range(depth)]
    g_ada = jnp.stack(g_ada).reshape(depth * D, ada_cols)

    results = {}

    def flat2(a):
        return a.reshape(-1, a.shape[-1])

    results["w_ada"] = [t.reshape(w_ada.shape) for t in
                        adamw("adamw_w_ada", flat2(w_ada), [g_ada], flat2(m_w_ada), flat2(v_w_ada))]
    for n in ("conv_w_pw1", "conv_w_pw2", "gla_w_in", "gla_w_out", "ffn_w_in", "ffn_w_out"):
        w = weights[n]
        arrived = scatter_chips("scatter_" + n, part[n])
        mine = sum_slots("sum_" + n, arrived)
        theirs = swap_cores("swap_" + n, mine)
        out = adamw("adamw_" + n, flat2(w), [mine, theirs], flat2(mom_m[n]), flat2(mom_v[n]))
        results[n] = [t.reshape(w.shape) for t in out]
    w_small = _flat_pack([weights[n] for n in small_names])
    out = adamw("adamw_small", w_small, [_flat_pack([small_tot[n] for n in small_names])],
                _flat_pack([mom_m[n] for n in small_names]), _flat_pack([mom_v[n] for n in small_names]))
    shapes = [weights[n].shape for n in small_names]
    unpacked = [_flat_unpack(t, shapes) for t in out]
    for q, n in enumerate(small_names):
        results[n] = [unpacked[r][q] for r in range(4)]

    outs = [loss, grad_x]
    for r in range(4):
        outs += [results[n][r] for n in order]
    return tuple(outs)
```

```python
import jax
import jax.numpy as jnp
from jax import lax
from jax.experimental import pallas as pl
from jax.experimental.pallas import tpu as pltpu

F32, BF16 = jnp.float32, jnp.bfloat16
MESH = pl.DeviceIdType.MESH
HBM = pl.BlockSpec(memory_space=pl.ANY)

EPS = 1e-6
VMEM_LIMIT_BYTES = 48 * 1024 * 1024
N_CHIPS = 4
N_DEVICES = 8
GLA_HEADS = 4
GLA_CHUNK = 64
GLA_TAU = 16.0
CONV_HALO = 32
ADAM_LR, ADAM_B1, ADAM_B2, ADAM_EPS, ADAM_WD, ADAM_STEP = 0.001, 0.9, 0.999, 1e-08, 0.01, 10


def _pc(body, **kw):
    return pl.pallas_call(body, **kw)


def _params(*sem):
    return pltpu.CompilerParams(dimension_semantics=sem, vmem_limit_bytes=VMEM_LIMIT_BYTES)


def _pick(n, cands):
    for c in cands:
        if c <= n and n % c == 0:
            return c
    return n


def _fold8(z):
    r, w = z.shape
    return z.reshape(r // 8, 8, w).sum(axis=0)


def _sigmoid(x):
    return 1.0 / (1.0 + jnp.exp(-x))


def _exchange(name, xs, out_shape, masks, src_of, dst_of, local=True):
    n_in = len(xs)

    def body(*refs):
        x_refs, o_ref = refs[:n_in], refs[n_in]
        send_sems, recv_sems, local_sems = refs[n_in + 1:]
        x, y, c = lax.axis_index("x"), lax.axis_index("y"), lax.axis_index("c")
        me = (x, y, c)
        peers = [(1 - x if a else x, 1 - y if b else y, 1 - c if d else c) for a, b, d in masks]
        started = []
        if local:
            for q, (s, t) in enumerate(zip(src_of(x_refs, me), dst_of(o_ref, me))):
                cp = pltpu.make_async_copy(s, t, local_sems.at[q])
                cp.start()
                started.append(cp)
        sends = []
        for k, peer in enumerate(peers):
            for q, (s, t) in enumerate(zip(src_of(x_refs, peer), dst_of(o_ref, me))):
                cp = pltpu.make_async_remote_copy(
                    src_ref=s, dst_ref=t, send_sem=send_sems.at[k, q], recv_sem=recv_sems.at[k, q],
                    device_id=peer, device_id_type=MESH)
                cp.start()
                sends.append(cp)
        for k, peer in enumerate(peers):
            for q, (s, t) in enumerate(zip(src_of(x_refs, me), dst_of(o_ref, peer))):
                pltpu.make_async_remote_copy(
                    src_ref=s, dst_ref=t, send_sem=send_sems.at[k, q], recv_sem=recv_sems.at[k, q],
                    device_id=peer, device_id_type=MESH).wait_recv()
        for cp in sends:
            cp.wait_send()
        for cp in started:
            cp.wait()

    n_q = n_in if len(xs) > 1 else 1
    return _pc(
        body, name=name, out_shape=out_shape,
        in_specs=[HBM] * n_in, out_specs=HBM,
        scratch_shapes=[pltpu.SemaphoreType.DMA((len(masks), n_q)),
                        pltpu.SemaphoreType.DMA((len(masks), n_q)),
                        pltpu.SemaphoreType.DMA((n_q,))],
    )(*xs)


_CHIP_MASKS = [(1, 0, 0), (0, 1, 0), (1, 1, 0)]
_ALL_MASKS = [(a, b, d) for a in (0, 1) for b in (0, 1) for d in (0, 1) if (a, b, d) != (0, 0, 0)]


def _chip(p):
    return 2 * p[0] + p[1]


def _dev(p):
    return 4 * p[0] + 2 * p[1] + p[2]


def swap_cores(name, a):
    return _exchange(
        name, [a], jax.ShapeDtypeStruct(a.shape, a.dtype), [(0, 0, 1)],
        lambda xr, peer: [xr[0]], lambda o, src: [o], local=False)


def allgather_devices(name, v):
    r, c = v.shape
    return _exchange(
        name, [v], jax.ShapeDtypeStruct((N_DEVICES, r, c), v.dtype), _ALL_MASKS,
        lambda xr, peer: [xr[0]], lambda o, src: [o.at[_dev(src)]])


_HBM = pl.BlockSpec(memory_space=pltpu.HBM)
_SEM = pl.BlockSpec(memory_space=pltpu.SEMAPHORE)
_N_PEER_CHIPS = len(_CHIP_MASKS)


def _peer_chips():
    x, y, c = lax.axis_index("x"), lax.axis_index("y"), lax.axis_index("c")
    return (x, y), [((1 - x if a else x, 1 - y if b else y), c) for a, b, _ in _CHIP_MASKS]


def _chip_copies(src_ref, land_ref, send_sems, recv_sems, whole_src):
    me, peers = _peer_chips()
    out = []
    for k, (chip, c) in enumerate(peers):
        src = src_ref if whole_src else src_ref.at[_chip(chip)]
        send = pltpu.make_async_remote_copy(
            src_ref=src, dst_ref=land_ref.at[_chip(me)], send_sem=send_sems.at[k], recv_sem=recv_sems.at[k],
            device_id=(*chip, c), device_id_type=MESH)
        recv = pltpu.make_async_remote_copy(
            src_ref=src, dst_ref=land_ref.at[_chip(chip)], send_sem=send_sems.at[k], recv_sem=recv_sems.at[k],
            device_id=(*chip, c), device_id_type=MESH)
        out.append((send, recv))
    return out


def chips_start(name, src, own, my_chip, whole_src, after=()):
    r, c = own.shape
    land = lax.dynamic_update_slice(lax.empty((N_CHIPS, r, c), src.dtype), own[None], (my_chip, 0, 0))
    n_after = len(after)

    def body(*refs):
        src_ref, land_ref = refs[0], refs[1]
        send_sems, recv_sems, _, _, token = refs[2 + n_after:]
        for send, _ in _chip_copies(src_ref, land_ref, send_sems, recv_sems, whole_src):
            send.start()
        token[...] = jnp.zeros(token.shape, token.dtype)

    send_sems, recv_sems, src_thru, land_thru, token = _pc(
        body, name=name,
        out_shape=(pltpu.SemaphoreType.DMA((_N_PEER_CHIPS,)), pltpu.SemaphoreType.DMA((_N_PEER_CHIPS,)),
                   pltpu.HBM(src.shape, src.dtype), pltpu.HBM(land.shape, land.dtype),
                   jax.ShapeDtypeStruct((8, 128), F32)),
        in_specs=[_HBM, _HBM] + [HBM] * n_after,
        out_specs=(_SEM, _SEM, _HBM, _HBM, pl.BlockSpec(memory_space=pltpu.VMEM)),
        input_output_aliases={0: 2, 1: 3},
        compiler_params=pltpu.CompilerParams(has_side_effects=pltpu.SideEffectType.DATAFLOW_SIDE_EFFECTING),
    )(pltpu.with_memory_space_constraint(src, pltpu.HBM), pltpu.with_memory_space_constraint(land, pltpu.HBM),
      *after)
    return (send_sems, recv_sems, src_thru, land_thru, whole_src), token


def chips_wait(name, handle, after=()):
    send_sems, recv_sems, src_thru, land_thru, whole_src = handle
    n_after = len(after)

    def body(src_ref, land_ref, send_sems, recv_sems, *rest):
        for send, recv in _chip_copies(src_ref, land_ref, send_sems, recv_sems, whole_src):
            send.wait_send()
            recv.wait_recv()

    return _pc(
        body, name=name,
        out_shape=(pltpu.HBM(src_thru.shape, src_thru.dtype), pltpu.HBM(land_thru.shape, land_thru.dtype)),
        in_specs=[_HBM, _HBM, _SEM, _SEM] + [HBM] * n_after, out_specs=(_HBM, _HBM),
        input_output_aliases={0: 0, 1: 1},
        compiler_params=pltpu.CompilerParams(has_side_effects=pltpu.SideEffectType.DATAFLOW_SIDE_EFFECTING),
    )(src_thru, land_thru, send_sems, recv_sems, *after)[1]


class Wt:
    def __init__(self, arr, kind, layer, rows, cols, tn=None):
        self.arr, self.kind, self.l, self.R, self.C, self.tn = arr, kind, layer, rows, cols, tn
        self.K = 4 * rows if kind == "row" else rows
        self.N = 4 * cols if kind == "col" else cols

    def spec(self, tk, tn):
        l, R, C = self.l, self.R, self.C
        if self.kind == "plain":
            off = l * (R // tk)
            return (tk, tn), lambda kb, jb: (off + kb, jb)
        if self.kind == "col":
            assert tn == self.tn
            per, off = C // tn, l * (R // tk)
            return (None, tk, tn), lambda kb, jb: (2 * (jb % 2) + (jb // 2) // per, off + kb, (jb // 2) % per)
        per = R // tk
        return (None, tk, tn), lambda kb, jb: (kb // per, l * per + kb % per, jb)

    def tile_k(self, cands):
        return _pick(self.R, cands)

    def tile_n(self, cands):
        return self.tn if self.kind == "col" else _pick(self.C, cands)


_TM = (1024, 512, 256, 128, 64, 32, 16, 8)
_TK = (2048, 1408, 1024, 896, 512, 256, 128, 64, 32, 16)
_TN = (1024, 896, 768, 512, 384, 256, 128)
_TR = (1024, 1408, 512, 256, 128, 64, 32, 16)


def _accumulate(part, acc_ref, k, nk, finish):
    if nk == 1:
        finish(part)
        return

    @pl.when(k == 0)
    def _():
        acc_ref[...] = part

    @pl.when(k > 0)
    def _():
        acc_ref[...] += part

    @pl.when(k == nk - 1)
    def _():
        finish(acc_ref[...])


def mm_nn(name, a, w, *, bias=None, out_dtype=F32):
    M, K = a.shape
    assert K == w.K
    N = w.N
    tm, tk, tn = _pick(M, _TM), w.tile_k(_TK), w.tile_n(_TN)
    nk = K // tk
    wblock, wmap = w.spec(tk, tn)
    in_specs = [pl.BlockSpec((tm, tk), lambda i, j, k: (i, k)),
                pl.BlockSpec(wblock, lambda i, j, k: wmap(k, j))]
    args = [a, w.arr]
    if bias is not None:
        in_specs.append(pl.BlockSpec((1, tn), lambda i, j, k: (0, j)))
        args.append(bias)

    def body(*refs):
        a_ref, b_ref = refs[0], refs[1]
        bias_ref = refs[2] if bias is not None else None
        o_ref = refs[3] if bias is not None else refs[2]
        acc_ref = refs[-1] if nk > 1 else None
        part = jnp.dot(a_ref[...].astype(BF16), b_ref[...].astype(BF16), preferred_element_type=F32)

        def finish(acc):
            if bias_ref is not None:
                acc = acc + bias_ref[...]
            o_ref[...] = acc.astype(o_ref.dtype)

        _accumulate(part, acc_ref, pl.program_id(2), nk, finish)

    return _pc(
        body, name=name, out_shape=jax.ShapeDtypeStruct((M, N), out_dtype),
        grid=(M // tm, N // tn, nk), in_specs=in_specs,
        out_specs=pl.BlockSpec((tm, tn), lambda i, j, k: (i, j)),
        scratch_shapes=[pltpu.VMEM((tm, tn), F32)] if nk > 1 else [],
        compiler_params=_params("parallel", "parallel", "arbitrary"),
    )(*args)


def mm_nt(name, a, w, *, out_dtype=F32):
    M, N = a.shape
    assert N == w.N
    K = w.K
    tm, tj, tn = _pick(M, _TM), w.tile_k(_TR), w.tile_n(_TK)
    nn = N // tn
    wblock, wmap = w.spec(tj, tn)

    def body(a_ref, b_ref, o_ref, *scr):
        part = lax.dot_general(a_ref[...].astype(BF16), b_ref[...].astype(BF16),
                               (((1,), (1,)), ((), ())), preferred_element_type=F32)

        def finish(acc):
            o_ref[...] = acc.astype(o_ref.dtype)

        _accumulate(part, scr[0] if nn > 1 else None, pl.program_id(2), nn, finish)

    return _pc(
        body, name=name, out_shape=jax.ShapeDtypeStruct((M, K), out_dtype),
        grid=(M // tm, K // tj, nn),
        in_specs=[pl.BlockSpec((tm, tn), lambda i, j, n: (i, n)),
                  pl.BlockSpec(wblock, lambda i, j, n: wmap(j, n))],
        out_specs=pl.BlockSpec((tm, tj), lambda i, j, n: (i, j)),
        scratch_shapes=[pltpu.VMEM((tm, tj), F32)] if nn > 1 else [],
        compiler_params=_params("parallel", "parallel", "arbitrary"),
    )(a, w.arr)


def mm_tn(name, a, b, w, *, out_dtype=BF16):
    T, K = a.shape
    N = b.shape[1]
    assert (K, N) == (w.K, w.N) and w.l == 0
    tm = w.tile_k(_TR)
    tn = w.tile_n(_TN)
    tk = _pick(T, (1024, 512, 256, 128, 64, 32, 16))
    nk = T // tk
    oblock, omap = w.spec(tm, tn)
    shape = (w.R, w.C) if w.kind == "plain" else (N_CHIPS, w.R, w.C)

    def body(a_ref, b_ref, o_ref, *scr):
        part = lax.dot_general(a_ref[...].astype(BF16), b_ref[...].astype(BF16),
                               (((0,), (0,)), ((), ())), preferred_element_type=F32)

        def finish(acc):
            o_ref[...] = acc.astype(o_ref.dtype)

        _accumulate(part, scr[0] if nk > 1 else None, pl.program_id(2), nk, finish)

    return _pc(
        body, name=name, out_shape=jax.ShapeDtypeStruct(shape, out_dtype),
        grid=(K // tm, N // tn, nk),
        in_specs=[pl.BlockSpec((tk, tm), lambda i, j, k: (k, i)),
                  pl.BlockSpec((tk, tn), lambda i, j, k: (k, j))],
        out_specs=pl.BlockSpec(oblock, lambda i, j, k: omap(i, j)),
        scratch_shapes=[pltpu.VMEM((tm, tn), F32)] if nk > 1 else [],
        compiler_params=_params("parallel", "parallel", "arbitrary"),
    )(a, b)


def _rows(name, body, rows_in, vecs_in, rows_out, accs_out, tm=256, deps=()):
    T = rows_in[0].shape[0]
    tm = _pick(T, (tm, 128, 64, 32, 16, 8))
    n_r, n_v, n_o, n_d = len(rows_in), len(vecs_in), len(rows_out), len(deps)

    def kern(*refs):
        r_refs, v_refs = refs[:n_r], refs[n_r:n_r + n_v]
        refs = refs[n_r + n_v + n_d:]
        o_refs, a_refs = refs[:n_o], refs[n_o:]

        @pl.when(pl.program_id(0) == 0)
        def _():
            for a_ref in a_refs:
                a_ref[...] = jnp.zeros(a_ref.shape, F32)

        body(r_refs, v_refs, o_refs, a_refs)

    in_specs = [pl.BlockSpec((tm, a.shape[1]), lambda i: (i, 0)) for a in rows_in]
    in_specs += [pl.BlockSpec(v.shape, lambda i: (0, 0)) for v in vecs_in]
    in_specs += [HBM] * n_d
    out_shape = [jax.ShapeDtypeStruct((T, w), dt) for w, dt in rows_out]
    out_shape += [jax.ShapeDtypeStruct((8, w), F32) for w in accs_out]
    out_specs = [pl.BlockSpec((tm, w), lambda i: (i, 0)) for w, _ in rows_out]
    out_specs += [pl.BlockSpec((8, w), lambda i: (0, 0)) for w in accs_out]
    return _pc(kern, name=name, out_shape=out_shape, grid=(T // tm,), in_specs=in_specs,
               out_specs=out_specs, compiler_params=_params("arbitrary"))(*rows_in, *vecs_in, *deps)


def _rms(x):
    return lax.rsqrt(jnp.mean(x * x, axis=-1, keepdims=True) + EPS)


def prenorm_fwd(name, x, gain, shift, deps=()):
    def body(r, v, o, a):
        xv = r[0][...]
        o[0][...] = (xv * _rms(xv) * v[0][...] + v[1][...]).astype(BF16)
    return _rows(name, body, [x], [gain, shift], [(x.shape[1], BF16)], [], deps=deps)[0]


def postnorm_fwd(name, x, y, w):
    def body(r, v, o, a):
        yv = r[1][...]
        o[0][...] = r[0][...] + yv * _rms(yv) * v[0][...]
    return _rows(name, body, [x, y], [w], [(x.shape[1], F32)], [])[0]


def postnorm_bwd(name, dxo, y, post_g, gate):
    def body(r, v, o, a):
        d, yv = r[0][...], r[1][...]
        pg, gt = v[0][...], v[1][...]
        ry = _rms(yv)
        yn = yv * ry
        t = d * yn
        dyn = d * (gt * pg)
        dy = ry * (dyn - yn * jnp.mean(dyn * yn, axis=-1, keepdims=True))
        o[0][...] = dy.astype(BF16)
        a[0][...] += _fold8(t * pg)
        a[1][...] += _fold8(t * gt)
        a[2][...] += _fold8(dy)
    D = y.shape[1]
    return _rows(name, body, [dxo, y], [post_g, gate], [(D, BF16)], [D, D, D])


def prenorm_bwd(name, dh, x, dxo, pre_g, scale, deps=()):
    def body(r, v, o, a):
        dhv, xv, d = r[0][...], r[1][...], r[2][...]
        pg, sc1 = v[0][...], 1.0 + v[1][...]
        rx = _rms(xv)
        xn = xv * rx
        t = dhv * xn
        dxn = dhv * (pg * sc1)
        o[0][...] = d + rx * (dxn - xn * jnp.mean(dxn * xn, axis=-1, keepdims=True))
        a[0][...] += _fold8(dhv)
        a[1][...] += _fold8(t * pg)
        a[2][...] += _fold8(t * sc1)
    D = x.shape[1]
    return _rows(name, body, [dh, x, dxo], [pre_g, scale], [(D, F32)], [D, D, D], deps=deps)


def loss_bwd(name, y, target):
    D = y.shape[1]

    def body(r, v, o, a):
        e = r[0][...] - r[1][...]
        o[0][...] = e * (1.0 / D)
        a[0][...] += _fold8(e * e * (0.5 / D))
    return _rows(name, body, [y, target], [], [(D, F32)], [D])


def ln_silu_fwd(name, v_, g, b):
    def body(r, v, o, a):
        x = r[0][...]
        xc = x - jnp.mean(x, axis=-1, keepdims=True)
        ln = xc * lax.rsqrt(jnp.mean(xc * xc, axis=-1, keepdims=True) + EPS) * v[0][...] + v[1][...]
        o[0][...] = (ln * _sigmoid(ln)).astype(BF16)
    return _rows(name, body, [v_], [g, b], [(v_.shape[1], BF16)], [])[0]


def ln_silu_bwd(name, ds, v_, g, b):
    def body(r, v, o, a):
        dsv, x = r[0][...], r[1][...]
        xc = x - jnp.mean(x, axis=-1, keepdims=True)
        rstd = lax.rsqrt(jnp.mean(xc * xc, axis=-1, keepdims=True) + EPS)
        xh = xc * rstd
        ln = xh * v[0][...] + v[1][...]
        sg = _sigmoid(ln)
        dln = dsv * (sg * (1.0 + ln * (1.0 - sg)))
        dxh = dln * v[0][...]
        o[0][...] = rstd * (dxh - jnp.mean(dxh, axis=-1, keepdims=True)
                            - xh * jnp.mean(dxh * xh, axis=-1, keepdims=True))
        a[0][...] += _fold8(dln * xh)
        a[1][...] += _fold8(dln)
    D = v_.shape[1]
    return _rows(name, body, [ds, v_], [g, b], [(D, F32)], [D, D])


def _pairs(name, body, u, others, out_w, out_dtype, tn, n_acc=0, tm=256):
    T, F2 = u.shape
    F = F2 // 2
    tm = _pick(T, (tm, 128, 64, 32, 16, 8))

    def kern(*refs):
        u_ref, o_refs = refs[0], refs[1:1 + len(others)]
        out_ref, acc_refs = refs[1 + len(others)], refs[2 + len(others):]

        @pl.when(pl.program_id(1) == 0)
        def _():
            for a_ref in acc_refs:
                a_ref[...] = jnp.zeros(a_ref.shape, F32)

        body(u_ref, o_refs, out_ref, acc_refs)

    out_shape = [jax.ShapeDtypeStruct((T, out_w * F), out_dtype)]
    out_shape += [jax.ShapeDtypeStruct((8, F2), F32)] * n_acc
    out_specs = [pl.BlockSpec((tm, out_w * tn), lambda g, i: (i, g))]
    out_specs += [pl.BlockSpec((8, 2 * tn), lambda g, i: (0, g))] * n_acc
    return _pc(kern, name=name, out_shape=out_shape, grid=(F // tn, T // tm),
               in_specs=[pl.BlockSpec((tm, 2 * tn), lambda g, i: (i, g))]
               + [pl.BlockSpec((tm, tn), lambda g, i: (i, g))] * len(others),
               out_specs=out_specs, compiler_params=_params("parallel", "arbitrary"))(u, *others)


def swiglu_fwd(name, u, tn):
    def body(u_ref, o, out, acc):
        gate, up = u_ref[:, :tn].astype(F32), u_ref[:, tn:].astype(F32)
        out[...] = (gate * _sigmoid(gate) * up).astype(BF16)
    return _pairs(name, body, u, [], 1, BF16, tn)[0]


def swiglu_bwd(name, u, da, tn):
    def body(u_ref, o, out, acc):
        gate, up = u_ref[:, :tn].astype(F32), u_ref[:, tn:].astype(F32)
        d = o[0][...].astype(F32)
        sg = _sigmoid(gate)
        out[:, :tn] = (d * up * (sg * (1.0 + gate * (1.0 - sg)))).astype(BF16)
        out[:, tn:] = (d * gate * sg).astype(BF16)
    return _pairs(name, body, u, [da], 2, BF16, tn)[0]


def glu_fwd(name, u, tn):
    def body(u_ref, o, out, acc):
        out[...] = u_ref[:, :tn].astype(F32) * _sigmoid(u_ref[:, tn:].astype(F32))
    return _pairs(name, body, u, [], 1, F32, tn)[0]


def glu_bwd(name, u, dglu, tn):
    def body(u_ref, o, out, acc):
        a, g = u_ref[:, :tn].astype(F32), u_ref[:, tn:].astype(F32)
        d = o[0][...]
        sg = _sigmoid(g)
        da, dg = d * sg, d * a * (sg * (1.0 - sg))
        out[:, :tn] = da.astype(BF16)
        out[:, tn:] = dg.astype(BF16)
        acc[0][:, :tn] += _fold8(da)
        acc[0][:, tn:] += _fold8(dg)
    return _pairs(name, body, u, [dglu], 2, BF16, tn, n_acc=1)


_CONV_ROWS, _CONV_LANES = 64, 128


def dwconv_fwd(name, x, w, b):
    T, D = x.shape
    width = w.shape[0]
    tt, cb = _pick(T, (256, 128, 64)), _pick(D, (256, 128))
    off = CONV_HALO - (width - 1)
    rows = min(_CONV_ROWS, tt)

    def body(cur_ref, prev_ref, w_ref, b_ref, o_ref, ext):
        t = pl.program_id(1)
        tail = prev_ref[pl.ds(tt - CONV_HALO, CONV_HALO), :]
        ext[pl.ds(0, CONV_HALO), :] = jnp.where(t > 0, tail, 0.0)
        ext[pl.ds(CONV_HALO, tt), :] = cur_ref[...]
        for l0 in range(0, cb, _CONV_LANES):
            ln = pl.ds(l0, _CONV_LANES)
            for r0 in range(0, tt, rows):
                acc = jnp.broadcast_to(b_ref[:, ln], (rows, _CONV_LANES))
                for k in range(width):
                    acc = acc + ext[pl.ds(r0 + off + k, rows), ln] * w_ref[pl.ds(k, 1), ln]
                o_ref[pl.ds(r0, rows), ln] = acc

    return _pc(
        body, name=name, out_shape=jax.ShapeDtypeStruct((T, D), F32), grid=(D // cb, T // tt),
        in_specs=[pl.BlockSpec((tt, cb), lambda c, t: (t, c)),
                  pl.BlockSpec((tt, cb), lambda c, t: (jnp.maximum(t - 1, 0), c)),
                  pl.BlockSpec((width, cb), lambda c, t: (0, c)),
                  pl.BlockSpec((1, cb), lambda c, t: (0, c))],
        out_specs=pl.BlockSpec((tt, cb), lambda c, t: (t, c)),
        scratch_shapes=[pltpu.VMEM((tt + CONV_HALO, cb), F32)],
        compiler_params=_params("parallel", "arbitrary"),
    )(x, x, w, b)


def dwconv_bwd(name, dv, x, w):
    T, D = x.shape
    width = w.shape[0]
    tt, cb = _pick(T, (256, 128, 64)), _pick(D, (256, 128))
    off = CONV_HALO - (width - 1)
    rows = min(_CONV_ROWS, tt)
    nt = T // tt

    def body(dv_ref, dvn_ref, x_ref, xp_ref, w_ref, dx_ref, dw_ref, db_ref, ext_d, ext_x):
        t = pl.program_id(1)

        @pl.when(t == 0)
        def _():
            dw_ref[...] = jnp.zeros(dw_ref.shape, F32)
            db_ref[...] = jnp.zeros(db_ref.shape, F32)

        ext_d[pl.ds(0, tt), :] = dv_ref[...]
        ext_d[pl.ds(tt, CONV_HALO), :] = jnp.where(t < nt - 1, dvn_ref[pl.ds(0, CONV_HALO), :], 0.0)
        ext_x[pl.ds(0, CONV_HALO), :] = jnp.where(t > 0, xp_ref[pl.ds(tt - CONV_HALO, CONV_HALO), :], 0.0)
        ext_x[pl.ds(CONV_HALO, tt), :] = x_ref[...]
        db_ref[...] += _fold8(dv_ref[...])
        for l0 in range(0, cb, _CONV_LANES):
            ln = pl.ds(l0, _CONV_LANES)
            for r0 in range(0, tt, rows):
                acc = jnp.zeros((rows, _CONV_LANES), F32)
                for k in range(width):
                    acc = acc + ext_d[pl.ds(r0 + (width - 1) - k, rows), ln] * w_ref[pl.ds(k, 1), ln]
                dx_ref[pl.ds(r0, rows), ln] = acc
            for k in range(width):
                s = jnp.zeros((8, _CONV_LANES), F32)
                for r0 in range(0, tt, rows):
                    s = s + _fold8(ext_d[pl.ds(r0, rows), ln] * ext_x[pl.ds(r0 + off + k, rows), ln])
                dw_ref[pl.ds(8 * k, 8), ln] += s

    return _pc(
        body, name=name,
        out_shape=[jax.ShapeDtypeStruct((T, D), F32), jax.ShapeDtypeStruct((8 * width, D), F32),
                   jax.ShapeDtypeStruct((8, D), F32)],
        grid=(D // cb, nt),
        in_specs=[pl.BlockSpec((tt, cb), lambda c, t: (t, c)),
                  pl.BlockSpec((tt, cb), lambda c, t: (jnp.minimum(t + 1, nt - 1), c)),
                  pl.BlockSpec((tt, cb), lambda c, t: (t, c)),
                  pl.BlockSpec((tt, cb), lambda c, t: (jnp.maximum(t - 1, 0), c)),
                  pl.BlockSpec((width, cb), lambda c, t: (0, c))],
        out_specs=[pl.BlockSpec((tt, cb), lambda c, t: (t, c)),
                   pl.BlockSpec((8 * width, cb), lambda c, t: (0, c)),
                   pl.BlockSpec((8, cb), lambda c, t: (0, c))],
        scratch_shapes=[pltpu.VMEM((tt + CONV_HALO, cb), F32), pltpu.VMEM((tt + CONV_HALO, cb), F32)],
        compiler_params=_params("parallel", "arbitrary"),
    )(dv, dv, x, x, w)


def _gla_dims(proj, wgu):
    DK = wgu.shape[1]
    DV = (proj.shape[1] - 128 - 2 * DK) // 2
    return DK, DV, DK // GLA_HEADS, DV // GLA_HEADS


def _gla_decay(a_ref, w_ref, bg_ref):
    C = GLA_CHUNK
    z = jnp.dot(a_ref[...].astype(BF16), w_ref[...].astype(BF16), preferred_element_type=F32) + bg_ref[...]
    g = (jnp.minimum(z, 0.0) - jnp.log(1.0 + jnp.exp(-jnp.abs(z)))) * (1.0 / GLA_TAU)
    row = lax.broadcasted_iota(jnp.int32, (C, C), 0)
    col = lax.broadcasted_iota(jnp.int32, (C, C), 1)
    bc = jnp.dot((row >= col).astype(F32), g, precision=lax.Precision.HIGHEST, preferred_element_type=F32)
    last = lax.broadcasted_iota(jnp.int32, bc.shape, 0) == C - 1
    b_last = jnp.sum(jnp.where(last, bc, 0.0), axis=0, keepdims=True)
    return z, bc, b_last


def _gla_in_specs(proj, wgu, DK, DV, dk, dv, cidx):
    C = GLA_CHUNK
    return [pl.BlockSpec((C, dk), lambda h, c: (cidx(c), h)),
            pl.BlockSpec((C, dk), lambda h, c: (cidx(c), GLA_HEADS + h)),
            pl.BlockSpec((C, dv), lambda h, c: (cidx(c), (2 * DK) // dv + h)),
            pl.BlockSpec((C, 128), lambda h, c: (cidx(c), (2 * DK + 2 * DV) // 128)),
            pl.BlockSpec((wgu.shape[0], dk), lambda h, c: (0, h)),
            pl.BlockSpec((1, dk), lambda h, c: (0, h))]


def _dot_nt(a, b):
    return lax.dot_general(a.astype(BF16), b.astype(BF16), (((1,), (1,)), ((), ())), preferred_element_type=F32)


def _dot_tn(a, b):
    return lax.dot_general(a.astype(BF16), b.astype(BF16), (((0,), (0,)), ((), ())), preferred_element_type=F32)


def _dot(a, b):
    return jnp.dot(a.astype(BF16), b.astype(BF16), preferred_element_type=F32)


def gla_fwd(name, proj, wgu, bg):
    T = proj.shape[0]
    C, H = GLA_CHUNK, GLA_HEADS
    DK, DV, dk, dv = _gla_dims(proj, wgu)
    nc = T // C
    scale = dk ** -0.5

    def body(q_ref, k_ref, v_ref, a_ref, w_ref, bg_ref, o_ref, s_ref, st):
        @pl.when(pl.program_id(1) == 0)
        def _():
            st[...] = jnp.zeros(st.shape, F32)

        _, bc, b_last = _gla_decay(a_ref, w_ref, bg_ref)
        q, k, v = q_ref[...], k_ref[...], v_ref[...]
        qe = q * scale * jnp.exp(bc)
        ke = k * jnp.exp(-bc)
        kd = k * jnp.exp(b_last - bc)
        sp = st[...]
        s_ref[...] = sp.astype(BF16)
        row = lax.broadcasted_iota(jnp.int32, (C, C), 0)
        col = lax.broadcasted_iota(jnp.int32, (C, C), 1)
        att = jnp.where(row >= col, _dot_nt(qe, ke), 0.0)
        o_ref[...] = _dot_nt(qe, sp) + _dot(att, v)
        st[...] = sp * jnp.exp(b_last) + _dot_tn(v, kd)

    return _pc(
        body, name=name,
        out_shape=[jax.ShapeDtypeStruct((T, DV), F32), jax.ShapeDtypeStruct((H, nc, dv, dk), BF16)],
        grid=(H, nc), in_specs=_gla_in_specs(proj, wgu, DK, DV, dk, dv, lambda c: c),
        out_specs=[pl.BlockSpec((C, dv), lambda h, c: (c, h)),
                   pl.BlockSpec((None, None, dv, dk), lambda h, c: (h, c, 0, 0))],
        scratch_shapes=[pltpu.VMEM((dv, dk), F32)],
        compiler_params=_params("parallel", "arbitrary"),
    )(proj, proj, proj, proj, wgu, bg)


def gla_bwd(name, proj, wgu, bg, states, do):
    T = proj.shape[0]
    C, H = GLA_CHUNK, GLA_HEADS
    DK, DV, dk, dv = _gla_dims(proj, wgu)
    nc = T // C
    scale = dk ** -0.5
    rev = lambda c: nc - 1 - c

    def body(q_ref, k_ref, v_ref, a_ref, w_ref, bg_ref, s_ref, do_ref, dq_ref, dk_ref, dv_ref, dz_ref, dbg_ref, dst):
        @pl.when(pl.program_id(1) == 0)
        def _():
            dst[...] = jnp.zeros(dst.shape, F32)
            dbg_ref[...] = jnp.zeros(dbg_ref.shape, F32)

        z, bc, b_last = _gla_decay(a_ref, w_ref, bg_ref)
        q, k, v, d_o = q_ref[...], k_ref[...], v_ref[...], do_ref[...]
        eb, enb, ed, el = jnp.exp(bc), jnp.exp(-bc), jnp.exp(b_last - bc), jnp.exp(b_last)
        qe, ke, kd = q * scale * eb, k * enb, k * ed
        sp = s_ref[...].astype(F32)
        ds_ = dst[...]
        row = lax.broadcasted_iota(jnp.int32, (C, C), 0)
        col = lax.broadcasted_iota(jnp.int32, (C, C), 1)
        keep = row >= col
        att = jnp.where(keep, _dot_nt(qe, ke), 0.0)
        datt = jnp.where(keep, _dot_nt(d_o, v), 0.0)
        dqe = _dot(d_o, sp) + _dot(datt, ke)
        dke = _dot_tn(datt, qe)
        dv_ref[...] = (_dot_tn(att, d_o) + _dot_nt(kd, ds_)).astype(BF16)
        dkd = _dot(v, ds_)
        dst[...] = ds_ * el + _dot_tn(d_o, qe)
        dq_ref[...] = (dqe * scale * eb).astype(BF16)
        dk_ref[...] = (dke * enb + dkd * ed).astype(BF16)
        d_el = jnp.sum(sp * ds_, axis=0, keepdims=True)
        db_last = jnp.sum(dkd * kd, axis=0, keepdims=True) + d_el * el
        last = lax.broadcasted_iota(jnp.int32, bc.shape, 0) == C - 1
        db = dqe * qe - dke * ke - dkd * kd + jnp.where(last, db_last, 0.0)
        dg = jnp.dot((row <= col).astype(F32), db, precision=lax.Precision.HIGHEST, preferred_element_type=F32)
        dz = dg * (1.0 / GLA_TAU) * _sigmoid(-z)
        dz_ref[...] = dz.astype(BF16)
        dbg_ref[...] += _fold8(dz)

    return _pc(
        body, name=name,
        out_shape=[jax.ShapeDtypeStruct((T, DK), BF16), jax.ShapeDtypeStruct((T, DK), BF16),
                   jax.ShapeDtypeStruct((T, DV), BF16), jax.ShapeDtypeStruct((T, DK), BF16),
                   jax.ShapeDtypeStruct((8, DK), F32)],
        grid=(H, nc),
        in_specs=_gla_in_specs(proj, wgu, DK, DV, dk, dv, rev)
        + [pl.BlockSpec((None, None, dv, dk), lambda h, c: (h, rev(c), 0, 0)),
           pl.BlockSpec((C, dv), lambda h, c: (rev(c), h))],
        out_specs=[pl.BlockSpec((C, dk), lambda h, c: (rev(c), h)),
                   pl.BlockSpec((C, dk), lambda h, c: (rev(c), h)),
                   pl.BlockSpec((C, dv), lambda h, c: (rev(c), h)),
                   pl.BlockSpec((C, dk), lambda h, c: (rev(c), h)),
                   pl.BlockSpec((8, dk), lambda h, c: (0, h))],
        scratch_shapes=[pltpu.VMEM((dv, dk), F32)],
        compiler_params=_params("parallel", "arbitrary"),
    )(proj, proj, proj, proj, wgu, bg, states, do)


def gla_out_fwd(name, o, proj, norm_g, r_block0):
    T, DV = o.shape
    dv = DV // GLA_HEADS
    tm = _pick(T, (512, 256, 128, 64))

    def body(o_ref, r_ref, g_ref, out_ref):
        ov, rv = o_ref[...], r_ref[...]
        out_ref[...] = (ov * _rms(ov) * g_ref[...] * (rv * _sigmoid(rv))).astype(BF16)

    return _pc(
        body, name=name, out_shape=jax.ShapeDtypeStruct((T, DV), BF16), grid=(GLA_HEADS, T // tm),
        in_specs=[pl.BlockSpec((tm, dv), lambda h, i: (i, h)),
                  pl.BlockSpec((tm, dv), lambda h, i: (i, r_block0 + h)),
                  pl.BlockSpec((1, dv), lambda h, i: (0, h))],
        out_specs=pl.BlockSpec((tm, dv), lambda h, i: (i, h)),
        compiler_params=_params("parallel", "parallel"),
    )(o, proj, norm_g)


def gla_out_bwd(name, dog, o, proj, norm_g, r_block0):
    T, DV = o.shape
    dv = DV // GLA_HEADS
    tm = _pick(T, (512, 256, 128, 64))

    def body(d_ref, o_ref, r_ref, g_ref, do_ref, dr_ref, dg_ref):
        @pl.when(pl.program_id(1) == 0)
        def _():
            dg_ref[...] = jnp.zeros(dg_ref.shape, F32)

        d, ov, rv, g = d_ref[...], o_ref[...], r_ref[...], g_ref[...]
        ro = _rms(ov)
        oh = ov * ro
        sg = _sigmoid(rv)
        dn = d * (rv * sg)
        dr_ref[...] = (d * (oh * g) * (sg * (1.0 + rv * (1.0 - sg)))).astype(BF16)
        doh = dn * g
        do_ref[...] = ro * (doh - oh * jnp.mean(doh * oh, axis=-1, keepdims=True))
        dg_ref[...] += _fold8(dn * oh)

    return _pc(
        body, name=name,
        out_shape=[jax.ShapeDtypeStruct((T, DV), F32), jax.ShapeDtypeStruct((T, DV), BF16),
                   jax.ShapeDtypeStruct((8, DV), F32)],
        grid=(GLA_HEADS, T // tm),
        in_specs=[pl.BlockSpec((tm, dv), lambda h, i: (i, h)),
                  pl.BlockSpec((tm, dv), lambda h, i: (i, h)),
                  pl.BlockSpec((tm, dv), lambda h, i: (i, r_block0 + h)),
                  pl.BlockSpec((1, dv), lambda h, i: (0, h))],
        out_specs=[pl.BlockSpec((tm, dv), lambda h, i: (i, h)),
                   pl.BlockSpec((tm, dv), lambda h, i: (i, h)),
                   pl.BlockSpec((8, dv), lambda h, i: (0, h))],
        compiler_params=_params("parallel", "arbitrary"),
    )(dog, o, proj, norm_g)


def _ew_rows(r, c):
    return _pick(r, tuple(t for t in (512, 256, 128, 64, 32, 16, 8) if t * c <= 256 * 1024) or (8,))


def sum_slots_layers(name, lands):
    nl = len(lands)
    n, r, c = lands[0].shape
    tr = _ew_rows(r, c)
    nb = r // tr

    def body(*refs):
        o_ref = refs[nl]
        for l in range(nl):
            @pl.when(pl.program_id(0) == l)
            def _(p_ref=refs[l]):
                acc = p_ref[0].astype(F32)
                for s in range(1, n):
                    acc = acc + p_ref[s].astype(F32)
                o_ref[...] = acc

    def in_map(l):
        return lambda q, i: (0, jnp.where(q == l, i, 0), 0)

    return _pc(body, name=name, out_shape=jax.ShapeDtypeStruct((nl * r, c), F32), grid=(nl, nb),
               in_specs=[pl.BlockSpec((n, tr, c), in_map(l)) for l in range(nl)],
               out_specs=pl.BlockSpec((tr, c), lambda q, i: (q * nb + i, 0)),
               compiler_params=_params("arbitrary", "arbitrary"))(*lands)


def sum_slots(name, p):
    n, r, c = p.shape
    tr = _ew_rows(r, c)

    def body(p_ref, o_ref):
        acc = p_ref[0].astype(F32)
        for s in range(1, n):
            acc = acc + p_ref[s].astype(F32)
        o_ref[...] = acc

    return _pc(body, name=name, out_shape=jax.ShapeDtypeStruct((r, c), F32), grid=(r // tr,),
               in_specs=[pl.BlockSpec((n, tr, c), lambda i: (0, i, 0))],
               out_specs=pl.BlockSpec((tr, c), lambda i: (i, 0)),
               compiler_params=_params("parallel"))(p)


def adamw(name, w, gs, m, v):
    r, c = w.shape
    tr = _ew_rows(r, c)
    n_g = len(gs)
    m_corr = 1.0 / (1.0 - ADAM_B1 ** ADAM_STEP)
    v_corr = 1.0 / (1.0 - ADAM_B2 ** ADAM_STEP)

    def body(*refs):
        w_ref, g_refs, m_ref, v_ref = refs[0], refs[1:1 + n_g], refs[1 + n_g], refs[2 + n_g]
        g_out, d_out, m_out, v_out = refs[3 + n_g:]
        g = g_refs[0][...]
        if n_g == 2:
            g = g + g_refs[1][...]
        mn = ADAM_B1 * m_ref[...] + (1.0 - ADAM_B1) * g
        vn = ADAM_B2 * v_ref[...] + (1.0 - ADAM_B2) * (g * g)
        g_out[...] = g
        m_out[...] = mn
        v_out[...] = vn
        d_out[...] = -ADAM_LR * ((mn * m_corr) / (jnp.sqrt(vn * v_corr) + ADAM_EPS) + ADAM_WD * w_ref[...])

    spec = pl.BlockSpec((tr, c), lambda i: (i, 0))
    return _pc(body, name=name, out_shape=[jax.ShapeDtypeStruct((r, c), F32)] * 4, grid=(r // tr,),
               in_specs=[spec] * (3 + n_g), out_specs=[spec] * 4,
               compiler_params=_params("parallel"))(w, *gs, m, v)


def _flat_pack(arrs):
    flat = jnp.concatenate([a.reshape(-1).astype(F32) for a in arrs])
    n = flat.shape[0]
    pad = (-n) % 1024
    return jnp.pad(flat, (0, pad)).reshape(-1, 128)


def _flat_unpack(packed, shapes, lead=()):
    flat = packed.reshape(lead + (-1,))
    out, pos = [], 0
    for s in shapes:
        n = 1
        for d in s:
            n *= d
        out.append(flat[..., pos:pos + n].reshape(lead + tuple(s)))
        pos += n
    return out


def _interleave_vec(b, tn):
    f = b.shape[-1] // 2
    return b.reshape(2, f // tn, tn).transpose(1, 0, 2).reshape(1, 2 * f)


def _deinterleave_vec(b, tn):
    f = b.shape[-1] // 2
    return b.reshape(f // tn, 2, tn).transpose(1, 0, 2).reshape(2 * f)


def kernel(x, c, w_ada, b_ada, pre_mix_g, post_mix_g, pre_ffn_g, post_ffn_g, conv_w_pw1, conv_b_pw1, conv_w_dw, conv_b_dw, conv_ln_g, conv_ln_b, conv_w_pw2, conv_b_pw2, gla_w_in, gla_w_gate_up, gla_b_gate, gla_norm_g, gla_w_out, ffn_w_in, ffn_w_out, loss_target, m_w_ada, m_b_ada, m_pre_mix_g, m_post_mix_g, m_pre_ffn_g, m_post_ffn_g, m_conv_w_pw1, m_conv_b_pw1, m_conv_w_dw, m_conv_b_dw, m_conv_ln_g, m_conv_ln_b, m_conv_w_pw2, m_conv_b_pw2, m_gla_w_in, m_gla_w_gate_up, m_gla_b_gate, m_gla_norm_g, m_gla_w_out, m_ffn_w_in, m_ffn_w_out, v_w_ada, v_b_ada, v_pre_mix_g, v_post_mix_g, v_pre_ffn_g, v_post_ffn_g, v_conv_w_pw1, v_conv_b_pw1, v_conv_w_dw, v_conv_b_dw, v_conv_ln_g, v_conv_ln_b, v_conv_w_pw2, v_conv_b_pw2, v_gla_w_in, v_gla_w_gate_up, v_gla_b_gate, v_gla_norm_g, v_gla_w_out, v_ffn_w_in, v_ffn_w_out):
    weights = dict(w_ada=w_ada, b_ada=b_ada, pre_mix_g=pre_mix_g, post_mix_g=post_mix_g, pre_ffn_g=pre_ffn_g, post_ffn_g=post_ffn_g, conv_w_pw1=conv_w_pw1, conv_b_pw1=conv_b_pw1, conv_w_dw=conv_w_dw, conv_b_dw=conv_b_dw, conv_ln_g=conv_ln_g, conv_ln_b=conv_ln_b, conv_w_pw2=conv_w_pw2, conv_b_pw2=conv_b_pw2, gla_w_in=gla_w_in, gla_w_gate_up=gla_w_gate_up, gla_b_gate=gla_b_gate, gla_norm_g=gla_norm_g, gla_w_out=gla_w_out, ffn_w_in=ffn_w_in, ffn_w_out=ffn_w_out)
    mom_m = dict(w_ada=m_w_ada, b_ada=m_b_ada, pre_mix_g=m_pre_mix_g, post_mix_g=m_post_mix_g, pre_ffn_g=m_pre_ffn_g, post_ffn_g=m_post_ffn_g, conv_w_pw1=m_conv_w_pw1, conv_b_pw1=m_conv_b_pw1, conv_w_dw=m_conv_w_dw, conv_b_dw=m_conv_b_dw, conv_ln_g=m_conv_ln_g, conv_ln_b=m_conv_ln_b, conv_w_pw2=m_conv_w_pw2, conv_b_pw2=m_conv_b_pw2, gla_w_in=m_gla_w_in, gla_w_gate_up=m_gla_w_gate_up, gla_b_gate=m_gla_b_gate, gla_norm_g=m_gla_norm_g, gla_w_out=m_gla_w_out, ffn_w_in=m_ffn_w_in, ffn_w_out=m_ffn_w_out)
    mom_v = dict(w_ada=v_w_ada, b_ada=v_b_ada, pre_mix_g=v_pre_mix_g, post_mix_g=v_post_mix_g, pre_ffn_g=v_pre_ffn_g, post_ffn_g=v_post_ffn_g, conv_w_pw1=v_conv_w_pw1, conv_b_pw1=v_conv_b_pw1, conv_w_dw=v_conv_w_dw, conv_b_dw=v_conv_b_dw, conv_ln_g=v_conv_ln_g, conv_ln_b=v_conv_ln_b, conv_w_pw2=v_conv_w_pw2, conv_b_pw2=v_conv_b_pw2, gla_w_in=v_gla_w_in, gla_w_gate_up=v_gla_w_gate_up, gla_b_gate=v_gla_b_gate, gla_norm_g=v_gla_norm_g, gla_w_out=v_gla_w_out, ffn_w_in=v_ffn_w_in, ffn_w_out=v_ffn_w_out)
    order = list(weights)

    ax, ay, ac = lax.axis_index("x"), lax.axis_index("y"), lax.axis_index("c")
    my_chip, my_dev = 2 * ax + ay, 4 * ax + 2 * ay + ac

    x = x[0]
    target = loss_target[0]
    T, D = x.shape
    depth = w_ada.shape[0]
    n_conv, n_gla = conv_w_pw1.shape[0], gla_w_in.shape[0]
    width = conv_w_dw.shape[1]
    F = ffn_w_out.shape[1] * N_CHIPS
    DK = gla_w_gate_up.shape[2] * N_CHIPS
    rank = gla_w_gate_up.shape[1]
    gla_cols = 2 * DK + 2 * D + rank
    P = 2 * DK + 2 * D + 128
    dvh = D // GLA_HEADS
    tn_pw1 = _pick(conv_w_pw1.shape[2], (1024, 512, 256, 128))
    tn_ffn = _pick(ffn_w_in.shape[2], (1408, 1024, 512, 384, 256, 128))

    small_sharded = ["conv_w_dw", "gla_w_gate_up", "gla_b_gate", "gla_norm_g"]
    packed = _flat_pack([weights[n] for n in small_sharded])
    got = allgather_devices("gather_small", packed)[0::2]
    parts = _flat_unpack(got, [weights[n].shape for n in small_sharded], lead=(N_CHIPS,))
    w_dw_full, wgu_full, bgate_full, normg_full = [
        jnp.concatenate([p[s] for s in range(N_CHIPS)], axis=-1) for p in parts]
    w_dw_full = w_dw_full
    wgu_pad = jnp.pad(wgu_full, ((0, 0), (0, 128 - rank), (0, 0)))

    c_act = c * _sigmoid(c)
    c_all = allgather_devices("gather_c", jnp.pad(c_act, ((0, 7), (0, 0))))[:, 0, :]
    c16 = jnp.pad(c_all, ((0, 8), (0, 0))).astype(BF16)
    ada_cols = w_ada.shape[2]
    w_ada2 = w_ada.reshape(depth * D, ada_cols)
    mod_cols = [mm_nn(f"ada_fwd_{i}", c16, Wt(w_ada2, "plain", i, D, ada_cols)) for i in range(depth)]
    mod_cols = jnp.stack(mod_cols).reshape(depth * 16, ada_cols)
    mod_all = allgather_devices("gather_mod", mod_cols)[0::2]
    mod_all = mod_all.reshape(N_CHIPS, depth, 16, ada_cols).transpose(1, 2, 0, 3).reshape(depth, 16, 6 * D)
    mod = lax.dynamic_index_in_dim(mod_all, my_dev, axis=1, keepdims=False) + b_ada
    mod = mod.reshape(depth, 6, 1, D)

    gin_cols = gla_w_in.shape[2]

    def sublayer_weights(g):
        i = g // 2
        if g % 2:
            return [("ffn_w_in", i), ("ffn_w_out", i)]
        return [("conv_w_pw1", i // 2), ("conv_w_pw2", i // 2)] if i % 2 == 0 else [("gla_w_in", i // 2), ("gla_w_out", i // 2)]

    issued = []

    def start_gathers(g, after):
        handles, tokens = {}, []
        for n, l in sublayer_weights(g):
            shard = weights[n][l].astype(BF16)
            handles[n], token = chips_start(f"gather_start_{n}_{l}", shard, shard, my_chip, True,
                                            tuple(after) + tuple(issued[-1:]))
            issued.append(token)
            tokens.append(token)
        return handles, tokens

    def wait_gathers(g, handles, after):
        out = []
        for n, l in sublayer_weights(g):
            got = chips_wait(f"gather_wait_{n}_{l}", handles[n], after)
            r, cc = got.shape[1:]
            if n == "gla_w_in":
                full = got.transpose(1, 0, 2).reshape(D, N_CHIPS * gin_cols)
                out.append(Wt(jnp.pad(full, ((0, 0), (0, P - gla_cols))), "plain", 0, D, P))
            elif n in ("conv_w_pw1", "ffn_w_in"):
                out.append(Wt(got, "col", 0, r, cc, tn_pw1 if n == "conv_w_pw1" else tn_ffn))
            else:
                out.append(Wt(got, "row", 0, r, cc))
        return out

    n_sub = 2 * depth
    in_flight = {0: start_gathers(0, ()), 1: start_gathers(1, ())}

    def enter_sublayer(g, xs):
        if g + 2 < n_sub:
            in_flight[g + 2] = start_gathers(g + 2, (xs,))
            return in_flight[g + 2][1]
        return ()

    w_pw1, w_pw2, w_gin, w_gout = [None] * n_conv, [None] * n_conv, [None] * n_gla, [None] * n_gla
    w_fin, w_fout = [None] * depth, [None] * depth
    row = lambda a: a.reshape(1, -1)
    saved = []
    xs = x
    for i in range(depth):
        j = i // 2
        sh1, sc1, gt1, sh2, sc2, gt2 = [mod[i, q] for q in range(6)]
        s = dict(x_in=xs)
        deps = list(enter_sublayer(2 * i, xs))
        if i == 0:
            deps += in_flight[0][1] + in_flight[1][1]
        h = prenorm_fwd(f"pre_mix_{i}", xs, row(pre_mix_g[i]) * (1.0 + sc1), sh1, deps=deps)
        s["h_mix"] = h
        if i % 2 == 0:
            w_pw1[j], w_pw2[j] = wait_gathers(2 * i, in_flight[2 * i][0], (h,))
        else:
            w_gin[j], w_gout[j] = wait_gathers(2 * i, in_flight[2 * i][0], (h,))
        if i % 2 == 0:
            u = mm_nn(f"pw1_{i}", h, w_pw1[j], bias=_interleave_vec(row(conv_b_pw1[j]), tn_pw1), out_dtype=BF16)
            glu = glu_fwd(f"glu_{i}", u, tn_pw1)
            v = dwconv_fwd(f"dwconv_{i}", glu, w_dw_full[j], row(conv_b_dw[j]))
            sl = ln_silu_fwd(f"ln_silu_{i}", v, row(conv_ln_g[j]), row(conv_ln_b[j]))
            y = mm_nn(f"pw2_{i}", sl, w_pw2[j], bias=row(conv_b_pw2[j]))
            s.update(u=u, glu=glu, v=v, s=sl)
        else:
            proj = mm_nn(f"gla_in_{i}", h, w_gin[j])
            o, states = gla_fwd(f"gla_{i}", proj, wgu_pad[j], row(bgate_full[j]))
            og = gla_out_fwd(f"gla_out_{i}", o, proj, row(normg_full[j]), (2 * DK + D) // dvh)
            y = mm_nn(f"gla_wout_{i}", og, w_gout[j])
            s.update(proj=proj, o=o, states=states, og=og)
        s["y_mix"] = y
        xs = postnorm_fwd(f"post_mix_{i}", xs, y, gt1 * row(post_mix_g[i]))
        s["x_mid"] = xs
        deps = enter_sublayer(2 * i + 1, xs)
        h = prenorm_fwd(f"pre_ffn_{i}", xs, row(pre_ffn_g[i]) * (1.0 + sc2), sh2, deps=deps)
        w_fin[i], w_fout[i] = wait_gathers(2 * i + 1, in_flight[2 * i + 1][0], (h,))
        u = mm_nn(f"ffn_in_{i}", h, w_fin[i], out_dtype=BF16)
        a = swiglu_fwd(f"swiglu_{i}", u, tn_ffn)
        y = mm_nn(f"ffn_out_{i}", a, w_fout[i])
        s.update(h_ffn=h, u_ffn=u, a_ffn=a, y_ffn=y)
        xs = postnorm_fwd(f"post_ffn_{i}", xs, y, gt2 * row(post_ffn_g[i]))
        saved.append(s)

    dx, loss_acc = loss_bwd("loss", xs, target)
    loss = lax.psum(jnp.sum(loss_acc), ("x", "y", "c"))

    fold = lambda a: jnp.sum(a, axis=0)
    small_g = {n: [None] * weights[n].shape[0] for n in order if n not in
               ("w_ada", "conv_w_pw1", "conv_w_pw2", "gla_w_in", "gla_w_out", "ffn_w_in", "ffn_w_out")}
    grad_w = lambda wt: Wt(None, wt.kind, 0, wt.R, wt.C, wt.tn)
    scattering = {}

    def start_scatter(n, l, p):
        own = lax.dynamic_index_in_dim(p, my_chip, axis=0, keepdims=False)
        scattering[n, l], token = chips_start(f"scatter_start_{n}_{l}", p, own, my_chip, False)
        return token

    for i in reversed(range(depth)):
        j = i // 2
        s = saved[i]
        sh1, sc1, gt1, sh2, sc2, gt2 = [mod[i, q] for q in range(6)]
        dy, d_gt2, d_pg, _ = postnorm_bwd(f"post_ffn_bwd_{i}", dx, s["y_ffn"], row(post_ffn_g[i]), gt2)
        small_g["post_ffn_g"][i] = fold(d_pg)
        da = mm_nt(f"ffn_out_dgrad_{i}", dy, w_fout[i], out_dtype=BF16)
        t_out = start_scatter("ffn_w_out", i, mm_tn(f"ffn_out_wgrad_{i}", s["a_ffn"], dy, grad_w(w_fout[i])))
        du = swiglu_bwd(f"swiglu_bwd_{i}", s["u_ffn"], da, tn_ffn)
        t_in = start_scatter("ffn_w_in", i, mm_tn(f"ffn_in_wgrad_{i}", s["h_ffn"], du, grad_w(w_fin[i])))
        dh = mm_nt(f"ffn_in_dgrad_{i}", du, w_fin[i])
        dx, d_sh2, d_sc2, d_pg = prenorm_bwd(f"pre_ffn_bwd_{i}", dh, s["x_mid"], dx, row(pre_ffn_g[i]), sc2,
                                             deps=(t_out, t_in))
        small_g["pre_ffn_g"][i] = fold(d_pg)
        dy, d_gt1, d_pg, dy_sum = postnorm_bwd(f"post_mix_bwd_{i}", dx, s["y_mix"], row(post_mix_g[i]), gt1)
        small_g["post_mix_g"][i] = fold(d_pg)
        if i % 2 == 0:
            small_g["conv_b_pw2"][j] = fold(dy_sum)
            dsl = mm_nt(f"pw2_dgrad_{i}", dy, w_pw2[j])
            t_out = start_scatter("conv_w_pw2", j, mm_tn(f"pw2_wgrad_{i}", s["s"], dy, grad_w(w_pw2[j])))
            dv, d_lg, d_lb = ln_silu_bwd(f"ln_silu_bwd_{i}", dsl, s["v"], row(conv_ln_g[j]), row(conv_ln_b[j]))
            small_g["conv_ln_g"][j], small_g["conv_ln_b"][j] = fold(d_lg), fold(d_lb)
            dglu, d_wdw, d_bdw = dwconv_bwd(f"dwconv_bwd_{i}", dv, s["glu"], w_dw_full[j])
            small_g["conv_w_dw"][j] = d_wdw.reshape(width, 8, D).sum(axis=1)
            small_g["conv_b_dw"][j] = fold(d_bdw)
            du, du_sum = glu_bwd(f"glu_bwd_{i}", s["u"], dglu, tn_pw1)
            small_g["conv_b_pw1"][j] = _deinterleave_vec(fold(du_sum), tn_pw1)
            t_in = start_scatter("conv_w_pw1", j, mm_tn(f"pw1_wgrad_{i}", s["h_mix"], du, grad_w(w_pw1[j])))
            dh = mm_nt(f"pw1_dgrad_{i}", du, w_pw1[j])
        else:
            dog = mm_nt(f"gla_wout_dgrad_{i}", dy, w_gout[j])
            t_out = start_scatter("gla_w_out", j, mm_tn(f"gla_wout_wgrad_{i}", s["og"], dy, grad_w(w_gout[j])))
            d_o, d_r, d_ng = gla_out_bwd(f"gla_out_bwd_{i}", dog, s["o"], s["proj"], row(normg_full[j]),
                                         (2 * DK + D) // dvh)
            small_g["gla_norm_g"][j] = fold(d_ng)
            dq, dk_, dv_, dz, d_bg = gla_bwd(f"gla_bwd_{i}", s["proj"], wgu_pad[j], row(bgate_full[j]),
                                             s["states"], d_o)
            small_g["gla_b_gate"][j] = fold(d_bg)
            wgu_w = Wt(wgu_pad[j].astype(BF16), "plain", 0, 128, DK)
            d_a = mm_nt(f"gla_gate_dgrad_{i}", dz, wgu_w, out_dtype=BF16)
            a_low = s["proj"][:, 2 * DK + 2 * D:].astype(BF16)
            d_wgu = mm_tn(f"gla_gate_wgrad_{i}", a_low, dz, Wt(None, "plain", 0, 128, DK), out_dtype=F32)
            small_g["gla_w_gate_up"][j] = d_wgu[:rank]
            dproj = jnp.concatenate([dq, dk_, dv_, d_r, d_a], axis=1)
            g_in = mm_tn(f"gla_in_wgrad_{i}", s["h_mix"], dproj, Wt(None, "plain", 0, D, P))
            t_in = start_scatter("gla_w_in", j, g_in[:, :gla_cols].reshape(D, N_CHIPS, gin_cols).transpose(1, 0, 2))
            dh = mm_nt(f"gla_in_dgrad_{i}", dproj, w_gin[j])
        dx, d_sh1, d_sc1, d_pg = prenorm_bwd(f"pre_mix_bwd_{i}", dh, s["x_in"], dx, row(pre_mix_g[i]), sc1,
                                             deps=(t_out, t_in))
        small_g["pre_mix_g"][i] = fold(d_pg)
        small_g["b_ada"][i] = jnp.concatenate([fold(t) for t in (d_sh1, d_sc1, d_gt1, d_sh2, d_sc2, d_gt2)])
    grad_x = dx[None]

    small_names = list(small_g)
    local_small = [jnp.stack(small_g[n]) for n in small_names]
    full_shapes = [a.shape for a in local_small]
    small_all = allgather_devices("gather_small_grads", _flat_pack(local_small))
    small_sum = sum_slots("sum_small_grads", small_all)
    small_tot = dict(zip(small_names, _flat_unpack(small_sum, full_shapes)))
    for n in small_sharded:
        cols = weights[n].shape[-1]
        small_tot[n] = lax.dynamic_slice_in_dim(small_tot[n], my_chip * cols, cols, axis=small_tot[n].ndim - 1)

    dmod_all = _flat_unpack(small_all, full_shapes, lead=(N_DEVICES,))[small_names.index("b_ada")]
    dmod_cols = lax.dynamic_slice_in_dim(
        dmod_all.reshape(N_DEVICES, depth, N_CHIPS, ada_cols), my_chip, 1, axis=2)[:, :, 0, :]
    dmod16 = jnp.pad(dmod_cols.reshape(N_DEVICES, depth * ada_cols), ((0, 8), (0, 0))).astype(BF16)
    g_ada = [mm_tn(f"ada_wgrad_{i}", c16, dmod16[:, i * ada_cols:(i + 1) * ada_cols],
                   Wt(None, "plain", 0, D, ada_cols), out_dtype=F32) for i in range(depth)]
    g_ada = jnp.stack(g_ada).reshape(depth * D, ada_cols)

    results = {}

    def flat2(a):
        return a.reshape(-1, a.shape[-1])

    results["w_ada"] = [t.reshape(w_ada.shape) for t in
                        adamw("adamw_w_ada", flat2(w_ada), [g_ada], flat2(m_w_ada), flat2(v_w_ada))]
    for n in ("conv_w_pw1", "conv_w_pw2", "gla_w_in", "gla_w_out", "ffn_w_in", "ffn_w_out"):
        w = weights[n]
        arrived = [chips_wait(f"scatter_wait_{n}_{l}", scattering[n, l], (dx,)) for l in range(w.shape[0])]
        mine = sum_slots_layers("sum_" + n, arrived)
        theirs = swap_cores("swap_" + n, mine)
        out = adamw("adamw_" + n, flat2(w), [mine, theirs], flat2(mom_m[n]), flat2(mom_v[n]))
        results[n] = [t.reshape(w.shape) for t in out]
    w_small = _flat_pack([weights[n] for n in small_names])
    out = adamw("adamw_small", w_small, [_flat_pack([small_tot[n] for n in small_names])],
                _flat_pack([mom_m[n] for n in small_names]), _flat_pack([mom_v[n] for n in small_names]))
    shapes = [weights[n].shape for n in small_names]
    unpacked = [_flat_unpack(t, shapes) for t in out]
    for q, n in enumerate(small_names):
        results[n] = [unpacked[r][q] for r in range(4)]

    outs = [loss, grad_x]
    for r in range(4):
        outs += [results[n][r] for n in order]
    return tuple(outs)
```

```python
import jax
import jax.numpy as jnp
from jax import lax
from jax.experimental import pallas as pl
from jax.experimental.pallas import tpu as pltpu

F32, BF16 = jnp.float32, jnp.bfloat16
MESH = pl.DeviceIdType.MESH
HBM = pl.BlockSpec(memory_space=pl.ANY)

EPS = 1e-6
VMEM_LIMIT_BYTES = 48 * 1024 * 1024
N_CHIPS = 4
N_DEVICES = 8
GLA_HEADS = 4
GLA_CHUNK = 64
GLA_TAU = 16.0
CONV_HALO = 32
ADAM_LR, ADAM_B1, ADAM_B2, ADAM_EPS, ADAM_WD, ADAM_STEP = 0.001, 0.9, 0.999, 1e-08, 0.01, 10


def _pc(body, **kw):
    return pl.pallas_call(body, **kw)


def _params(*sem):
    return pltpu.CompilerParams(dimension_semantics=sem, vmem_limit_bytes=VMEM_LIMIT_BYTES)


def _pick(n, cands):
    for c in cands:
        if c <= n and n % c == 0:
            return c
    return n


def _fold8(z):
    r, w = z.shape
    return z.reshape(r // 8, 8, w).sum(axis=0)


def _sigmoid(x):
    return 1.0 / (1.0 + jnp.exp(-x))


def _exchange(name, xs, out_shape, masks, src_of, dst_of, local=True):
    n_in = len(xs)

    def body(*refs):
        x_refs, o_ref = refs[:n_in], refs[n_in]
        send_sems, recv_sems, local_sems = refs[n_in + 1:]
        x, y, c = lax.axis_index("x"), lax.axis_index("y"), lax.axis_index("c")
        me = (x, y, c)
        peers = [(1 - x if a else x, 1 - y if b else y, 1 - c if d else c) for a, b, d in masks]
        started = []
        if local:
            for q, (s, t) in enumerate(zip(src_of(x_refs, me), dst_of(o_ref, me))):
                cp = pltpu.make_async_copy(s, t, local_sems.at[q])
                cp.start()
                started.append(cp)
        sends = []
        for k, peer in enumerate(peers):
            for q, (s, t) in enumerate(zip(src_of(x_refs, peer), dst_of(o_ref, me))):
                cp = pltpu.make_async_remote_copy(
                    src_ref=s, dst_ref=t, send_sem=send_sems.at[k, q], recv_sem=recv_sems.at[k, q],
                    device_id=peer, device_id_type=MESH)
                cp.start()
                sends.append(cp)
        for k, peer in enumerate(peers):
            for q, (s, t) in enumerate(zip(src_of(x_refs, me), dst_of(o_ref, peer))):
                pltpu.make_async_remote_copy(
                    src_ref=s, dst_ref=t, send_sem=send_sems.at[k, q], recv_sem=recv_sems.at[k, q],
                    device_id=peer, device_id_type=MESH).wait_recv()
        for cp in sends:
            cp.wait_send()
        for cp in started:
            cp.wait()

    n_q = n_in if len(xs) > 1 else 1
    return _pc(
        body, name=name, out_shape=out_shape,
        in_specs=[HBM] * n_in, out_specs=HBM,
        scratch_shapes=[pltpu.SemaphoreType.DMA((len(masks), n_q)),
                        pltpu.SemaphoreType.DMA((len(masks), n_q)),
                        pltpu.SemaphoreType.DMA((n_q,))],
    )(*xs)


_CHIP_MASKS = [(1, 0, 0), (0, 1, 0), (1, 1, 0)]
_ALL_MASKS = [(a, b, d) for a in (0, 1) for b in (0, 1) for d in (0, 1) if (a, b, d) != (0, 0, 0)]


def _chip(p):
    return 2 * p[0] + p[1]


def _dev(p):
    return 4 * p[0] + 2 * p[1] + p[2]


def swap_cores(name, a):
    return _exchange(
        name, [a], jax.ShapeDtypeStruct(a.shape, a.dtype), [(0, 0, 1)],
        lambda xr, peer: [xr[0]], lambda o, src: [o], local=False)


def allgather_devices(name, v):
    r, c = v.shape
    return _exchange(
        name, [v], jax.ShapeDtypeStruct((N_DEVICES, r, c), v.dtype), _ALL_MASKS,
        lambda xr, peer: [xr[0]], lambda o, src: [o.at[_dev(src)]])


_HBM = pl.BlockSpec(memory_space=pltpu.HBM)
_SEM = pl.BlockSpec(memory_space=pltpu.SEMAPHORE)
_N_PEER_CHIPS = len(_CHIP_MASKS)


def _peer_chips():
    x, y, c = lax.axis_index("x"), lax.axis_index("y"), lax.axis_index("c")
    return (x, y), [((1 - x if a else x, 1 - y if b else y), c) for a, b, _ in _CHIP_MASKS]


def _chip_copies(src_ref, land_ref, send_sems, recv_sems, whole_src, incoming):
    me, peers = _peer_chips()
    out = []
    for k, (chip, c) in enumerate(peers):
        out.append(pltpu.make_async_remote_copy(
            src_ref=src_ref if whole_src else src_ref.at[_chip(chip)],
            dst_ref=land_ref.at[_chip(chip) if incoming else _chip(me)],
            send_sem=send_sems.at[k], recv_sem=recv_sems.at[k], device_id=(*chip, c), device_id_type=MESH))
    return out


def chips_start(name, src, own, my_chip, whole_src, after=()):
    r, c = own.shape
    land = lax.dynamic_update_slice(lax.empty((N_CHIPS, r, c), src.dtype), own[None], (my_chip, 0, 0))
    n_after = len(after)

    def body(*refs):
        src_ref, land_ref = refs[0], refs[1]
        send_sems, recv_sems, _, _, token = refs[2 + n_after:]
        for send in _chip_copies(src_ref, land_ref, send_sems, recv_sems, whole_src, False):
            send.start()
        token[...] = jnp.zeros(token.shape, token.dtype)

    send_sems, recv_sems, src_thru, land_thru, token = _pc(
        body, name=name,
        out_shape=(pltpu.SemaphoreType.DMA((_N_PEER_CHIPS,)), pltpu.SemaphoreType.DMA((_N_PEER_CHIPS,)),
                   pltpu.HBM(src.shape, src.dtype), pltpu.HBM(land.shape, land.dtype),
                   jax.ShapeDtypeStruct((8, 128), F32)),
        in_specs=[_HBM, _HBM] + [HBM] * n_after,
        out_specs=(_SEM, _SEM, _HBM, _HBM, pl.BlockSpec(memory_space=pltpu.VMEM)),
        input_output_aliases={0: 2, 1: 3},
        compiler_params=pltpu.CompilerParams(has_side_effects=pltpu.SideEffectType.DATAFLOW_SIDE_EFFECTING),
    )(pltpu.with_memory_space_constraint(src, pltpu.HBM), pltpu.with_memory_space_constraint(land, pltpu.HBM),
      *after)
    return (send_sems, recv_sems, src_thru, land_thru, whole_src), token


def chips_wait(name, handle, after=()):
    send_sems, recv_sems, src_thru, land_thru, whole_src = handle
    n_after = len(after)

    def body(src_ref, land_ref, send_sems, recv_sems, *rest):
        for send in _chip_copies(src_ref, land_ref, send_sems, recv_sems, whole_src, False):
            send.wait_send()
        for recv in _chip_copies(src_ref, land_ref, send_sems, recv_sems, whole_src, True):
            recv.wait_recv()

    return _pc(
        body, name=name,
        out_shape=(pltpu.HBM(src_thru.shape, src_thru.dtype), pltpu.HBM(land_thru.shape, land_thru.dtype)),
        in_specs=[_HBM, _HBM, _SEM, _SEM] + [HBM] * n_after, out_specs=(_HBM, _HBM),
        input_output_aliases={0: 0, 1: 1},
        compiler_params=pltpu.CompilerParams(has_side_effects=pltpu.SideEffectType.DATAFLOW_SIDE_EFFECTING),
    )(src_thru, land_thru, send_sems, recv_sems, *after)[1]


class Wt:
    def __init__(self, arr, kind, layer, rows, cols, tn=None):
        self.arr, self.kind, self.l, self.R, self.C, self.tn = arr, kind, layer, rows, cols, tn
        self.K = 4 * rows if kind == "row" else rows
        self.N = 4 * cols if kind == "col" else cols

    def spec(self, tk, tn):
        l, R, C = self.l, self.R, self.C
        if self.kind == "plain":
            off = l * (R // tk)
            return (tk, tn), lambda kb, jb: (off + kb, jb)
        if self.kind == "col":
            assert tn == self.tn
            per, off = C // tn, l * (R // tk)
            return (None, tk, tn), lambda kb, jb: (2 * (jb % 2) + (jb // 2) // per, off + kb, (jb // 2) % per)
        per = R // tk
        return (None, tk, tn), lambda kb, jb: (kb // per, l * per + kb % per, jb)

    def tile_k(self, cands):
        return _pick(self.R, cands)

    def tile_n(self, cands):
        return self.tn if self.kind == "col" else _pick(self.C, cands)


_TM = (1024, 512, 256, 128, 64, 32, 16, 8)
_TK = (2048, 1408, 1024, 896, 512, 256, 128, 64, 32, 16)
_TN = (1024, 896, 768, 512, 384, 256, 128)
_TR = (1024, 1408, 512, 256, 128, 64, 32, 16)


def _accumulate(part, acc_ref, k, nk, finish):
    if nk == 1:
        finish(part)
        return

    @pl.when(k == 0)
    def _():
        acc_ref[...] = part

    @pl.when(k > 0)
    def _():
        acc_ref[...] += part

    @pl.when(k == nk - 1)
    def _():
        finish(acc_ref[...])


def mm_nn(name, a, w, *, bias=None, out_dtype=F32):
    M, K = a.shape
    assert K == w.K
    N = w.N
    tm, tk, tn = _pick(M, _TM), w.tile_k(_TK), w.tile_n(_TN)
    nk = K // tk
    wblock, wmap = w.spec(tk, tn)
    in_specs = [pl.BlockSpec((tm, tk), lambda i, j, k: (i, k)),
                pl.BlockSpec(wblock, lambda i, j, k: wmap(k, j))]
    args = [a, w.arr]
    if bias is not None:
        in_specs.append(pl.BlockSpec((1, tn), lambda i, j, k: (0, j)))
        args.append(bias)

    def body(*refs):
        a_ref, b_ref = refs[0], refs[1]
        bias_ref = refs[2] if bias is not None else None
        o_ref = refs[3] if bias is not None else refs[2]
        acc_ref = refs[-1] if nk > 1 else None
        part = jnp.dot(a_ref[...].astype(BF16), b_ref[...].astype(BF16), preferred_element_type=F32)

        def finish(acc):
            if bias_ref is not None:
                acc = acc + bias_ref[...]
            o_ref[...] = acc.astype(o_ref.dtype)

        _accumulate(part, acc_ref, pl.program_id(2), nk, finish)

    return _pc(
        body, name=name, out_shape=jax.ShapeDtypeStruct((M, N), out_dtype),
        grid=(M // tm, N // tn, nk), in_specs=in_specs,
        out_specs=pl.BlockSpec((tm, tn), lambda i, j, k: (i, j)),
        scratch_shapes=[pltpu.VMEM((tm, tn), F32)] if nk > 1 else [],
        compiler_params=_params("parallel", "parallel", "arbitrary"),
    )(*args)


def mm_nt(name, a, w, *, out_dtype=F32):
    M, N = a.shape
    assert N == w.N
    K = w.K
    tm, tj, tn = _pick(M, _TM), w.tile_k(_TR), w.tile_n(_TK)
    nn = N // tn
    wblock, wmap = w.spec(tj, tn)

    def body(a_ref, b_ref, o_ref, *scr):
        part = lax.dot_general(a_ref[...].astype(BF16), b_ref[...].astype(BF16),
                               (((1,), (1,)), ((), ())), preferred_element_type=F32)

        def finish(acc):
            o_ref[...] = acc.astype(o_ref.dtype)

        _accumulate(part, scr[0] if nn > 1 else None, pl.program_id(2), nn, finish)

    return _pc(
        body, name=name, out_shape=jax.ShapeDtypeStruct((M, K), out_dtype),
        grid=(M // tm, K // tj, nn),
        in_specs=[pl.BlockSpec((tm, tn), lambda i, j, n: (i, n)),
                  pl.BlockSpec(wblock, lambda i, j, n: wmap(j, n))],
        out_specs=pl.BlockSpec((tm, tj), lambda i, j, n: (i, j)),
        scratch_shapes=[pltpu.VMEM((tm, tj), F32)] if nn > 1 else [],
        compiler_params=_params("parallel", "parallel", "arbitrary"),
    )(a, w.arr)


def mm_tn(name, a, b, w, *, out_dtype=BF16):
    T, K = a.shape
    N = b.shape[1]
    assert (K, N) == (w.K, w.N) and w.l == 0
    tm = w.tile_k(_TR)
    tn = w.tile_n(_TN)
    tk = _pick(T, (1024, 512, 256, 128, 64, 32, 16))
    nk = T // tk
    oblock, omap = w.spec(tm, tn)
    shape = (w.R, w.C) if w.kind == "plain" else (N_CHIPS, w.R, w.C)

    def body(a_ref, b_ref, o_ref, *scr):
        part = lax.dot_general(a_ref[...].astype(BF16), b_ref[...].astype(BF16),
                               (((0,), (0,)), ((), ())), preferred_element_type=F32)

        def finish(acc):
            o_ref[...] = acc.astype(o_ref.dtype)

        _accumulate(part, scr[0] if nk > 1 else None, pl.program_id(2), nk, finish)

    return _pc(
        body, name=name, out_shape=jax.ShapeDtypeStruct(shape, out_dtype),
        grid=(K // tm, N // tn, nk),
        in_specs=[pl.BlockSpec((tk, tm), lambda i, j, k: (k, i)),
                  pl.BlockSpec((tk, tn), lambda i, j, k: (k, j))],
        out_specs=pl.BlockSpec(oblock, lambda i, j, k: omap(i, j)),
        scratch_shapes=[pltpu.VMEM((tm, tn), F32)] if nk > 1 else [],
        compiler_params=_params("parallel", "parallel", "arbitrary"),
    )(a, b)


_TM_FUSED = (512, 256, 128, 64, 32, 16, 8)


def mm_nn_post(name, a, w, x, wvec, *, bias=None):
    M, K = a.shape
    assert K == w.K
    N = w.N
    tm, tk = _pick(M, _TM_FUSED), w.tile_k(_TK)
    rows = _pick(tm, (64, 32, 16, 8))
    nk = K // tk
    wblock, wmap = w.spec(tk, N)
    in_specs = [pl.BlockSpec((tm, tk), lambda i, k: (i, k)),
                pl.BlockSpec(wblock, lambda i, k: wmap(k, 0)),
                pl.BlockSpec((tm, N), lambda i, k: (i, 0)),
                pl.BlockSpec((1, N), lambda i, k: (0, 0))]
    args = [a, w.arr, x, wvec]
    if bias is not None:
        in_specs.append(pl.BlockSpec((1, N), lambda i, k: (0, 0)))
        args.append(bias)

    def body(*refs):
        a_ref, b_ref, x_ref, wv_ref = refs[:4]
        bias_ref = refs[4] if bias is not None else None
        y_ref, xn_ref = refs[-3:-1] if nk > 1 else refs[-2:]
        part = jnp.dot(a_ref[...].astype(BF16), b_ref[...].astype(BF16), preferred_element_type=F32)

        def finish(acc):
            if bias_ref is not None:
                acc = acc + bias_ref[...]
            y_ref[...] = acc
            for r0 in range(0, tm, rows):
                rr = pl.ds(r0, rows)
                yv = y_ref[rr, :]
                xn_ref[rr, :] = x_ref[rr, :] + yv * _rms(yv) * wv_ref[...]

        _accumulate(part, refs[-1] if nk > 1 else None, pl.program_id(1), nk, finish)

    spec = pl.BlockSpec((tm, N), lambda i, k: (i, 0))
    return _pc(
        body, name=name, out_shape=[jax.ShapeDtypeStruct((M, N), F32)] * 2,
        grid=(M // tm, nk), in_specs=in_specs, out_specs=[spec, spec],
        scratch_shapes=[pltpu.VMEM((tm, N), F32)] if nk > 1 else [],
        compiler_params=_params("parallel", "arbitrary"),
    )(*args)


def ffn_in_fwd(name, h, w):
    M, K = h.shape
    assert w.kind == "col" and K == w.K == w.R
    tn, F = w.tn, w.N // 2
    tm = _pick(M, _TM_FUSED)
    wblock, wmap = w.spec(K, tn)

    def body(h_ref, wg_ref, wu_ref, u_ref, act_ref):
        hv = h_ref[...].astype(BF16)
        gate = jnp.dot(hv, wg_ref[...].astype(BF16), preferred_element_type=F32)
        up = jnp.dot(hv, wu_ref[...].astype(BF16), preferred_element_type=F32)
        u_ref[:, :tn] = gate.astype(BF16)
        u_ref[:, tn:] = up.astype(BF16)
        act_ref[...] = (gate * _sigmoid(gate) * up).astype(BF16)

    return _pc(
        body, name=name,
        out_shape=[jax.ShapeDtypeStruct((M, 2 * F), BF16), jax.ShapeDtypeStruct((M, F), BF16)],
        grid=(F // tn, M // tm),
        in_specs=[pl.BlockSpec((tm, K), lambda g, i: (i, 0)),
                  pl.BlockSpec(wblock, lambda g, i: wmap(0, 2 * g)),
                  pl.BlockSpec(wblock, lambda g, i: wmap(0, 2 * g + 1))],
        out_specs=[pl.BlockSpec((tm, 2 * tn), lambda g, i: (i, g)),
                   pl.BlockSpec((tm, tn), lambda g, i: (i, g))],
        compiler_params=_params("parallel", "parallel"),
    )(h, w.arr, w.arr)


def ffn_out_dgrad(name, dy, w, u):
    M, N = dy.shape
    assert w.kind == "row" and N == w.N == w.C
    tj, F = w.R, w.K
    tm = _pick(M, _TM_FUSED)
    wblock, wmap = w.spec(tj, N)

    def body(dy_ref, b_ref, u_ref, du_ref):
        d = lax.dot_general(dy_ref[...].astype(BF16), b_ref[...].astype(BF16),
                            (((1,), (1,)), ((), ())), preferred_element_type=F32)
        gate, up = u_ref[:, :tj].astype(F32), u_ref[:, tj:].astype(F32)
        sg = _sigmoid(gate)
        du_ref[:, :tj] = (d * up * (sg * (1.0 + gate * (1.0 - sg)))).astype(BF16)
        du_ref[:, tj:] = (d * gate * sg).astype(BF16)

    return _pc(
        body, name=name, out_shape=jax.ShapeDtypeStruct((M, 2 * F), BF16),
        grid=(F // tj, M // tm),
        in_specs=[pl.BlockSpec((tm, N), lambda j, i: (i, 0)),
                  pl.BlockSpec(wblock, lambda j, i: wmap(j, 0)),
                  pl.BlockSpec((tm, 2 * tj), lambda j, i: (i, j))],
        out_specs=pl.BlockSpec((tm, 2 * tj), lambda j, i: (i, j)),
        compiler_params=_params("parallel", "parallel"),
    )(dy, w.arr, u)


def _rows(name, body, rows_in, vecs_in, rows_out, accs_out, tm=256, deps=()):
    T = rows_in[0].shape[0]
    tm = _pick(T, (tm, 128, 64, 32, 16, 8))
    n_r, n_v, n_o, n_d = len(rows_in), len(vecs_in), len(rows_out), len(deps)

    def kern(*refs):
        r_refs, v_refs = refs[:n_r], refs[n_r:n_r + n_v]
        refs = refs[n_r + n_v + n_d:]
        o_refs, a_refs = refs[:n_o], refs[n_o:]

        @pl.when(pl.program_id(0) == 0)
        def _():
            for a_ref in a_refs:
                a_ref[...] = jnp.zeros(a_ref.shape, F32)

        body(r_refs, v_refs, o_refs, a_refs)

    in_specs = [pl.BlockSpec((tm, a.shape[1]), lambda i: (i, 0)) for a in rows_in]
    in_specs += [pl.BlockSpec(v.shape, lambda i: (0, 0)) for v in vecs_in]
    in_specs += [HBM] * n_d
    out_shape = [jax.ShapeDtypeStruct((T, w), dt) for w, dt in rows_out]
    out_shape += [jax.ShapeDtypeStruct((8, w), F32) for w in accs_out]
    out_specs = [pl.BlockSpec((tm, w), lambda i: (i, 0)) for w, _ in rows_out]
    out_specs += [pl.BlockSpec((8, w), lambda i: (0, 0)) for w in accs_out]
    return _pc(kern, name=name, out_shape=out_shape, grid=(T // tm,), in_specs=in_specs,
               out_specs=out_specs, compiler_params=_params("arbitrary"))(*rows_in, *vecs_in, *deps)


def _rms(x):
    return lax.rsqrt(jnp.mean(x * x, axis=-1, keepdims=True) + EPS)


def prenorm_fwd(name, x, gain, shift, deps=()):
    def body(r, v, o, a):
        xv = r[0][...]
        o[0][...] = (xv * _rms(xv) * v[0][...] + v[1][...]).astype(BF16)
    return _rows(name, body, [x], [gain, shift], [(x.shape[1], BF16)], [], deps=deps)[0]


def postnorm_bwd(name, dxo, y, post_g, gate):
    def body(r, v, o, a):
        d, yv = r[0][...], r[1][...]
        pg, gt = v[0][...], v[1][...]
        ry = _rms(yv)
        yn = yv * ry
        t = d * yn
        dyn = d * (gt * pg)
        dy = ry * (dyn - yn * jnp.mean(dyn * yn, axis=-1, keepdims=True))
        o[0][...] = dy.astype(BF16)
        a[0][...] += _fold8(t * pg)
        a[1][...] += _fold8(t * gt)
        a[2][...] += _fold8(dy)
    D = y.shape[1]
    return _rows(name, body, [dxo, y], [post_g, gate], [(D, BF16)], [D, D, D])


def prenorm_bwd(name, dh, x, dxo, pre_g, scale, deps=()):
    def body(r, v, o, a):
        dhv, xv, d = r[0][...], r[1][...], r[2][...]
        pg, sc1 = v[0][...], 1.0 + v[1][...]
        rx = _rms(xv)
        xn = xv * rx
        t = dhv * xn
        dxn = dhv * (pg * sc1)
        o[0][...] = d + rx * (dxn - xn * jnp.mean(dxn * xn, axis=-1, keepdims=True))
        a[0][...] += _fold8(dhv)
        a[1][...] += _fold8(t * pg)
        a[2][...] += _fold8(t * sc1)
    D = x.shape[1]
    return _rows(name, body, [dh, x, dxo], [pre_g, scale], [(D, F32)], [D, D, D], deps=deps)


def loss_bwd(name, y, target):
    D = y.shape[1]

    def body(r, v, o, a):
        e = r[0][...] - r[1][...]
        o[0][...] = e * (1.0 / D)
        a[0][...] += _fold8(e * e * (0.5 / D))
    return _rows(name, body, [y, target], [], [(D, F32)], [D])


def ln_silu_fwd(name, v_, g, b):
    def body(r, v, o, a):
        x = r[0][...]
        xc = x - jnp.mean(x, axis=-1, keepdims=True)
        ln = xc * lax.rsqrt(jnp.mean(xc * xc, axis=-1, keepdims=True) + EPS) * v[0][...] + v[1][...]
        o[0][...] = (ln * _sigmoid(ln)).astype(BF16)
    return _rows(name, body, [v_], [g, b], [(v_.shape[1], BF16)], [])[0]


def ln_silu_bwd(name, ds, v_, g, b):
    def body(r, v, o, a):
        dsv, x = r[0][...], r[1][...]
        xc = x - jnp.mean(x, axis=-1, keepdims=True)
        rstd = lax.rsqrt(jnp.mean(xc * xc, axis=-1, keepdims=True) + EPS)
        xh = xc * rstd
        ln = xh * v[0][...] + v[1][...]
        sg = _sigmoid(ln)
        dln = dsv * (sg * (1.0 + ln * (1.0 - sg)))
        dxh = dln * v[0][...]
        o[0][...] = rstd * (dxh - jnp.mean(dxh, axis=-1, keepdims=True)
                            - xh * jnp.mean(dxh * xh, axis=-1, keepdims=True))
        a[0][...] += _fold8(dln * xh)
        a[1][...] += _fold8(dln)
    D = v_.shape[1]
    return _rows(name, body, [ds, v_], [g, b], [(D, F32)], [D, D])


def _pairs(name, body, u, others, out_w, out_dtype, tn, n_acc=0, tm=256):
    T, F2 = u.shape
    F = F2 // 2
    tm = _pick(T, (tm, 128, 64, 32, 16, 8))

    def kern(*refs):
        u_ref, o_refs = refs[0], refs[1:1 + len(others)]
        out_ref, acc_refs = refs[1 + len(others)], refs[2 + len(others):]

        @pl.when(pl.program_id(1) == 0)
        def _():
            for a_ref in acc_refs:
                a_ref[...] = jnp.zeros(a_ref.shape, F32)

        body(u_ref, o_refs, out_ref, acc_refs)

    out_shape = [jax.ShapeDtypeStruct((T, out_w * F), out_dtype)]
    out_shape += [jax.ShapeDtypeStruct((8, F2), F32)] * n_acc
    out_specs = [pl.BlockSpec((tm, out_w * tn), lambda g, i: (i, g))]
    out_specs += [pl.BlockSpec((8, 2 * tn), lambda g, i: (0, g))] * n_acc
    return _pc(kern, name=name, out_shape=out_shape, grid=(F // tn, T // tm),
               in_specs=[pl.BlockSpec((tm, 2 * tn), lambda g, i: (i, g))]
               + [pl.BlockSpec((tm, tn), lambda g, i: (i, g))] * len(others),
               out_specs=out_specs, compiler_params=_params("parallel", "arbitrary"))(u, *others)


def glu_fwd(name, u, tn):
    def body(u_ref, o, out, acc):
        out[...] = u_ref[:, :tn].astype(F32) * _sigmoid(u_ref[:, tn:].astype(F32))
    return _pairs(name, body, u, [], 1, F32, tn)[0]


def glu_bwd(name, u, dglu, tn):
    def body(u_ref, o, out, acc):
        a, g = u_ref[:, :tn].astype(F32), u_ref[:, tn:].astype(F32)
        d = o[0][...]
        sg = _sigmoid(g)
        da, dg = d * sg, d * a * (sg * (1.0 - sg))
        out[:, :tn] = da.astype(BF16)
        out[:, tn:] = dg.astype(BF16)
        acc[0][:, :tn] += _fold8(da)
        acc[0][:, tn:] += _fold8(dg)
    return _pairs(name, body, u, [dglu], 2, BF16, tn, n_acc=1)


_CONV_ROWS, _CONV_LANES = 64, 128
_SUBLANES = 8


def _fill_shifts(ext, sh, n):
    for r in range(1, _SUBLANES):
        sh[r - 1, pl.ds(0, n), :] = ext[pl.ds(r, n), :]


def _tap(ext, sh, o, r0, rows, ln):
    q, r = divmod(o, _SUBLANES)
    if r == 0:
        return ext[pl.ds(r0 + o, rows), ln]
    return sh[r - 1, pl.ds(r0 + _SUBLANES * q, rows), ln]


def dwconv_fwd(name, x, w, b):
    T, D = x.shape
    width = w.shape[0]
    tt, cb = _pick(T, (256, 128, 64)), _pick(D, (256, 128))
    off = CONV_HALO - (width - 1)
    rows = min(_CONV_ROWS, tt)
    n_sh = tt + CONV_HALO - _SUBLANES

    def body(cur_ref, prev_ref, w_ref, b_ref, o_ref, ext, sh):
        t = pl.program_id(1)
        tail = prev_ref[pl.ds(tt - CONV_HALO, CONV_HALO), :]
        ext[pl.ds(0, CONV_HALO), :] = jnp.where(t > 0, tail, 0.0)
        ext[pl.ds(CONV_HALO, tt), :] = cur_ref[...]
        _fill_shifts(ext, sh, n_sh)
        for l0 in range(0, cb, _CONV_LANES):
            ln = pl.ds(l0, _CONV_LANES)
            for r0 in range(0, tt, rows):
                acc = jnp.broadcast_to(b_ref[:, ln], (rows, _CONV_LANES))
                for k in range(width):
                    acc = acc + _tap(ext, sh, off + k, r0, rows, ln) * w_ref[pl.ds(k, 1), ln]
                o_ref[pl.ds(r0, rows), ln] = acc

    return _pc(
        body, name=name, out_shape=jax.ShapeDtypeStruct((T, D), F32), grid=(D // cb, T // tt),
        in_specs=[pl.BlockSpec((tt, cb), lambda c, t: (t, c)),
                  pl.BlockSpec((tt, cb), lambda c, t: (jnp.maximum(t - 1, 0), c)),
                  pl.BlockSpec((width, cb), lambda c, t: (0, c)),
                  pl.BlockSpec((1, cb), lambda c, t: (0, c))],
        out_specs=pl.BlockSpec((tt, cb), lambda c, t: (t, c)),
        scratch_shapes=[pltpu.VMEM((tt + CONV_HALO, cb), F32), pltpu.VMEM((_SUBLANES - 1, tt + CONV_HALO, cb), F32)],
        compiler_params=_params("parallel", "arbitrary"),
    )(x, x, w, b)


def dwconv_bwd(name, dv, x, w):
    T, D = x.shape
    width = w.shape[0]
    tt, cb = _pick(T, (256, 128, 64)), _pick(D, (256, 128))
    off = CONV_HALO - (width - 1)
    rows = min(_CONV_ROWS, tt)
    nt = T // tt
    n_sh = tt + CONV_HALO - _SUBLANES

    def body(dv_ref, dvn_ref, x_ref, xp_ref, w_ref, dx_ref, dw_ref, db_ref, ext_d, ext_x, sh_d, sh_x):
        t = pl.program_id(1)

        @pl.when(t == 0)
        def _():
            dw_ref[...] = jnp.zeros(dw_ref.shape, F32)
            db_ref[...] = jnp.zeros(db_ref.shape, F32)

        ext_d[pl.ds(0, tt), :] = dv_ref[...]
        ext_d[pl.ds(tt, CONV_HALO), :] = jnp.where(t < nt - 1, dvn_ref[pl.ds(0, CONV_HALO), :], 0.0)
        ext_x[pl.ds(0, CONV_HALO), :] = jnp.where(t > 0, xp_ref[pl.ds(tt - CONV_HALO, CONV_HALO), :], 0.0)
        ext_x[pl.ds(CONV_HALO, tt), :] = x_ref[...]
        db_ref[...] += _fold8(dv_ref[...])
        _fill_shifts(ext_d, sh_d, n_sh)
        _fill_shifts(ext_x, sh_x, n_sh)
        for l0 in range(0, cb, _CONV_LANES):
            ln = pl.ds(l0, _CONV_LANES)
            for r0 in range(0, tt, rows):
                acc = jnp.zeros((rows, _CONV_LANES), F32)
                for k in range(width):
                    acc = acc + _tap(ext_d, sh_d, (width - 1) - k, r0, rows, ln) * w_ref[pl.ds(k, 1), ln]
                dx_ref[pl.ds(r0, rows), ln] = acc
            for k in range(width):
                s = jnp.zeros((8, _CONV_LANES), F32)
                for r0 in range(0, tt, rows):
                    s = s + _fold8(ext_d[pl.ds(r0, rows), ln] * _tap(ext_x, sh_x, off + k, r0, rows, ln))
                dw_ref[pl.ds(8 * k, 8), ln] += s

    return _pc(
        body, name=name,
        out_shape=[jax.ShapeDtypeStruct((T, D), F32), jax.ShapeDtypeStruct((8 * width, D), F32),
                   jax.ShapeDtypeStruct((8, D), F32)],
        grid=(D // cb, nt),
        in_specs=[pl.BlockSpec((tt, cb), lambda c, t: (t, c)),
                  pl.BlockSpec((tt, cb), lambda c, t: (jnp.minimum(t + 1, nt - 1), c)),
                  pl.BlockSpec((tt, cb), lambda c, t: (t, c)),
                  pl.BlockSpec((tt, cb), lambda c, t: (jnp.maximum(t - 1, 0), c)),
                  pl.BlockSpec((width, cb), lambda c, t: (0, c))],
        out_specs=[pl.BlockSpec((tt, cb), lambda c, t: (t, c)),
                   pl.BlockSpec((8 * width, cb), lambda c, t: (0, c)),
                   pl.BlockSpec((8, cb), lambda c, t: (0, c))],
        scratch_shapes=[pltpu.VMEM((tt + CONV_HALO, cb), F32), pltpu.VMEM((tt + CONV_HALO, cb), F32),
                        pltpu.VMEM((_SUBLANES - 1, tt + CONV_HALO, cb), F32),
                        pltpu.VMEM((_SUBLANES - 1, tt + CONV_HALO, cb), F32)],
        compiler_params=_params("parallel", "arbitrary"),
    )(dv, dv, x, x, w)


def _gla_dims(proj, wgu):
    DK = wgu.shape[1]
    DV = (proj.shape[1] - 128 - 2 * DK) // 2
    return DK, DV, DK // GLA_HEADS, DV // GLA_HEADS


def _gla_decay(a_ref, w_ref, bg_ref):
    C = GLA_CHUNK
    z = jnp.dot(a_ref[...].astype(BF16), w_ref[...].astype(BF16), preferred_element_type=F32) + bg_ref[...]
    g = (jnp.minimum(z, 0.0) - jnp.log(1.0 + jnp.exp(-jnp.abs(z)))) * (1.0 / GLA_TAU)
    row = lax.broadcasted_iota(jnp.int32, (C, C), 0)
    col = lax.broadcasted_iota(jnp.int32, (C, C), 1)
    bc = jnp.dot((row >= col).astype(F32), g, precision=lax.Precision.HIGHEST, preferred_element_type=F32)
    last = lax.broadcasted_iota(jnp.int32, bc.shape, 0) == C - 1
    b_last = jnp.sum(jnp.where(last, bc, 0.0), axis=0, keepdims=True)
    return z, bc, b_last


def _gla_in_specs(wgu, DK, DV, cidx):
    C = GLA_CHUNK
    return [pl.BlockSpec((C, DK), lambda c: (cidx(c), 0)),
            pl.BlockSpec((C, DK), lambda c: (cidx(c), 1)),
            pl.BlockSpec((C, DV), lambda c: (cidx(c), (2 * DK) // DV)),
            pl.BlockSpec((C, 128), lambda c: (cidx(c), (2 * DK + 2 * DV) // 128)),
            pl.BlockSpec(wgu.shape, lambda c: (0, 0)),
            pl.BlockSpec((1, DK), lambda c: (0, 0))]


def _dot_nt(a, b):
    return lax.dot_general(a.astype(BF16), b.astype(BF16), (((1,), (1,)), ((), ())), preferred_element_type=F32)


def _dot_tn(a, b):
    return lax.dot_general(a.astype(BF16), b.astype(BF16), (((0,), (0,)), ((), ())), preferred_element_type=F32)


def _dot(a, b):
    return jnp.dot(a.astype(BF16), b.astype(BF16), preferred_element_type=F32)


def gla_fwd(name, proj, wgu, bg):
    T = proj.shape[0]
    C, H = GLA_CHUNK, GLA_HEADS
    DK, DV, dk, dv = _gla_dims(proj, wgu)
    nc = T // C
    scale = dk ** -0.5

    def body(q_ref, k_ref, v_ref, a_ref, w_ref, bg_ref, o_ref, s_ref, st):
        @pl.when(pl.program_id(0) == 0)
        def _():
            st[...] = jnp.zeros(st.shape, F32)

        _, bc, b_last = _gla_decay(a_ref, w_ref, bg_ref)
        k = k_ref[...]
        qe = (q_ref[...] * scale * jnp.exp(bc)).astype(BF16)
        ke = (k * jnp.exp(-bc)).astype(BF16)
        kd = (k * jnp.exp(b_last - bc)).astype(BF16)
        el = jnp.exp(b_last)
        row = lax.broadcasted_iota(jnp.int32, (C, C), 0)
        col = lax.broadcasted_iota(jnp.int32, (C, C), 1)
        for h in range(H):
            kk, vv = slice(h * dk, (h + 1) * dk), slice(h * dv, (h + 1) * dv)
            sp = st[h]
            s_ref[h] = sp.astype(BF16)
            v = v_ref[:, vv]
            att = jnp.where(row >= col, _dot_nt(qe[:, kk], ke[:, kk]), 0.0)
            o_ref[:, vv] = _dot_nt(qe[:, kk], sp) + _dot(att, v)
            st[h] = sp * el[:, kk] + _dot_tn(v, kd[:, kk])

    return _pc(
        body, name=name,
        out_shape=[jax.ShapeDtypeStruct((T, DV), F32), jax.ShapeDtypeStruct((H, nc, dv, dk), BF16)],
        grid=(nc,), in_specs=_gla_in_specs(wgu, DK, DV, lambda c: c),
        out_specs=[pl.BlockSpec((C, DV), lambda c: (c, 0)),
                   pl.BlockSpec((H, None, dv, dk), lambda c: (0, c, 0, 0))],
        scratch_shapes=[pltpu.VMEM((H, dv, dk), F32)],
        compiler_params=_params("arbitrary"),
    )(proj, proj, proj, proj, wgu, bg)


def gla_bwd(name, proj, wgu, bg, states, do):
    T = proj.shape[0]
    C, H = GLA_CHUNK, GLA_HEADS
    DK, DV, dk, dv = _gla_dims(proj, wgu)
    nc = T // C
    scale = dk ** -0.5
    rev = lambda c: nc - 1 - c

    def body(q_ref, k_ref, v_ref, a_ref, w_ref, bg_ref, s_ref, do_ref,
             dq_ref, dk_ref, dv_ref, dz_ref, dbg_ref, dst, db_scr):
        @pl.when(pl.program_id(0) == 0)
        def _():
            dst[...] = jnp.zeros(dst.shape, F32)
            dbg_ref[...] = jnp.zeros(dbg_ref.shape, F32)

        z, bc, b_last = _gla_decay(a_ref, w_ref, bg_ref)
        k = k_ref[...]
        eb, enb, ed, el = jnp.exp(bc), jnp.exp(-bc), jnp.exp(b_last - bc), jnp.exp(b_last)
        qe, ke, kd = q_ref[...] * scale * eb, k * enb, k * ed
        row = lax.broadcasted_iota(jnp.int32, (C, C), 0)
        col = lax.broadcasted_iota(jnp.int32, (C, C), 1)
        keep = row >= col
        last = lax.broadcasted_iota(jnp.int32, (C, dk), 0) == C - 1
        for h in range(H):
            kk, vv = slice(h * dk, (h + 1) * dk), slice(h * dv, (h + 1) * dv)
            qe_h, ke_h, kd_h, el_h = qe[:, kk], ke[:, kk], kd[:, kk], el[:, kk]
            v, d_o = v_ref[:, vv], do_ref[:, vv]
            sp = s_ref[h].astype(F32)
            ds_ = dst[h]
            att = jnp.where(keep, _dot_nt(qe_h, ke_h), 0.0)
            datt = jnp.where(keep, _dot_nt(d_o, v), 0.0)
            dqe = _dot(d_o, sp) + _dot(datt, ke_h)
            dke = _dot_tn(datt, qe_h)
            dv_ref[:, vv] = (_dot_tn(att, d_o) + _dot_nt(kd_h, ds_)).astype(BF16)
            dkd = _dot(v, ds_)
            dst[h] = ds_ * el_h + _dot_tn(d_o, qe_h)
            dq_ref[:, kk] = (dqe * scale * eb[:, kk]).astype(BF16)
            dk_ref[:, kk] = (dke * enb[:, kk] + dkd * ed[:, kk]).astype(BF16)
            d_el = jnp.sum(sp * ds_, axis=0, keepdims=True)
            db_last = jnp.sum(dkd * kd_h, axis=0, keepdims=True) + d_el * el_h
            db_scr[:, kk] = dqe * qe_h - dke * ke_h - dkd * kd_h + jnp.where(last, db_last, 0.0)
        dg = jnp.dot((row <= col).astype(F32), db_scr[...], precision=lax.Precision.HIGHEST,
                     preferred_element_type=F32)
        dz = dg * (1.0 / GLA_TAU) * _sigmoid(-z)
        dz_ref[...] = dz.astype(BF16)
        dbg_ref[...] += _fold8(dz)

    return _pc(
        body, name=name,
        out_shape=[jax.ShapeDtypeStruct((T, DK), BF16), jax.ShapeDtypeStruct((T, DK), BF16),
                   jax.ShapeDtypeStruct((T, DV), BF16), jax.ShapeDtypeStruct((T, DK), BF16),
                   jax.ShapeDtypeStruct((8, DK), F32)],
        grid=(nc,),
        in_specs=_gla_in_specs(wgu, DK, DV, rev)
        + [pl.BlockSpec((H, None, dv, dk), lambda c: (0, rev(c), 0, 0)),
           pl.BlockSpec((C, DV), lambda c: (rev(c), 0))],
        out_specs=[pl.BlockSpec((C, DK), lambda c: (rev(c), 0)),
                   pl.BlockSpec((C, DK), lambda c: (rev(c), 0)),
                   pl.BlockSpec((C, DV), lambda c: (rev(c), 0)),
                   pl.BlockSpec((C, DK), lambda c: (rev(c), 0)),
                   pl.BlockSpec((8, DK), lambda c: (0, 0))],
        scratch_shapes=[pltpu.VMEM((H, dv, dk), F32), pltpu.VMEM((C, DK), F32)],
        compiler_params=_params("arbitrary"),
    )(proj, proj, proj, proj, wgu, bg, states, do)


def gla_out_fwd(name, o, proj, norm_g, r_block0):
    T, DV = o.shape
    dv = DV // GLA_HEADS
    tm = _pick(T, (512, 256, 128, 64))

    def body(o_ref, r_ref, g_ref, out_ref):
        ov, rv = o_ref[...], r_ref[...]
        out_ref[...] = (ov * _rms(ov) * g_ref[...] * (rv * _sigmoid(rv))).astype(BF16)

    return _pc(
        body, name=name, out_shape=jax.ShapeDtypeStruct((T, DV), BF16), grid=(GLA_HEADS, T // tm),
        in_specs=[pl.BlockSpec((tm, dv), lambda h, i: (i, h)),
                  pl.BlockSpec((tm, dv), lambda h, i: (i, r_block0 + h)),
                  pl.BlockSpec((1, dv), lambda h, i: (0, h))],
        out_specs=pl.BlockSpec((tm, dv), lambda h, i: (i, h)),
        compiler_params=_params("parallel", "parallel"),
    )(o, proj, norm_g)


def gla_out_bwd(name, dog, o, proj, norm_g, r_block0):
    T, DV = o.shape
    dv = DV // GLA_HEADS
    tm = _pick(T, (512, 256, 128, 64))

    def body(d_ref, o_ref, r_ref, g_ref, do_ref, dr_ref, dg_ref):
        @pl.when(pl.program_id(1) == 0)
        def _():
            dg_ref[...] = jnp.zeros(dg_ref.shape, F32)

        d, ov, rv, g = d_ref[...], o_ref[...], r_ref[...], g_ref[...]
        ro = _rms(ov)
        oh = ov * ro
        sg = _sigmoid(rv)
        dn = d * (rv * sg)
        dr_ref[...] = (d * (oh * g) * (sg * (1.0 + rv * (1.0 - sg)))).astype(BF16)
        doh = dn * g
        do_ref[...] = ro * (doh - oh * jnp.mean(doh * oh, axis=-1, keepdims=True))
        dg_ref[...] += _fold8(dn * oh)

    return _pc(
        body, name=name,
        out_shape=[jax.ShapeDtypeStruct((T, DV), F32), jax.ShapeDtypeStruct((T, DV), BF16),
                   jax.ShapeDtypeStruct((8, DV), F32)],
        grid=(GLA_HEADS, T // tm),
        in_specs=[pl.BlockSpec((tm, dv), lambda h, i: (i, h)),
                  pl.BlockSpec((tm, dv), lambda h, i: (i, h)),
                  pl.BlockSpec((tm, dv), lambda h, i: (i, r_block0 + h)),
                  pl.BlockSpec((1, dv), lambda h, i: (0, h))],
        out_specs=[pl.BlockSpec((tm, dv), lambda h, i: (i, h)),
                   pl.BlockSpec((tm, dv), lambda h, i: (i, h)),
                   pl.BlockSpec((8, dv), lambda h, i: (0, h))],
        compiler_params=_params("parallel", "arbitrary"),
    )(dog, o, proj, norm_g)


def _ew_rows(r, c):
    return _pick(r, tuple(t for t in (512, 256, 128, 64, 32, 16, 8) if t * c <= 256 * 1024) or (8,))


def sum_slots_layers(name, lands):
    nl = len(lands)
    n, r, c = lands[0].shape
    tr = _ew_rows(r, c)
    nb = r // tr

    def body(*refs):
        o_ref = refs[nl]
        for l in range(nl):
            @pl.when(pl.program_id(0) == l)
            def _(p_ref=refs[l]):
                acc = p_ref[0].astype(F32)
                for s in range(1, n):
                    acc = acc + p_ref[s].astype(F32)
                o_ref[...] = acc

    def in_map(l):
        return lambda q, i: (0, jnp.where(q == l, i, 0), 0)

    return _pc(body, name=name, out_shape=jax.ShapeDtypeStruct((nl * r, c), F32), grid=(nl, nb),
               in_specs=[pl.BlockSpec((n, tr, c), in_map(l)) for l in range(nl)],
               out_specs=pl.BlockSpec((tr, c), lambda q, i: (q * nb + i, 0)),
               compiler_params=_params("arbitrary", "arbitrary"))(*lands)


def sum_slots(name, p):
    n, r, c = p.shape
    tr = _ew_rows(r, c)

    def body(p_ref, o_ref):
        acc = p_ref[0].astype(F32)
        for s in range(1, n):
            acc = acc + p_ref[s].astype(F32)
        o_ref[...] = acc

    return _pc(body, name=name, out_shape=jax.ShapeDtypeStruct((r, c), F32), grid=(r // tr,),
               in_specs=[pl.BlockSpec((n, tr, c), lambda i: (0, i, 0))],
               out_specs=pl.BlockSpec((tr, c), lambda i: (i, 0)),
               compiler_params=_params("parallel"))(p)


def adamw(name, w, gs, m, v):
    r, c = w.shape
    tr = _ew_rows(r, c)
    n_g = len(gs)
    m_corr = 1.0 / (1.0 - ADAM_B1 ** ADAM_STEP)
    v_corr = 1.0 / (1.0 - ADAM_B2 ** ADAM_STEP)

    def body(*refs):
        w_ref, g_refs, m_ref, v_ref = refs[0], refs[1:1 + n_g], refs[1 + n_g], refs[2 + n_g]
        g_out, d_out, m_out, v_out = refs[3 + n_g:]
        g = g_refs[0][...]
        if n_g == 2:
            g = g + g_refs[1][...]
        mn = ADAM_B1 * m_ref[...] + (1.0 - ADAM_B1) * g
        vn = ADAM_B2 * v_ref[...] + (1.0 - ADAM_B2) * (g * g)
        g_out[...] = g
        m_out[...] = mn
        v_out[...] = vn
        d_out[...] = -ADAM_LR * ((mn * m_corr) / (jnp.sqrt(vn * v_corr) + ADAM_EPS) + ADAM_WD * w_ref[...])

    spec = pl.BlockSpec((tr, c), lambda i: (i, 0))
    return _pc(body, name=name, out_shape=[jax.ShapeDtypeStruct((r, c), F32)] * 4, grid=(r // tr,),
               in_specs=[spec] * (3 + n_g), out_specs=[spec] * 4,
               compiler_params=_params("parallel"))(w, *gs, m, v)


def _flat_pack(arrs):
    flat = jnp.concatenate([a.reshape(-1).astype(F32) for a in arrs])
    n = flat.shape[0]
    pad = (-n) % 1024
    return jnp.pad(flat, (0, pad)).reshape(-1, 128)


def _flat_unpack(packed, shapes, lead=()):
    flat = packed.reshape(lead + (-1,))
    out, pos = [], 0
    for s in shapes:
        n = 1
        for d in s:
            n *= d
        out.append(flat[..., pos:pos + n].reshape(lead + tuple(s)))
        pos += n
    return out


def _interleave_vec(b, tn):
    f = b.shape[-1] // 2
    return b.reshape(2, f // tn, tn).transpose(1, 0, 2).reshape(1, 2 * f)


def _deinterleave_vec(b, tn):
    f = b.shape[-1] // 2
    return b.reshape(f // tn, 2, tn).transpose(1, 0, 2).reshape(2 * f)


def kernel(x, c, w_ada, b_ada, pre_mix_g, post_mix_g, pre_ffn_g, post_ffn_g, conv_w_pw1, conv_b_pw1, conv_w_dw, conv_b_dw, conv_ln_g, conv_ln_b, conv_w_pw2, conv_b_pw2, gla_w_in, gla_w_gate_up, gla_b_gate, gla_norm_g, gla_w_out, ffn_w_in, ffn_w_out, loss_target, m_w_ada, m_b_ada, m_pre_mix_g, m_post_mix_g, m_pre_ffn_g, m_post_ffn_g, m_conv_w_pw1, m_conv_b_pw1, m_conv_w_dw, m_conv_b_dw, m_conv_ln_g, m_conv_ln_b, m_conv_w_pw2, m_conv_b_pw2, m_gla_w_in, m_gla_w_gate_up, m_gla_b_gate, m_gla_norm_g, m_gla_w_out, m_ffn_w_in, m_ffn_w_out, v_w_ada, v_b_ada, v_pre_mix_g, v_post_mix_g, v_pre_ffn_g, v_post_ffn_g, v_conv_w_pw1, v_conv_b_pw1, v_conv_w_dw, v_conv_b_dw, v_conv_ln_g, v_conv_ln_b, v_conv_w_pw2, v_conv_b_pw2, v_gla_w_in, v_gla_w_gate_up, v_gla_b_gate, v_gla_norm_g, v_gla_w_out, v_ffn_w_in, v_ffn_w_out):
    weights = dict(w_ada=w_ada, b_ada=b_ada, pre_mix_g=pre_mix_g, post_mix_g=post_mix_g, pre_ffn_g=pre_ffn_g, post_ffn_g=post_ffn_g, conv_w_pw1=conv_w_pw1, conv_b_pw1=conv_b_pw1, conv_w_dw=conv_w_dw, conv_b_dw=conv_b_dw, conv_ln_g=conv_ln_g, conv_ln_b=conv_ln_b, conv_w_pw2=conv_w_pw2, conv_b_pw2=conv_b_pw2, gla_w_in=gla_w_in, gla_w_gate_up=gla_w_gate_up, gla_b_gate=gla_b_gate, gla_norm_g=gla_norm_g, gla_w_out=gla_w_out, ffn_w_in=ffn_w_in, ffn_w_out=ffn_w_out)
    mom_m = dict(w_ada=m_w_ada, b_ada=m_b_ada, pre_mix_g=m_pre_mix_g, post_mix_g=m_post_mix_g, pre_ffn_g=m_pre_ffn_g, post_ffn_g=m_post_ffn_g, conv_w_pw1=m_conv_w_pw1, conv_b_pw1=m_conv_b_pw1, conv_w_dw=m_conv_w_dw, conv_b_dw=m_conv_b_dw, conv_ln_g=m_conv_ln_g, conv_ln_b=m_conv_ln_b, conv_w_pw2=m_conv_w_pw2, conv_b_pw2=m_conv_b_pw2, gla_w_in=m_gla_w_in, gla_w_gate_up=m_gla_w_gate_up, gla_b_gate=m_gla_b_gate, gla_norm_g=m_gla_norm_g, gla_w_out=m_gla_w_out, ffn_w_in=m_ffn_w_in, ffn_w_out=m_ffn_w_out)
    mom_v = dict(w_ada=v_w_ada, b_ada=v_b_ada, pre_mix_g=v_pre_mix_g, post_mix_g=v_post_mix_g, pre_ffn_g=v_pre_ffn_g, post_ffn_g=v_post_ffn_g, conv_w_pw1=v_conv_w_pw1, conv_b_pw1=v_conv_b_pw1, conv_w_dw=v_conv_w_dw, conv_b_dw=v_conv_b_dw, conv_ln_g=v_conv_ln_g, conv_ln_b=v_conv_ln_b, conv_w_pw2=v_conv_w_pw2, conv_b_pw2=v_conv_b_pw2, gla_w_in=v_gla_w_in, gla_w_gate_up=v_gla_w_gate_up, gla_b_gate=v_gla_b_gate, gla_norm_g=v_gla_norm_g, gla_w_out=v_gla_w_out, ffn_w_in=v_ffn_w_in, ffn_w_out=v_ffn_w_out)
    order = list(weights)

    ax, ay, ac = lax.axis_index("x"), lax.axis_index("y"), lax.axis_index("c")
    my_chip, my_dev = 2 * ax + ay, 4 * ax + 2 * ay + ac

    x = x[0]
    target = loss_target[0]
    T, D = x.shape
    depth = w_ada.shape[0]
    n_conv, n_gla = conv_w_pw1.shape[0], gla_w_in.shape[0]
    width = conv_w_dw.shape[1]
    F = ffn_w_out.shape[1] * N_CHIPS
    DK = gla_w_gate_up.shape[2] * N_CHIPS
    rank = gla_w_gate_up.shape[1]
    gla_cols = 2 * DK + 2 * D + rank
    P = 2 * DK + 2 * D + 128
    dvh = D // GLA_HEADS
    tn_pw1 = _pick(conv_w_pw1.shape[2], (1024, 512, 256, 128))
    tn_ffn = ffn_w_out.shape[1]

    small_sharded = ["conv_w_dw", "gla_w_gate_up", "gla_b_gate", "gla_norm_g"]
    packed = _flat_pack([weights[n] for n in small_sharded])
    got = allgather_devices("gather_small", packed)[0::2]
    parts = _flat_unpack(got, [weights[n].shape for n in small_sharded], lead=(N_CHIPS,))
    w_dw_full, wgu_full, bgate_full, normg_full = [
        jnp.concatenate([p[s] for s in range(N_CHIPS)], axis=-1) for p in parts]
    w_dw_full = w_dw_full
    wgu_pad = jnp.pad(wgu_full, ((0, 0), (0, 128 - rank), (0, 0)))

    c_act = c * _sigmoid(c)
    c_all = allgather_devices("gather_c", jnp.pad(c_act, ((0, 7), (0, 0))))[:, 0, :]
    c16 = jnp.pad(c_all, ((0, 8), (0, 0))).astype(BF16)
    ada_cols = w_ada.shape[2]
    w_ada2 = w_ada.reshape(depth * D, ada_cols)
    mod_cols = [mm_nn(f"ada_fwd_{i}", c16, Wt(w_ada2, "plain", i, D, ada_cols)) for i in range(depth)]
    mod_cols = jnp.stack(mod_cols).reshape(depth * 16, ada_cols)
    mod_all = allgather_devices("gather_mod", mod_cols)[0::2]
    mod_all = mod_all.reshape(N_CHIPS, depth, 16, ada_cols).transpose(1, 2, 0, 3).reshape(depth, 16, 6 * D)
    mod = lax.dynamic_index_in_dim(mod_all, my_dev, axis=1, keepdims=False) + b_ada
    mod = mod.reshape(depth, 6, 1, D)

    gin_cols = gla_w_in.shape[2]

    def sublayer_weights(g):
        i = g // 2
        if g % 2:
            return [("ffn_w_in", i), ("ffn_w_out", i)]
        return [("conv_w_pw1", i // 2), ("conv_w_pw2", i // 2)] if i % 2 == 0 else [("gla_w_in", i // 2), ("gla_w_out", i // 2)]

    issued = []

    def start_gathers(g, after):
        handles, tokens = {}, []
        for n, l in sublayer_weights(g):
            shard = weights[n][l].astype(BF16)
            handles[n], token = chips_start(f"gather_start_{n}_{l}", shard, shard, my_chip, True,
                                            tuple(after) + tuple(issued[-1:]))
            issued.append(token)
            tokens.append(token)
        return handles, tokens

    def wait_gathers(g, handles, after):
        out = []
        for n, l in sublayer_weights(g):
            got = chips_wait(f"gather_wait_{n}_{l}", handles[n], after)
            r, cc = got.shape[1:]
            if n == "gla_w_in":
                full = got.transpose(1, 0, 2).reshape(D, N_CHIPS * gin_cols)
                out.append(Wt(jnp.pad(full, ((0, 0), (0, P - gla_cols))), "plain", 0, D, P))
            elif n in ("conv_w_pw1", "ffn_w_in"):
                out.append(Wt(got, "col", 0, r, cc, tn_pw1 if n == "conv_w_pw1" else tn_ffn))
            else:
                out.append(Wt(got, "row", 0, r, cc))
        return out

    n_sub = 2 * depth
    in_flight = {0: start_gathers(0, ()), 1: start_gathers(1, ())}

    def enter_sublayer(g, xs):
        if g + 2 < n_sub:
            in_flight[g + 2] = start_gathers(g + 2, (xs,))
            return in_flight[g + 2][1]
        return ()

    w_pw1, w_pw2, w_gin, w_gout = [None] * n_conv, [None] * n_conv, [None] * n_gla, [None] * n_gla
    w_fin, w_fout = [None] * depth, [None] * depth
    row = lambda a: a.reshape(1, -1)
    saved = []
    xs = x
    for i in range(depth):
        j = i // 2
        sh1, sc1, gt1, sh2, sc2, gt2 = [mod[i, q] for q in range(6)]
        s = dict(x_in=xs)
        deps = list(enter_sublayer(2 * i, xs))
        if i == 0:
            deps += in_flight[0][1] + in_flight[1][1]
        h = prenorm_fwd(f"pre_mix_{i}", xs, row(pre_mix_g[i]) * (1.0 + sc1), sh1, deps=deps)
        s["h_mix"] = h
        if i % 2 == 0:
            w_pw1[j], w_pw2[j] = wait_gathers(2 * i, in_flight[2 * i][0], (h,))
        else:
            w_gin[j], w_gout[j] = wait_gathers(2 * i, in_flight[2 * i][0], (h,))
        if i % 2 == 0:
            u = mm_nn(f"pw1_{i}", h, w_pw1[j], bias=_interleave_vec(row(conv_b_pw1[j]), tn_pw1), out_dtype=BF16)
            glu = glu_fwd(f"glu_{i}", u, tn_pw1)
            v = dwconv_fwd(f"dwconv_{i}", glu, w_dw_full[j], row(conv_b_dw[j]))
            sl = ln_silu_fwd(f"ln_silu_{i}", v, row(conv_ln_g[j]), row(conv_ln_b[j]))
            y, x_mid = mm_nn_post(f"pw2_{i}", sl, w_pw2[j], xs, gt1 * row(post_mix_g[i]), bias=row(conv_b_pw2[j]))
            s.update(u=u, glu=glu, v=v, s=sl)
        else:
            proj = mm_nn(f"gla_in_{i}", h, w_gin[j])
            o, states = gla_fwd(f"gla_{i}", proj, wgu_pad[j], row(bgate_full[j]))
            og = gla_out_fwd(f"gla_out_{i}", o, proj, row(normg_full[j]), (2 * DK + D) // dvh)
            y, x_mid = mm_nn_post(f"gla_wout_{i}", og, w_gout[j], xs, gt1 * row(post_mix_g[i]))
            s.update(proj=proj, o=o, states=states, og=og)
        s["y_mix"] = y
        xs = x_mid
        s["x_mid"] = xs
        deps = enter_sublayer(2 * i + 1, xs)
        h = prenorm_fwd(f"pre_ffn_{i}", xs, row(pre_ffn_g[i]) * (1.0 + sc2), sh2, deps=deps)
        w_fin[i], w_fout[i] = wait_gathers(2 * i + 1, in_flight[2 * i + 1][0], (h,))
        u, a = ffn_in_fwd(f"ffn_in_{i}", h, w_fin[i])
        y, xs = mm_nn_post(f"ffn_out_{i}", a, w_fout[i], xs, gt2 * row(post_ffn_g[i]))
        s.update(h_ffn=h, u_ffn=u, a_ffn=a, y_ffn=y)
        saved.append(s)

    dx, loss_acc = loss_bwd("loss", xs, target)
    loss = lax.psum(jnp.sum(loss_acc), ("x", "y", "c"))

    fold = lambda a: jnp.sum(a, axis=0)
    small_g = {n: [None] * weights[n].shape[0] for n in order if n not in
               ("w_ada", "conv_w_pw1", "conv_w_pw2", "gla_w_in", "gla_w_out", "ffn_w_in", "ffn_w_out")}
    grad_w = lambda wt: Wt(None, wt.kind, 0, wt.R, wt.C, wt.tn)
    scattering = {}

    def start_scatter(n, l, p):
        own = lax.dynamic_index_in_dim(p, my_chip, axis=0, keepdims=False)
        scattering[n, l], token = chips_start(f"scatter_start_{n}_{l}", p, own, my_chip, False)
        return token

    for i in reversed(range(depth)):
        j = i // 2
        s = saved[i]
        sh1, sc1, gt1, sh2, sc2, gt2 = [mod[i, q] for q in range(6)]
        dy, d_gt2, d_pg, _ = postnorm_bwd(f"post_ffn_bwd_{i}", dx, s["y_ffn"], row(post_ffn_g[i]), gt2)
        small_g["post_ffn_g"][i] = fold(d_pg)
        du = ffn_out_dgrad(f"ffn_out_dgrad_{i}", dy, w_fout[i], s["u_ffn"])
        t_out = start_scatter("ffn_w_out", i, mm_tn(f"ffn_out_wgrad_{i}", s["a_ffn"], dy, grad_w(w_fout[i])))
        t_in = start_scatter("ffn_w_in", i, mm_tn(f"ffn_in_wgrad_{i}", s["h_ffn"], du, grad_w(w_fin[i])))
        dh = mm_nt(f"ffn_in_dgrad_{i}", du, w_fin[i])
        dx, d_sh2, d_sc2, d_pg = prenorm_bwd(f"pre_ffn_bwd_{i}", dh, s["x_mid"], dx, row(pre_ffn_g[i]), sc2,
                                             deps=(t_out, t_in))
        small_g["pre_ffn_g"][i] = fold(d_pg)
        dy, d_gt1, d_pg, dy_sum = postnorm_bwd(f"post_mix_bwd_{i}", dx, s["y_mix"], row(post_mix_g[i]), gt1)
        small_g["post_mix_g"][i] = fold(d_pg)
        if i % 2 == 0:
            small_g["conv_b_pw2"][j] = fold(dy_sum)
            dsl = mm_nt(f"pw2_dgrad_{i}", dy, w_pw2[j])
            t_out = start_scatter("conv_w_pw2", j, mm_tn(f"pw2_wgrad_{i}", s["s"], dy, grad_w(w_pw2[j])))
            dv, d_lg, d_lb = ln_silu_bwd(f"ln_silu_bwd_{i}", dsl, s["v"], row(conv_ln_g[j]), row(conv_ln_b[j]))
            small_g["conv_ln_g"][j], small_g["conv_ln_b"][j] = fold(d_lg), fold(d_lb)
            dglu, d_wdw, d_bdw = dwconv_bwd(f"dwconv_bwd_{i}", dv, s["glu"], w_dw_full[j])
            small_g["conv_w_dw"][j] = d_wdw.reshape(width, 8, D).sum(axis=1)
            small_g["conv_b_dw"][j] = fold(d_bdw)
            du, du_sum = glu_bwd(f"glu_bwd_{i}", s["u"], dglu, tn_pw1)
            small_g["conv_b_pw1"][j] = _deinterleave_vec(fold(du_sum), tn_pw1)
            t_in = start_scatter("conv_w_pw1", j, mm_tn(f"pw1_wgrad_{i}", s["h_mix"], du, grad_w(w_pw1[j])))
            dh = mm_nt(f"pw1_dgrad_{i}", du, w_pw1[j])
        else:
            dog = mm_nt(f"gla_wout_dgrad_{i}", dy, w_gout[j])
            t_out = start_scatter("gla_w_out", j, mm_tn(f"gla_wout_wgrad_{i}", s["og"], dy, grad_w(w_gout[j])))
            d_o, d_r, d_ng = gla_out_bwd(f"gla_out_bwd_{i}", dog, s["o"], s["proj"], row(normg_full[j]),
                                         (2 * DK + D) // dvh)
            small_g["gla_norm_g"][j] = fold(d_ng)
            dq, dk_, dv_, dz, d_bg = gla_bwd(f"gla_bwd_{i}", s["proj"], wgu_pad[j], row(bgate_full[j]),
                                             s["states"], d_o)
            small_g["gla_b_gate"][j] = fold(d_bg)
            wgu_w = Wt(wgu_pad[j].astype(BF16), "plain", 0, 128, DK)
            d_a = mm_nt(f"gla_gate_dgrad_{i}", dz, wgu_w, out_dtype=BF16)
            a_low = s["proj"][:, 2 * DK + 2 * D:].astype(BF16)
            d_wgu = mm_tn(f"gla_gate_wgrad_{i}", a_low, dz, Wt(None, "plain", 0, 128, DK), out_dtype=F32)
            small_g["gla_w_gate_up"][j] = d_wgu[:rank]
            dproj = jnp.concatenate([dq, dk_, dv_, d_r, d_a], axis=1)
            g_in = mm_tn(f"gla_in_wgrad_{i}", s["h_mix"], dproj, Wt(None, "plain", 0, D, P))
            t_in = start_scatter("gla_w_in", j, g_in[:, :gla_cols].reshape(D, N_CHIPS, gin_cols).transpose(1, 0, 2))
            dh = mm_nt(f"gla_in_dgrad_{i}", dproj, w_gin[j])
        dx, d_sh1, d_sc1, d_pg = prenorm_bwd(f"pre_mix_bwd_{i}", dh, s["x_in"], dx, row(pre_mix_g[i]), sc1,
                                             deps=(t_out, t_in))
        small_g["pre_mix_g"][i] = fold(d_pg)
        small_g["b_ada"][i] = jnp.concatenate([fold(t) for t in (d_sh1, d_sc1, d_gt1, d_sh2, d_sc2, d_gt2)])
    grad_x = dx[None]

    small_names = list(small_g)
    local_small = [jnp.stack(small_g[n]) for n in small_names]
    full_shapes = [a.shape for a in local_small]
    small_all = allgather_devices("gather_small_grads", _flat_pack(local_small))
    small_sum = sum_slots("sum_small_grads", small_all)
    small_tot = dict(zip(small_names, _flat_unpack(small_sum, full_shapes)))
    for n in small_sharded:
        cols = weights[n].shape[-1]
        small_tot[n] = lax.dynamic_slice_in_dim(small_tot[n], my_chip * cols, cols, axis=small_tot[n].ndim - 1)

    dmod_all = _flat_unpack(small_all, full_shapes, lead=(N_DEVICES,))[small_names.index("b_ada")]
    dmod_cols = lax.dynamic_slice_in_dim(
        dmod_all.reshape(N_DEVICES, depth, N_CHIPS, ada_cols), my_chip, 1, axis=2)[:, :, 0, :]
    dmod16 = jnp.pad(dmod_cols.reshape(N_DEVICES, depth * ada_cols), ((0, 8), (0, 0))).astype(BF16)
    g_ada = [mm_tn(f"ada_wgrad_{i}", c16, dmod16[:, i * ada_cols:(i + 1) * ada_cols],
                   Wt(None, "plain", 0, D, ada_cols), out_dtype=F32) for i in range(depth)]
    g_ada = jnp.stack(g_ada).reshape(depth * D, ada_cols)

    results = {}

    def flat2(a):
        return a.reshape(-1, a.shape[-1])

    results["w_ada"] = [t.reshape(w_ada.shape) for t in
                        adamw("adamw_w_ada", flat2(w_ada), [g_ada], flat2(m_w_ada), flat2(v_w_ada))]
    for n in ("conv_w_pw1", "conv_w_pw2", "gla_w_in", "gla_w_out", "ffn_w_in", "ffn_w_out"):
        w = weights[n]
        arrived = [chips_wait(f"scatter_wait_{n}_{l}", scattering[n, l], (dx,)) for l in range(w.shape[0])]
        mine = sum_slots_layers("sum_" + n, arrived)
        theirs = swap_cores("swap_" + n, mine)
        out = adamw("adamw_" + n, flat2(w), [mine, theirs], flat2(mom_m[n]), flat2(mom_v[n]))
        results[n] = [t.reshape(w.shape) for t in out]
    w_small = _flat_pack([weights[n] for n in small_names])
    out = adamw("adamw_small", w_small, [_flat_pack([small_tot[n] for n in small_names])],
                _flat_pack([mom_m[n] for n in small_names]), _flat_pack([mom_v[n] for n in small_names]))
    shapes = [weights[n].shape for n in small_names]
    unpacked = [_flat_unpack(t, shapes) for t in out]
    for q, n in enumerate(small_names):
        results[n] = [unpacked[r][q] for r in range(4)]

    outs = [loss, grad_x]
    for r in range(4):
        outs += [results[n][r] for n in order]
    return tuple(outs)
```

```python
import jax
import jax.numpy as jnp
from jax import lax
from jax.experimental import pallas as pl
from jax.experimental.pallas import tpu as pltpu

F32, BF16 = jnp.float32, jnp.bfloat16
MESH = pl.DeviceIdType.MESH
HBM = pl.BlockSpec(memory_space=pl.ANY)

EPS = 1e-6
VMEM_LIMIT_BYTES = 48 * 1024 * 1024
N_CHIPS = 4
N_DEVICES = 8
GLA_HEADS = 4
GLA_CHUNK = 64
GLA_TAU = 16.0
CONV_HALO = 32
ADAM_LR, ADAM_B1, ADAM_B2, ADAM_EPS, ADAM_WD, ADAM_STEP = 0.001, 0.9, 0.999, 1e-08, 0.01, 10


def _pc(body, **kw):
    return pl.pallas_call(body, **kw)


def _params(*sem):
    return pltpu.CompilerParams(dimension_semantics=sem, vmem_limit_bytes=VMEM_LIMIT_BYTES)


def _pick(n, cands):
    for c in cands:
        if c <= n and n % c == 0:
            return c
    return n


def _fold8(z):
    r, w = z.shape
    return z.reshape(r // 8, 8, w).sum(axis=0)


def _sigmoid(x):
    return 1.0 / (1.0 + jnp.exp(-x))


def _exchange(name, xs, out_shape, masks, src_of, dst_of, local=True):
    n_in = len(xs)

    def body(*refs):
        x_refs, o_ref = refs[:n_in], refs[n_in]
        send_sems, recv_sems, local_sems = refs[n_in + 1:]
        x, y, c = lax.axis_index("x"), lax.axis_index("y"), lax.axis_index("c")
        me = (x, y, c)
        peers = [(1 - x if a else x, 1 - y if b else y, 1 - c if d else c) for a, b, d in masks]
        started = []
        if local:
            for q, (s, t) in enumerate(zip(src_of(x_refs, me), dst_of(o_ref, me))):
                cp = pltpu.make_async_copy(s, t, local_sems.at[q])
                cp.start()
                started.append(cp)
        sends = []
        for k, peer in enumerate(peers):
            for q, (s, t) in enumerate(zip(src_of(x_refs, peer), dst_of(o_ref, me))):
                cp = pltpu.make_async_remote_copy(
                    src_ref=s, dst_ref=t, send_sem=send_sems.at[k, q], recv_sem=recv_sems.at[k, q],
                    device_id=peer, device_id_type=MESH)
                cp.start()
                sends.append(cp)
        for k, peer in enumerate(peers):
            for q, (s, t) in enumerate(zip(src_of(x_refs, me), dst_of(o_ref, peer))):
                pltpu.make_async_remote_copy(
                    src_ref=s, dst_ref=t, send_sem=send_sems.at[k, q], recv_sem=recv_sems.at[k, q],
                    device_id=peer, device_id_type=MESH).wait_recv()
        for cp in sends:
            cp.wait_send()
        for cp in started:
            cp.wait()

    n_q = n_in if len(xs) > 1 else 1
    return _pc(
        body, name=name, out_shape=out_shape,
        in_specs=[HBM] * n_in, out_specs=HBM,
        scratch_shapes=[pltpu.SemaphoreType.DMA((len(masks), n_q)),
                        pltpu.SemaphoreType.DMA((len(masks), n_q)),
                        pltpu.SemaphoreType.DMA((n_q,))],
    )(*xs)


_CHIP_MASKS = [(1, 0, 0), (0, 1, 0), (1, 1, 0)]
_ALL_MASKS = [(a, b, d) for a in (0, 1) for b in (0, 1) for d in (0, 1) if (a, b, d) != (0, 0, 0)]


def _chip(p):
    return 2 * p[0] + p[1]


def _dev(p):
    return 4 * p[0] + 2 * p[1] + p[2]


def allgather_devices(name, v):
    r, c = v.shape
    return _exchange(
        name, [v], jax.ShapeDtypeStruct((N_DEVICES, r, c), v.dtype), _ALL_MASKS,
        lambda xr, peer: [xr[0]], lambda o, src: [o.at[_dev(src)]])


_HBM = pl.BlockSpec(memory_space=pltpu.HBM)
_SEM = pl.BlockSpec(memory_space=pltpu.SEMAPHORE)
_N_PEER_CHIPS = len(_CHIP_MASKS)


def _peer_chips():
    x, y, c = lax.axis_index("x"), lax.axis_index("y"), lax.axis_index("c")
    return (x, y), [((1 - x if a else x, 1 - y if b else y), c) for a, b, _ in _CHIP_MASKS]


def _chip_copies(src_ref, land_ref, send_sems, recv_sems, whole_src, incoming):
    me, peers = _peer_chips()
    out = []
    for k, (chip, c) in enumerate(peers):
        out.append(pltpu.make_async_remote_copy(
            src_ref=src_ref if whole_src else src_ref.at[_chip(chip)],
            dst_ref=land_ref.at[_chip(chip) if incoming else _chip(me)],
            send_sem=send_sems.at[k], recv_sem=recv_sems.at[k], device_id=(*chip, c), device_id_type=MESH))
    return out


def _core_copies(src_ref, land_ref, send_sems, recv_sems, incoming):
    x, y, c = lax.axis_index("x"), lax.axis_index("y"), lax.axis_index("c")
    return [pltpu.make_async_remote_copy(
        src_ref=src_ref, dst_ref=land_ref, send_sem=send_sems.at[0], recv_sem=recv_sems.at[0],
        device_id=(x, y, 1 - c), device_id_type=MESH)]


def _split_start(name, src, land, n_copies, copies, after):
    n_after = len(after)

    def body(*refs):
        src_ref, land_ref = refs[0], refs[1]
        send_sems, recv_sems, _, _, token = refs[2 + n_after:]
        for send in copies(src_ref, land_ref, send_sems, recv_sems, False):
            send.start()
        token[...] = jnp.zeros(token.shape, token.dtype)

    send_sems, recv_sems, src_thru, land_thru, token = _pc(
        body, name=name,
        out_shape=(pltpu.SemaphoreType.DMA((n_copies,)), pltpu.SemaphoreType.DMA((n_copies,)),
                   pltpu.HBM(src.shape, src.dtype), pltpu.HBM(land.shape, land.dtype),
                   jax.ShapeDtypeStruct((8, 128), F32)),
        in_specs=[_HBM, _HBM] + [HBM] * n_after,
        out_specs=(_SEM, _SEM, _HBM, _HBM, pl.BlockSpec(memory_space=pltpu.VMEM)),
        input_output_aliases={0: 2, 1: 3},
        compiler_params=pltpu.CompilerParams(has_side_effects=pltpu.SideEffectType.DATAFLOW_SIDE_EFFECTING),
    )(pltpu.with_memory_space_constraint(src, pltpu.HBM), pltpu.with_memory_space_constraint(land, pltpu.HBM),
      *after)
    return (send_sems, recv_sems, src_thru, land_thru, copies), token


def _split_wait(name, handle, after, with_src=False):
    send_sems, recv_sems, src_thru, land_thru, copies = handle
    n_after = len(after)

    def body(src_ref, land_ref, send_sems, recv_sems, *rest):
        for send in copies(src_ref, land_ref, send_sems, recv_sems, False):
            send.wait_send()
        for recv in copies(src_ref, land_ref, send_sems, recv_sems, True):
            recv.wait_recv()

    src, land = _pc(
        body, name=name,
        out_shape=(pltpu.HBM(src_thru.shape, src_thru.dtype), pltpu.HBM(land_thru.shape, land_thru.dtype)),
        in_specs=[_HBM, _HBM, _SEM, _SEM] + [HBM] * n_after, out_specs=(_HBM, _HBM),
        input_output_aliases={0: 0, 1: 1},
        compiler_params=pltpu.CompilerParams(has_side_effects=pltpu.SideEffectType.DATAFLOW_SIDE_EFFECTING),
    )(src_thru, land_thru, send_sems, recv_sems, *after)
    return (src, land) if with_src else land


def chips_start(name, src, own, my_chip, whole_src, after=()):
    r, c = own.shape
    land = lax.dynamic_update_slice(lax.empty((N_CHIPS, r, c), src.dtype), own[None], (my_chip, 0, 0))

    def copies(src_ref, land_ref, send_sems, recv_sems, incoming):
        return _chip_copies(src_ref, land_ref, send_sems, recv_sems, whole_src, incoming)

    return _split_start(name, src, land, _N_PEER_CHIPS, copies, after)


def swap_start(name, a, after=()):
    return _split_start(name, a, lax.empty(a.shape, a.dtype), 1, _core_copies, after)


class Wt:
    def __init__(self, arr, kind, layer, rows, cols, tn=None):
        self.arr, self.kind, self.l, self.R, self.C, self.tn = arr, kind, layer, rows, cols, tn
        self.K = 4 * rows if kind == "row" else rows
        self.N = 4 * cols if kind == "col" else cols

    def spec(self, tk, tn):
        l, R, C = self.l, self.R, self.C
        if self.kind == "plain":
            off = l * (R // tk)
            return (tk, tn), lambda kb, jb: (off + kb, jb)
        if self.kind == "col":
            assert tn == self.tn
            per, off = C // tn, l * (R // tk)
            return (None, tk, tn), lambda kb, jb: (2 * (jb % 2) + (jb // 2) // per, off + kb, (jb // 2) % per)
        per = R // tk
        return (None, tk, tn), lambda kb, jb: (kb // per, l * per + kb % per, jb)

    def tile_k(self, cands):
        return _pick(self.R, cands)

    def tile_n(self, cands):
        return self.tn if self.kind == "col" else _pick(self.C, cands)


_TM = (1024, 512, 256, 128, 64, 32, 16, 8)
_TK = (2048, 1408, 1024, 896, 512, 256, 128, 64, 32, 16)
_TN = (1024, 896, 768, 512, 384, 256, 128)
_TR = (1024, 1408, 512, 256, 128, 64, 32, 16)


def _accumulate(part, acc_ref, k, nk, finish):
    if nk == 1:
        finish(part)
        return

    @pl.when(k == 0)
    def _():
        acc_ref[...] = part

    @pl.when(k > 0)
    def _():
        acc_ref[...] += part

    @pl.when(k == nk - 1)
    def _():
        finish(acc_ref[...])


def mm_nn(name, a, w, *, bias=None, out_dtype=F32):
    M, K = a.shape
    assert K == w.K
    N = w.N
    tm, tk, tn = _pick(M, _TM), w.tile_k(_TK), w.tile_n(_TN)
    nk = K // tk
    wblock, wmap = w.spec(tk, tn)
    in_specs = [pl.BlockSpec((tm, tk), lambda i, j, k: (i, k)),
                pl.BlockSpec(wblock, lambda i, j, k: wmap(k, j))]
    args = [a, w.arr]
    if bias is not None:
        in_specs.append(pl.BlockSpec((1, tn), lambda i, j, k: (0, j)))
        args.append(bias)

    def body(*refs):
        a_ref, b_ref = refs[0], refs[1]
        bias_ref = refs[2] if bias is not None else None
        o_ref = refs[3] if bias is not None else refs[2]
        acc_ref = refs[-1] if nk > 1 else None
        part = jnp.dot(a_ref[...].astype(BF16), b_ref[...].astype(BF16), preferred_element_type=F32)

        def finish(acc):
            if bias_ref is not None:
                acc = acc + bias_ref[...]
            o_ref[...] = acc.astype(o_ref.dtype)

        _accumulate(part, acc_ref, pl.program_id(2), nk, finish)

    return _pc(
        body, name=name, out_shape=jax.ShapeDtypeStruct((M, N), out_dtype),
        grid=(M // tm, N // tn, nk), in_specs=in_specs,
        out_specs=pl.BlockSpec((tm, tn), lambda i, j, k: (i, j)),
        scratch_shapes=[pltpu.VMEM((tm, tn), F32)] if nk > 1 else [],
        compiler_params=_params("parallel", "parallel", "arbitrary"),
    )(*args)


def mm_nt(name, a, w, *, out_dtype=F32):
    M, N = a.shape
    assert N == w.N
    K = w.K
    tm, tj, tn = _pick(M, _TM), w.tile_k(_TR), w.tile_n(_TK)
    wblock, wmap = w.spec(tj, tn)
    nb = 2 if w.kind == "col" else 1
    nn = N // (nb * tn)

    def body(a_ref, *refs):
        b_refs, o_ref, scr = refs[:nb], refs[nb], refs[nb + 1:]
        part = None
        for q, b_ref in enumerate(b_refs):
            p = lax.dot_general(a_ref[:, q * tn:(q + 1) * tn].astype(BF16), b_ref[...].astype(BF16),
                                (((1,), (1,)), ((), ())), preferred_element_type=F32)
            part = p if part is None else part + p

        def finish(acc):
            o_ref[...] = acc.astype(o_ref.dtype)

        _accumulate(part, scr[0] if nn > 1 else None, pl.program_id(2), nn, finish)

    def b_spec(q):
        return pl.BlockSpec(wblock, lambda i, j, n: wmap(j, nb * n + q))

    return _pc(
        body, name=name, out_shape=jax.ShapeDtypeStruct((M, K), out_dtype),
        grid=(M // tm, K // tj, nn),
        in_specs=[pl.BlockSpec((tm, nb * tn), lambda i, j, n: (i, n))] + [b_spec(q) for q in range(nb)],
        out_specs=pl.BlockSpec((tm, tj), lambda i, j, n: (i, j)),
        scratch_shapes=[pltpu.VMEM((tm, tj), F32)] if nn > 1 else [],
        compiler_params=_params("parallel", "parallel", "arbitrary"),
    )(a, *([w.arr] * nb))


def mm_tn(name, a, b, w, *, out_dtype=BF16):
    T, K = a.shape
    N = b.shape[1]
    assert (K, N) == (w.K, w.N) and w.l == 0
    tm = w.tile_k(_TR)
    tn = w.tile_n(_TN)
    tk = _pick(T, (2048, 1024, 512, 256, 128, 64, 32, 16))
    nk = T // tk
    oblock, omap = w.spec(tm, tn)
    shape = (w.R, w.C) if w.kind == "plain" else (N_CHIPS, w.R, w.C)

    def body(a_ref, b_ref, o_ref, *scr):
        part = lax.dot_general(a_ref[...].astype(BF16), b_ref[...].astype(BF16),
                               (((0,), (0,)), ((), ())), preferred_element_type=F32)

        def finish(acc):
            o_ref[...] = acc.astype(o_ref.dtype)

        _accumulate(part, scr[0] if nk > 1 else None, pl.program_id(2), nk, finish)

    return _pc(
        body, name=name, out_shape=jax.ShapeDtypeStruct(shape, out_dtype),
        grid=(K // tm, N // tn, nk),
        in_specs=[pl.BlockSpec((tk, tm), lambda i, j, k: (k, i)),
                  pl.BlockSpec((tk, tn), lambda i, j, k: (k, j))],
        out_specs=pl.BlockSpec(oblock, lambda i, j, k: omap(i, j)),
        scratch_shapes=[pltpu.VMEM((tm, tn), F32)] if nk > 1 else [],
        compiler_params=_params("parallel", "parallel", "arbitrary"),
    )(a, b)


_TM_FUSED = (512, 256, 128, 64, 32, 16, 8)


def mm_nn_post(name, a, w, x, wvec, *, bias=None):
    M, K = a.shape
    assert K == w.K
    N = w.N
    tm, tk = _pick(M, _TM_FUSED), w.tile_k(_TK)
    rows = _pick(tm, (64, 32, 16, 8))
    nk = K // tk
    wblock, wmap = w.spec(tk, N)
    in_specs = [pl.BlockSpec((tm, tk), lambda i, k: (i, k)),
                pl.BlockSpec(wblock, lambda i, k: wmap(k, 0)),
                pl.BlockSpec((tm, N), lambda i, k: (i, 0)),
                pl.BlockSpec((1, N), lambda i, k: (0, 0))]
    args = [a, w.arr, x, wvec]
    if bias is not None:
        in_specs.append(pl.BlockSpec((1, N), lambda i, k: (0, 0)))
        args.append(bias)

    def body(*refs):
        a_ref, b_ref, x_ref, wv_ref = refs[:4]
        bias_ref = refs[4] if bias is not None else None
        y_ref, xn_ref, acc_ref = refs[-3:]
        part = jnp.dot(a_ref[...].astype(BF16), b_ref[...].astype(BF16), preferred_element_type=F32)
        k = pl.program_id(1)

        @pl.when(k == 0)
        def _():
            acc_ref[...] = part

        @pl.when(k > 0)
        def _():
            acc_ref[...] += part

        @pl.when(k == nk - 1)
        def _():
            for r0 in range(0, tm, rows):
                rr = pl.ds(r0, rows)
                yv = acc_ref[rr, :]
                if bias_ref is not None:
                    yv = yv + bias_ref[...]
                y_ref[rr, :] = yv.astype(BF16)
                xn_ref[rr, :] = x_ref[rr, :] + yv * _rms(yv) * wv_ref[...]

    spec = pl.BlockSpec((tm, N), lambda i, k: (i, 0))
    return _pc(
        body, name=name, out_shape=[jax.ShapeDtypeStruct((M, N), BF16), jax.ShapeDtypeStruct((M, N), F32)],
        grid=(M // tm, nk), in_specs=in_specs, out_specs=[spec, spec],
        scratch_shapes=[pltpu.VMEM((tm, N), F32)],
        compiler_params=_params("parallel", "arbitrary"),
    )(*args)


def gated_in_fwd(name, h, w, unit, *, bias=None):
    M, K = h.shape
    assert w.kind == "col" and K == w.K == w.R
    tn, F = w.tn, w.N // 2
    tm = _pick(M, _TM_FUSED)
    wblock, wmap = w.spec(K, tn)
    in_specs = [pl.BlockSpec((tm, K), lambda g, i: (i, 0)),
                pl.BlockSpec(wblock, lambda g, i: wmap(0, 2 * g)),
                pl.BlockSpec(wblock, lambda g, i: wmap(0, 2 * g + 1))]
    args = [h, w.arr, w.arr]
    if bias is not None:
        in_specs.append(pl.BlockSpec((1, 2 * tn), lambda g, i: (0, g)))
        args.append(bias)

    def body(h_ref, w1_ref, w2_ref, *refs):
        u_ref, act_ref = refs[-2:]
        hv = h_ref[...].astype(BF16)
        first = jnp.dot(hv, w1_ref[...].astype(BF16), preferred_element_type=F32)
        second = jnp.dot(hv, w2_ref[...].astype(BF16), preferred_element_type=F32)
        if bias is not None:
            first, second = first + refs[0][:, :tn], second + refs[0][:, tn:]
        u_ref[:, :tn] = first.astype(BF16)
        u_ref[:, tn:] = second.astype(BF16)
        if unit == "swiglu":
            act_ref[...] = (first * _sigmoid(first) * second).astype(act_ref.dtype)
        else:
            act_ref[...] = (first * _sigmoid(second)).astype(act_ref.dtype)

    return _pc(
        body, name=name,
        out_shape=[jax.ShapeDtypeStruct((M, 2 * F), BF16),
                   jax.ShapeDtypeStruct((M, F), BF16 if unit == "swiglu" else F32)],
        grid=(F // tn, M // tm), in_specs=in_specs,
        out_specs=[pl.BlockSpec((tm, 2 * tn), lambda g, i: (i, g)),
                   pl.BlockSpec((tm, tn), lambda g, i: (i, g))],
        compiler_params=_params("parallel", "parallel"),
    )(*args)


def ffn_out_dgrad(name, dy, w, u):
    M, N = dy.shape
    assert w.kind == "row" and N == w.N == w.C
    tj, F = w.R, w.K
    tm = _pick(M, _TM_FUSED)
    wblock, wmap = w.spec(tj, N)

    def body(dy_ref, b_ref, u_ref, du_ref):
        d = lax.dot_general(dy_ref[...].astype(BF16), b_ref[...].astype(BF16),
                            (((1,), (1,)), ((), ())), preferred_element_type=F32)
        gate, up = u_ref[:, :tj].astype(F32), u_ref[:, tj:].astype(F32)
        sg = _sigmoid(gate)
        du_ref[:, :tj] = (d * up * (sg * (1.0 + gate * (1.0 - sg)))).astype(BF16)
        du_ref[:, tj:] = (d * gate * sg).astype(BF16)

    return _pc(
        body, name=name, out_shape=jax.ShapeDtypeStruct((M, 2 * F), BF16),
        grid=(F // tj, M // tm),
        in_specs=[pl.BlockSpec((tm, N), lambda j, i: (i, 0)),
                  pl.BlockSpec(wblock, lambda j, i: wmap(j, 0)),
                  pl.BlockSpec((tm, 2 * tj), lambda j, i: (i, j))],
        out_specs=pl.BlockSpec((tm, 2 * tj), lambda j, i: (i, j)),
        compiler_params=_params("parallel", "parallel"),
    )(dy, w.arr, u)


def _rows(name, body, rows_in, vecs_in, rows_out, accs_out, tm=256, deps=()):
    T = rows_in[0].shape[0]
    tm = _pick(T, (tm, 128, 64, 32, 16, 8))
    n_r, n_v, n_o, n_d = len(rows_in), len(vecs_in), len(rows_out), len(deps)

    def kern(*refs):
        r_refs, v_refs = refs[:n_r], refs[n_r:n_r + n_v]
        refs = refs[n_r + n_v + n_d:]
        o_refs, a_refs = refs[:n_o], refs[n_o:]

        @pl.when(pl.program_id(0) == 0)
        def _():
            for a_ref in a_refs:
                a_ref[...] = jnp.zeros(a_ref.shape, F32)

        body(r_refs, v_refs, o_refs, a_refs)

    in_specs = [pl.BlockSpec((tm, a.shape[1]), lambda i: (i, 0)) for a in rows_in]
    in_specs += [pl.BlockSpec(v.shape, lambda i: (0, 0)) for v in vecs_in]
    in_specs += [HBM] * n_d
    out_shape = [jax.ShapeDtypeStruct((T, w), dt) for w, dt in rows_out]
    out_shape += [jax.ShapeDtypeStruct((8, w), F32) for w in accs_out]
    out_specs = [pl.BlockSpec((tm, w), lambda i: (i, 0)) for w, _ in rows_out]
    out_specs += [pl.BlockSpec((8, w), lambda i: (0, 0)) for w in accs_out]
    return _pc(kern, name=name, out_shape=out_shape, grid=(T // tm,), in_specs=in_specs,
               out_specs=out_specs, compiler_params=_params("arbitrary"))(*rows_in, *vecs_in, *deps)


def _rms(x):
    return lax.rsqrt(jnp.mean(x * x, axis=-1, keepdims=True) + EPS)


def prenorm_fwd(name, x, gain, shift, deps=()):
    def body(r, v, o, a):
        xv = r[0][...]
        o[0][...] = (xv * _rms(xv) * v[0][...] + v[1][...]).astype(BF16)
    return _rows(name, body, [x], [gain, shift], [(x.shape[1], BF16)], [], deps=deps)[0]


def postnorm_bwd(name, dxo, y, post_g, gate):
    def body(r, v, o, a):
        d, yv = r[0][...], r[1][...].astype(F32)
        pg, gt = v[0][...], v[1][...]
        ry = _rms(yv)
        yn = yv * ry
        t = d * yn
        dyn = d * (gt * pg)
        dy = ry * (dyn - yn * jnp.mean(dyn * yn, axis=-1, keepdims=True))
        o[0][...] = dy.astype(BF16)
        a[0][...] += _fold8(t * pg)
        a[1][...] += _fold8(t * gt)
        a[2][...] += _fold8(dy)
    D = y.shape[1]
    return _rows(name, body, [dxo, y], [post_g, gate], [(D, BF16)], [D, D, D])


def prenorm_bwd(name, dh, x, dxo, pre_g, scale, deps=()):
    def body(r, v, o, a):
        dhv, xv, d = r[0][...].astype(F32), r[1][...], r[2][...]
        pg, sc1 = v[0][...], 1.0 + v[1][...]
        rx = _rms(xv)
        xn = xv * rx
        t = dhv * xn
        dxn = dhv * (pg * sc1)
        o[0][...] = d + rx * (dxn - xn * jnp.mean(dxn * xn, axis=-1, keepdims=True))
        a[0][...] += _fold8(dhv)
        a[1][...] += _fold8(t * pg)
        a[2][...] += _fold8(t * sc1)
    D = x.shape[1]
    return _rows(name, body, [dh, x, dxo], [pre_g, scale], [(D, F32)], [D, D, D], deps=deps)


def loss_bwd(name, y, target):
    D = y.shape[1]

    def body(r, v, o, a):
        e = r[0][...] - r[1][...]
        o[0][...] = e * (1.0 / D)
        a[0][...] += _fold8(e * e * (0.5 / D))
    return _rows(name, body, [y, target], [], [(D, F32)], [D])


def ln_silu_fwd(name, v_, g, b):
    def body(r, v, o, a):
        x = r[0][...]
        xc = x - jnp.mean(x, axis=-1, keepdims=True)
        ln = xc * lax.rsqrt(jnp.mean(xc * xc, axis=-1, keepdims=True) + EPS) * v[0][...] + v[1][...]
        o[0][...] = (ln * _sigmoid(ln)).astype(BF16)
    return _rows(name, body, [v_], [g, b], [(v_.shape[1], BF16)], [])[0]


def ln_silu_bwd(name, ds, v_, g, b):
    def body(r, v, o, a):
        dsv, x = r[0][...], r[1][...]
        xc = x - jnp.mean(x, axis=-1, keepdims=True)
        rstd = lax.rsqrt(jnp.mean(xc * xc, axis=-1, keepdims=True) + EPS)
        xh = xc * rstd
        ln = xh * v[0][...] + v[1][...]
        sg = _sigmoid(ln)
        dln = dsv * (sg * (1.0 + ln * (1.0 - sg)))
        dxh = dln * v[0][...]
        o[0][...] = rstd * (dxh - jnp.mean(dxh, axis=-1, keepdims=True)
                            - xh * jnp.mean(dxh * xh, axis=-1, keepdims=True))
        a[0][...] += _fold8(dln * xh)
        a[1][...] += _fold8(dln)
    D = v_.shape[1]
    return _rows(name, body, [ds, v_], [g, b], [(D, F32)], [D, D])


def _pairs(name, body, u, others, out_w, out_dtype, tn, n_acc=0, tm=256):
    T, F2 = u.shape
    F = F2 // 2
    tm = _pick(T, (tm, 128, 64, 32, 16, 8))

    def kern(*refs):
        u_ref, o_refs = refs[0], refs[1:1 + len(others)]
        out_ref, acc_refs = refs[1 + len(others)], refs[2 + len(others):]

        @pl.when(pl.program_id(1) == 0)
        def _():
            for a_ref in acc_refs:
                a_ref[...] = jnp.zeros(a_ref.shape, F32)

        body(u_ref, o_refs, out_ref, acc_refs)

    out_shape = [jax.ShapeDtypeStruct((T, out_w * F), out_dtype)]
    out_shape += [jax.ShapeDtypeStruct((8, F2), F32)] * n_acc
    out_specs = [pl.BlockSpec((tm, out_w * tn), lambda g, i: (i, g))]
    out_specs += [pl.BlockSpec((8, 2 * tn), lambda g, i: (0, g))] * n_acc
    return _pc(kern, name=name, out_shape=out_shape, grid=(F // tn, T // tm),
               in_specs=[pl.BlockSpec((tm, 2 * tn), lambda g, i: (i, g))]
               + [pl.BlockSpec((tm, tn), lambda g, i: (i, g))] * len(others),
               out_specs=out_specs, compiler_params=_params("parallel", "arbitrary"))(u, *others)


def glu_bwd(name, u, dglu, tn):
    def body(u_ref, o, out, acc):
        a, g = u_ref[:, :tn].astype(F32), u_ref[:, tn:].astype(F32)
        d = o[0][...]
        sg = _sigmoid(g)
        da, dg = d * sg, d * a * (sg * (1.0 - sg))
        out[:, :tn] = da.astype(BF16)
        out[:, tn:] = dg.astype(BF16)
        acc[0][:, :tn] += _fold8(da)
        acc[0][:, tn:] += _fold8(dg)
    return _pairs(name, body, u, [dglu], 2, BF16, tn, n_acc=1)


_CONV_ROWS, _CONV_LANES = 64, 128
_SUBLANES = 8


def _fill_shifts(ext, sh, n):
    for r in range(1, _SUBLANES):
        sh[r - 1, pl.ds(0, n), :] = ext[pl.ds(r, n), :]


def _tap(ext, sh, o, r0, rows, ln):
    q, r = divmod(o, _SUBLANES)
    if r == 0:
        return ext[pl.ds(r0 + o, rows), ln]
    return sh[r - 1, pl.ds(r0 + _SUBLANES * q, rows), ln]


def dwconv_fwd(name, x, w, b):
    T, D = x.shape
    width = w.shape[0]
    tt, cb = _pick(T, (256, 128, 64)), _pick(D, (256, 128))
    off = CONV_HALO - (width - 1)
    rows = min(_CONV_ROWS, tt)
    n_sh = tt + CONV_HALO - _SUBLANES

    def body(cur_ref, prev_ref, w_ref, b_ref, o_ref, ext, sh):
        t = pl.program_id(1)
        tail = prev_ref[pl.ds(tt - CONV_HALO, CONV_HALO), :]
        ext[pl.ds(0, CONV_HALO), :] = jnp.where(t > 0, tail, 0.0)
        ext[pl.ds(CONV_HALO, tt), :] = cur_ref[...]
        _fill_shifts(ext, sh, n_sh)
        for l0 in range(0, cb, _CONV_LANES):
            ln = pl.ds(l0, _CONV_LANES)
            for r0 in range(0, tt, rows):
                acc = jnp.broadcast_to(b_ref[:, ln], (rows, _CONV_LANES))
                for k in range(width):
                    acc = acc + _tap(ext, sh, off + k, r0, rows, ln) * w_ref[pl.ds(k, 1), ln]
                o_ref[pl.ds(r0, rows), ln] = acc

    return _pc(
        body, name=name, out_shape=jax.ShapeDtypeStruct((T, D), F32), grid=(D // cb, T // tt),
        in_specs=[pl.BlockSpec((tt, cb), lambda c, t: (t, c)),
                  pl.BlockSpec((tt, cb), lambda c, t: (jnp.maximum(t - 1, 0), c)),
                  pl.BlockSpec((width, cb), lambda c, t: (0, c)),
                  pl.BlockSpec((1, cb), lambda c, t: (0, c))],
        out_specs=pl.BlockSpec((tt, cb), lambda c, t: (t, c)),
        scratch_shapes=[pltpu.VMEM((tt + CONV_HALO, cb), F32), pltpu.VMEM((_SUBLANES - 1, tt + CONV_HALO, cb), F32)],
        compiler_params=_params("parallel", "arbitrary"),
    )(x, x, w, b)


def dwconv_bwd(name, dv, x, w):
    T, D = x.shape
    width = w.shape[0]
    tt, cb = _pick(T, (256, 128, 64)), _pick(D, (256, 128))
    off = CONV_HALO - (width - 1)
    rows = min(_CONV_ROWS, tt)
    nt = T // tt
    n_sh = tt + CONV_HALO - _SUBLANES

    def body(dv_ref, dvn_ref, x_ref, xp_ref, w_ref, dx_ref, dw_ref, db_ref, ext_d, ext_x, sh_d, sh_x):
        t = pl.program_id(1)

        @pl.when(t == 0)
        def _():
            dw_ref[...] = jnp.zeros(dw_ref.shape, F32)
            db_ref[...] = jnp.zeros(db_ref.shape, F32)

        ext_d[pl.ds(0, tt), :] = dv_ref[...]
        ext_d[pl.ds(tt, CONV_HALO), :] = jnp.where(t < nt - 1, dvn_ref[pl.ds(0, CONV_HALO), :], 0.0)
        ext_x[pl.ds(0, CONV_HALO), :] = jnp.where(t > 0, xp_ref[pl.ds(tt - CONV_HALO, CONV_HALO), :], 0.0)
        ext_x[pl.ds(CONV_HALO, tt), :] = x_ref[...]
        db_ref[...] += _fold8(dv_ref[...])
        _fill_shifts(ext_d, sh_d, n_sh)
        _fill_shifts(ext_x, sh_x, n_sh)
        for l0 in range(0, cb, _CONV_LANES):
            ln = pl.ds(l0, _CONV_LANES)
            for r0 in range(0, tt, rows):
                acc = jnp.zeros((rows, _CONV_LANES), F32)
                for k in range(width):
                    acc = acc + _tap(ext_d, sh_d, (width - 1) - k, r0, rows, ln) * w_ref[pl.ds(k, 1), ln]
                dx_ref[pl.ds(r0, rows), ln] = acc
            sums = [jnp.zeros((8, _CONV_LANES), F32)] * width
            for r0 in range(0, tt, rows):
                dvb = ext_d[pl.ds(r0, rows), ln]
                sums = [sums[k] + _fold8(dvb * _tap(ext_x, sh_x, off + k, r0, rows, ln)) for k in range(width)]
            for k in range(width):
                dw_ref[pl.ds(8 * k, 8), ln] += sums[k]

    return _pc(
        body, name=name,
        out_shape=[jax.ShapeDtypeStruct((T, D), F32), jax.ShapeDtypeStruct((8 * width, D), F32),
                   jax.ShapeDtypeStruct((8, D), F32)],
        grid=(D // cb, nt),
        in_specs=[pl.BlockSpec((tt, cb), lambda c, t: (t, c)),
                  pl.BlockSpec((tt, cb), lambda c, t: (jnp.minimum(t + 1, nt - 1), c)),
                  pl.BlockSpec((tt, cb), lambda c, t: (t, c)),
                  pl.BlockSpec((tt, cb), lambda c, t: (jnp.maximum(t - 1, 0), c)),
                  pl.BlockSpec((width, cb), lambda c, t: (0, c))],
        out_specs=[pl.BlockSpec((tt, cb), lambda c, t: (t, c)),
                   pl.BlockSpec((8 * width, cb), lambda c, t: (0, c)),
                   pl.BlockSpec((8, cb), lambda c, t: (0, c))],
        scratch_shapes=[pltpu.VMEM((tt + CONV_HALO, cb), F32), pltpu.VMEM((tt + CONV_HALO, cb), F32),
                        pltpu.VMEM((_SUBLANES - 1, tt + CONV_HALO, cb), F32),
                        pltpu.VMEM((_SUBLANES - 1, tt + CONV_HALO, cb), F32)],
        compiler_params=_params("parallel", "arbitrary"),
    )(dv, dv, x, x, w)


def _gla_dims(proj, wgu):
    DK = wgu.shape[1]
    DV = (proj.shape[1] - 128 - 2 * DK) // 2
    return DK, DV, DK // GLA_HEADS, DV // GLA_HEADS


def _gla_decay(a_ref, w_ref, bg_ref):
    C = GLA_CHUNK
    z = jnp.dot(a_ref[...].astype(BF16), w_ref[...].astype(BF16), preferred_element_type=F32) + bg_ref[...]
    g = (jnp.minimum(z, 0.0) - jnp.log(1.0 + jnp.exp(-jnp.abs(z)))) * (1.0 / GLA_TAU)
    row = lax.broadcasted_iota(jnp.int32, (C, C), 0)
    col = lax.broadcasted_iota(jnp.int32, (C, C), 1)
    bc = jnp.dot((row >= col).astype(F32), g, precision=lax.Precision.HIGHEST, preferred_element_type=F32)
    last = lax.broadcasted_iota(jnp.int32, bc.shape, 0) == C - 1
    b_last = jnp.sum(jnp.where(last, bc, 0.0), axis=0, keepdims=True)
    return z, bc, b_last


def _gla_in_specs(wgu, DK, DV, cidx):
    C = GLA_CHUNK
    return [pl.BlockSpec((C, DK), lambda c: (cidx(c), 0)),
            pl.BlockSpec((C, DK), lambda c: (cidx(c), 1)),
            pl.BlockSpec((C, DV), lambda c: (cidx(c), (2 * DK) // DV)),
            pl.BlockSpec((C, 128), lambda c: (cidx(c), (2 * DK + 2 * DV) // 128)),
            pl.BlockSpec(wgu.shape, lambda c: (0, 0)),
            pl.BlockSpec((1, DK), lambda c: (0, 0))]


def _dot_nt(a, b):
    return lax.dot_general(a.astype(BF16), b.astype(BF16), (((1,), (1,)), ((), ())), preferred_element_type=F32)


def _dot_tn(a, b):
    return lax.dot_general(a.astype(BF16), b.astype(BF16), (((0,), (0,)), ((), ())), preferred_element_type=F32)


def _dot(a, b):
    return jnp.dot(a.astype(BF16), b.astype(BF16), preferred_element_type=F32)


def gla_fwd(name, proj, wgu, bg):
    T = proj.shape[0]
    C, H = GLA_CHUNK, GLA_HEADS
    DK, DV, dk, dv = _gla_dims(proj, wgu)
    nc = T // C
    scale = dk ** -0.5

    def body(q_ref, k_ref, v_ref, a_ref, w_ref, bg_ref, o_ref, s_ref, st):
        @pl.when(pl.program_id(0) == 0)
        def _():
            st[...] = jnp.zeros(st.shape, F32)

        _, bc, b_last = _gla_decay(a_ref, w_ref, bg_ref)
        k = k_ref[...]
        qe = (q_ref[...] * scale * jnp.exp(bc)).astype(BF16)
        ke = (k * jnp.exp(-bc)).astype(BF16)
        kd = (k * jnp.exp(b_last - bc)).astype(BF16)
        el = jnp.exp(b_last)
        row = lax.broadcasted_iota(jnp.int32, (C, C), 0)
        col = lax.broadcasted_iota(jnp.int32, (C, C), 1)
        for h in range(H):
            kk, vv = slice(h * dk, (h + 1) * dk), slice(h * dv, (h + 1) * dv)
            sp = st[h]
            s_ref[h] = sp.astype(BF16)
            v = v_ref[:, vv]
            att = jnp.where(row >= col, _dot_nt(qe[:, kk], ke[:, kk]), 0.0)
            o_ref[:, vv] = _dot_nt(qe[:, kk], sp) + _dot(att, v)
            st[h] = sp * el[:, kk] + _dot_tn(v, kd[:, kk])

    return _pc(
        body, name=name,
        out_shape=[jax.ShapeDtypeStruct((T, DV), F32), jax.ShapeDtypeStruct((H, nc, dv, dk), BF16)],
        grid=(nc,), in_specs=_gla_in_specs(wgu, DK, DV, lambda c: c),
        out_specs=[pl.BlockSpec((C, DV), lambda c: (c, 0)),
                   pl.BlockSpec((H, None, dv, dk), lambda c: (0, c, 0, 0))],
        scratch_shapes=[pltpu.VMEM((H, dv, dk), F32)],
        compiler_params=_params("arbitrary"),
    )(proj, proj, proj, proj, wgu, bg)


def gla_bwd(name, proj, wgu, bg, states, do):
    T = proj.shape[0]
    C, H = GLA_CHUNK, GLA_HEADS
    DK, DV, dk, dv = _gla_dims(proj, wgu)
    nc = T // C
    scale = dk ** -0.5
    rev = lambda c: nc - 1 - c

    def body(q_ref, k_ref, v_ref, a_ref, w_ref, bg_ref, s_ref, do_ref,
             dq_ref, dk_ref, dv_ref, dz_ref, dbg_ref, dst, db_scr):
        @pl.when(pl.program_id(0) == 0)
        def _():
            dst[...] = jnp.zeros(dst.shape, F32)
            dbg_ref[...] = jnp.zeros(dbg_ref.shape, F32)

        z, bc, b_last = _gla_decay(a_ref, w_ref, bg_ref)
        k = k_ref[...]
        eb, enb, ed, el = jnp.exp(bc), jnp.exp(-bc), jnp.exp(b_last - bc), jnp.exp(b_last)
        qe, ke, kd = q_ref[...] * scale * eb, k * enb, k * ed
        row = lax.broadcasted_iota(jnp.int32, (C, C), 0)
        col = lax.broadcasted_iota(jnp.int32, (C, C), 1)
        keep = row >= col
        last = lax.broadcasted_iota(jnp.int32, (C, dk), 0) == C - 1
        for h in range(H):
            kk, vv = slice(h * dk, (h + 1) * dk), slice(h * dv, (h + 1) * dv)
            qe_h, ke_h, kd_h, el_h = qe[:, kk], ke[:, kk], kd[:, kk], el[:, kk]
            v, d_o = v_ref[:, vv], do_ref[:, vv]
            sp = s_ref[h].astype(F32)
            ds_ = dst[h]
            att = jnp.where(keep, _dot_nt(qe_h, ke_h), 0.0)
            datt = jnp.where(keep, _dot_nt(d_o, v), 0.0)
            dqe = _dot(d_o, sp) + _dot(datt, ke_h)
            dke = _dot_tn(datt, qe_h)
            dv_ref[:, vv] = (_dot_tn(att, d_o) + _dot_nt(kd_h, ds_)).astype(BF16)
            dkd = _dot(v, ds_)
            dst[h] = ds_ * el_h + _dot_tn(d_o, qe_h)
            dq_ref[:, kk] = (dqe * scale * eb[:, kk]).astype(BF16)
            dk_ref[:, kk] = (dke * enb[:, kk] + dkd * ed[:, kk]).astype(BF16)
            d_el = jnp.sum(sp * ds_, axis=0, keepdims=True)
            db_last = jnp.sum(dkd * kd_h, axis=0, keepdims=True) + d_el * el_h
            db_scr[:, kk] = dqe * qe_h - dke * ke_h - dkd * kd_h + jnp.where(last, db_last, 0.0)
        dg = jnp.dot((row <= col).astype(F32), db_scr[...], precision=lax.Precision.HIGHEST,
                     preferred_element_type=F32)
        dz = dg * (1.0 / GLA_TAU) * _sigmoid(-z)
        dz_ref[...] = dz.astype(BF16)
        dbg_ref[...] += _fold8(dz)

    return _pc(
        body, name=name,
        out_shape=[jax.ShapeDtypeStruct((T, DK), BF16), jax.ShapeDtypeStruct((T, DK), BF16),
                   jax.ShapeDtypeStruct((T, DV), BF16), jax.ShapeDtypeStruct((T, DK), BF16),
                   jax.ShapeDtypeStruct((8, DK), F32)],
        grid=(nc,),
        in_specs=_gla_in_specs(wgu, DK, DV, rev)
        + [pl.BlockSpec((H, None, dv, dk), lambda c: (0, rev(c), 0, 0)),
           pl.BlockSpec((C, DV), lambda c: (rev(c), 0))],
        out_specs=[pl.BlockSpec((C, DK), lambda c: (rev(c), 0)),
                   pl.BlockSpec((C, DK), lambda c: (rev(c), 0)),
                   pl.BlockSpec((C, DV), lambda c: (rev(c), 0)),
                   pl.BlockSpec((C, DK), lambda c: (rev(c), 0)),
                   pl.BlockSpec((8, DK), lambda c: (0, 0))],
        scratch_shapes=[pltpu.VMEM((H, dv, dk), F32), pltpu.VMEM((C, DK), F32)],
        compiler_params=_params("arbitrary"),
    )(proj, proj, proj, proj, wgu, bg, states, do)


def gla_out_fwd(name, o, proj, norm_g, r_block0):
    T, DV = o.shape
    dv = DV // GLA_HEADS
    tm = _pick(T, (512, 256, 128, 64))

    def body(o_ref, r_ref, g_ref, out_ref):
        ov, rv = o_ref[...], r_ref[...]
        out_ref[...] = (ov * _rms(ov) * g_ref[...] * (rv * _sigmoid(rv))).astype(BF16)

    return _pc(
        body, name=name, out_shape=jax.ShapeDtypeStruct((T, DV), BF16), grid=(GLA_HEADS, T // tm),
        in_specs=[pl.BlockSpec((tm, dv), lambda h, i: (i, h)),
                  pl.BlockSpec((tm, dv), lambda h, i: (i, r_block0 + h)),
                  pl.BlockSpec((1, dv), lambda h, i: (0, h))],
        out_specs=pl.BlockSpec((tm, dv), lambda h, i: (i, h)),
        compiler_params=_params("parallel", "parallel"),
    )(o, proj, norm_g)


def gla_out_bwd(name, dog, o, proj, norm_g, r_block0):
    T, DV = o.shape
    dv = DV // GLA_HEADS
    tm = _pick(T, (512, 256, 128, 64))

    def body(d_ref, o_ref, r_ref, g_ref, do_ref, dr_ref, dg_ref):
        @pl.when(pl.program_id(1) == 0)
        def _():
            dg_ref[...] = jnp.zeros(dg_ref.shape, F32)

        d, ov, rv, g = d_ref[...], o_ref[...], r_ref[...], g_ref[...]
        ro = _rms(ov)
        oh = ov * ro
        sg = _sigmoid(rv)
        dn = d * (rv * sg)
        dr_ref[...] = (d * (oh * g) * (sg * (1.0 + rv * (1.0 - sg)))).astype(BF16)
        doh = dn * g
        do_ref[...] = ro * (doh - oh * jnp.mean(doh * oh, axis=-1, keepdims=True))
        dg_ref[...] += _fold8(dn * oh)

    return _pc(
        body, name=name,
        out_shape=[jax.ShapeDtypeStruct((T, DV), F32), jax.ShapeDtypeStruct((T, DV), BF16),
                   jax.ShapeDtypeStruct((8, DV), F32)],
        grid=(GLA_HEADS, T // tm),
        in_specs=[pl.BlockSpec((tm, dv), lambda h, i: (i, h)),
                  pl.BlockSpec((tm, dv), lambda h, i: (i, h)),
                  pl.BlockSpec((tm, dv), lambda h, i: (i, r_block0 + h)),
                  pl.BlockSpec((1, dv), lambda h, i: (0, h))],
        out_specs=[pl.BlockSpec((tm, dv), lambda h, i: (i, h)),
                   pl.BlockSpec((tm, dv), lambda h, i: (i, h)),
                   pl.BlockSpec((8, dv), lambda h, i: (0, h))],
        compiler_params=_params("parallel", "arbitrary"),
    )(dog, o, proj, norm_g)


def _ew_rows(r, c):
    return _pick(r, tuple(t for t in (512, 256, 128, 64, 32, 16, 8) if t * c <= 256 * 1024) or (8,))


def sum_slots_layers(name, lands):
    nl = len(lands)
    n, r, c = lands[0].shape
    tr = _ew_rows(r, c)
    nb = r // tr

    def body(*refs):
        o_ref = refs[nl]
        for l in range(nl):
            @pl.when(pl.program_id(0) == l)
            def _(p_ref=refs[l]):
                acc = p_ref[0].astype(F32)
                for s in range(1, n):
                    acc = acc + p_ref[s].astype(F32)
                o_ref[...] = acc

    def in_map(l):
        return lambda q, i: (0, jnp.where(q == l, i, 0), 0)

    return _pc(body, name=name, out_shape=jax.ShapeDtypeStruct((nl * r, c), F32), grid=(nl, nb),
               in_specs=[pl.BlockSpec((n, tr, c), in_map(l)) for l in range(nl)],
               out_specs=pl.BlockSpec((tr, c), lambda q, i: (q * nb + i, 0)),
               compiler_params=_params("arbitrary", "arbitrary"))(*lands)


def sum_slots(name, p):
    n, r, c = p.shape
    tr = _ew_rows(r, c)

    def body(p_ref, o_ref):
        acc = p_ref[0].astype(F32)
        for s in range(1, n):
            acc = acc + p_ref[s].astype(F32)
        o_ref[...] = acc

    return _pc(body, name=name, out_shape=jax.ShapeDtypeStruct((r, c), F32), grid=(r // tr,),
               in_specs=[pl.BlockSpec((n, tr, c), lambda i: (0, i, 0))],
               out_specs=pl.BlockSpec((tr, c), lambda i: (i, 0)),
               compiler_params=_params("parallel"))(p)


def adamw(name, w, gs, m, v, deps=()):
    r, c = w.shape
    tr = _ew_rows(r, c)
    n_g = len(gs)
    m_corr = 1.0 / (1.0 - ADAM_B1 ** ADAM_STEP)
    v_corr = 1.0 / (1.0 - ADAM_B2 ** ADAM_STEP)

    def body(*refs):
        w_ref, g_refs, m_ref, v_ref = refs[0], refs[1:1 + n_g], refs[1 + n_g], refs[2 + n_g]
        g_out, d_out, m_out, v_out = refs[-4:]
        g = g_refs[0][...]
        if n_g == 2:
            g = g + g_refs[1][...]
        mn = ADAM_B1 * m_ref[...] + (1.0 - ADAM_B1) * g
        vn = ADAM_B2 * v_ref[...] + (1.0 - ADAM_B2) * (g * g)
        g_out[...] = g
        m_out[...] = mn
        v_out[...] = vn
        d_out[...] = -ADAM_LR * ((mn * m_corr) / (jnp.sqrt(vn * v_corr) + ADAM_EPS) + ADAM_WD * w_ref[...])

    spec = pl.BlockSpec((tr, c), lambda i: (i, 0))
    return _pc(body, name=name, out_shape=[jax.ShapeDtypeStruct((r, c), F32)] * 4, grid=(r // tr,),
               in_specs=[spec] * (3 + n_g) + [HBM] * len(deps), out_specs=[spec] * 4,
               compiler_params=_params("parallel"))(w, *gs, m, v, *deps)


def _flat_pack(arrs):
    flat = jnp.concatenate([a.reshape(-1).astype(F32) for a in arrs])
    n = flat.shape[0]
    pad = (-n) % 1024
    return jnp.pad(flat, (0, pad)).reshape(-1, 128)


def _flat_unpack(packed, shapes, lead=()):
    flat = packed.reshape(lead + (-1,))
    out, pos = [], 0
    for s in shapes:
        n = 1
        for d in s:
            n *= d
        out.append(flat[..., pos:pos + n].reshape(lead + tuple(s)))
        pos += n
    return out


def _interleave_vec(b, tn):
    f = b.shape[-1] // 2
    return b.reshape(2, f // tn, tn).transpose(1, 0, 2).reshape(1, 2 * f)


def _deinterleave_vec(b, tn):
    f = b.shape[-1] // 2
    return b.reshape(f // tn, 2, tn).transpose(1, 0, 2).reshape(2 * f)


def kernel(x, c, w_ada, b_ada, pre_mix_g, post_mix_g, pre_ffn_g, post_ffn_g, conv_w_pw1, conv_b_pw1, conv_w_dw, conv_b_dw, conv_ln_g, conv_ln_b, conv_w_pw2, conv_b_pw2, gla_w_in, gla_w_gate_up, gla_b_gate, gla_norm_g, gla_w_out, ffn_w_in, ffn_w_out, loss_target, m_w_ada, m_b_ada, m_pre_mix_g, m_post_mix_g, m_pre_ffn_g, m_post_ffn_g, m_conv_w_pw1, m_conv_b_pw1, m_conv_w_dw, m_conv_b_dw, m_conv_ln_g, m_conv_ln_b, m_conv_w_pw2, m_conv_b_pw2, m_gla_w_in, m_gla_w_gate_up, m_gla_b_gate, m_gla_norm_g, m_gla_w_out, m_ffn_w_in, m_ffn_w_out, v_w_ada, v_b_ada, v_pre_mix_g, v_post_mix_g, v_pre_ffn_g, v_post_ffn_g, v_conv_w_pw1, v_conv_b_pw1, v_conv_w_dw, v_conv_b_dw, v_conv_ln_g, v_conv_ln_b, v_conv_w_pw2, v_conv_b_pw2, v_gla_w_in, v_gla_w_gate_up, v_gla_b_gate, v_gla_norm_g, v_gla_w_out, v_ffn_w_in, v_ffn_w_out):
    weights = dict(w_ada=w_ada, b_ada=b_ada, pre_mix_g=pre_mix_g, post_mix_g=post_mix_g, pre_ffn_g=pre_ffn_g, post_ffn_g=post_ffn_g, conv_w_pw1=conv_w_pw1, conv_b_pw1=conv_b_pw1, conv_w_dw=conv_w_dw, conv_b_dw=conv_b_dw, conv_ln_g=conv_ln_g, conv_ln_b=conv_ln_b, conv_w_pw2=conv_w_pw2, conv_b_pw2=conv_b_pw2, gla_w_in=gla_w_in, gla_w_gate_up=gla_w_gate_up, gla_b_gate=gla_b_gate, gla_norm_g=gla_norm_g, gla_w_out=gla_w_out, ffn_w_in=ffn_w_in, ffn_w_out=ffn_w_out)
    mom_m = dict(w_ada=m_w_ada, b_ada=m_b_ada, pre_mix_g=m_pre_mix_g, post_mix_g=m_post_mix_g, pre_ffn_g=m_pre_ffn_g, post_ffn_g=m_post_ffn_g, conv_w_pw1=m_conv_w_pw1, conv_b_pw1=m_conv_b_pw1, conv_w_dw=m_conv_w_dw, conv_b_dw=m_conv_b_dw, conv_ln_g=m_conv_ln_g, conv_ln_b=m_conv_ln_b, conv_w_pw2=m_conv_w_pw2, conv_b_pw2=m_conv_b_pw2, gla_w_in=m_gla_w_in, gla_w_gate_up=m_gla_w_gate_up, gla_b_gate=m_gla_b_gate, gla_norm_g=m_gla_norm_g, gla_w_out=m_gla_w_out, ffn_w_in=m_ffn_w_in, ffn_w_out=m_ffn_w_out)
    mom_v = dict(w_ada=v_w_ada, b_ada=v_b_ada, pre_mix_g=v_pre_mix_g, post_mix_g=v_post_mix_g, pre_ffn_g=v_pre_ffn_g, post_ffn_g=v_post_ffn_g, conv_w_pw1=v_conv_w_pw1, conv_b_pw1=v_conv_b_pw1, conv_w_dw=v_conv_w_dw, conv_b_dw=v_conv_b_dw, conv_ln_g=v_conv_ln_g, conv_ln_b=v_conv_ln_b, conv_w_pw2=v_conv_w_pw2, conv_b_pw2=v_conv_b_pw2, gla_w_in=v_gla_w_in, gla_w_gate_up=v_gla_w_gate_up, gla_b_gate=v_gla_b_gate, gla_norm_g=v_gla_norm_g, gla_w_out=v_gla_w_out, ffn_w_in=v_ffn_w_in, ffn_w_out=v_ffn_w_out)
    order = list(weights)

    ax, ay, ac = lax.axis_index("x"), lax.axis_index("y"), lax.axis_index("c")
    my_chip, my_dev = 2 * ax + ay, 4 * ax + 2 * ay + ac

    x = x[0]
    target = loss_target[0]
    T, D = x.shape
    depth = w_ada.shape[0]
    n_conv, n_gla = conv_w_pw1.shape[0], gla_w_in.shape[0]
    width = conv_w_dw.shape[1]
    F = ffn_w_out.shape[1] * N_CHIPS
    DK = gla_w_gate_up.shape[2] * N_CHIPS
    rank = gla_w_gate_up.shape[1]
    gla_cols = 2 * DK + 2 * D + rank
    P = 2 * DK + 2 * D + 128
    dvh = D // GLA_HEADS
    tn_pw1 = _pick(conv_w_pw1.shape[2], (1024, 512, 256, 128))
    tn_ffn = ffn_w_out.shape[1]

    small_sharded = ["conv_w_dw", "gla_w_gate_up", "gla_b_gate", "gla_norm_g"]
    packed = _flat_pack([weights[n] for n in small_sharded])
    got = allgather_devices("gather_small", packed)[0::2]
    parts = _flat_unpack(got, [weights[n].shape for n in small_sharded], lead=(N_CHIPS,))
    w_dw_full, wgu_full, bgate_full, normg_full = [
        jnp.concatenate([p[s] for s in range(N_CHIPS)], axis=-1) for p in parts]
    w_dw_full = w_dw_full
    wgu_pad = jnp.pad(wgu_full, ((0, 0), (0, 128 - rank), (0, 0)))

    c_act = c * _sigmoid(c)
    c_all = allgather_devices("gather_c", jnp.pad(c_act, ((0, 7), (0, 0))))[:, 0, :]
    c16 = jnp.pad(c_all, ((0, 8), (0, 0))).astype(BF16)
    ada_cols = w_ada.shape[2]
    w_ada2 = w_ada.reshape(depth * D, ada_cols)
    mod_cols = [mm_nn(f"ada_fwd_{i}", c16, Wt(w_ada2, "plain", i, D, ada_cols)) for i in range(depth)]
    mod_cols = jnp.stack(mod_cols).reshape(depth * 16, ada_cols)
    mod_all = allgather_devices("gather_mod", mod_cols)[0::2]
    mod_all = mod_all.reshape(N_CHIPS, depth, 16, ada_cols).transpose(1, 2, 0, 3).reshape(depth, 16, 6 * D)
    mod = lax.dynamic_index_in_dim(mod_all, my_dev, axis=1, keepdims=False) + b_ada
    mod = mod.reshape(depth, 6, 1, D)

    gin_cols = gla_w_in.shape[2]

    def sublayer_weights(g):
        i = g // 2
        if g % 2:
            return [("ffn_w_in", i), ("ffn_w_out", i)]
        return [("conv_w_pw1", i // 2), ("conv_w_pw2", i // 2)] if i % 2 == 0 else [("gla_w_in", i // 2), ("gla_w_out", i // 2)]

    issued = []

    def start_gathers(g, after):
        handles, tokens = {}, []
        for n, l in sublayer_weights(g):
            shard = weights[n][l].astype(BF16)
            handles[n], token = chips_start(f"gather_start_{n}_{l}", shard, shard, my_chip, True,
                                            tuple(after) + tuple(issued[-1:]))
            issued.append(token)
            tokens.append(token)
        return handles, tokens

    def wait_gathers(g, handles, after):
        out = []
        for n, l in sublayer_weights(g):
            got = _split_wait(f"gather_wait_{n}_{l}", handles[n], after)
            r, cc = got.shape[1:]
            if n == "gla_w_in":
                full = got.transpose(1, 0, 2).reshape(D, N_CHIPS * gin_cols)
                out.append(Wt(jnp.pad(full, ((0, 0), (0, P - gla_cols))), "plain", 0, D, P))
            elif n in ("conv_w_pw1", "ffn_w_in"):
                out.append(Wt(got, "col", 0, r, cc, tn_pw1 if n == "conv_w_pw1" else tn_ffn))
            else:
                out.append(Wt(got, "row", 0, r, cc))
        return out

    n_sub = 2 * depth
    in_flight = {0: start_gathers(0, ()), 1: start_gathers(1, ())}

    def enter_sublayer(g, xs):
        if g + 2 < n_sub:
            in_flight[g + 2] = start_gathers(g + 2, (xs,))
            return in_flight[g + 2][1]
        return ()

    w_pw1, w_pw2, w_gin, w_gout = [None] * n_conv, [None] * n_conv, [None] * n_gla, [None] * n_gla
    w_fin, w_fout = [None] * depth, [None] * depth
    row = lambda a: a.reshape(1, -1)
    saved = []
    xs = x
    for i in range(depth):
        j = i // 2
        sh1, sc1, gt1, sh2, sc2, gt2 = [mod[i, q] for q in range(6)]
        s = dict(x_in=xs)
        deps = list(enter_sublayer(2 * i, xs))
        if i == 0:
            deps += in_flight[0][1] + in_flight[1][1]
        h = prenorm_fwd(f"pre_mix_{i}", xs, row(pre_mix_g[i]) * (1.0 + sc1), sh1, deps=deps)
        s["h_mix"] = h
        if i % 2 == 0:
            w_pw1[j], w_pw2[j] = wait_gathers(2 * i, in_flight[2 * i][0], (h,))
        else:
            w_gin[j], w_gout[j] = wait_gathers(2 * i, in_flight[2 * i][0], (h,))
        if i % 2 == 0:
            u, glu = gated_in_fwd(f"pw1_{i}", h, w_pw1[j], "glu", bias=_interleave_vec(row(conv_b_pw1[j]), tn_pw1))
            v = dwconv_fwd(f"dwconv_{i}", glu, w_dw_full[j], row(conv_b_dw[j]))
            sl = ln_silu_fwd(f"ln_silu_{i}", v, row(conv_ln_g[j]), row(conv_ln_b[j]))
            y, x_mid = mm_nn_post(f"pw2_{i}", sl, w_pw2[j], xs, gt1 * row(post_mix_g[i]), bias=row(conv_b_pw2[j]))
            s.update(u=u, glu=glu, v=v, s=sl)
        else:
            proj = mm_nn(f"gla_in_{i}", h, w_gin[j])
            o, states = gla_fwd(f"gla_{i}", proj, wgu_pad[j], row(bgate_full[j]))
            og = gla_out_fwd(f"gla_out_{i}", o, proj, row(normg_full[j]), (2 * DK + D) // dvh)
            y, x_mid = mm_nn_post(f"gla_wout_{i}", og, w_gout[j], xs, gt1 * row(post_mix_g[i]))
            s.update(proj=proj, o=o, states=states, og=og)
        s["y_mix"] = y
        xs = x_mid
        s["x_mid"] = xs
        deps = enter_sublayer(2 * i + 1, xs)
        h = prenorm_fwd(f"pre_ffn_{i}", xs, row(pre_ffn_g[i]) * (1.0 + sc2), sh2, deps=deps)
        w_fin[i], w_fout[i] = wait_gathers(2 * i + 1, in_flight[2 * i + 1][0], (h,))
        u, a = gated_in_fwd(f"ffn_in_{i}", h, w_fin[i], "swiglu")
        y, xs = mm_nn_post(f"ffn_out_{i}", a, w_fout[i], xs, gt2 * row(post_ffn_g[i]))
        s.update(h_ffn=h, u_ffn=u, a_ffn=a, y_ffn=y)
        saved.append(s)

    dx, loss_acc = loss_bwd("loss", xs, target)
    loss = lax.psum(jnp.sum(loss_acc), ("x", "y", "c"))

    fold = lambda a: jnp.sum(a, axis=0)
    small_g = {n: [None] * weights[n].shape[0] for n in order if n not in
               ("w_ada", "conv_w_pw1", "conv_w_pw2", "gla_w_in", "gla_w_out", "ffn_w_in", "ffn_w_out")}
    grad_w = lambda wt: Wt(None, wt.kind, 0, wt.R, wt.C, wt.tn)
    scattering = {}

    def start_scatter(n, l, p):
        own = lax.dynamic_index_in_dim(p, my_chip, axis=0, keepdims=False)
        scattering[n, l], token = chips_start(f"scatter_start_{n}_{l}", p, own, my_chip, False)
        return token

    for i in reversed(range(depth)):
        j = i // 2
        s = saved[i]
        sh1, sc1, gt1, sh2, sc2, gt2 = [mod[i, q] for q in range(6)]
        dy, d_gt2, d_pg, _ = postnorm_bwd(f"post_ffn_bwd_{i}", dx, s["y_ffn"], row(post_ffn_g[i]), gt2)
        small_g["post_ffn_g"][i] = fold(d_pg)
        du = ffn_out_dgrad(f"ffn_out_dgrad_{i}", dy, w_fout[i], s["u_ffn"])
        t_out = start_scatter("ffn_w_out", i, mm_tn(f"ffn_out_wgrad_{i}", s["a_ffn"], dy, grad_w(w_fout[i])))
        t_in = start_scatter("ffn_w_in", i, mm_tn(f"ffn_in_wgrad_{i}", s["h_ffn"], du, grad_w(w_fin[i])))
        dh = mm_nt(f"ffn_in_dgrad_{i}", du, w_fin[i], out_dtype=BF16)
        dx, d_sh2, d_sc2, d_pg = prenorm_bwd(f"pre_ffn_bwd_{i}", dh, s["x_mid"], dx, row(pre_ffn_g[i]), sc2,
                                             deps=(t_out, t_in))
        small_g["pre_ffn_g"][i] = fold(d_pg)
        dy, d_gt1, d_pg, dy_sum = postnorm_bwd(f"post_mix_bwd_{i}", dx, s["y_mix"], row(post_mix_g[i]), gt1)
        small_g["post_mix_g"][i] = fold(d_pg)
        if i % 2 == 0:
            small_g["conv_b_pw2"][j] = fold(dy_sum)
            dsl = mm_nt(f"pw2_dgrad_{i}", dy, w_pw2[j])
            t_out = start_scatter("conv_w_pw2", j, mm_tn(f"pw2_wgrad_{i}", s["s"], dy, grad_w(w_pw2[j])))
            dv, d_lg, d_lb = ln_silu_bwd(f"ln_silu_bwd_{i}", dsl, s["v"], row(conv_ln_g[j]), row(conv_ln_b[j]))
            small_g["conv_ln_g"][j], small_g["conv_ln_b"][j] = fold(d_lg), fold(d_lb)
            dglu, d_wdw, d_bdw = dwconv_bwd(f"dwconv_bwd_{i}", dv, s["glu"], w_dw_full[j])
            small_g["conv_w_dw"][j] = d_wdw.reshape(width, 8, D).sum(axis=1)
            small_g["conv_b_dw"][j] = fold(d_bdw)
            du, du_sum = glu_bwd(f"glu_bwd_{i}", s["u"], dglu, tn_pw1)
            small_g["conv_b_pw1"][j] = _deinterleave_vec(fold(du_sum), tn_pw1)
            t_in = start_scatter("conv_w_pw1", j, mm_tn(f"pw1_wgrad_{i}", s["h_mix"], du, grad_w(w_pw1[j])))
            dh = mm_nt(f"pw1_dgrad_{i}", du, w_pw1[j], out_dtype=BF16)
        else:
            dog = mm_nt(f"gla_wout_dgrad_{i}", dy, w_gout[j])
            t_out = start_scatter("gla_w_out", j, mm_tn(f"gla_wout_wgrad_{i}", s["og"], dy, grad_w(w_gout[j])))
            d_o, d_r, d_ng = gla_out_bwd(f"gla_out_bwd_{i}", dog, s["o"], s["proj"], row(normg_full[j]),
                                         (2 * DK + D) // dvh)
            small_g["gla_norm_g"][j] = fold(d_ng)
            dq, dk_, dv_, dz, d_bg = gla_bwd(f"gla_bwd_{i}", s["proj"], wgu_pad[j], row(bgate_full[j]),
                                             s["states"], d_o)
            small_g["gla_b_gate"][j] = fold(d_bg)
            wgu_w = Wt(wgu_pad[j].astype(BF16), "plain", 0, 128, DK)
            d_a = mm_nt(f"gla_gate_dgrad_{i}", dz, wgu_w, out_dtype=BF16)
            a_low = s["proj"][:, 2 * DK + 2 * D:].astype(BF16)
            d_wgu = mm_tn(f"gla_gate_wgrad_{i}", a_low, dz, Wt(None, "plain", 0, 128, DK), out_dtype=F32)
            small_g["gla_w_gate_up"][j] = d_wgu[:rank]
            dproj = jnp.concatenate([dq, dk_, dv_, d_r, d_a], axis=1)
            g_in = mm_tn(f"gla_in_wgrad_{i}", s["h_mix"], dproj, Wt(None, "plain", 0, D, P))
            t_in = start_scatter("gla_w_in", j, g_in[:, :gla_cols].reshape(D, N_CHIPS, gin_cols).transpose(1, 0, 2))
            dh = mm_nt(f"gla_in_dgrad_{i}", dproj, w_gin[j], out_dtype=BF16)
        dx, d_sh1, d_sc1, d_pg = prenorm_bwd(f"pre_mix_bwd_{i}", dh, s["x_in"], dx, row(pre_mix_g[i]), sc1,
                                             deps=(t_out, t_in))
        small_g["pre_mix_g"][i] = fold(d_pg)
        small_g["b_ada"][i] = jnp.concatenate([fold(t) for t in (d_sh1, d_sc1, d_gt1, d_sh2, d_sc2, d_gt2)])
    grad_x = dx[None]

    small_names = list(small_g)
    local_small = [jnp.stack(small_g[n]) for n in small_names]
    full_shapes = [a.shape for a in local_small]
    small_all = allgather_devices("gather_small_grads", _flat_pack(local_small))
    small_sum = sum_slots("sum_small_grads", small_all)
    small_tot = dict(zip(small_names, _flat_unpack(small_sum, full_shapes)))
    for n in small_sharded:
        cols = weights[n].shape[-1]
        small_tot[n] = lax.dynamic_slice_in_dim(small_tot[n], my_chip * cols, cols, axis=small_tot[n].ndim - 1)

    dmod_all = _flat_unpack(small_all, full_shapes, lead=(N_DEVICES,))[small_names.index("b_ada")]
    dmod_cols = lax.dynamic_slice_in_dim(
        dmod_all.reshape(N_DEVICES, depth, N_CHIPS, ada_cols), my_chip, 1, axis=2)[:, :, 0, :]
    dmod16 = jnp.pad(dmod_cols.reshape(N_DEVICES, depth * ada_cols), ((0, 8), (0, 0))).astype(BF16)
    g_ada = [mm_tn(f"ada_wgrad_{i}", c16, dmod16[:, i * ada_cols:(i + 1) * ada_cols],
                   Wt(None, "plain", 0, D, ada_cols), out_dtype=F32) for i in range(depth)]
    g_ada = jnp.stack(g_ada).reshape(depth * D, ada_cols)

    results = {}

    def flat2(a):
        return a.reshape(-1, a.shape[-1])

    swapping, last = [], ()
    for n in ("conv_w_pw1", "conv_w_pw2", "gla_w_in", "gla_w_out", "ffn_w_in", "ffn_w_out"):
        arrived = [_split_wait(f"scatter_wait_{n}_{l}", scattering[n, l], (dx,)) for l in range(weights[n].shape[0])]
        mine = sum_slots_layers("sum_" + n, arrived)
        handle, token = swap_start("swap_start_" + n, mine, last)
        swapping.append((n, handle))
        last = (token,)
    out = adamw("adamw_w_ada", flat2(w_ada), [g_ada], flat2(m_w_ada), flat2(v_w_ada), deps=last)
    results["w_ada"] = [t.reshape(w_ada.shape) for t in out]
    w_small = _flat_pack([weights[n] for n in small_names])
    out_small = adamw("adamw_small", w_small, [_flat_pack([small_tot[n] for n in small_names])],
                      _flat_pack([mom_m[n] for n in small_names]), _flat_pack([mom_v[n] for n in small_names]))
    for n, handle in swapping:
        w = weights[n]
        mine, theirs = _split_wait("swap_wait_" + n, handle, (out[0],), with_src=True)
        res = adamw("adamw_" + n, flat2(w), [mine, theirs], flat2(mom_m[n]), flat2(mom_v[n]))
        results[n] = [t.reshape(w.shape) for t in res]
    out = out_small
    shapes = [weights[n].shape for n in small_names]
    unpacked = [_flat_unpack(t, shapes) for t in out]
    for q, n in enumerate(small_names):
        results[n] = [unpacked[r][q] for r in range(4)]

    outs = [loss, grad_x]
    for r in range(4):
        outs += [results[n][r] for n in order]
    return tuple(outs)
```

```python
import jax
import jax.numpy as jnp
from jax import lax
from jax.experimental import pallas as pl
from jax.experimental.pallas import tpu as pltpu

F32, BF16 = jnp.float32, jnp.bfloat16
MESH = pl.DeviceIdType.MESH
HBM = pl.BlockSpec(memory_space=pl.ANY)

EPS = 1e-6
VMEM_LIMIT_BYTES = 48 * 1024 * 1024
N_CHIPS = 4
N_DEVICES = 8
GLA_HEADS = 4
GLA_CHUNK = 64
GLA_TAU = 16.0
CONV_HALO = 32
ADAM_LR, ADAM_B1, ADAM_B2, ADAM_EPS, ADAM_WD, ADAM_STEP = 0.001, 0.9, 0.999, 1e-08, 0.01, 10


def _pc(body, **kw):
    return pl.pallas_call(body, **kw)


def _params(*sem):
    return pltpu.CompilerParams(dimension_semantics=sem, vmem_limit_bytes=VMEM_LIMIT_BYTES)


def _pick(n, cands):
    for c in cands:
        if c <= n and n % c == 0:
            return c
    return n


def _fold8(z):
    r, w = z.shape
    return z.reshape(r // 8, 8, w).sum(axis=0)


def _sigmoid(x):
    return 1.0 / (1.0 + jnp.exp(-x))


def _exchange(name, xs, out_shape, masks, src_of, dst_of, local=True):
    n_in = len(xs)

    def body(*refs):
        x_refs, o_ref = refs[:n_in], refs[n_in]
        send_sems, recv_sems, local_sems = refs[n_in + 1:]
        x, y, c = lax.axis_index("x"), lax.axis_index("y"), lax.axis_index("c")
        me = (x, y, c)
        peers = [(1 - x if a else x, 1 - y if b else y, 1 - c if d else c) for a, b, d in masks]
        started = []
        if local:
            for q, (s, t) in enumerate(zip(src_of(x_refs, me), dst_of(o_ref, me))):
                cp = pltpu.make_async_copy(s, t, local_sems.at[q])
                cp.start()
                started.append(cp)
        sends = []
        for k, peer in enumerate(peers):
            for q, (s, t) in enumerate(zip(src_of(x_refs, peer), dst_of(o_ref, me))):
                cp = pltpu.make_async_remote_copy(
                    src_ref=s, dst_ref=t, send_sem=send_sems.at[k, q], recv_sem=recv_sems.at[k, q],
                    device_id=peer, device_id_type=MESH)
                cp.start()
                sends.append(cp)
        for k, peer in enumerate(peers):
            for q, (s, t) in enumerate(zip(src_of(x_refs, me), dst_of(o_ref, peer))):
                pltpu.make_async_remote_copy(
                    src_ref=s, dst_ref=t, send_sem=send_sems.at[k, q], recv_sem=recv_sems.at[k, q],
                    device_id=peer, device_id_type=MESH).wait_recv()
        for cp in sends:
            cp.wait_send()
        for cp in started:
            cp.wait()

    n_q = n_in if len(xs) > 1 else 1
    return _pc(
        body, name=name, out_shape=out_shape,
        in_specs=[HBM] * n_in, out_specs=HBM,
        scratch_shapes=[pltpu.SemaphoreType.DMA((len(masks), n_q)),
                        pltpu.SemaphoreType.DMA((len(masks), n_q)),
                        pltpu.SemaphoreType.DMA((n_q,))],
    )(*xs)


_CHIP_MASKS = [(1, 0, 0), (0, 1, 0), (1, 1, 0)]
_ALL_MASKS = [(a, b, d) for a in (0, 1) for b in (0, 1) for d in (0, 1) if (a, b, d) != (0, 0, 0)]


def _chip(p):
    return 2 * p[0] + p[1]


def _dev(p):
    return 4 * p[0] + 2 * p[1] + p[2]


def allgather_devices(name, v):
    r, c = v.shape
    return _exchange(
        name, [v], jax.ShapeDtypeStruct((N_DEVICES, r, c), v.dtype), _ALL_MASKS,
        lambda xr, peer: [xr[0]], lambda o, src: [o.at[_dev(src)]])


_HBM = pl.BlockSpec(memory_space=pltpu.HBM)
_SEM = pl.BlockSpec(memory_space=pltpu.SEMAPHORE)
_N_PEER_CHIPS = len(_CHIP_MASKS)


def _peer_chips():
    x, y, c = lax.axis_index("x"), lax.axis_index("y"), lax.axis_index("c")
    return (x, y), [((1 - x if a else x, 1 - y if b else y), c) for a, b, _ in _CHIP_MASKS]


def _chip_copies(src_ref, land_ref, send_sems, recv_sems, whole_src, incoming):
    me, peers = _peer_chips()
    out = []
    for k, (chip, c) in enumerate(peers):
        out.append(pltpu.make_async_remote_copy(
            src_ref=src_ref if whole_src else src_ref.at[_chip(chip)],
            dst_ref=land_ref.at[_chip(chip) if incoming else _chip(me)],
            send_sem=send_sems.at[k], recv_sem=recv_sems.at[k], device_id=(*chip, c), device_id_type=MESH))
    return out


def _core_copies(src_ref, land_ref, send_sems, recv_sems, incoming):
    x, y, c = lax.axis_index("x"), lax.axis_index("y"), lax.axis_index("c")
    return [pltpu.make_async_remote_copy(
        src_ref=src_ref, dst_ref=land_ref, send_sem=send_sems.at[0], recv_sem=recv_sems.at[0],
        device_id=(x, y, 1 - c), device_id_type=MESH)]


def _split_start(name, src, land, n_copies, copies, after):
    n_after = len(after)

    def body(*refs):
        src_ref, land_ref = refs[0], refs[1]
        send_sems, recv_sems, _, _, token = refs[2 + n_after:]
        for send in copies(src_ref, land_ref, send_sems, recv_sems, False):
            send.start()
        token[...] = jnp.zeros(token.shape, token.dtype)

    send_sems, recv_sems, src_thru, land_thru, token = _pc(
        body, name=name,
        out_shape=(pltpu.SemaphoreType.DMA((n_copies,)), pltpu.SemaphoreType.DMA((n_copies,)),
                   pltpu.HBM(src.shape, src.dtype), pltpu.HBM(land.shape, land.dtype),
                   jax.ShapeDtypeStruct((8, 128), F32)),
        in_specs=[_HBM, _HBM] + [HBM] * n_after,
        out_specs=(_SEM, _SEM, _HBM, _HBM, pl.BlockSpec(memory_space=pltpu.VMEM)),
        input_output_aliases={0: 2, 1: 3},
        compiler_params=pltpu.CompilerParams(has_side_effects=pltpu.SideEffectType.DATAFLOW_SIDE_EFFECTING),
    )(pltpu.with_memory_space_constraint(src, pltpu.HBM), pltpu.with_memory_space_constraint(land, pltpu.HBM),
      *after)
    return (send_sems, recv_sems, src_thru, land_thru, copies), token


def _split_wait(name, handle, after, with_src=False):
    send_sems, recv_sems, src_thru, land_thru, copies = handle
    n_after = len(after)

    def body(src_ref, land_ref, send_sems, recv_sems, *rest):
        for send in copies(src_ref, land_ref, send_sems, recv_sems, False):
            send.wait_send()
        for recv in copies(src_ref, land_ref, send_sems, recv_sems, True):
            recv.wait_recv()

    src, land = _pc(
        body, name=name,
        out_shape=(pltpu.HBM(src_thru.shape, src_thru.dtype), pltpu.HBM(land_thru.shape, land_thru.dtype)),
        in_specs=[_HBM, _HBM, _SEM, _SEM] + [HBM] * n_after, out_specs=(_HBM, _HBM),
        input_output_aliases={0: 0, 1: 1},
        compiler_params=pltpu.CompilerParams(has_side_effects=pltpu.SideEffectType.DATAFLOW_SIDE_EFFECTING),
    )(src_thru, land_thru, send_sems, recv_sems, *after)
    return (src, land) if with_src else land


def chips_start(name, src, own, my_chip, whole_src, after=()):
    r, c = own.shape
    land = lax.dynamic_update_slice(lax.empty((N_CHIPS, r, c), src.dtype), own[None], (my_chip, 0, 0))

    def copies(src_ref, land_ref, send_sems, recv_sems, incoming):
        return _chip_copies(src_ref, land_ref, send_sems, recv_sems, whole_src, incoming)

    return _split_start(name, src, land, _N_PEER_CHIPS, copies, after)


def swap_start(name, a, after=()):
    return _split_start(name, a, lax.empty(a.shape, a.dtype), 1, _core_copies, after)


class Wt:
    def __init__(self, arr, kind, layer, rows, cols, tn=None):
        self.arr, self.kind, self.l, self.R, self.C, self.tn = arr, kind, layer, rows, cols, tn
        self.K = 4 * rows if kind == "row" else rows
        self.N = 4 * cols if kind == "col" else cols

    def spec(self, tk, tn):
        l, R, C = self.l, self.R, self.C
        if self.kind == "plain":
            off = l * (R // tk)
            return (tk, tn), lambda kb, jb: (off + kb, jb)
        if self.kind == "col":
            assert tn == self.tn
            per, off = C // tn, l * (R // tk)
            return (None, tk, tn), lambda kb, jb: (2 * (jb % 2) + (jb // 2) // per, off + kb, (jb // 2) % per)
        per = R // tk
        return (None, tk, tn), lambda kb, jb: (kb // per, l * per + kb % per, jb)

    def tile_k(self, cands):
        return _pick(self.R, cands)

    def tile_n(self, cands):
        return self.tn if self.kind == "col" else _pick(self.C, cands)


_TM = (1024, 512, 256, 128, 64, 32, 16, 8)
_TK = (2048, 1408, 1024, 896, 512, 256, 128, 64, 32, 16)
_TN = (1024, 896, 768, 512, 384, 256, 128)
_TR = (1024, 1408, 512, 256, 128, 64, 32, 16)


def _accumulate(part, acc_ref, k, nk, finish):
    if nk == 1:
        finish(part)
        return

    @pl.when(k == 0)
    def _():
        acc_ref[...] = part

    @pl.when(k > 0)
    def _():
        acc_ref[...] += part

    @pl.when(k == nk - 1)
    def _():
        finish(acc_ref[...])


def mm_nn(name, a, w, *, bias=None, out_dtype=F32):
    M, K = a.shape
    assert K == w.K
    N = w.N
    tm, tk, tn = _pick(M, _TM), w.tile_k(_TK), w.tile_n(_TN)
    nk = K // tk
    wblock, wmap = w.spec(tk, tn)
    in_specs = [pl.BlockSpec((tm, tk), lambda i, j, k: (i, k)),
                pl.BlockSpec(wblock, lambda i, j, k: wmap(k, j))]
    args = [a, w.arr]
    if bias is not None:
        in_specs.append(pl.BlockSpec((1, tn), lambda i, j, k: (0, j)))
        args.append(bias)

    def body(*refs):
        a_ref, b_ref = refs[0], refs[1]
        bias_ref = refs[2] if bias is not None else None
        o_ref = refs[3] if bias is not None else refs[2]
        acc_ref = refs[-1] if nk > 1 else None
        part = jnp.dot(a_ref[...].astype(BF16), b_ref[...].astype(BF16), preferred_element_type=F32)

        def finish(acc):
            if bias_ref is not None:
                acc = acc + bias_ref[...]
            o_ref[...] = acc.astype(o_ref.dtype)

        _accumulate(part, acc_ref, pl.program_id(2), nk, finish)

    return _pc(
        body, name=name, out_shape=jax.ShapeDtypeStruct((M, N), out_dtype),
        grid=(M // tm, N // tn, nk), in_specs=in_specs,
        out_specs=pl.BlockSpec((tm, tn), lambda i, j, k: (i, j)),
        scratch_shapes=[pltpu.VMEM((tm, tn), F32)] if nk > 1 else [],
        compiler_params=_params("parallel", "parallel", "arbitrary"),
    )(*args)


def mm_nt(name, a, w, *, out_dtype=F32):
    M, N = a.shape
    assert N == w.N
    K = w.K
    tm, tj, tn = _pick(M, _TM), w.tile_k(_TR), w.tile_n(_TK)
    wblock, wmap = w.spec(tj, tn)
    nb = 2 if w.kind == "col" else 1
    nn = N // (nb * tn)

    def body(a_ref, *refs):
        b_refs, o_ref, scr = refs[:nb], refs[nb], refs[nb + 1:]
        part = None
        for q, b_ref in enumerate(b_refs):
            p = lax.dot_general(a_ref[:, q * tn:(q + 1) * tn].astype(BF16), b_ref[...].astype(BF16),
                                (((1,), (1,)), ((), ())), preferred_element_type=F32)
            part = p if part is None else part + p

        def finish(acc):
            o_ref[...] = acc.astype(o_ref.dtype)

        _accumulate(part, scr[0] if nn > 1 else None, pl.program_id(2), nn, finish)

    def b_spec(q):
        return pl.BlockSpec(wblock, lambda i, j, n: wmap(j, nb * n + q))

    return _pc(
        body, name=name, out_shape=jax.ShapeDtypeStruct((M, K), out_dtype),
        grid=(M // tm, K // tj, nn),
        in_specs=[pl.BlockSpec((tm, nb * tn), lambda i, j, n: (i, n))] + [b_spec(q) for q in range(nb)],
        out_specs=pl.BlockSpec((tm, tj), lambda i, j, n: (i, j)),
        scratch_shapes=[pltpu.VMEM((tm, tj), F32)] if nn > 1 else [],
        compiler_params=_params("parallel", "parallel", "arbitrary"),
    )(a, *([w.arr] * nb))


def mm_tn(name, a, b, w, *, out_dtype=BF16):
    T, K = a.shape
    N = b.shape[1]
    assert (K, N) == (w.K, w.N) and w.l == 0
    tm = w.tile_k(_TR)
    tn = w.tile_n(_TN)
    tk = _pick(T, (2048, 1024, 512, 256, 128, 64, 32, 16))
    nk = T // tk
    oblock, omap = w.spec(tm, tn)
    shape = (w.R, w.C) if w.kind == "plain" else (N_CHIPS, w.R, w.C)

    def body(a_ref, b_ref, o_ref, *scr):
        part = lax.dot_general(a_ref[...].astype(BF16), b_ref[...].astype(BF16),
                               (((0,), (0,)), ((), ())), preferred_element_type=F32)

        def finish(acc):
            o_ref[...] = acc.astype(o_ref.dtype)

        _accumulate(part, scr[0] if nk > 1 else None, pl.program_id(2), nk, finish)

    return _pc(
        body, name=name, out_shape=jax.ShapeDtypeStruct(shape, out_dtype),
        grid=(K // tm, N // tn, nk),
        in_specs=[pl.BlockSpec((tk, tm), lambda i, j, k: (k, i)),
                  pl.BlockSpec((tk, tn), lambda i, j, k: (k, j))],
        out_specs=pl.BlockSpec(oblock, lambda i, j, k: omap(i, j)),
        scratch_shapes=[pltpu.VMEM((tm, tn), F32)] if nk > 1 else [],
        compiler_params=_params("parallel", "parallel", "arbitrary"),
    )(a, b)


_TM_FUSED = (512, 256, 128, 64, 32, 16, 8)


def mm_nn_post(name, a, w, x, wvec, *, bias=None):
    M, K = a.shape
    assert K == w.K
    N = w.N
    tm, tk = _pick(M, _TM_FUSED), w.tile_k(_TK)
    rows = _pick(tm, (64, 32, 16, 8))
    nk = K // tk
    wblock, wmap = w.spec(tk, N)
    in_specs = [pl.BlockSpec((tm, tk), lambda i, k: (i, k)),
                pl.BlockSpec(wblock, lambda i, k: wmap(k, 0)),
                pl.BlockSpec((tm, N), lambda i, k: (i, 0)),
                pl.BlockSpec((1, N), lambda i, k: (0, 0))]
    args = [a, w.arr, x, wvec]
    if bias is not None:
        in_specs.append(pl.BlockSpec((1, N), lambda i, k: (0, 0)))
        args.append(bias)

    def body(*refs):
        a_ref, b_ref, x_ref, wv_ref = refs[:4]
        bias_ref = refs[4] if bias is not None else None
        y_ref, xn_ref, acc_ref = refs[-3:]
        part = jnp.dot(a_ref[...].astype(BF16), b_ref[...].astype(BF16), preferred_element_type=F32)
        k = pl.program_id(1)

        @pl.when(k == 0)
        def _():
            acc_ref[...] = part

        @pl.when(k > 0)
        def _():
            acc_ref[...] += part

        @pl.when(k == nk - 1)
        def _():
            for r0 in range(0, tm, rows):
                rr = pl.ds(r0, rows)
                yv = acc_ref[rr, :]
                if bias_ref is not None:
                    yv = yv + bias_ref[...]
                y_ref[rr, :] = yv.astype(BF16)
                xn_ref[rr, :] = x_ref[rr, :] + yv * _rms(yv) * wv_ref[...]

    spec = pl.BlockSpec((tm, N), lambda i, k: (i, 0))
    return _pc(
        body, name=name, out_shape=[jax.ShapeDtypeStruct((M, N), BF16), jax.ShapeDtypeStruct((M, N), F32)],
        grid=(M // tm, nk), in_specs=in_specs, out_specs=[spec, spec],
        scratch_shapes=[pltpu.VMEM((tm, N), F32)],
        compiler_params=_params("parallel", "arbitrary"),
    )(*args)


def gated_in_fwd(name, h, w, unit, *, bias=None):
    M, K = h.shape
    assert w.kind == "col" and K == w.K == w.R
    tn, F = w.tn, w.N // 2
    tm = _pick(M, _TM_FUSED)
    wblock, wmap = w.spec(K, tn)
    in_specs = [pl.BlockSpec((tm, K), lambda g, i: (i, 0)),
                pl.BlockSpec(wblock, lambda g, i: wmap(0, 2 * g)),
                pl.BlockSpec(wblock, lambda g, i: wmap(0, 2 * g + 1))]
    args = [h, w.arr, w.arr]
    if bias is not None:
        in_specs.append(pl.BlockSpec((1, 2 * tn), lambda g, i: (0, g)))
        args.append(bias)

    def body(h_ref, w1_ref, w2_ref, *refs):
        u_ref, act_ref = refs[-2:]
        hv = h_ref[...].astype(BF16)
        first = jnp.dot(hv, w1_ref[...].astype(BF16), preferred_element_type=F32)
        second = jnp.dot(hv, w2_ref[...].astype(BF16), preferred_element_type=F32)
        if bias is not None:
            first, second = first + refs[0][:, :tn], second + refs[0][:, tn:]
        u_ref[:, :tn] = first.astype(BF16)
        u_ref[:, tn:] = second.astype(BF16)
        if unit == "swiglu":
            act_ref[...] = (first * _sigmoid(first) * second).astype(act_ref.dtype)
        else:
            act_ref[...] = (first * _sigmoid(second)).astype(act_ref.dtype)

    return _pc(
        body, name=name,
        out_shape=[jax.ShapeDtypeStruct((M, 2 * F), BF16),
                   jax.ShapeDtypeStruct((M, F), BF16 if unit == "swiglu" else F32)],
        grid=(F // tn, M // tm), in_specs=in_specs,
        out_specs=[pl.BlockSpec((tm, 2 * tn), lambda g, i: (i, g)),
                   pl.BlockSpec((tm, tn), lambda g, i: (i, g))],
        compiler_params=_params("parallel", "parallel"),
    )(*args)


def ffn_out_dgrad(name, dy, w, u):
    M, N = dy.shape
    assert w.kind == "row" and N == w.N == w.C
    tj, F = w.R, w.K
    tm = _pick(M, _TM_FUSED)
    wblock, wmap = w.spec(tj, N)

    def body(dy_ref, b_ref, u_ref, du_ref):
        d = lax.dot_general(dy_ref[...].astype(BF16), b_ref[...].astype(BF16),
                            (((1,), (1,)), ((), ())), preferred_element_type=F32)
        gate, up = u_ref[:, :tj].astype(F32), u_ref[:, tj:].astype(F32)
        sg = _sigmoid(gate)
        du_ref[:, :tj] = (d * up * (sg * (1.0 + gate * (1.0 - sg)))).astype(BF16)
        du_ref[:, tj:] = (d * gate * sg).astype(BF16)

    return _pc(
        body, name=name, out_shape=jax.ShapeDtypeStruct((M, 2 * F), BF16),
        grid=(F // tj, M // tm),
        in_specs=[pl.BlockSpec((tm, N), lambda j, i: (i, 0)),
                  pl.BlockSpec(wblock, lambda j, i: wmap(j, 0)),
                  pl.BlockSpec((tm, 2 * tj), lambda j, i: (i, j))],
        out_specs=pl.BlockSpec((tm, 2 * tj), lambda j, i: (i, j)),
        compiler_params=_params("parallel", "parallel"),
    )(dy, w.arr, u)


def _rows(name, body, rows_in, vecs_in, rows_out, accs_out, tm=256, deps=()):
    T = rows_in[0].shape[0]
    tm = _pick(T, (tm, 128, 64, 32, 16, 8))
    n_r, n_v, n_o, n_d = len(rows_in), len(vecs_in), len(rows_out), len(deps)

    def kern(*refs):
        r_refs, v_refs = refs[:n_r], refs[n_r:n_r + n_v]
        refs = refs[n_r + n_v + n_d:]
        o_refs, a_refs = refs[:n_o], refs[n_o:]

        @pl.when(pl.program_id(0) == 0)
        def _():
            for a_ref in a_refs:
                a_ref[...] = jnp.zeros(a_ref.shape, F32)

        body(r_refs, v_refs, o_refs, a_refs)

    in_specs = [pl.BlockSpec((tm, a.shape[1]), lambda i: (i, 0)) for a in rows_in]
    in_specs += [pl.BlockSpec(v.shape, lambda i: (0, 0)) for v in vecs_in]
    in_specs += [HBM] * n_d
    out_shape = [jax.ShapeDtypeStruct((T, w), dt) for w, dt in rows_out]
    out_shape += [jax.ShapeDtypeStruct((8, w), F32) for w in accs_out]
    out_specs = [pl.BlockSpec((tm, w), lambda i: (i, 0)) for w, _ in rows_out]
    out_specs += [pl.BlockSpec((8, w), lambda i: (0, 0)) for w in accs_out]
    return _pc(kern, name=name, out_shape=out_shape, grid=(T // tm,), in_specs=in_specs,
               out_specs=out_specs, compiler_params=_params("arbitrary"))(*rows_in, *vecs_in, *deps)


def _rms(x):
    return lax.rsqrt(jnp.mean(x * x, axis=-1, keepdims=True) + EPS)


def prenorm_fwd(name, x, gain, shift, deps=()):
    def body(r, v, o, a):
        xv = r[0][...]
        o[0][...] = (xv * _rms(xv) * v[0][...] + v[1][...]).astype(BF16)
    return _rows(name, body, [x], [gain, shift], [(x.shape[1], BF16)], [], deps=deps)[0]


def postnorm_bwd(name, dxo, y, post_g, gate):
    def body(r, v, o, a):
        d, yv = r[0][...], r[1][...].astype(F32)
        pg, gt = v[0][...], v[1][...]
        ry = _rms(yv)
        yn = yv * ry
        t = d * yn
        dyn = d * (gt * pg)
        dy = ry * (dyn - yn * jnp.mean(dyn * yn, axis=-1, keepdims=True))
        o[0][...] = dy.astype(BF16)
        a[0][...] += _fold8(t * pg)
        a[1][...] += _fold8(t * gt)
        a[2][...] += _fold8(dy)
    D = y.shape[1]
    return _rows(name, body, [dxo, y], [post_g, gate], [(D, BF16)], [D, D, D])


def prenorm_bwd(name, dh, x, dxo, pre_g, scale, deps=()):
    def body(r, v, o, a):
        dhv, xv, d = r[0][...].astype(F32), r[1][...], r[2][...]
        pg, sc1 = v[0][...], 1.0 + v[1][...]
        rx = _rms(xv)
        xn = xv * rx
        t = dhv * xn
        dxn = dhv * (pg * sc1)
        o[0][...] = d + rx * (dxn - xn * jnp.mean(dxn * xn, axis=-1, keepdims=True))
        a[0][...] += _fold8(dhv)
        a[1][...] += _fold8(t * pg)
        a[2][...] += _fold8(t * sc1)
    D = x.shape[1]
    return _rows(name, body, [dh, x, dxo], [pre_g, scale], [(D, F32)], [D, D, D], deps=deps)


def norm_chain_bwd(name, dh, x, dxo, pre_g, scale, y, post_g, gate, deps=()):
    def body(r, v, o, a):
        dhv, xv, d, yv = r[0][...].astype(F32), r[1][...], r[2][...], r[3][...].astype(F32)
        pg, sc1, pg2, gt = v[0][...], 1.0 + v[1][...], v[2][...], v[3][...]
        rx = _rms(xv)
        xn = xv * rx
        t = dhv * xn
        dxn = dhv * (pg * sc1)
        dx = d + rx * (dxn - xn * jnp.mean(dxn * xn, axis=-1, keepdims=True))
        o[0][...] = dx
        a[0][...] += _fold8(dhv)
        a[1][...] += _fold8(t * pg)
        a[2][...] += _fold8(t * sc1)
        ry = _rms(yv)
        yn = yv * ry
        t2 = dx * yn
        dyn = dx * (gt * pg2)
        dy = ry * (dyn - yn * jnp.mean(dyn * yn, axis=-1, keepdims=True))
        o[1][...] = dy.astype(BF16)
        a[3][...] += _fold8(t2 * pg2)
        a[4][...] += _fold8(t2 * gt)
        a[5][...] += _fold8(dy)
    D = x.shape[1]
    return _rows(name, body, [dh, x, dxo, y], [pre_g, scale, post_g, gate], [(D, F32), (D, BF16)], [D] * 6,
                 deps=deps)


def loss_bwd(name, y, target):
    D = y.shape[1]

    def body(r, v, o, a):
        e = r[0][...] - r[1][...]
        o[0][...] = e * (1.0 / D)
        a[0][...] += _fold8(e * e * (0.5 / D))
    return _rows(name, body, [y, target], [], [(D, F32)], [D])


def ln_silu_fwd(name, v_, g, b):
    def body(r, v, o, a):
        x = r[0][...]
        xc = x - jnp.mean(x, axis=-1, keepdims=True)
        ln = xc * lax.rsqrt(jnp.mean(xc * xc, axis=-1, keepdims=True) + EPS) * v[0][...] + v[1][...]
        o[0][...] = (ln * _sigmoid(ln)).astype(BF16)
    return _rows(name, body, [v_], [g, b], [(v_.shape[1], BF16)], [])[0]


def ln_silu_bwd(name, ds, v_, g, b):
    def body(r, v, o, a):
        dsv, x = r[0][...], r[1][...]
        xc = x - jnp.mean(x, axis=-1, keepdims=True)
        rstd = lax.rsqrt(jnp.mean(xc * xc, axis=-1, keepdims=True) + EPS)
        xh = xc * rstd
        ln = xh * v[0][...] + v[1][...]
        sg = _sigmoid(ln)
        dln = dsv * (sg * (1.0 + ln * (1.0 - sg)))
        dxh = dln * v[0][...]
        o[0][...] = rstd * (dxh - jnp.mean(dxh, axis=-1, keepdims=True)
                            - xh * jnp.mean(dxh * xh, axis=-1, keepdims=True))
        a[0][...] += _fold8(dln * xh)
        a[1][...] += _fold8(dln)
    D = v_.shape[1]
    return _rows(name, body, [ds, v_], [g, b], [(D, F32)], [D, D])


def _pairs(name, body, u, others, out_w, out_dtype, tn, n_acc=0, tm=256):
    T, F2 = u.shape
    F = F2 // 2
    tm = _pick(T, (tm, 128, 64, 32, 16, 8))

    def kern(*refs):
        u_ref, o_refs = refs[0], refs[1:1 + len(others)]
        out_ref, acc_refs = refs[1 + len(others)], refs[2 + len(others):]

        @pl.when(pl.program_id(1) == 0)
        def _():
            for a_ref in acc_refs:
                a_ref[...] = jnp.zeros(a_ref.shape, F32)

        body(u_ref, o_refs, out_ref, acc_refs)

    out_shape = [jax.ShapeDtypeStruct((T, out_w * F), out_dtype)]
    out_shape += [jax.ShapeDtypeStruct((8, F2), F32)] * n_acc
    out_specs = [pl.BlockSpec((tm, out_w * tn), lambda g, i: (i, g))]
    out_specs += [pl.BlockSpec((8, 2 * tn), lambda g, i: (0, g))] * n_acc
    return _pc(kern, name=name, out_shape=out_shape, grid=(F // tn, T // tm),
               in_specs=[pl.BlockSpec((tm, 2 * tn), lambda g, i: (i, g))]
               + [pl.BlockSpec((tm, tn), lambda g, i: (i, g))] * len(others),
               out_specs=out_specs, compiler_params=_params("parallel", "arbitrary"))(u, *others)


def glu_bwd(name, u, dglu, tn):
    def body(u_ref, o, out, acc):
        a, g = u_ref[:, :tn].astype(F32), u_ref[:, tn:].astype(F32)
        d = o[0][...]
        sg = _sigmoid(g)
        da, dg = d * sg, d * a * (sg * (1.0 - sg))
        out[:, :tn] = da.astype(BF16)
        out[:, tn:] = dg.astype(BF16)
        acc[0][:, :tn] += _fold8(da)
        acc[0][:, tn:] += _fold8(dg)
    return _pairs(name, body, u, [dglu], 2, BF16, tn, n_acc=1)


_CONV_ROWS, _CONV_LANES = 64, 128
_SUBLANES = 8


def _fill_shifts(ext, sh, n):
    for r in range(1, _SUBLANES):
        sh[r - 1, pl.ds(0, n), :] = ext[pl.ds(r, n), :]


def _tap(ext, sh, o, r0, rows, ln):
    q, r = divmod(o, _SUBLANES)
    if r == 0:
        return ext[pl.ds(r0 + o, rows), ln]
    return sh[r - 1, pl.ds(r0 + _SUBLANES * q, rows), ln]


def dwconv_fwd(name, x, w, b):
    T, D = x.shape
    width = w.shape[0]
    tt, cb = _pick(T, (256, 128, 64)), _pick(D, (256, 128))
    off = CONV_HALO - (width - 1)
    rows = min(_CONV_ROWS, tt)
    n_sh = tt + CONV_HALO - _SUBLANES

    def body(cur_ref, prev_ref, w_ref, b_ref, o_ref, ext, sh):
        t = pl.program_id(1)
        tail = prev_ref[pl.ds(tt - CONV_HALO, CONV_HALO), :]
        ext[pl.ds(0, CONV_HALO), :] = jnp.where(t > 0, tail, 0.0)
        ext[pl.ds(CONV_HALO, tt), :] = cur_ref[...]
        _fill_shifts(ext, sh, n_sh)
        for l0 in range(0, cb, _CONV_LANES):
            ln = pl.ds(l0, _CONV_LANES)
            for r0 in range(0, tt, rows):
                acc = jnp.broadcast_to(b_ref[:, ln], (rows, _CONV_LANES))
                for k in range(width):
                    acc = acc + _tap(ext, sh, off + k, r0, rows, ln) * w_ref[pl.ds(k, 1), ln]
                o_ref[pl.ds(r0, rows), ln] = acc

    return _pc(
        body, name=name, out_shape=jax.ShapeDtypeStruct((T, D), F32), grid=(D // cb, T // tt),
        in_specs=[pl.BlockSpec((tt, cb), lambda c, t: (t, c)),
                  pl.BlockSpec((tt, cb), lambda c, t: (jnp.maximum(t - 1, 0), c)),
                  pl.BlockSpec((width, cb), lambda c, t: (0, c)),
                  pl.BlockSpec((1, cb), lambda c, t: (0, c))],
        out_specs=pl.BlockSpec((tt, cb), lambda c, t: (t, c)),
        scratch_shapes=[pltpu.VMEM((tt + CONV_HALO, cb), F32), pltpu.VMEM((_SUBLANES - 1, tt + CONV_HALO, cb), F32)],
        compiler_params=_params("parallel", "arbitrary"),
    )(x, x, w, b)


def dwconv_bwd(name, dv, x, w):
    T, D = x.shape
    width = w.shape[0]
    tt, cb = _pick(T, (256, 128, 64)), _pick(D, (256, 128))
    off = CONV_HALO - (width - 1)
    rows = min(_CONV_ROWS, tt)
    nt = T // tt
    n_sh = tt + CONV_HALO - _SUBLANES

    def body(dv_ref, dvn_ref, x_ref, xp_ref, w_ref, dx_ref, dw_ref, db_ref, ext_d, ext_x, sh_d, sh_x):
        t = pl.program_id(1)

        @pl.when(t == 0)
        def _():
            dw_ref[...] = jnp.zeros(dw_ref.shape, F32)
            db_ref[...] = jnp.zeros(db_ref.shape, F32)

        ext_d[pl.ds(0, tt), :] = dv_ref[...]
        ext_d[pl.ds(tt, CONV_HALO), :] = jnp.where(t < nt - 1, dvn_ref[pl.ds(0, CONV_HALO), :], 0.0)
        ext_x[pl.ds(0, CONV_HALO), :] = jnp.where(t > 0, xp_ref[pl.ds(tt - CONV_HALO, CONV_HALO), :], 0.0)
        ext_x[pl.ds(CONV_HALO, tt), :] = x_ref[...]
        db_ref[...] += _fold8(dv_ref[...])
        _fill_shifts(ext_d, sh_d, n_sh)
        _fill_shifts(ext_x, sh_x, n_sh)
        for l0 in range(0, cb, _CONV_LANES):
            ln = pl.ds(l0, _CONV_LANES)
            for r0 in range(0, tt, rows):
                acc = jnp.zeros((rows, _CONV_LANES), F32)
                for k in range(width):
                    acc = acc + _tap(ext_d, sh_d, (width - 1) - k, r0, rows, ln) * w_ref[pl.ds(k, 1), ln]
                dx_ref[pl.ds(r0, rows), ln] = acc
            sums = [jnp.zeros((8, _CONV_LANES), F32)] * width
            for r0 in range(0, tt, rows):
                dvb = ext_d[pl.ds(r0, rows), ln]
                sums = [sums[k] + _fold8(dvb * _tap(ext_x, sh_x, off + k, r0, rows, ln)) for k in range(width)]
            for k in range(width):
                dw_ref[pl.ds(8 * k, 8), ln] += sums[k]

    return _pc(
        body, name=name,
        out_shape=[jax.ShapeDtypeStruct((T, D), F32), jax.ShapeDtypeStruct((8 * width, D), F32),
                   jax.ShapeDtypeStruct((8, D), F32)],
        grid=(D // cb, nt),
        in_specs=[pl.BlockSpec((tt, cb), lambda c, t: (t, c)),
                  pl.BlockSpec((tt, cb), lambda c, t: (jnp.minimum(t + 1, nt - 1), c)),
                  pl.BlockSpec((tt, cb), lambda c, t: (t, c)),
                  pl.BlockSpec((tt, cb), lambda c, t: (jnp.maximum(t - 1, 0), c)),
                  pl.BlockSpec((width, cb), lambda c, t: (0, c))],
        out_specs=[pl.BlockSpec((tt, cb), lambda c, t: (t, c)),
                   pl.BlockSpec((8 * width, cb), lambda c, t: (0, c)),
                   pl.BlockSpec((8, cb), lambda c, t: (0, c))],
        scratch_shapes=[pltpu.VMEM((tt + CONV_HALO, cb), F32), pltpu.VMEM((tt + CONV_HALO, cb), F32),
                        pltpu.VMEM((_SUBLANES - 1, tt + CONV_HALO, cb), F32),
                        pltpu.VMEM((_SUBLANES - 1, tt + CONV_HALO, cb), F32)],
        compiler_params=_params("parallel", "arbitrary"),
    )(dv, dv, x, x, w)


def _gla_dims(proj, wgu):
    DK = wgu.shape[1]
    DV = (proj.shape[1] - 128 - 2 * DK) // 2
    return DK, DV, DK // GLA_HEADS, DV // GLA_HEADS


def _gla_decay(a_ref, w_ref, bg_ref):
    C = GLA_CHUNK
    z = jnp.dot(a_ref[...].astype(BF16), w_ref[...].astype(BF16), preferred_element_type=F32) + bg_ref[...]
    g = (jnp.minimum(z, 0.0) - jnp.log(1.0 + jnp.exp(-jnp.abs(z)))) * (1.0 / GLA_TAU)
    row = lax.broadcasted_iota(jnp.int32, (C, C), 0)
    col = lax.broadcasted_iota(jnp.int32, (C, C), 1)
    bc = jnp.dot((row >= col).astype(F32), g, precision=lax.Precision.HIGHEST, preferred_element_type=F32)
    last = lax.broadcasted_iota(jnp.int32, bc.shape, 0) == C - 1
    b_last = jnp.sum(jnp.where(last, bc, 0.0), axis=0, keepdims=True)
    return z, bc, b_last


def _gla_in_specs(wgu, DK, DV, cidx):
    C = GLA_CHUNK
    return [pl.BlockSpec((C, DK), lambda c: (cidx(c), 0)),
            pl.BlockSpec((C, DK), lambda c: (cidx(c), 1)),
            pl.BlockSpec((C, DV), lambda c: (cidx(c), (2 * DK) // DV)),
            pl.BlockSpec((C, 128), lambda c: (cidx(c), (2 * DK + 2 * DV) // 128)),
            pl.BlockSpec(wgu.shape, lambda c: (0, 0)),
            pl.BlockSpec((1, DK), lambda c: (0, 0))]


def _dot_nt(a, b):
    return lax.dot_general(a.astype(BF16), b.astype(BF16), (((1,), (1,)), ((), ())), preferred_element_type=F32)


def _dot_tn(a, b):
    return lax.dot_general(a.astype(BF16), b.astype(BF16), (((0,), (0,)), ((), ())), preferred_element_type=F32)


def _dot(a, b):
    return jnp.dot(a.astype(BF16), b.astype(BF16), preferred_element_type=F32)


def gla_fwd(name, proj, wgu, bg):
    T = proj.shape[0]
    C, H = GLA_CHUNK, GLA_HEADS
    DK, DV, dk, dv = _gla_dims(proj, wgu)
    nc = T // C
    scale = dk ** -0.5

    def body(q_ref, k_ref, v_ref, a_ref, w_ref, bg_ref, o_ref, s_ref, st):
        @pl.when(pl.program_id(0) == 0)
        def _():
            st[...] = jnp.zeros(st.shape, F32)

        _, bc, b_last = _gla_decay(a_ref, w_ref, bg_ref)
        k = k_ref[...]
        qe = (q_ref[...] * scale * jnp.exp(bc)).astype(BF16)
        ke = (k * jnp.exp(-bc)).astype(BF16)
        kd = (k * jnp.exp(b_last - bc)).astype(BF16)
        el = jnp.exp(b_last)
        row = lax.broadcasted_iota(jnp.int32, (C, C), 0)
        col = lax.broadcasted_iota(jnp.int32, (C, C), 1)
        for h in range(H):
            kk, vv = slice(h * dk, (h + 1) * dk), slice(h * dv, (h + 1) * dv)
            sp = st[h]
            s_ref[h] = sp.astype(BF16)
            v = v_ref[:, vv]
            att = jnp.where(row >= col, _dot_nt(qe[:, kk], ke[:, kk]), 0.0)
            o_ref[:, vv] = _dot_nt(qe[:, kk], sp) + _dot(att, v)
            st[h] = sp * el[:, kk] + _dot_tn(v, kd[:, kk])

    return _pc(
        body, name=name,
        out_shape=[jax.ShapeDtypeStruct((T, DV), F32), jax.ShapeDtypeStruct((H, nc, dv, dk), BF16)],
        grid=(nc,), in_specs=_gla_in_specs(wgu, DK, DV, lambda c: c),
        out_specs=[pl.BlockSpec((C, DV), lambda c: (c, 0)),
                   pl.BlockSpec((H, None, dv, dk), lambda c: (0, c, 0, 0))],
        scratch_shapes=[pltpu.VMEM((H, dv, dk), F32)],
        compiler_params=_params("arbitrary"),
    )(proj, proj, proj, proj, wgu, bg)


def gla_bwd(name, proj, wgu, bg, states, do, d_r):
    T = proj.shape[0]
    C, H = GLA_CHUNK, GLA_HEADS
    DK, DV, dk, dv = _gla_dims(proj, wgu)
    nc = T // C
    scale = dk ** -0.5
    rev = lambda c: nc - 1 - c

    def body(q_ref, k_ref, v_ref, a_ref, w_ref, bg_ref, s_ref, do_ref, dr_ref,
             dp_ref, dz_ref, dbg_ref, dst, db_scr):
        @pl.when(pl.program_id(0) == 0)
        def _():
            dst[...] = jnp.zeros(dst.shape, F32)
            dbg_ref[...] = jnp.zeros(dbg_ref.shape, F32)

        z, bc, b_last = _gla_decay(a_ref, w_ref, bg_ref)
        k = k_ref[...]
        eb, enb, ed, el = jnp.exp(bc), jnp.exp(-bc), jnp.exp(b_last - bc), jnp.exp(b_last)
        qe, ke, kd = q_ref[...] * scale * eb, k * enb, k * ed
        row = lax.broadcasted_iota(jnp.int32, (C, C), 0)
        col = lax.broadcasted_iota(jnp.int32, (C, C), 1)
        keep = row >= col
        last = lax.broadcasted_iota(jnp.int32, (C, dk), 0) == C - 1
        for h in range(H):
            kk, vv = slice(h * dk, (h + 1) * dk), slice(h * dv, (h + 1) * dv)
            qe_h, ke_h, kd_h, el_h = qe[:, kk], ke[:, kk], kd[:, kk], el[:, kk]
            v, d_o = v_ref[:, vv], do_ref[:, vv]
            sp = s_ref[h].astype(F32)
            ds_ = dst[h]
            att = jnp.where(keep, _dot_nt(qe_h, ke_h), 0.0)
            datt = jnp.where(keep, _dot_nt(d_o, v), 0.0)
            dqe = _dot(d_o, sp) + _dot(datt, ke_h)
            dke = _dot_tn(datt, qe_h)
            dp_ref[:, 2 * DK + h * dv:2 * DK + (h + 1) * dv] = (_dot_tn(att, d_o) + _dot_nt(kd_h, ds_)).astype(BF16)
            dkd = _dot(v, ds_)
            dst[h] = ds_ * el_h + _dot_tn(d_o, qe_h)
            dp_ref[:, kk] = (dqe * scale * eb[:, kk]).astype(BF16)
            dp_ref[:, DK + h * dk:DK + (h + 1) * dk] = (dke * enb[:, kk] + dkd * ed[:, kk]).astype(BF16)
            d_el = jnp.sum(sp * ds_, axis=0, keepdims=True)
            db_last = jnp.sum(dkd * kd_h, axis=0, keepdims=True) + d_el * el_h
            db_scr[:, kk] = dqe * qe_h - dke * ke_h - dkd * kd_h + jnp.where(last, db_last, 0.0)
        dg = jnp.dot((row <= col).astype(F32), db_scr[...], precision=lax.Precision.HIGHEST,
                     preferred_element_type=F32)
        dz = dg * (1.0 / GLA_TAU) * _sigmoid(-z)
        dz_ref[...] = dz.astype(BF16)
        dbg_ref[...] += _fold8(dz)
        dp_ref[:, 2 * DK + DV:2 * DK + 2 * DV] = dr_ref[...]
        dp_ref[:, 2 * DK + 2 * DV:] = _dot_nt(dz, w_ref[...]).astype(BF16)

    return _pc(
        body, name=name,
        out_shape=[jax.ShapeDtypeStruct((T, proj.shape[1]), BF16), jax.ShapeDtypeStruct((T, DK), BF16),
                   jax.ShapeDtypeStruct((8, DK), F32)],
        grid=(nc,),
        in_specs=_gla_in_specs(wgu, DK, DV, rev)
        + [pl.BlockSpec((H, None, dv, dk), lambda c: (0, rev(c), 0, 0)),
           pl.BlockSpec((C, DV), lambda c: (rev(c), 0)),
           pl.BlockSpec((C, DV), lambda c: (rev(c), 0))],
        out_specs=[pl.BlockSpec((C, proj.shape[1]), lambda c: (rev(c), 0)),
                   pl.BlockSpec((C, DK), lambda c: (rev(c), 0)),
                   pl.BlockSpec((8, DK), lambda c: (0, 0))],
        scratch_shapes=[pltpu.VMEM((H, dv, dk), F32), pltpu.VMEM((C, DK), F32)],
        compiler_params=_params("arbitrary"),
    )(proj, proj, proj, proj, wgu, bg, states, do, d_r)


def gla_out_fwd(name, o, proj, norm_g, r_block0):
    T, DV = o.shape
    dv = DV // GLA_HEADS
    tm = _pick(T, (512, 256, 128, 64))

    def body(o_ref, r_ref, g_ref, out_ref):
        ov, rv = o_ref[...], r_ref[...]
        out_ref[...] = (ov * _rms(ov) * g_ref[...] * (rv * _sigmoid(rv))).astype(BF16)

    return _pc(
        body, name=name, out_shape=jax.ShapeDtypeStruct((T, DV), BF16), grid=(GLA_HEADS, T // tm),
        in_specs=[pl.BlockSpec((tm, dv), lambda h, i: (i, h)),
                  pl.BlockSpec((tm, dv), lambda h, i: (i, r_block0 + h)),
                  pl.BlockSpec((1, dv), lambda h, i: (0, h))],
        out_specs=pl.BlockSpec((tm, dv), lambda h, i: (i, h)),
        compiler_params=_params("parallel", "parallel"),
    )(o, proj, norm_g)


def gla_out_bwd(name, dog, o, proj, norm_g, r_block0):
    T, DV = o.shape
    dv = DV // GLA_HEADS
    tm = _pick(T, (512, 256, 128, 64))

    def body(d_ref, o_ref, r_ref, g_ref, do_ref, dr_ref, dg_ref):
        @pl.when(pl.program_id(1) == 0)
        def _():
            dg_ref[...] = jnp.zeros(dg_ref.shape, F32)

        d, ov, rv, g = d_ref[...], o_ref[...], r_ref[...], g_ref[...]
        ro = _rms(ov)
        oh = ov * ro
        sg = _sigmoid(rv)
        dn = d * (rv * sg)
        dr_ref[...] = (d * (oh * g) * (sg * (1.0 + rv * (1.0 - sg)))).astype(BF16)
        doh = dn * g
        do_ref[...] = ro * (doh - oh * jnp.mean(doh * oh, axis=-1, keepdims=True))
        dg_ref[...] += _fold8(dn * oh)

    return _pc(
        body, name=name,
        out_shape=[jax.ShapeDtypeStruct((T, DV), F32), jax.ShapeDtypeStruct((T, DV), BF16),
                   jax.ShapeDtypeStruct((8, DV), F32)],
        grid=(GLA_HEADS, T // tm),
        in_specs=[pl.BlockSpec((tm, dv), lambda h, i: (i, h)),
                  pl.BlockSpec((tm, dv), lambda h, i: (i, h)),
                  pl.BlockSpec((tm, dv), lambda h, i: (i, r_block0 + h)),
                  pl.BlockSpec((1, dv), lambda h, i: (0, h))],
        out_specs=[pl.BlockSpec((tm, dv), lambda h, i: (i, h)),
                   pl.BlockSpec((tm, dv), lambda h, i: (i, h)),
                   pl.BlockSpec((8, dv), lambda h, i: (0, h))],
        compiler_params=_params("parallel", "arbitrary"),
    )(dog, o, proj, norm_g)


def _ew_rows(r, c):
    return _pick(r, tuple(t for t in (512, 256, 128, 64, 32, 16, 8) if t * c <= 256 * 1024) or (8,))


def sum_slots_layers(name, lands):
    nl = len(lands)
    n, r, c = lands[0].shape
    tr = _ew_rows(r, c)
    nb = r // tr

    def body(*refs):
        o_ref = refs[nl]
        for l in range(nl):
            @pl.when(pl.program_id(0) == l)
            def _(p_ref=refs[l]):
                acc = p_ref[0].astype(F32)
                for s in range(1, n):
                    acc = acc + p_ref[s].astype(F32)
                o_ref[...] = acc.astype(o_ref.dtype)

    def in_map(l):
        return lambda q, i: (0, jnp.where(q == l, i, 0), 0)

    return _pc(body, name=name, out_shape=jax.ShapeDtypeStruct((nl * r, c), BF16), grid=(nl, nb),
               in_specs=[pl.BlockSpec((n, tr, c), in_map(l)) for l in range(nl)],
               out_specs=pl.BlockSpec((tr, c), lambda q, i: (q * nb + i, 0)),
               compiler_params=_params("arbitrary", "arbitrary"))(*lands)


def sum_slots(name, p):
    n, r, c = p.shape
    tr = _ew_rows(r, c)

    def body(p_ref, o_ref):
        acc = p_ref[0].astype(F32)
        for s in range(1, n):
            acc = acc + p_ref[s].astype(F32)
        o_ref[...] = acc

    return _pc(body, name=name, out_shape=jax.ShapeDtypeStruct((r, c), F32), grid=(r // tr,),
               in_specs=[pl.BlockSpec((n, tr, c), lambda i: (0, i, 0))],
               out_specs=pl.BlockSpec((tr, c), lambda i: (i, 0)),
               compiler_params=_params("parallel"))(p)


def adamw(name, w, gs, m, v, deps=()):
    r, c = w.shape
    tr = _ew_rows(r, c)
    n_g = len(gs)
    m_corr = 1.0 / (1.0 - ADAM_B1 ** ADAM_STEP)
    v_corr = 1.0 / (1.0 - ADAM_B2 ** ADAM_STEP)

    def body(*refs):
        w_ref, g_refs, m_ref, v_ref = refs[0], refs[1:1 + n_g], refs[1 + n_g], refs[2 + n_g]
        g_out, d_out, m_out, v_out = refs[-4:]
        g = g_refs[0][...].astype(F32)
        if n_g == 2:
            g = g + g_refs[1][...].astype(F32)
        mn = ADAM_B1 * m_ref[...] + (1.0 - ADAM_B1) * g
        vn = ADAM_B2 * v_ref[...] + (1.0 - ADAM_B2) * (g * g)
        g_out[...] = g
        m_out[...] = mn
        v_out[...] = vn
        d_out[...] = -ADAM_LR * ((mn * m_corr) / (jnp.sqrt(vn * v_corr) + ADAM_EPS) + ADAM_WD * w_ref[...])

    spec = pl.BlockSpec((tr, c), lambda i: (i, 0))
    return _pc(body, name=name, out_shape=[jax.ShapeDtypeStruct((r, c), F32)] * 4, grid=(r // tr,),
               in_specs=[spec] * (3 + n_g) + [HBM] * len(deps), out_specs=[spec] * 4,
               compiler_params=_params("parallel"))(w, *gs, m, v, *deps)


def _flat_pack(arrs):
    flat = jnp.concatenate([a.reshape(-1).astype(F32) for a in arrs])
    n = flat.shape[0]
    pad = (-n) % 1024
    return jnp.pad(flat, (0, pad)).reshape(-1, 128)


def _flat_unpack(packed, shapes, lead=()):
    flat = packed.reshape(lead + (-1,))
    out, pos = [], 0
    for s in shapes:
        n = 1
        for d in s:
            n *= d
        out.append(flat[..., pos:pos + n].reshape(lead + tuple(s)))
        pos += n
    return out


def _interleave_vec(b, tn):
    f = b.shape[-1] // 2
    return b.reshape(2, f // tn, tn).transpose(1, 0, 2).reshape(1, 2 * f)


def _deinterleave_vec(b, tn):
    f = b.shape[-1] // 2
    return b.reshape(f // tn, 2, tn).transpose(1, 0, 2).reshape(2 * f)


def kernel(x, c, w_ada, b_ada, pre_mix_g, post_mix_g, pre_ffn_g, post_ffn_g, conv_w_pw1, conv_b_pw1, conv_w_dw, conv_b_dw, conv_ln_g, conv_ln_b, conv_w_pw2, conv_b_pw2, gla_w_in, gla_w_gate_up, gla_b_gate, gla_norm_g, gla_w_out, ffn_w_in, ffn_w_out, loss_target, m_w_ada, m_b_ada, m_pre_mix_g, m_post_mix_g, m_pre_ffn_g, m_post_ffn_g, m_conv_w_pw1, m_conv_b_pw1, m_conv_w_dw, m_conv_b_dw, m_conv_ln_g, m_conv_ln_b, m_conv_w_pw2, m_conv_b_pw2, m_gla_w_in, m_gla_w_gate_up, m_gla_b_gate, m_gla_norm_g, m_gla_w_out, m_ffn_w_in, m_ffn_w_out, v_w_ada, v_b_ada, v_pre_mix_g, v_post_mix_g, v_pre_ffn_g, v_post_ffn_g, v_conv_w_pw1, v_conv_b_pw1, v_conv_w_dw, v_conv_b_dw, v_conv_ln_g, v_conv_ln_b, v_conv_w_pw2, v_conv_b_pw2, v_gla_w_in, v_gla_w_gate_up, v_gla_b_gate, v_gla_norm_g, v_gla_w_out, v_ffn_w_in, v_ffn_w_out):
    weights = dict(w_ada=w_ada, b_ada=b_ada, pre_mix_g=pre_mix_g, post_mix_g=post_mix_g, pre_ffn_g=pre_ffn_g, post_ffn_g=post_ffn_g, conv_w_pw1=conv_w_pw1, conv_b_pw1=conv_b_pw1, conv_w_dw=conv_w_dw, conv_b_dw=conv_b_dw, conv_ln_g=conv_ln_g, conv_ln_b=conv_ln_b, conv_w_pw2=conv_w_pw2, conv_b_pw2=conv_b_pw2, gla_w_in=gla_w_in, gla_w_gate_up=gla_w_gate_up, gla_b_gate=gla_b_gate, gla_norm_g=gla_norm_g, gla_w_out=gla_w_out, ffn_w_in=ffn_w_in, ffn_w_out=ffn_w_out)
    mom_m = dict(w_ada=m_w_ada, b_ada=m_b_ada, pre_mix_g=m_pre_mix_g, post_mix_g=m_post_mix_g, pre_ffn_g=m_pre_ffn_g, post_ffn_g=m_post_ffn_g, conv_w_pw1=m_conv_w_pw1, conv_b_pw1=m_conv_b_pw1, conv_w_dw=m_conv_w_dw, conv_b_dw=m_conv_b_dw, conv_ln_g=m_conv_ln_g, conv_ln_b=m_conv_ln_b, conv_w_pw2=m_conv_w_pw2, conv_b_pw2=m_conv_b_pw2, gla_w_in=m_gla_w_in, gla_w_gate_up=m_gla_w_gate_up, gla_b_gate=m_gla_b_gate, gla_norm_g=m_gla_norm_g, gla_w_out=m_gla_w_out, ffn_w_in=m_ffn_w_in, ffn_w_out=m_ffn_w_out)
    mom_v = dict(w_ada=v_w_ada, b_ada=v_b_ada, pre_mix_g=v_pre_mix_g, post_mix_g=v_post_mix_g, pre_ffn_g=v_pre_ffn_g, post_ffn_g=v_post_ffn_g, conv_w_pw1=v_conv_w_pw1, conv_b_pw1=v_conv_b_pw1, conv_w_dw=v_conv_w_dw, conv_b_dw=v_conv_b_dw, conv_ln_g=v_conv_ln_g, conv_ln_b=v_conv_ln_b, conv_w_pw2=v_conv_w_pw2, conv_b_pw2=v_conv_b_pw2, gla_w_in=v_gla_w_in, gla_w_gate_up=v_gla_w_gate_up, gla_b_gate=v_gla_b_gate, gla_norm_g=v_gla_norm_g, gla_w_out=v_gla_w_out, ffn_w_in=v_ffn_w_in, ffn_w_out=v_ffn_w_out)
    order = list(weights)

    ax, ay, ac = lax.axis_index("x"), lax.axis_index("y"), lax.axis_index("c")
    my_chip, my_dev = 2 * ax + ay, 4 * ax + 2 * ay + ac

    x = x[0]
    target = loss_target[0]
    T, D = x.shape
    depth = w_ada.shape[0]
    n_conv, n_gla = conv_w_pw1.shape[0], gla_w_in.shape[0]
    width = conv_w_dw.shape[1]
    F = ffn_w_out.shape[1] * N_CHIPS
    DK = gla_w_gate_up.shape[2] * N_CHIPS
    rank = gla_w_gate_up.shape[1]
    gla_cols = 2 * DK + 2 * D + rank
    P = 2 * DK + 2 * D + 128
    dvh = D // GLA_HEADS
    tn_pw1 = _pick(conv_w_pw1.shape[2], (1024, 512, 256, 128))
    tn_ffn = ffn_w_out.shape[1]

    gin_cols = gla_w_in.shape[2]

    def sublayer_weights(g):
        i = g // 2
        if g % 2:
            return [("ffn_w_in", i), ("ffn_w_out", i)]
        return [("conv_w_pw1", i // 2), ("conv_w_pw2", i // 2)] if i % 2 == 0 else [("gla_w_in", i // 2), ("gla_w_out", i // 2)]

    issued = []

    def start_gathers(g, after):
        handles, tokens = {}, []
        for n, l in sublayer_weights(g):
            shard = weights[n][l].astype(BF16)
            handles[n], token = chips_start(f"gather_start_{n}_{l}", shard, shard, my_chip, True,
                                            tuple(after) + tuple(issued[-1:]))
            issued.append(token)
            tokens.append(token)
        return handles, tokens

    def wait_gathers(g, handles, after):
        out = []
        for n, l in sublayer_weights(g):
            got = _split_wait(f"gather_wait_{n}_{l}", handles[n], after)
            r, cc = got.shape[1:]
            if n == "gla_w_in":
                full = got.transpose(1, 0, 2).reshape(D, N_CHIPS * gin_cols)
                out.append(Wt(jnp.pad(full, ((0, 0), (0, P - gla_cols))), "plain", 0, D, P))
            elif n in ("conv_w_pw1", "ffn_w_in"):
                out.append(Wt(got, "col", 0, r, cc, tn_pw1 if n == "conv_w_pw1" else tn_ffn))
            else:
                out.append(Wt(got, "row", 0, r, cc))
        return out

    n_sub = 2 * depth
    in_flight = {g: start_gathers(g, ()) for g in range(min(3, n_sub))}
    started = in_flight[min(3, n_sub) - 1][1][-1][:1, :1]

    def enter_sublayer(g, xs):
        if g + 2 >= n_sub:
            return ()
        if g + 2 not in in_flight:
            in_flight[g + 2] = start_gathers(g + 2, (xs,))
        return in_flight[g + 2][1]

    small_sharded = ["conv_w_dw", "gla_w_gate_up", "gla_b_gate", "gla_norm_g"]
    packed = _flat_pack([weights[n] for n in small_sharded]) + started
    got = allgather_devices("gather_small", packed)[0::2]
    parts = _flat_unpack(got, [weights[n].shape for n in small_sharded], lead=(N_CHIPS,))
    w_dw_full, wgu_full, bgate_full, normg_full = [
        jnp.concatenate([p[s] for s in range(N_CHIPS)], axis=-1) for p in parts]
    w_dw_full = w_dw_full
    wgu_pad = jnp.pad(wgu_full, ((0, 0), (0, 128 - rank), (0, 0)))

    c_act = c * _sigmoid(c) + started
    c_all = allgather_devices("gather_c", jnp.pad(c_act, ((0, 7), (0, 0))))[:, 0, :]
    c16 = jnp.pad(c_all, ((0, 8), (0, 0))).astype(BF16)
    ada_cols = w_ada.shape[2]
    w_ada2 = w_ada.reshape(depth * D, ada_cols)
    mod_cols = [mm_nn(f"ada_fwd_{i}", c16, Wt(w_ada2, "plain", i, D, ada_cols)) for i in range(depth)]
    mod_cols = jnp.stack(mod_cols).reshape(depth * 16, ada_cols)
    mod_all = allgather_devices("gather_mod", mod_cols)[0::2]
    mod_all = mod_all.reshape(N_CHIPS, depth, 16, ada_cols).transpose(1, 2, 0, 3).reshape(depth, 16, 6 * D)
    mod = lax.dynamic_index_in_dim(mod_all, my_dev, axis=1, keepdims=False) + b_ada
    mod = mod.reshape(depth, 6, 1, D)

    w_pw1, w_pw2, w_gin, w_gout = [None] * n_conv, [None] * n_conv, [None] * n_gla, [None] * n_gla
    w_fin, w_fout = [None] * depth, [None] * depth
    row = lambda a: a.reshape(1, -1)
    saved = []
    xs = x
    for i in range(depth):
        j = i // 2
        sh1, sc1, gt1, sh2, sc2, gt2 = [mod[i, q] for q in range(6)]
        s = dict(x_in=xs)
        deps = list(enter_sublayer(2 * i, xs))
        if i == 0:
            deps += in_flight[0][1] + in_flight[1][1]
        h = prenorm_fwd(f"pre_mix_{i}", xs, row(pre_mix_g[i]) * (1.0 + sc1), sh1, deps=deps)
        s["h_mix"] = h
        if i % 2 == 0:
            w_pw1[j], w_pw2[j] = wait_gathers(2 * i, in_flight[2 * i][0], (h,))
        else:
            w_gin[j], w_gout[j] = wait_gathers(2 * i, in_flight[2 * i][0], (h,))
        if i % 2 == 0:
            u, glu = gated_in_fwd(f"pw1_{i}", h, w_pw1[j], "glu", bias=_interleave_vec(row(conv_b_pw1[j]), tn_pw1))
            v = dwconv_fwd(f"dwconv_{i}", glu, w_dw_full[j], row(conv_b_dw[j]))
            sl = ln_silu_fwd(f"ln_silu_{i}", v, row(conv_ln_g[j]), row(conv_ln_b[j]))
            y, x_mid = mm_nn_post(f"pw2_{i}", sl, w_pw2[j], xs, gt1 * row(post_mix_g[i]), bias=row(conv_b_pw2[j]))
            s.update(u=u, glu=glu, v=v, s=sl)
        else:
            proj = mm_nn(f"gla_in_{i}", h, w_gin[j])
            o, states = gla_fwd(f"gla_{i}", proj, wgu_pad[j], row(bgate_full[j]))
            og = gla_out_fwd(f"gla_out_{i}", o, proj, row(normg_full[j]), (2 * DK + D) // dvh)
            y, x_mid = mm_nn_post(f"gla_wout_{i}", og, w_gout[j], xs, gt1 * row(post_mix_g[i]))
            s.update(proj=proj, o=o, states=states, og=og)
        s["y_mix"] = y
        xs = x_mid
        s["x_mid"] = xs
        deps = enter_sublayer(2 * i + 1, xs)
        h = prenorm_fwd(f"pre_ffn_{i}", xs, row(pre_ffn_g[i]) * (1.0 + sc2), sh2, deps=deps)
        w_fin[i], w_fout[i] = wait_gathers(2 * i + 1, in_flight[2 * i + 1][0], (h,))
        u, a = gated_in_fwd(f"ffn_in_{i}", h, w_fin[i], "swiglu")
        y, xs = mm_nn_post(f"ffn_out_{i}", a, w_fout[i], xs, gt2 * row(post_ffn_g[i]))
        s.update(h_ffn=h, u_ffn=u, a_ffn=a, y_ffn=y)
        saved.append(s)

    dx, loss_acc = loss_bwd("loss", xs, target)
    loss = lax.psum(jnp.sum(loss_acc), ("x", "y", "c"))

    fold = lambda a: jnp.sum(a, axis=0)
    small_g = {n: [None] * weights[n].shape[0] for n in order if n not in
               ("w_ada", "conv_w_pw1", "conv_w_pw2", "gla_w_in", "gla_w_out", "ffn_w_in", "ffn_w_out")}
    grad_w = lambda wt: Wt(None, wt.kind, 0, wt.R, wt.C, wt.tn)
    scattering = {}

    def start_scatter(n, l, p):
        own = lax.dynamic_index_in_dim(p, my_chip, axis=0, keepdims=False)
        scattering[n, l], token = chips_start(f"scatter_start_{n}_{l}", p, own, my_chip, False)
        return token

    for i in reversed(range(depth)):
        j = i // 2
        s = saved[i]
        sh1, sc1, gt1, sh2, sc2, gt2 = [mod[i, q] for q in range(6)]
        if i == depth - 1:
            dy, d_gt2, d_pg, _ = postnorm_bwd(f"post_ffn_bwd_{i}", dx, s["y_ffn"], row(post_ffn_g[i]), gt2)
        else:
            dy, d_gt2, d_pg = carried
        small_g["post_ffn_g"][i] = fold(d_pg)
        du = ffn_out_dgrad(f"ffn_out_dgrad_{i}", dy, w_fout[i], s["u_ffn"])
        t_out = start_scatter("ffn_w_out", i, mm_tn(f"ffn_out_wgrad_{i}", s["a_ffn"], dy, grad_w(w_fout[i])))
        t_in = start_scatter("ffn_w_in", i, mm_tn(f"ffn_in_wgrad_{i}", s["h_ffn"], du, grad_w(w_fin[i])))
        dh = mm_nt(f"ffn_in_dgrad_{i}", du, w_fin[i], out_dtype=BF16)
        dx, dy, d_sh2, d_sc2, d_pg, d_gt1, d_pg1, dy_sum = norm_chain_bwd(
            f"pre_ffn_bwd_{i}", dh, s["x_mid"], dx, row(pre_ffn_g[i]), sc2,
            s["y_mix"], row(post_mix_g[i]), gt1, deps=(t_out, t_in))
        small_g["pre_ffn_g"][i] = fold(d_pg)
        small_g["post_mix_g"][i] = fold(d_pg1)
        if i % 2 == 0:
            small_g["conv_b_pw2"][j] = fold(dy_sum)
            dsl = mm_nt(f"pw2_dgrad_{i}", dy, w_pw2[j])
            t_out = start_scatter("conv_w_pw2", j, mm_tn(f"pw2_wgrad_{i}", s["s"], dy, grad_w(w_pw2[j])))
            dv, d_lg, d_lb = ln_silu_bwd(f"ln_silu_bwd_{i}", dsl, s["v"], row(conv_ln_g[j]), row(conv_ln_b[j]))
            small_g["conv_ln_g"][j], small_g["conv_ln_b"][j] = fold(d_lg), fold(d_lb)
            dglu, d_wdw, d_bdw = dwconv_bwd(f"dwconv_bwd_{i}", dv, s["glu"], w_dw_full[j])
            small_g["conv_w_dw"][j] = d_wdw.reshape(width, 8, D).sum(axis=1)
            small_g["conv_b_dw"][j] = fold(d_bdw)
            du, du_sum = glu_bwd(f"glu_bwd_{i}", s["u"], dglu, tn_pw1)
            small_g["conv_b_pw1"][j] = _deinterleave_vec(fold(du_sum), tn_pw1)
            t_in = start_scatter("conv_w_pw1", j, mm_tn(f"pw1_wgrad_{i}", s["h_mix"], du, grad_w(w_pw1[j])))
            dh = mm_nt(f"pw1_dgrad_{i}", du, w_pw1[j], out_dtype=BF16)
        else:
            dog = mm_nt(f"gla_wout_dgrad_{i}", dy, w_gout[j])
            t_out = start_scatter("gla_w_out", j, mm_tn(f"gla_wout_wgrad_{i}", s["og"], dy, grad_w(w_gout[j])))
            d_o, d_r, d_ng = gla_out_bwd(f"gla_out_bwd_{i}", dog, s["o"], s["proj"], row(normg_full[j]),
                                         (2 * DK + D) // dvh)
            small_g["gla_norm_g"][j] = fold(d_ng)
            dproj, dz, d_bg = gla_bwd(f"gla_bwd_{i}", s["proj"], wgu_pad[j], row(bgate_full[j]),
                                      s["states"], d_o, d_r)
            small_g["gla_b_gate"][j] = fold(d_bg)
            a_low = s["proj"][:, 2 * DK + 2 * D:].astype(BF16)
            d_wgu = mm_tn(f"gla_gate_wgrad_{i}", a_low, dz, Wt(None, "plain", 0, 128, DK), out_dtype=F32)
            small_g["gla_w_gate_up"][j] = d_wgu[:rank]
            g_in = mm_tn(f"gla_in_wgrad_{i}", s["h_mix"], dproj, Wt(None, "plain", 0, D, P))
            t_in = start_scatter("gla_w_in", j, g_in[:, :gla_cols].reshape(D, N_CHIPS, gin_cols).transpose(1, 0, 2))
            dh = mm_nt(f"gla_in_dgrad_{i}", dproj, w_gin[j], out_dtype=BF16)
        if i > 0:
            dx, dy_b, d_sh1, d_sc1, d_pg, d_gtb, d_pgb, _ = norm_chain_bwd(
                f"pre_mix_bwd_{i}", dh, s["x_in"], dx, row(pre_mix_g[i]), sc1,
                saved[i - 1]["y_ffn"], row(post_ffn_g[i - 1]), mod[i - 1, 5], deps=(t_out, t_in))
            carried = (dy_b, d_gtb, d_pgb)
        else:
            dx, d_sh1, d_sc1, d_pg = prenorm_bwd(f"pre_mix_bwd_{i}", dh, s["x_in"], dx, row(pre_mix_g[i]), sc1,
                                                 deps=(t_out, t_in))
        small_g["pre_mix_g"][i] = fold(d_pg)
        small_g["b_ada"][i] = jnp.concatenate([fold(t) for t in (d_sh1, d_sc1, d_gt1, d_sh2, d_sc2, d_gt2)])
    grad_x = dx[None]

    small_names = list(small_g)
    local_small = [jnp.stack(small_g[n]) for n in small_names]
    full_shapes = [a.shape for a in local_small]
    small_all = allgather_devices("gather_small_grads", _flat_pack(local_small))
    small_sum = sum_slots("sum_small_grads", small_all)
    small_tot = dict(zip(small_names, _flat_unpack(small_sum, full_shapes)))
    for n in small_sharded:
        cols = weights[n].shape[-1]
        small_tot[n] = lax.dynamic_slice_in_dim(small_tot[n], my_chip * cols, cols, axis=small_tot[n].ndim - 1)

    dmod_all = _flat_unpack(small_all, full_shapes, lead=(N_DEVICES,))[small_names.index("b_ada")]
    dmod_cols = lax.dynamic_slice_in_dim(
        dmod_all.reshape(N_DEVICES, depth, N_CHIPS, ada_cols), my_chip, 1, axis=2)[:, :, 0, :]
    dmod16 = jnp.pad(dmod_cols.reshape(N_DEVICES, depth * ada_cols), ((0, 8), (0, 0))).astype(BF16)
    g_ada = [mm_tn(f"ada_wgrad_{i}", c16, dmod16[:, i * ada_cols:(i + 1) * ada_cols],
                   Wt(None, "plain", 0, D, ada_cols), out_dtype=F32) for i in range(depth)]
    g_ada = jnp.stack(g_ada).reshape(depth * D, ada_cols)

    results = {}

    def flat2(a):
        return a.reshape(-1, a.shape[-1])

    swapping, last = [], ()
    for n in ("conv_w_pw1", "conv_w_pw2", "gla_w_in", "gla_w_out", "ffn_w_in", "ffn_w_out"):
        arrived = [_split_wait(f"scatter_wait_{n}_{l}", scattering[n, l], (dx,)) for l in range(weights[n].shape[0])]
        mine = sum_slots_layers("sum_" + n, arrived)
        handle, token = swap_start("swap_start_" + n, mine, last)
        swapping.append((n, handle))
        last = (token,)
    out = adamw("adamw_w_ada", flat2(w_ada), [g_ada], flat2(m_w_ada), flat2(v_w_ada), deps=last)
    results["w_ada"] = [t.reshape(w_ada.shape) for t in out]
    w_small = _flat_pack([weights[n] for n in small_names])
    out_small = adamw("adamw_small", w_small, [_flat_pack([small_tot[n] for n in small_names])],
                      _flat_pack([mom_m[n] for n in small_names]), _flat_pack([mom_v[n] for n in small_names]))
    for n, handle in swapping:
        w = weights[n]
        mine, theirs = _split_wait("swap_wait_" + n, handle, (out[0],), with_src=True)
        res = adamw("adamw_" + n, flat2(w), [mine, theirs], flat2(mom_m[n]), flat2(mom_v[n]))
        results[n] = [t.reshape(w.shape) for t in res]
    out = out_small
    shapes = [weights[n].shape for n in small_names]
    unpacked = [_flat_unpack(t, shapes) for t in out]
    for q, n in enumerate(small_names):
        results[n] = [unpacked[r][q] for r in range(4)]

    outs = [loss, grad_x]
    for r in range(4):
        outs += [results[n][r] for n in order]
    return tuple(outs)
```

```python
import jax
import jax.numpy as jnp
from jax import lax
from jax.experimental import pallas as pl
from jax.experimental.pallas import tpu as pltpu

F32, BF16 = jnp.float32, jnp.bfloat16
MESH = pl.DeviceIdType.MESH
HBM = pl.BlockSpec(memory_space=pl.ANY)

EPS = 1e-6
VMEM_LIMIT_BYTES = 48 * 1024 * 1024
N_CHIPS = 4
N_DEVICES = 8
GLA_HEADS = 4
GLA_CHUNK = 64
GLA_TAU = 16.0
CONV_HALO = 32
ADAM_LR, ADAM_B1, ADAM_B2, ADAM_EPS, ADAM_WD, ADAM_STEP = 0.001, 0.9, 0.999, 1e-08, 0.01, 10


def _pc(body, **kw):
    return pl.pallas_call(body, **kw)


def _params(*sem):
    return pltpu.CompilerParams(dimension_semantics=sem, vmem_limit_bytes=VMEM_LIMIT_BYTES)


def _pick(n, cands):
    for c in cands:
        if c <= n and n % c == 0:
            return c
    return n


def _fold8(z):
    r, w = z.shape
    return z.reshape(r // 8, 8, w).sum(axis=0)


def _sigmoid(x):
    return 1.0 / (1.0 + jnp.exp(-x))


def _exchange(name, xs, out_shape, masks, src_of, dst_of, local=True):
    n_in = len(xs)

    def body(*refs):
        x_refs, o_ref = refs[:n_in], refs[n_in]
        send_sems, recv_sems, local_sems = refs[n_in + 1:]
        x, y, c = lax.axis_index("x"), lax.axis_index("y"), lax.axis_index("c")
        me = (x, y, c)
        peers = [(1 - x if a else x, 1 - y if b else y, 1 - c if d else c) for a, b, d in masks]
        started = []
        if local:
            for q, (s, t) in enumerate(zip(src_of(x_refs, me), dst_of(o_ref, me))):
                cp = pltpu.make_async_copy(s, t, local_sems.at[q])
                cp.start()
                started.append(cp)
        sends = []
        for k, peer in enumerate(peers):
            for q, (s, t) in enumerate(zip(src_of(x_refs, peer), dst_of(o_ref, me))):
                cp = pltpu.make_async_remote_copy(
                    src_ref=s, dst_ref=t, send_sem=send_sems.at[k, q], recv_sem=recv_sems.at[k, q],
                    device_id=peer, device_id_type=MESH)
                cp.start()
                sends.append(cp)
        for k, peer in enumerate(peers):
            for q, (s, t) in enumerate(zip(src_of(x_refs, me), dst_of(o_ref, peer))):
                pltpu.make_async_remote_copy(
                    src_ref=s, dst_ref=t, send_sem=send_sems.at[k, q], recv_sem=recv_sems.at[k, q],
                    device_id=peer, device_id_type=MESH).wait_recv()
        for cp in sends:
            cp.wait_send()
        for cp in started:
            cp.wait()

    n_q = n_in if len(xs) > 1 else 1
    return _pc(
        body, name=name, out_shape=out_shape,
        in_specs=[HBM] * n_in, out_specs=HBM,
        scratch_shapes=[pltpu.SemaphoreType.DMA((len(masks), n_q)),
                        pltpu.SemaphoreType.DMA((len(masks), n_q)),
                        pltpu.SemaphoreType.DMA((n_q,))],
    )(*xs)


_CHIP_MASKS = [(1, 0, 0), (0, 1, 0), (1, 1, 0)]
_ALL_MASKS = [(a, b, d) for a in (0, 1) for b in (0, 1) for d in (0, 1) if (a, b, d) != (0, 0, 0)]


def _chip(p):
    return 2 * p[0] + p[1]


def _dev(p):
    return 4 * p[0] + 2 * p[1] + p[2]


def allgather_devices(name, v):
    r, c = v.shape
    return _exchange(
        name, [v], jax.ShapeDtypeStruct((N_DEVICES, r, c), v.dtype), _ALL_MASKS,
        lambda xr, peer: [xr[0]], lambda o, src: [o.at[_dev(src)]])


_HBM = pl.BlockSpec(memory_space=pltpu.HBM)
_SEM = pl.BlockSpec(memory_space=pltpu.SEMAPHORE)
_N_PEER_CHIPS = len(_CHIP_MASKS)


def _peer_chips():
    x, y, c = lax.axis_index("x"), lax.axis_index("y"), lax.axis_index("c")
    return (x, y), [((1 - x if a else x, 1 - y if b else y), c) for a, b, _ in _CHIP_MASKS]


def _chip_copies(src_ref, land_ref, send_sems, recv_sems, whole_src, incoming):
    me, peers = _peer_chips()
    out = []
    for k, (chip, c) in enumerate(peers):
        out.append(pltpu.make_async_remote_copy(
            src_ref=src_ref if whole_src else src_ref.at[_chip(chip)],
            dst_ref=land_ref.at[_chip(chip) if incoming else _chip(me)],
            send_sem=send_sems.at[k], recv_sem=recv_sems.at[k], device_id=(*chip, c), device_id_type=MESH))
    return out


def _core_copies(src_ref, land_ref, send_sems, recv_sems, incoming):
    x, y, c = lax.axis_index("x"), lax.axis_index("y"), lax.axis_index("c")
    return [pltpu.make_async_remote_copy(
        src_ref=src_ref, dst_ref=land_ref, send_sem=send_sems.at[0], recv_sem=recv_sems.at[0],
        device_id=(x, y, 1 - c), device_id_type=MESH)]


def _split_start(name, src, land, n_copies, copies, after):
    n_after = len(after)

    def body(*refs):
        src_ref, land_ref = refs[0], refs[1]
        send_sems, recv_sems, _, _, token = refs[2 + n_after:]
        for send in copies(src_ref, land_ref, send_sems, recv_sems, False):
            send.start()
        token[...] = jnp.zeros(token.shape, token.dtype)

    send_sems, recv_sems, src_thru, land_thru, token = _pc(
        body, name=name,
        out_shape=(pltpu.SemaphoreType.DMA((n_copies,)), pltpu.SemaphoreType.DMA((n_copies,)),
                   pltpu.HBM(src.shape, src.dtype), pltpu.HBM(land.shape, land.dtype),
                   jax.ShapeDtypeStruct((8, 128), F32)),
        in_specs=[_HBM, _HBM] + [HBM] * n_after,
        out_specs=(_SEM, _SEM, _HBM, _HBM, pl.BlockSpec(memory_space=pltpu.VMEM)),
        input_output_aliases={0: 2, 1: 3},
        compiler_params=pltpu.CompilerParams(has_side_effects=pltpu.SideEffectType.DATAFLOW_SIDE_EFFECTING),
    )(pltpu.with_memory_space_constraint(src, pltpu.HBM), pltpu.with_memory_space_constraint(land, pltpu.HBM),
      *after)
    return (send_sems, recv_sems, src_thru, land_thru, copies), token


def _split_wait(name, handle, after, with_src=False):
    send_sems, recv_sems, src_thru, land_thru, copies = handle
    n_after = len(after)

    def body(src_ref, land_ref, send_sems, recv_sems, *rest):
        for send in copies(src_ref, land_ref, send_sems, recv_sems, False):
            send.wait_send()
        for recv in copies(src_ref, land_ref, send_sems, recv_sems, True):
            recv.wait_recv()

    src, land = _pc(
        body, name=name,
        out_shape=(pltpu.HBM(src_thru.shape, src_thru.dtype), pltpu.HBM(land_thru.shape, land_thru.dtype)),
        in_specs=[_HBM, _HBM, _SEM, _SEM] + [HBM] * n_after, out_specs=(_HBM, _HBM),
        input_output_aliases={0: 0, 1: 1},
        compiler_params=pltpu.CompilerParams(has_side_effects=pltpu.SideEffectType.DATAFLOW_SIDE_EFFECTING),
    )(src_thru, land_thru, send_sems, recv_sems, *after)
    return (src, land) if with_src else land


def chips_start(name, src, own, my_chip, whole_src, after=()):
    r, c = own.shape
    land = lax.dynamic_update_slice(lax.empty((N_CHIPS, r, c), src.dtype), own[None], (my_chip, 0, 0))

    def copies(src_ref, land_ref, send_sems, recv_sems, incoming):
        return _chip_copies(src_ref, land_ref, send_sems, recv_sems, whole_src, incoming)

    return _split_start(name, src, land, _N_PEER_CHIPS, copies, after)


def _device_copies(src_ref, land_ref, send_sems, recv_sems, incoming):
    x, y, c = lax.axis_index("x"), lax.axis_index("y"), lax.axis_index("c")
    out = []
    for k, (a, b, d) in enumerate(_ALL_MASKS):
        peer = (1 - x if a else x, 1 - y if b else y, 1 - c if d else c)
        out.append(pltpu.make_async_remote_copy(
            src_ref=src_ref, dst_ref=land_ref.at[_dev(peer) if incoming else _dev((x, y, c))],
            send_sem=send_sems.at[k], recv_sem=recv_sems.at[k], device_id=peer, device_id_type=MESH))
    return out


def devices_start(name, v, my_dev, after=()):
    land = lax.dynamic_update_slice(lax.empty((N_DEVICES,) + v.shape, v.dtype), v[None], (my_dev, 0, 0))
    return _split_start(name, v, land, len(_ALL_MASKS), _device_copies, after)


def swap_start(name, a, after=()):
    return _split_start(name, a, lax.empty(a.shape, a.dtype), 1, _core_copies, after)


class Wt:
    def __init__(self, arr, kind, layer, rows, cols, tn=None):
        self.arr, self.kind, self.l, self.R, self.C, self.tn = arr, kind, layer, rows, cols, tn
        self.K = 4 * rows if kind == "row" else rows
        self.N = 4 * cols if kind == "col" else cols

    def spec(self, tk, tn):
        l, R, C = self.l, self.R, self.C
        if self.kind == "plain":
            off = l * (R // tk)
            return (tk, tn), lambda kb, jb: (off + kb, jb)
        if self.kind == "col":
            assert tn == self.tn
            per, off = C // tn, l * (R // tk)
            return (None, tk, tn), lambda kb, jb: (2 * (jb % 2) + (jb // 2) // per, off + kb, (jb // 2) % per)
        per = R // tk
        return (None, tk, tn), lambda kb, jb: (kb // per, l * per + kb % per, jb)

    def tile_k(self, cands):
        return _pick(self.R, cands)

    def tile_n(self, cands):
        return self.tn if self.kind == "col" else _pick(self.C, cands)


_TM = (1024, 512, 256, 128, 64, 32, 16, 8)
_TK = (2048, 1408, 1024, 896, 512, 256, 128, 64, 32, 16)
_TN = (1024, 896, 768, 512, 384, 256, 128)
_TR = (1024, 1408, 512, 256, 128, 64, 32, 16)


def _accumulate(part, acc_ref, k, nk, finish):
    if nk == 1:
        finish(part)
        return

    @pl.when(k == 0)
    def _():
        acc_ref[...] = part

    @pl.when(k > 0)
    def _():
        acc_ref[...] += part

    @pl.when(k == nk - 1)
    def _():
        finish(acc_ref[...])


def mm_nn(name, a, w, *, bias=None, out_dtype=F32):
    M, K = a.shape
    assert K == w.K
    N = w.N
    tm, tk, tn = _pick(M, _TM), w.tile_k(_TK), w.tile_n(_TN)
    nk = K // tk
    wblock, wmap = w.spec(tk, tn)
    in_specs = [pl.BlockSpec((tm, tk), lambda i, j, k: (i, k)),
                pl.BlockSpec(wblock, lambda i, j, k: wmap(k, j))]
    args = [a, w.arr]
    if bias is not None:
        in_specs.append(pl.BlockSpec((1, tn), lambda i, j, k: (0, j)))
        args.append(bias)

    def body(*refs):
        a_ref, b_ref = refs[0], refs[1]
        bias_ref = refs[2] if bias is not None else None
        o_ref = refs[3] if bias is not None else refs[2]
        acc_ref = refs[-1] if nk > 1 else None
        part = jnp.dot(a_ref[...].astype(BF16), b_ref[...].astype(BF16), preferred_element_type=F32)

        def finish(acc):
            if bias_ref is not None:
                acc = acc + bias_ref[...]
            o_ref[...] = acc.astype(o_ref.dtype)

        _accumulate(part, acc_ref, pl.program_id(2), nk, finish)

    return _pc(
        body, name=name, out_shape=jax.ShapeDtypeStruct((M, N), out_dtype),
        grid=(M // tm, N // tn, nk), in_specs=in_specs,
        out_specs=pl.BlockSpec((tm, tn), lambda i, j, k: (i, j)),
        scratch_shapes=[pltpu.VMEM((tm, tn), F32)] if nk > 1 else [],
        compiler_params=_params("parallel", "parallel", "arbitrary"),
    )(*args)


def mm_nt(name, a, w, *, out_dtype=F32):
    M, N = a.shape
    assert N == w.N
    K = w.K
    tm, tj, tn = _pick(M, _TM), w.tile_k(_TR), w.tile_n(_TK)
    wblock, wmap = w.spec(tj, tn)
    nb = 2 if w.kind == "col" else 1
    nn = N // (nb * tn)

    def body(a_ref, *refs):
        b_refs, o_ref, scr = refs[:nb], refs[nb], refs[nb + 1:]
        part = None
        for q, b_ref in enumerate(b_refs):
            p = lax.dot_general(a_ref[:, q * tn:(q + 1) * tn].astype(BF16), b_ref[...].astype(BF16),
                                (((1,), (1,)), ((), ())), preferred_element_type=F32)
            part = p if part is None else part + p

        def finish(acc):
            o_ref[...] = acc.astype(o_ref.dtype)

        _accumulate(part, scr[0] if nn > 1 else None, pl.program_id(2), nn, finish)

    def b_spec(q):
        return pl.BlockSpec(wblock, lambda i, j, n: wmap(j, nb * n + q))

    return _pc(
        body, name=name, out_shape=jax.ShapeDtypeStruct((M, K), out_dtype),
        grid=(M // tm, K // tj, nn),
        in_specs=[pl.BlockSpec((tm, nb * tn), lambda i, j, n: (i, n))] + [b_spec(q) for q in range(nb)],
        out_specs=pl.BlockSpec((tm, tj), lambda i, j, n: (i, j)),
        scratch_shapes=[pltpu.VMEM((tm, tj), F32)] if nn > 1 else [],
        compiler_params=_params("parallel", "parallel", "arbitrary"),
    )(a, *([w.arr] * nb))


def mm_tn(name, a, b, w, *, out_dtype=BF16):
    T, K = a.shape
    N = b.shape[1]
    assert (K, N) == (w.K, w.N) and w.l == 0
    tm = w.tile_k(_TR)
    tn = w.tile_n(_TN)
    tk = _pick(T, (2048, 1024, 512, 256, 128, 64, 32, 16))
    nk = T // tk
    oblock, omap = w.spec(tm, tn)
    shape = (w.R, w.C) if w.kind == "plain" else (N_CHIPS, w.R, w.C)

    def body(a_ref, b_ref, o_ref, *scr):
        part = lax.dot_general(a_ref[...].astype(BF16), b_ref[...].astype(BF16),
                               (((0,), (0,)), ((), ())), preferred_element_type=F32)

        def finish(acc):
            o_ref[...] = acc.astype(o_ref.dtype)

        _accumulate(part, scr[0] if nk > 1 else None, pl.program_id(2), nk, finish)

    return _pc(
        body, name=name, out_shape=jax.ShapeDtypeStruct(shape, out_dtype),
        grid=(K // tm, N // tn, nk),
        in_specs=[pl.BlockSpec((tk, tm), lambda i, j, k: (k, i)),
                  pl.BlockSpec((tk, tn), lambda i, j, k: (k, j))],
        out_specs=pl.BlockSpec(oblock, lambda i, j, k: omap(i, j)),
        scratch_shapes=[pltpu.VMEM((tm, tn), F32)] if nk > 1 else [],
        compiler_params=_params("parallel", "parallel", "arbitrary"),
    )(a, b)


_TM_FUSED = (512, 256, 128, 64, 32, 16, 8)


def mm_nn_post(name, a, w, x, wvec, *, bias=None):
    M, K = a.shape
    assert K == w.K
    N = w.N
    tm, tk = _pick(M, _TM_FUSED), w.tile_k(_TK)
    rows = _pick(tm, (64, 32, 16, 8))
    nk = K // tk
    wblock, wmap = w.spec(tk, N)
    in_specs = [pl.BlockSpec((tm, tk), lambda i, k: (i, k)),
                pl.BlockSpec(wblock, lambda i, k: wmap(k, 0)),
                pl.BlockSpec((tm, N), lambda i, k: (i, 0)),
                pl.BlockSpec((1, N), lambda i, k: (0, 0))]
    args = [a, w.arr, x, wvec]
    if bias is not None:
        in_specs.append(pl.BlockSpec((1, N), lambda i, k: (0, 0)))
        args.append(bias)

    def body(*refs):
        a_ref, b_ref, x_ref, wv_ref = refs[:4]
        bias_ref = refs[4] if bias is not None else None
        y_ref, xn_ref, acc_ref = refs[-3:]
        part = jnp.dot(a_ref[...].astype(BF16), b_ref[...].astype(BF16), preferred_element_type=F32)
        k = pl.program_id(1)

        @pl.when(k == 0)
        def _():
            acc_ref[...] = part

        @pl.when(k > 0)
        def _():
            acc_ref[...] += part

        @pl.when(k == nk - 1)
        def _():
            for r0 in range(0, tm, rows):
                rr = pl.ds(r0, rows)
                yv = acc_ref[rr, :]
                if bias_ref is not None:
                    yv = yv + bias_ref[...]
                y_ref[rr, :] = yv.astype(BF16)
                xn_ref[rr, :] = x_ref[rr, :] + yv * _rms(yv) * wv_ref[...]

    spec = pl.BlockSpec((tm, N), lambda i, k: (i, 0))
    return _pc(
        body, name=name, out_shape=[jax.ShapeDtypeStruct((M, N), BF16), jax.ShapeDtypeStruct((M, N), F32)],
        grid=(M // tm, nk), in_specs=in_specs, out_specs=[spec, spec],
        scratch_shapes=[pltpu.VMEM((tm, N), F32)],
        compiler_params=_params("parallel", "arbitrary"),
    )(*args)


def gated_in_fwd(name, h, w, unit, *, bias=None):
    M, K = h.shape
    assert w.kind == "col" and K == w.K == w.R
    tn, F = w.tn, w.N // 2
    tm = _pick(M, _TM_FUSED)
    wblock, wmap = w.spec(K, tn)
    in_specs = [pl.BlockSpec((tm, K), lambda g, i: (i, 0)),
                pl.BlockSpec(wblock, lambda g, i: wmap(0, 2 * g)),
                pl.BlockSpec(wblock, lambda g, i: wmap(0, 2 * g + 1))]
    args = [h, w.arr, w.arr]
    if bias is not None:
        in_specs.append(pl.BlockSpec((1, 2 * tn), lambda g, i: (0, g)))
        args.append(bias)

    def body(h_ref, w1_ref, w2_ref, *refs):
        u_ref, act_ref = refs[-2:]
        hv = h_ref[...].astype(BF16)
        first = jnp.dot(hv, w1_ref[...].astype(BF16), preferred_element_type=F32)
        second = jnp.dot(hv, w2_ref[...].astype(BF16), preferred_element_type=F32)
        if bias is not None:
            first, second = first + refs[0][:, :tn], second + refs[0][:, tn:]
        u_ref[:, :tn] = first.astype(BF16)
        u_ref[:, tn:] = second.astype(BF16)
        if unit == "swiglu":
            act_ref[...] = (first * _sigmoid(first) * second).astype(act_ref.dtype)
        else:
            act_ref[...] = (first * _sigmoid(second)).astype(act_ref.dtype)

    return _pc(
        body, name=name,
        out_shape=[jax.ShapeDtypeStruct((M, 2 * F), BF16),
                   jax.ShapeDtypeStruct((M, F), BF16 if unit == "swiglu" else F32)],
        grid=(F // tn, M // tm), in_specs=in_specs,
        out_specs=[pl.BlockSpec((tm, 2 * tn), lambda g, i: (i, g)),
                   pl.BlockSpec((tm, tn), lambda g, i: (i, g))],
        compiler_params=_params("parallel", "parallel"),
    )(*args)


def ffn_out_dgrad(name, dy, w, u):
    M, N = dy.shape
    assert w.kind == "row" and N == w.N == w.C
    tj, F = w.R, w.K
    tm = _pick(M, _TM_FUSED)
    wblock, wmap = w.spec(tj, N)

    def body(dy_ref, b_ref, u_ref, du_ref):
        d = lax.dot_general(dy_ref[...].astype(BF16), b_ref[...].astype(BF16),
                            (((1,), (1,)), ((), ())), preferred_element_type=F32)
        gate, up = u_ref[:, :tj].astype(F32), u_ref[:, tj:].astype(F32)
        sg = _sigmoid(gate)
        du_ref[:, :tj] = (d * up * (sg * (1.0 + gate * (1.0 - sg)))).astype(BF16)
        du_ref[:, tj:] = (d * gate * sg).astype(BF16)

    return _pc(
        body, name=name, out_shape=jax.ShapeDtypeStruct((M, 2 * F), BF16),
        grid=(F // tj, M // tm),
        in_specs=[pl.BlockSpec((tm, N), lambda j, i: (i, 0)),
                  pl.BlockSpec(wblock, lambda j, i: wmap(j, 0)),
                  pl.BlockSpec((tm, 2 * tj), lambda j, i: (i, j))],
        out_specs=pl.BlockSpec((tm, 2 * tj), lambda j, i: (i, j)),
        compiler_params=_params("parallel", "parallel"),
    )(dy, w.arr, u)


def _rows(name, body, rows_in, vecs_in, rows_out, accs_out, tm=256, deps=()):
    T = rows_in[0].shape[0]
    tm = _pick(T, (tm, 128, 64, 32, 16, 8))
    n_r, n_v, n_o, n_d = len(rows_in), len(vecs_in), len(rows_out), len(deps)

    def kern(*refs):
        r_refs, v_refs = refs[:n_r], refs[n_r:n_r + n_v]
        refs = refs[n_r + n_v + n_d:]
        o_refs, a_refs = refs[:n_o], refs[n_o:]

        @pl.when(pl.program_id(0) == 0)
        def _():
            for a_ref in a_refs:
                a_ref[...] = jnp.zeros(a_ref.shape, F32)

        body(r_refs, v_refs, o_refs, a_refs)

    in_specs = [pl.BlockSpec((tm, a.shape[1]), lambda i: (i, 0)) for a in rows_in]
    in_specs += [pl.BlockSpec(v.shape, lambda i: (0, 0)) for v in vecs_in]
    in_specs += [HBM] * n_d
    out_shape = [jax.ShapeDtypeStruct((T, w), dt) for w, dt in rows_out]
    out_shape += [jax.ShapeDtypeStruct((8, w), F32) for w in accs_out]
    out_specs = [pl.BlockSpec((tm, w), lambda i: (i, 0)) for w, _ in rows_out]
    out_specs += [pl.BlockSpec((8, w), lambda i: (0, 0)) for w in accs_out]
    return _pc(kern, name=name, out_shape=out_shape, grid=(T // tm,), in_specs=in_specs,
               out_specs=out_specs, compiler_params=_params("arbitrary"))(*rows_in, *vecs_in, *deps)


def _rms(x):
    return lax.rsqrt(jnp.mean(x * x, axis=-1, keepdims=True) + EPS)


def prenorm_fwd(name, x, gain, shift, deps=()):
    def body(r, v, o, a):
        xv = r[0][...]
        o[0][...] = (xv * _rms(xv) * v[0][...] + v[1][...]).astype(BF16)
    return _rows(name, body, [x], [gain, shift], [(x.shape[1], BF16)], [], deps=deps)[0]


def postnorm_bwd(name, dxo, y, post_g, gate):
    def body(r, v, o, a):
        d, yv = r[0][...], r[1][...].astype(F32)
        pg, gt = v[0][...], v[1][...]
        ry = _rms(yv)
        yn = yv * ry
        t = d * yn
        dyn = d * (gt * pg)
        dy = ry * (dyn - yn * jnp.mean(dyn * yn, axis=-1, keepdims=True))
        o[0][...] = dy.astype(BF16)
        a[0][...] += _fold8(t * pg)
        a[1][...] += _fold8(t * gt)
        a[2][...] += _fold8(dy)
    D = y.shape[1]
    return _rows(name, body, [dxo, y], [post_g, gate], [(D, BF16)], [D, D, D])


def prenorm_bwd(name, dh, x, dxo, pre_g, scale, deps=()):
    def body(r, v, o, a):
        dhv, xv, d = r[0][...].astype(F32), r[1][...], r[2][...]
        pg, sc1 = v[0][...], 1.0 + v[1][...]
        rx = _rms(xv)
        xn = xv * rx
        t = dhv * xn
        dxn = dhv * (pg * sc1)
        o[0][...] = d + rx * (dxn - xn * jnp.mean(dxn * xn, axis=-1, keepdims=True))
        a[0][...] += _fold8(dhv)
        a[1][...] += _fold8(t * pg)
        a[2][...] += _fold8(t * sc1)
    D = x.shape[1]
    return _rows(name, body, [dh, x, dxo], [pre_g, scale], [(D, F32)], [D, D, D], deps=deps)


def norm_chain_bwd(name, dh, x, dxo, pre_g, scale, y, post_g, gate, deps=()):
    def body(r, v, o, a):
        dhv, xv, d, yv = r[0][...].astype(F32), r[1][...], r[2][...], r[3][...].astype(F32)
        pg, sc1, pg2, gt = v[0][...], 1.0 + v[1][...], v[2][...], v[3][...]
        rx = _rms(xv)
        xn = xv * rx
        t = dhv * xn
        dxn = dhv * (pg * sc1)
        dx = d + rx * (dxn - xn * jnp.mean(dxn * xn, axis=-1, keepdims=True))
        o[0][...] = dx
        a[0][...] += _fold8(dhv)
        a[1][...] += _fold8(t * pg)
        a[2][...] += _fold8(t * sc1)
        ry = _rms(yv)
        yn = yv * ry
        t2 = dx * yn
        dyn = dx * (gt * pg2)
        dy = ry * (dyn - yn * jnp.mean(dyn * yn, axis=-1, keepdims=True))
        o[1][...] = dy.astype(BF16)
        a[3][...] += _fold8(t2 * pg2)
        a[4][...] += _fold8(t2 * gt)
        a[5][...] += _fold8(dy)
    D = x.shape[1]
    return _rows(name, body, [dh, x, dxo, y], [pre_g, scale, post_g, gate], [(D, F32), (D, BF16)], [D] * 6,
                 deps=deps)


def loss_bwd(name, y, target):
    D = y.shape[1]

    def body(r, v, o, a):
        e = r[0][...] - r[1][...]
        o[0][...] = e * (1.0 / D)
        a[0][...] += _fold8(e * e * (0.5 / D))
    return _rows(name, body, [y, target], [], [(D, F32)], [D])


def ln_silu_fwd(name, v_, g, b):
    def body(r, v, o, a):
        x = r[0][...]
        xc = x - jnp.mean(x, axis=-1, keepdims=True)
        ln = xc * lax.rsqrt(jnp.mean(xc * xc, axis=-1, keepdims=True) + EPS) * v[0][...] + v[1][...]
        o[0][...] = (ln * _sigmoid(ln)).astype(BF16)
    return _rows(name, body, [v_], [g, b], [(v_.shape[1], BF16)], [])[0]


def ln_silu_bwd(name, ds, v_, g, b):
    def body(r, v, o, a):
        dsv, x = r[0][...], r[1][...]
        xc = x - jnp.mean(x, axis=-1, keepdims=True)
        rstd = lax.rsqrt(jnp.mean(xc * xc, axis=-1, keepdims=True) + EPS)
        xh = xc * rstd
        ln = xh * v[0][...] + v[1][...]
        sg = _sigmoid(ln)
        dln = dsv * (sg * (1.0 + ln * (1.0 - sg)))
        dxh = dln * v[0][...]
        o[0][...] = rstd * (dxh - jnp.mean(dxh, axis=-1, keepdims=True)
                            - xh * jnp.mean(dxh * xh, axis=-1, keepdims=True))
        a[0][...] += _fold8(dln * xh)
        a[1][...] += _fold8(dln)
    D = v_.shape[1]
    return _rows(name, body, [ds, v_], [g, b], [(D, F32)], [D, D])


def _pairs(name, body, u, others, out_w, out_dtype, tn, n_acc=0, tm=256):
    T, F2 = u.shape
    F = F2 // 2
    tm = _pick(T, (tm, 128, 64, 32, 16, 8))

    def kern(*refs):
        u_ref, o_refs = refs[0], refs[1:1 + len(others)]
        out_ref, acc_refs = refs[1 + len(others)], refs[2 + len(others):]

        @pl.when(pl.program_id(1) == 0)
        def _():
            for a_ref in acc_refs:
                a_ref[...] = jnp.zeros(a_ref.shape, F32)

        body(u_ref, o_refs, out_ref, acc_refs)

    out_shape = [jax.ShapeDtypeStruct((T, out_w * F), out_dtype)]
    out_shape += [jax.ShapeDtypeStruct((8, F2), F32)] * n_acc
    out_specs = [pl.BlockSpec((tm, out_w * tn), lambda g, i: (i, g))]
    out_specs += [pl.BlockSpec((8, 2 * tn), lambda g, i: (0, g))] * n_acc
    return _pc(kern, name=name, out_shape=out_shape, grid=(F // tn, T // tm),
               in_specs=[pl.BlockSpec((tm, 2 * tn), lambda g, i: (i, g))]
               + [pl.BlockSpec((tm, tn), lambda g, i: (i, g))] * len(others),
               out_specs=out_specs, compiler_params=_params("parallel", "arbitrary"))(u, *others)


def glu_bwd(name, u, dglu, tn):
    def body(u_ref, o, out, acc):
        a, g = u_ref[:, :tn].astype(F32), u_ref[:, tn:].astype(F32)
        d = o[0][...]
        sg = _sigmoid(g)
        da, dg = d * sg, d * a * (sg * (1.0 - sg))
        out[:, :tn] = da.astype(BF16)
        out[:, tn:] = dg.astype(BF16)
        acc[0][:, :tn] += _fold8(da)
        acc[0][:, tn:] += _fold8(dg)
    return _pairs(name, body, u, [dglu], 2, BF16, tn, n_acc=1)


_CONV_ROWS, _CONV_LANES = 64, 128
_SUBLANES = 8


def _fill_shifts(ext, sh, n):
    for r in range(1, _SUBLANES):
        sh[r - 1, pl.ds(0, n), :] = ext[pl.ds(r, n), :]


def _tap(ext, sh, o, r0, rows, ln):
    q, r = divmod(o, _SUBLANES)
    if r == 0:
        return ext[pl.ds(r0 + o, rows), ln]
    return sh[r - 1, pl.ds(r0 + _SUBLANES * q, rows), ln]


def dwconv_fwd(name, x, w, b):
    T, D = x.shape
    width = w.shape[0]
    tt, cb = _pick(T, (256, 128, 64)), _pick(D, (256, 128))
    off = CONV_HALO - (width - 1)
    rows = min(_CONV_ROWS, tt)
    n_sh = tt + CONV_HALO - _SUBLANES

    def body(cur_ref, prev_ref, w_ref, b_ref, o_ref, ext, sh):
        t = pl.program_id(1)
        tail = prev_ref[pl.ds(tt - CONV_HALO, CONV_HALO), :]
        ext[pl.ds(0, CONV_HALO), :] = jnp.where(t > 0, tail, 0.0)
        ext[pl.ds(CONV_HALO, tt), :] = cur_ref[...]
        _fill_shifts(ext, sh, n_sh)
        for l0 in range(0, cb, _CONV_LANES):
            ln = pl.ds(l0, _CONV_LANES)
            for r0 in range(0, tt, rows):
                acc = jnp.broadcast_to(b_ref[:, ln], (rows, _CONV_LANES))
                for k in range(width):
                    acc = acc + _tap(ext, sh, off + k, r0, rows, ln) * w_ref[pl.ds(k, 1), ln]
                o_ref[pl.ds(r0, rows), ln] = acc

    return _pc(
        body, name=name, out_shape=jax.ShapeDtypeStruct((T, D), F32), grid=(D // cb, T // tt),
        in_specs=[pl.BlockSpec((tt, cb), lambda c, t: (t, c)),
                  pl.BlockSpec((tt, cb), lambda c, t: (jnp.maximum(t - 1, 0), c)),
                  pl.BlockSpec((width, cb), lambda c, t: (0, c)),
                  pl.BlockSpec((1, cb), lambda c, t: (0, c))],
        out_specs=pl.BlockSpec((tt, cb), lambda c, t: (t, c)),
        scratch_shapes=[pltpu.VMEM((tt + CONV_HALO, cb), F32), pltpu.VMEM((_SUBLANES - 1, tt + CONV_HALO, cb), F32)],
        compiler_params=_params("parallel", "arbitrary"),
    )(x, x, w, b)


def dwconv_bwd(name, dv, x, w):
    T, D = x.shape
    width = w.shape[0]
    tt, cb = _pick(T, (256, 128, 64)), _pick(D, (256, 128))
    off = CONV_HALO - (width - 1)
    rows = min(_CONV_ROWS, tt)
    nt = T // tt
    n_sh = tt + CONV_HALO - _SUBLANES

    def body(dv_ref, dvn_ref, x_ref, xp_ref, w_ref, dx_ref, dw_ref, db_ref, ext_d, ext_x, sh_d, sh_x):
        t = pl.program_id(1)

        @pl.when(t == 0)
        def _():
            dw_ref[...] = jnp.zeros(dw_ref.shape, F32)
            db_ref[...] = jnp.zeros(db_ref.shape, F32)

        ext_d[pl.ds(0, tt), :] = dv_ref[...]
        ext_d[pl.ds(tt, CONV_HALO), :] = jnp.where(t < nt - 1, dvn_ref[pl.ds(0, CONV_HALO), :], 0.0)
        ext_x[pl.ds(0, CONV_HALO), :] = jnp.where(t > 0, xp_ref[pl.ds(tt - CONV_HALO, CONV_HALO), :], 0.0)
        ext_x[pl.ds(CONV_HALO, tt), :] = x_ref[...]
        db_ref[...] += _fold8(dv_ref[...])
        _fill_shifts(ext_d, sh_d, n_sh)
        _fill_shifts(ext_x, sh_x, n_sh)
        for l0 in range(0, cb, _CONV_LANES):
            ln = pl.ds(l0, _CONV_LANES)
            for r0 in range(0, tt, rows):
                acc = jnp.zeros((rows, _CONV_LANES), F32)
                for k in range(width):
                    acc = acc + _tap(ext_d, sh_d, (width - 1) - k, r0, rows, ln) * w_ref[pl.ds(k, 1), ln]
                dx_ref[pl.ds(r0, rows), ln] = acc
            sums = [jnp.zeros((8, _CONV_LANES), F32)] * width
            for r0 in range(0, tt, rows):
                dvb = ext_d[pl.ds(r0, rows), ln]
                sums = [sums[k] + _fold8(dvb * _tap(ext_x, sh_x, off + k, r0, rows, ln)) for k in range(width)]
            for k in range(width):
                dw_ref[pl.ds(8 * k, 8), ln] += sums[k]

    return _pc(
        body, name=name,
        out_shape=[jax.ShapeDtypeStruct((T, D), F32), jax.ShapeDtypeStruct((8 * width, D), F32),
                   jax.ShapeDtypeStruct((8, D), F32)],
        grid=(D // cb, nt),
        in_specs=[pl.BlockSpec((tt, cb), lambda c, t: (t, c)),
                  pl.BlockSpec((tt, cb), lambda c, t: (jnp.minimum(t + 1, nt - 1), c)),
                  pl.BlockSpec((tt, cb), lambda c, t: (t, c)),
                  pl.BlockSpec((tt, cb), lambda c, t: (jnp.maximum(t - 1, 0), c)),
                  pl.BlockSpec((width, cb), lambda c, t: (0, c))],
        out_specs=[pl.BlockSpec((tt, cb), lambda c, t: (t, c)),
                   pl.BlockSpec((8 * width, cb), lambda c, t: (0, c)),
                   pl.BlockSpec((8, cb), lambda c, t: (0, c))],
        scratch_shapes=[pltpu.VMEM((tt + CONV_HALO, cb), F32), pltpu.VMEM((tt + CONV_HALO, cb), F32),
                        pltpu.VMEM((_SUBLANES - 1, tt + CONV_HALO, cb), F32),
                        pltpu.VMEM((_SUBLANES - 1, tt + CONV_HALO, cb), F32)],
        compiler_params=_params("parallel", "arbitrary"),
    )(dv, dv, x, x, w)


def _gla_dims(proj, wgu):
    DK = wgu.shape[1]
    DV = (proj.shape[1] - 128 - 2 * DK) // 2
    return DK, DV, DK // GLA_HEADS, DV // GLA_HEADS


def _gla_decay(a_ref, w_ref, bg_ref):
    C = GLA_CHUNK
    z = jnp.dot(a_ref[...].astype(BF16), w_ref[...].astype(BF16), preferred_element_type=F32) + bg_ref[...]
    g = (jnp.minimum(z, 0.0) - jnp.log(1.0 + jnp.exp(-jnp.abs(z)))) * (1.0 / GLA_TAU)
    row = lax.broadcasted_iota(jnp.int32, (C, C), 0)
    col = lax.broadcasted_iota(jnp.int32, (C, C), 1)
    bc = jnp.dot((row >= col).astype(F32), g, precision=lax.Precision.HIGHEST, preferred_element_type=F32)
    last = lax.broadcasted_iota(jnp.int32, bc.shape, 0) == C - 1
    b_last = jnp.sum(jnp.where(last, bc, 0.0), axis=0, keepdims=True)
    return z, bc, b_last


def _gla_in_specs(wgu, DK, DV, cidx):
    C = GLA_CHUNK
    return [pl.BlockSpec((C, DK), lambda c: (cidx(c), 0)),
            pl.BlockSpec((C, DK), lambda c: (cidx(c), 1)),
            pl.BlockSpec((C, DV), lambda c: (cidx(c), (2 * DK) // DV)),
            pl.BlockSpec((C, 128), lambda c: (cidx(c), (2 * DK + 2 * DV) // 128)),
            pl.BlockSpec(wgu.shape, lambda c: (0, 0)),
            pl.BlockSpec((1, DK), lambda c: (0, 0))]


def _dot_nt(a, b):
    return lax.dot_general(a.astype(BF16), b.astype(BF16), (((1,), (1,)), ((), ())), preferred_element_type=F32)


def _dot_tn(a, b):
    return lax.dot_general(a.astype(BF16), b.astype(BF16), (((0,), (0,)), ((), ())), preferred_element_type=F32)


def _dot(a, b):
    return jnp.dot(a.astype(BF16), b.astype(BF16), preferred_element_type=F32)


def gla_fwd(name, proj, wgu, bg):
    T = proj.shape[0]
    C, H = GLA_CHUNK, GLA_HEADS
    DK, DV, dk, dv = _gla_dims(proj, wgu)
    nc = T // C
    scale = dk ** -0.5

    def body(q_ref, k_ref, v_ref, a_ref, w_ref, bg_ref, o_ref, s_ref, st):
        @pl.when(pl.program_id(0) == 0)
        def _():
            st[...] = jnp.zeros(st.shape, F32)

        _, bc, b_last = _gla_decay(a_ref, w_ref, bg_ref)
        k = k_ref[...]
        qe = (q_ref[...] * scale * jnp.exp(bc)).astype(BF16)
        ke = (k * jnp.exp(-bc)).astype(BF16)
        kd = (k * jnp.exp(b_last - bc)).astype(BF16)
        el = jnp.exp(b_last)
        row = lax.broadcasted_iota(jnp.int32, (C, C), 0)
        col = lax.broadcasted_iota(jnp.int32, (C, C), 1)
        for h in range(H):
            kk, vv = slice(h * dk, (h + 1) * dk), slice(h * dv, (h + 1) * dv)
            sp = st[h]
            s_ref[h] = sp.astype(BF16)
            v = v_ref[:, vv]
            att = jnp.where(row >= col, _dot_nt(qe[:, kk], ke[:, kk]), 0.0)
            o_ref[:, vv] = _dot_nt(qe[:, kk], sp) + _dot(att, v)
            st[h] = sp * el[:, kk] + _dot_tn(v, kd[:, kk])

    return _pc(
        body, name=name,
        out_shape=[jax.ShapeDtypeStruct((T, DV), F32), jax.ShapeDtypeStruct((H, nc, dv, dk), BF16)],
        grid=(nc,), in_specs=_gla_in_specs(wgu, DK, DV, lambda c: c),
        out_specs=[pl.BlockSpec((C, DV), lambda c: (c, 0)),
                   pl.BlockSpec((H, None, dv, dk), lambda c: (0, c, 0, 0))],
        scratch_shapes=[pltpu.VMEM((H, dv, dk), F32)],
        compiler_params=_params("arbitrary"),
    )(proj, proj, proj, proj, wgu, bg)


def gla_bwd(name, proj, wgu, bg, states, do, d_r):
    T = proj.shape[0]
    C, H = GLA_CHUNK, GLA_HEADS
    DK, DV, dk, dv = _gla_dims(proj, wgu)
    nc = T // C
    scale = dk ** -0.5
    rev = lambda c: nc - 1 - c

    def body(q_ref, k_ref, v_ref, a_ref, w_ref, bg_ref, s_ref, do_ref, dr_ref,
             dp_ref, dz_ref, dbg_ref, dst, db_scr):
        @pl.when(pl.program_id(0) == 0)
        def _():
            dst[...] = jnp.zeros(dst.shape, F32)
            dbg_ref[...] = jnp.zeros(dbg_ref.shape, F32)

        z, bc, b_last = _gla_decay(a_ref, w_ref, bg_ref)
        k = k_ref[...]
        eb, enb, ed, el = jnp.exp(bc), jnp.exp(-bc), jnp.exp(b_last - bc), jnp.exp(b_last)
        qe, ke, kd = q_ref[...] * scale * eb, k * enb, k * ed
        row = lax.broadcasted_iota(jnp.int32, (C, C), 0)
        col = lax.broadcasted_iota(jnp.int32, (C, C), 1)
        keep = row >= col
        last = lax.broadcasted_iota(jnp.int32, (C, dk), 0) == C - 1
        for h in range(H):
            kk, vv = slice(h * dk, (h + 1) * dk), slice(h * dv, (h + 1) * dv)
            qe_h, ke_h, kd_h, el_h = qe[:, kk], ke[:, kk], kd[:, kk], el[:, kk]
            v, d_o = v_ref[:, vv], do_ref[:, vv]
            sp = s_ref[h].astype(F32)
            ds_ = dst[h]
            att = jnp.where(keep, _dot_nt(qe_h, ke_h), 0.0)
            datt = jnp.where(keep, _dot_nt(d_o, v), 0.0)
            dqe = _dot(d_o, sp) + _dot(datt, ke_h)
            dke = _dot_tn(datt, qe_h)
            dp_ref[:, 2 * DK + h * dv:2 * DK + (h + 1) * dv] = (_dot_tn(att, d_o) + _dot_nt(kd_h, ds_)).astype(BF16)
            dkd = _dot(v, ds_)
            dst[h] = ds_ * el_h + _dot_tn(d_o, qe_h)
            dp_ref[:, kk] = (dqe * scale * eb[:, kk]).astype(BF16)
            dp_ref[:, DK + h * dk:DK + (h + 1) * dk] = (dke * enb[:, kk] + dkd * ed[:, kk]).astype(BF16)
            d_el = jnp.sum(sp * ds_, axis=0, keepdims=True)
            db_last = jnp.sum(dkd * kd_h, axis=0, keepdims=True) + d_el * el_h
            db_scr[:, kk] = dqe * qe_h - dke * ke_h - dkd * kd_h + jnp.where(last, db_last, 0.0)
        dg = jnp.dot((row <= col).astype(F32), db_scr[...], precision=lax.Precision.HIGHEST,
                     preferred_element_type=F32)
        dz = dg * (1.0 / GLA_TAU) * _sigmoid(-z)
        dz_ref[...] = dz.astype(BF16)
        dbg_ref[...] += _fold8(dz)
        dp_ref[:, 2 * DK + DV:2 * DK + 2 * DV] = dr_ref[...]
        dp_ref[:, 2 * DK + 2 * DV:] = _dot_nt(dz, w_ref[...]).astype(BF16)

    return _pc(
        body, name=name,
        out_shape=[jax.ShapeDtypeStruct((T, proj.shape[1]), BF16), jax.ShapeDtypeStruct((T, DK), BF16),
                   jax.ShapeDtypeStruct((8, DK), F32)],
        grid=(nc,),
        in_specs=_gla_in_specs(wgu, DK, DV, rev)
        + [pl.BlockSpec((H, None, dv, dk), lambda c: (0, rev(c), 0, 0)),
           pl.BlockSpec((C, DV), lambda c: (rev(c), 0)),
           pl.BlockSpec((C, DV), lambda c: (rev(c), 0))],
        out_specs=[pl.BlockSpec((C, proj.shape[1]), lambda c: (rev(c), 0)),
                   pl.BlockSpec((C, DK), lambda c: (rev(c), 0)),
                   pl.BlockSpec((8, DK), lambda c: (0, 0))],
        scratch_shapes=[pltpu.VMEM((H, dv, dk), F32), pltpu.VMEM((C, DK), F32)],
        compiler_params=_params("arbitrary"),
    )(proj, proj, proj, proj, wgu, bg, states, do, d_r)


def gla_out_fwd(name, o, proj, norm_g, r_block0):
    T, DV = o.shape
    dv = DV // GLA_HEADS
    tm = _pick(T, (512, 256, 128, 64))

    def body(o_ref, r_ref, g_ref, out_ref):
        ov, rv = o_ref[...], r_ref[...]
        out_ref[...] = (ov * _rms(ov) * g_ref[...] * (rv * _sigmoid(rv))).astype(BF16)

    return _pc(
        body, name=name, out_shape=jax.ShapeDtypeStruct((T, DV), BF16), grid=(GLA_HEADS, T // tm),
        in_specs=[pl.BlockSpec((tm, dv), lambda h, i: (i, h)),
                  pl.BlockSpec((tm, dv), lambda h, i: (i, r_block0 + h)),
                  pl.BlockSpec((1, dv), lambda h, i: (0, h))],
        out_specs=pl.BlockSpec((tm, dv), lambda h, i: (i, h)),
        compiler_params=_params("parallel", "parallel"),
    )(o, proj, norm_g)


def gla_out_bwd(name, dog, o, proj, norm_g, r_block0):
    T, DV = o.shape
    dv = DV // GLA_HEADS
    tm = _pick(T, (512, 256, 128, 64))

    def body(d_ref, o_ref, r_ref, g_ref, do_ref, dr_ref, dg_ref):
        @pl.when(pl.program_id(1) == 0)
        def _():
            dg_ref[...] = jnp.zeros(dg_ref.shape, F32)

        d, ov, rv, g = d_ref[...], o_ref[...], r_ref[...], g_ref[...]
        ro = _rms(ov)
        oh = ov * ro
        sg = _sigmoid(rv)
        dn = d * (rv * sg)
        dr_ref[...] = (d * (oh * g) * (sg * (1.0 + rv * (1.0 - sg)))).astype(BF16)
        doh = dn * g
        do_ref[...] = ro * (doh - oh * jnp.mean(doh * oh, axis=-1, keepdims=True))
        dg_ref[...] += _fold8(dn * oh)

    return _pc(
        body, name=name,
        out_shape=[jax.ShapeDtypeStruct((T, DV), F32), jax.ShapeDtypeStruct((T, DV), BF16),
                   jax.ShapeDtypeStruct((8, DV), F32)],
        grid=(GLA_HEADS, T // tm),
        in_specs=[pl.BlockSpec((tm, dv), lambda h, i: (i, h)),
                  pl.BlockSpec((tm, dv), lambda h, i: (i, h)),
                  pl.BlockSpec((tm, dv), lambda h, i: (i, r_block0 + h)),
                  pl.BlockSpec((1, dv), lambda h, i: (0, h))],
        out_specs=[pl.BlockSpec((tm, dv), lambda h, i: (i, h)),
                   pl.BlockSpec((tm, dv), lambda h, i: (i, h)),
                   pl.BlockSpec((8, dv), lambda h, i: (0, h))],
        compiler_params=_params("parallel", "arbitrary"),
    )(dog, o, proj, norm_g)


def _ew_rows(r, c):
    return _pick(r, tuple(t for t in (512, 256, 128, 64, 32, 16, 8) if t * c <= 256 * 1024) or (8,))


def sum_slots_layers(name, lands):
    nl = len(lands)
    n, r, c = lands[0].shape
    tr = _ew_rows(r, c)
    nb = r // tr

    def body(*refs):
        o_ref = refs[nl]
        for l in range(nl):
            @pl.when(pl.program_id(0) == l)
            def _(p_ref=refs[l]):
                acc = p_ref[0].astype(F32)
                for s in range(1, n):
                    acc = acc + p_ref[s].astype(F32)
                o_ref[...] = acc.astype(o_ref.dtype)

    def in_map(l):
        return lambda q, i: (0, jnp.where(q == l, i, 0), 0)

    return _pc(body, name=name, out_shape=jax.ShapeDtypeStruct((nl * r, c), BF16), grid=(nl, nb),
               in_specs=[pl.BlockSpec((n, tr, c), in_map(l)) for l in range(nl)],
               out_specs=pl.BlockSpec((tr, c), lambda q, i: (q * nb + i, 0)),
               compiler_params=_params("arbitrary", "arbitrary"))(*lands)


def sum_slots(name, p):
    n, r, c = p.shape
    tr = _ew_rows(r, c)

    def body(p_ref, o_ref):
        acc = p_ref[0].astype(F32)
        for s in range(1, n):
            acc = acc + p_ref[s].astype(F32)
        o_ref[...] = acc

    return _pc(body, name=name, out_shape=jax.ShapeDtypeStruct((r, c), F32), grid=(r // tr,),
               in_specs=[pl.BlockSpec((n, tr, c), lambda i: (0, i, 0))],
               out_specs=pl.BlockSpec((tr, c), lambda i: (i, 0)),
               compiler_params=_params("parallel"))(p)


def adamw(name, w, gs, m, v, deps=()):
    r, c = w.shape
    tr = _ew_rows(r, c)
    n_g = len(gs)
    m_corr = 1.0 / (1.0 - ADAM_B1 ** ADAM_STEP)
    v_corr = 1.0 / (1.0 - ADAM_B2 ** ADAM_STEP)

    def body(*refs):
        w_ref, g_refs, m_ref, v_ref = refs[0], refs[1:1 + n_g], refs[1 + n_g], refs[2 + n_g]
        g_out, d_out, m_out, v_out = refs[-4:]
        g = g_refs[0][...].astype(F32)
        if n_g == 2:
            g = g + g_refs[1][...].astype(F32)
        mn = ADAM_B1 * m_ref[...] + (1.0 - ADAM_B1) * g
        vn = ADAM_B2 * v_ref[...] + (1.0 - ADAM_B2) * (g * g)
        g_out[...] = g
        m_out[...] = mn
        v_out[...] = vn
        d_out[...] = -ADAM_LR * ((mn * m_corr) / (jnp.sqrt(vn * v_corr) + ADAM_EPS) + ADAM_WD * w_ref[...])

    spec = pl.BlockSpec((tr, c), lambda i: (i, 0))
    return _pc(body, name=name, out_shape=[jax.ShapeDtypeStruct((r, c), F32)] * 4, grid=(r // tr,),
               in_specs=[spec] * (3 + n_g) + [HBM] * len(deps), out_specs=[spec] * 4,
               compiler_params=_params("parallel"))(w, *gs, m, v, *deps)


def _flat_pack(arrs):
    flat = jnp.concatenate([a.reshape(-1).astype(F32) for a in arrs])
    n = flat.shape[0]
    pad = (-n) % 1024
    return jnp.pad(flat, (0, pad)).reshape(-1, 128)


def _flat_unpack(packed, shapes, lead=()):
    flat = packed.reshape(lead + (-1,))
    out, pos = [], 0
    for s in shapes:
        n = 1
        for d in s:
            n *= d
        out.append(flat[..., pos:pos + n].reshape(lead + tuple(s)))
        pos += n
    return out


def _interleave_vec(b, tn):
    f = b.shape[-1] // 2
    return b.reshape(2, f // tn, tn).transpose(1, 0, 2).reshape(1, 2 * f)


def _deinterleave_vec(b, tn):
    f = b.shape[-1] // 2
    return b.reshape(f // tn, 2, tn).transpose(1, 0, 2).reshape(2 * f)


def kernel(x, c, w_ada, b_ada, pre_mix_g, post_mix_g, pre_ffn_g, post_ffn_g, conv_w_pw1, conv_b_pw1, conv_w_dw, conv_b_dw, conv_ln_g, conv_ln_b, conv_w_pw2, conv_b_pw2, gla_w_in, gla_w_gate_up, gla_b_gate, gla_norm_g, gla_w_out, ffn_w_in, ffn_w_out, loss_target, m_w_ada, m_b_ada, m_pre_mix_g, m_post_mix_g, m_pre_ffn_g, m_post_ffn_g, m_conv_w_pw1, m_conv_b_pw1, m_conv_w_dw, m_conv_b_dw, m_conv_ln_g, m_conv_ln_b, m_conv_w_pw2, m_conv_b_pw2, m_gla_w_in, m_gla_w_gate_up, m_gla_b_gate, m_gla_norm_g, m_gla_w_out, m_ffn_w_in, m_ffn_w_out, v_w_ada, v_b_ada, v_pre_mix_g, v_post_mix_g, v_pre_ffn_g, v_post_ffn_g, v_conv_w_pw1, v_conv_b_pw1, v_conv_w_dw, v_conv_b_dw, v_conv_ln_g, v_conv_ln_b, v_conv_w_pw2, v_conv_b_pw2, v_gla_w_in, v_gla_w_gate_up, v_gla_b_gate, v_gla_norm_g, v_gla_w_out, v_ffn_w_in, v_ffn_w_out):
    weights = dict(w_ada=w_ada, b_ada=b_ada, pre_mix_g=pre_mix_g, post_mix_g=post_mix_g, pre_ffn_g=pre_ffn_g, post_ffn_g=post_ffn_g, conv_w_pw1=conv_w_pw1, conv_b_pw1=conv_b_pw1, conv_w_dw=conv_w_dw, conv_b_dw=conv_b_dw, conv_ln_g=conv_ln_g, conv_ln_b=conv_ln_b, conv_w_pw2=conv_w_pw2, conv_b_pw2=conv_b_pw2, gla_w_in=gla_w_in, gla_w_gate_up=gla_w_gate_up, gla_b_gate=gla_b_gate, gla_norm_g=gla_norm_g, gla_w_out=gla_w_out, ffn_w_in=ffn_w_in, ffn_w_out=ffn_w_out)
    mom_m = dict(w_ada=m_w_ada, b_ada=m_b_ada, pre_mix_g=m_pre_mix_g, post_mix_g=m_post_mix_g, pre_ffn_g=m_pre_ffn_g, post_ffn_g=m_post_ffn_g, conv_w_pw1=m_conv_w_pw1, conv_b_pw1=m_conv_b_pw1, conv_w_dw=m_conv_w_dw, conv_b_dw=m_conv_b_dw, conv_ln_g=m_conv_ln_g, conv_ln_b=m_conv_ln_b, conv_w_pw2=m_conv_w_pw2, conv_b_pw2=m_conv_b_pw2, gla_w_in=m_gla_w_in, gla_w_gate_up=m_gla_w_gate_up, gla_b_gate=m_gla_b_gate, gla_norm_g=m_gla_norm_g, gla_w_out=m_gla_w_out, ffn_w_in=m_ffn_w_in, ffn_w_out=m_ffn_w_out)
    mom_v = dict(w_ada=v_w_ada, b_ada=v_b_ada, pre_mix_g=v_pre_mix_g, post_mix_g=v_post_mix_g, pre_ffn_g=v_pre_ffn_g, post_ffn_g=v_post_ffn_g, conv_w_pw1=v_conv_w_pw1, conv_b_pw1=v_conv_b_pw1, conv_w_dw=v_conv_w_dw, conv_b_dw=v_conv_b_dw, conv_ln_g=v_conv_ln_g, conv_ln_b=v_conv_ln_b, conv_w_pw2=v_conv_w_pw2, conv_b_pw2=v_conv_b_pw2, gla_w_in=v_gla_w_in, gla_w_gate_up=v_gla_w_gate_up, gla_b_gate=v_gla_b_gate, gla_norm_g=v_gla_norm_g, gla_w_out=v_gla_w_out, ffn_w_in=v_ffn_w_in, ffn_w_out=v_ffn_w_out)
    order = list(weights)

    ax, ay, ac = lax.axis_index("x"), lax.axis_index("y"), lax.axis_index("c")
    my_chip, my_dev = 2 * ax + ay, 4 * ax + 2 * ay + ac

    x = x[0]
    target = loss_target[0]
    T, D = x.shape
    depth = w_ada.shape[0]
    n_conv, n_gla = conv_w_pw1.shape[0], gla_w_in.shape[0]
    width = conv_w_dw.shape[1]
    F = ffn_w_out.shape[1] * N_CHIPS
    DK = gla_w_gate_up.shape[2] * N_CHIPS
    rank = gla_w_gate_up.shape[1]
    gla_cols = 2 * DK + 2 * D + rank
    P = 2 * DK + 2 * D + 128
    dvh = D // GLA_HEADS
    tn_pw1 = _pick(conv_w_pw1.shape[2], (1024, 512, 256, 128))
    tn_ffn = ffn_w_out.shape[1]

    small_sharded = ["conv_w_dw", "gla_w_gate_up", "gla_b_gate", "gla_norm_g"]
    packed = _flat_pack([weights[n] for n in small_sharded])
    got = allgather_devices("gather_small", packed)[0::2]
    parts = _flat_unpack(got, [weights[n].shape for n in small_sharded], lead=(N_CHIPS,))
    w_dw_full, wgu_full, bgate_full, normg_full = [
        jnp.concatenate([p[s] for s in range(N_CHIPS)], axis=-1) for p in parts]
    w_dw_full = w_dw_full
    wgu_pad = jnp.pad(wgu_full, ((0, 0), (0, 128 - rank), (0, 0)))

    c_act = c * _sigmoid(c)
    c_all = allgather_devices("gather_c", jnp.pad(c_act, ((0, 7), (0, 0))))[:, 0, :]
    c16 = jnp.pad(c_all, ((0, 8), (0, 0))).astype(BF16)
    ada_cols = w_ada.shape[2]
    w_ada2 = w_ada.reshape(depth * D, ada_cols)
    mod_cols = [mm_nn(f"ada_fwd_{i}", c16, Wt(w_ada2, "plain", i, D, ada_cols)) for i in range(depth)]
    mod_cols = jnp.stack(mod_cols).reshape(depth * 16, ada_cols)
    mod_all = allgather_devices("gather_mod", mod_cols)[0::2]
    mod_all = mod_all.reshape(N_CHIPS, depth, 16, ada_cols).transpose(1, 2, 0, 3).reshape(depth, 16, 6 * D)
    mod = lax.dynamic_index_in_dim(mod_all, my_dev, axis=1, keepdims=False) + b_ada
    mod = mod.reshape(depth, 6, 1, D)

    gin_cols = gla_w_in.shape[2]

    def sublayer_weights(g):
        i = g // 2
        if g % 2:
            return [("ffn_w_in", i), ("ffn_w_out", i)]
        return [("conv_w_pw1", i // 2), ("conv_w_pw2", i // 2)] if i % 2 == 0 else [("gla_w_in", i // 2), ("gla_w_out", i // 2)]

    issued = []

    def start_gathers(g, after):
        handles, tokens = {}, []
        for n, l in sublayer_weights(g):
            shard = weights[n][l].astype(BF16)
            handles[n], token = chips_start(f"gather_start_{n}_{l}", shard, shard, my_chip, True,
                                            tuple(after) + tuple(issued[-1:]))
            issued.append(token)
            tokens.append(token)
        return handles, tokens

    def wait_gathers(g, handles, after):
        out = []
        for n, l in sublayer_weights(g):
            got = _split_wait(f"gather_wait_{n}_{l}", handles[n], after)
            r, cc = got.shape[1:]
            if n == "gla_w_in":
                full = got.transpose(1, 0, 2).reshape(D, N_CHIPS * gin_cols)
                out.append(Wt(jnp.pad(full, ((0, 0), (0, P - gla_cols))), "plain", 0, D, P))
            elif n in ("conv_w_pw1", "ffn_w_in"):
                out.append(Wt(got, "col", 0, r, cc, tn_pw1 if n == "conv_w_pw1" else tn_ffn))
            else:
                out.append(Wt(got, "row", 0, r, cc))
        return out

    n_sub = 2 * depth
    in_flight = {g: start_gathers(g, (mod,)) for g in range(min(3, n_sub))}

    def enter_sublayer(g, xs):
        if g + 2 >= n_sub:
            return ()
        if g + 2 not in in_flight:
            in_flight[g + 2] = start_gathers(g + 2, (xs,))
        return in_flight[g + 2][1]

    w_pw1, w_pw2, w_gin, w_gout = [None] * n_conv, [None] * n_conv, [None] * n_gla, [None] * n_gla
    w_fin, w_fout = [None] * depth, [None] * depth
    row = lambda a: a.reshape(1, -1)
    saved = []
    xs = x
    for i in range(depth):
        j = i // 2
        sh1, sc1, gt1, sh2, sc2, gt2 = [mod[i, q] for q in range(6)]
        s = dict(x_in=xs)
        deps = list(enter_sublayer(2 * i, xs))
        if i == 0:
            deps += in_flight[0][1] + in_flight[1][1]
        h = prenorm_fwd(f"pre_mix_{i}", xs, row(pre_mix_g[i]) * (1.0 + sc1), sh1, deps=deps)
        s["h_mix"] = h
        if i % 2 == 0:
            w_pw1[j], w_pw2[j] = wait_gathers(2 * i, in_flight[2 * i][0], (h,))
        else:
            w_gin[j], w_gout[j] = wait_gathers(2 * i, in_flight[2 * i][0], (h,))
        if i % 2 == 0:
            u, glu = gated_in_fwd(f"pw1_{i}", h, w_pw1[j], "glu", bias=_interleave_vec(row(conv_b_pw1[j]), tn_pw1))
            v = dwconv_fwd(f"dwconv_{i}", glu, w_dw_full[j], row(conv_b_dw[j]))
            sl = ln_silu_fwd(f"ln_silu_{i}", v, row(conv_ln_g[j]), row(conv_ln_b[j]))
            y, x_mid = mm_nn_post(f"pw2_{i}", sl, w_pw2[j], xs, gt1 * row(post_mix_g[i]), bias=row(conv_b_pw2[j]))
            s.update(u=u, glu=glu, v=v, s=sl)
        else:
            proj = mm_nn(f"gla_in_{i}", h, w_gin[j])
            o, states = gla_fwd(f"gla_{i}", proj, wgu_pad[j], row(bgate_full[j]))
            og = gla_out_fwd(f"gla_out_{i}", o, proj, row(normg_full[j]), (2 * DK + D) // dvh)
            y, x_mid = mm_nn_post(f"gla_wout_{i}", og, w_gout[j], xs, gt1 * row(post_mix_g[i]))
            s.update(proj=proj, o=o, states=states, og=og)
        s["y_mix"] = y
        xs = x_mid
        s["x_mid"] = xs
        deps = enter_sublayer(2 * i + 1, xs)
        h = prenorm_fwd(f"pre_ffn_{i}", xs, row(pre_ffn_g[i]) * (1.0 + sc2), sh2, deps=deps)
        w_fin[i], w_fout[i] = wait_gathers(2 * i + 1, in_flight[2 * i + 1][0], (h,))
        u, a = gated_in_fwd(f"ffn_in_{i}", h, w_fin[i], "swiglu")
        y, xs = mm_nn_post(f"ffn_out_{i}", a, w_fout[i], xs, gt2 * row(post_ffn_g[i]))
        s.update(h_ffn=h, u_ffn=u, a_ffn=a, y_ffn=y)
        saved.append(s)

    dx, loss_acc = loss_bwd("loss", xs, target)
    loss = lax.psum(jnp.sum(loss_acc), ("x", "y", "c"))

    fold = lambda a: jnp.sum(a, axis=0)
    small_g = {n: [None] * weights[n].shape[0] for n in order if n not in
               ("w_ada", "conv_w_pw1", "conv_w_pw2", "gla_w_in", "gla_w_out", "ffn_w_in", "ffn_w_out")}
    grad_w = lambda wt: Wt(None, wt.kind, 0, wt.R, wt.C, wt.tn)
    scattering = {}

    def start_scatter(n, l, p):
        own = lax.dynamic_index_in_dim(p, my_chip, axis=0, keepdims=False)
        scattering[n, l], token = chips_start(f"scatter_start_{n}_{l}", p, own, my_chip, False)
        return token

    for i in reversed(range(depth)):
        j = i // 2
        s = saved[i]
        sh1, sc1, gt1, sh2, sc2, gt2 = [mod[i, q] for q in range(6)]
        if i == depth - 1:
            dy, d_gt2, d_pg, _ = postnorm_bwd(f"post_ffn_bwd_{i}", dx, s["y_ffn"], row(post_ffn_g[i]), gt2)
        else:
            dy, d_gt2, d_pg = carried
        small_g["post_ffn_g"][i] = fold(d_pg)
        du = ffn_out_dgrad(f"ffn_out_dgrad_{i}", dy, w_fout[i], s["u_ffn"])
        t_out = start_scatter("ffn_w_out", i, mm_tn(f"ffn_out_wgrad_{i}", s["a_ffn"], dy, grad_w(w_fout[i])))
        t_in = start_scatter("ffn_w_in", i, mm_tn(f"ffn_in_wgrad_{i}", s["h_ffn"], du, grad_w(w_fin[i])))
        dh = mm_nt(f"ffn_in_dgrad_{i}", du, w_fin[i], out_dtype=BF16)
        dx, dy, d_sh2, d_sc2, d_pg, d_gt1, d_pg1, dy_sum = norm_chain_bwd(
            f"pre_ffn_bwd_{i}", dh, s["x_mid"], dx, row(pre_ffn_g[i]), sc2,
            s["y_mix"], row(post_mix_g[i]), gt1, deps=(t_out, t_in))
        small_g["pre_ffn_g"][i] = fold(d_pg)
        small_g["post_mix_g"][i] = fold(d_pg1)
        if i % 2 == 0:
            small_g["conv_b_pw2"][j] = fold(dy_sum)
            dsl = mm_nt(f"pw2_dgrad_{i}", dy, w_pw2[j])
            t_out = start_scatter("conv_w_pw2", j, mm_tn(f"pw2_wgrad_{i}", s["s"], dy, grad_w(w_pw2[j])))
            dv, d_lg, d_lb = ln_silu_bwd(f"ln_silu_bwd_{i}", dsl, s["v"], row(conv_ln_g[j]), row(conv_ln_b[j]))
            small_g["conv_ln_g"][j], small_g["conv_ln_b"][j] = fold(d_lg), fold(d_lb)
            dglu, d_wdw, d_bdw = dwconv_bwd(f"dwconv_bwd_{i}", dv, s["glu"], w_dw_full[j])
            small_g["conv_w_dw"][j] = d_wdw.reshape(width, 8, D).sum(axis=1)
            small_g["conv_b_dw"][j] = fold(d_bdw)
            du, du_sum = glu_bwd(f"glu_bwd_{i}", s["u"], dglu, tn_pw1)
            small_g["conv_b_pw1"][j] = _deinterleave_vec(fold(du_sum), tn_pw1)
            t_in = start_scatter("conv_w_pw1", j, mm_tn(f"pw1_wgrad_{i}", s["h_mix"], du, grad_w(w_pw1[j])))
            dh = mm_nt(f"pw1_dgrad_{i}", du, w_pw1[j], out_dtype=BF16)
        else:
            dog = mm_nt(f"gla_wout_dgrad_{i}", dy, w_gout[j])
            t_out = start_scatter("gla_w_out", j, mm_tn(f"gla_wout_wgrad_{i}", s["og"], dy, grad_w(w_gout[j])))
            d_o, d_r, d_ng = gla_out_bwd(f"gla_out_bwd_{i}", dog, s["o"], s["proj"], row(normg_full[j]),
                                         (2 * DK + D) // dvh)
            small_g["gla_norm_g"][j] = fold(d_ng)
            dproj, dz, d_bg = gla_bwd(f"gla_bwd_{i}", s["proj"], wgu_pad[j], row(bgate_full[j]),
                                      s["states"], d_o, d_r)
            small_g["gla_b_gate"][j] = fold(d_bg)
            a_low = s["proj"][:, 2 * DK + 2 * D:].astype(BF16)
            d_wgu = mm_tn(f"gla_gate_wgrad_{i}", a_low, dz, Wt(None, "plain", 0, 128, DK), out_dtype=F32)
            small_g["gla_w_gate_up"][j] = d_wgu[:rank]
            g_in = mm_tn(f"gla_in_wgrad_{i}", s["h_mix"], dproj, Wt(None, "plain", 0, D, P))
            t_in = start_scatter("gla_w_in", j, g_in[:, :gla_cols].reshape(D, N_CHIPS, gin_cols).transpose(1, 0, 2))
            dh = mm_nt(f"gla_in_dgrad_{i}", dproj, w_gin[j], out_dtype=BF16)
        if i > 0:
            dx, dy_b, d_sh1, d_sc1, d_pg, d_gtb, d_pgb, _ = norm_chain_bwd(
                f"pre_mix_bwd_{i}", dh, s["x_in"], dx, row(pre_mix_g[i]), sc1,
                saved[i - 1]["y_ffn"], row(post_ffn_g[i - 1]), mod[i - 1, 5], deps=(t_out, t_in))
            carried = (dy_b, d_gtb, d_pgb)
        else:
            dx, d_sh1, d_sc1, d_pg = prenorm_bwd(f"pre_mix_bwd_{i}", dh, s["x_in"], dx, row(pre_mix_g[i]), sc1,
                                                 deps=(t_out, t_in))
        small_g["pre_mix_g"][i] = fold(d_pg)
        small_g["b_ada"][i] = jnp.concatenate([fold(t) for t in (d_sh1, d_sc1, d_gt1, d_sh2, d_sc2, d_gt2)])
    grad_x = dx[None]

    small_names = list(small_g)
    local_small = [jnp.stack(small_g[n]) for n in small_names]
    full_shapes = [a.shape for a in local_small]
    small_handle, small_token = devices_start("small_grads_start", _flat_pack(local_small), my_dev)

    def flat2(a):
        return a.reshape(-1, a.shape[-1])

    swapping, last = [], (small_token,)
    for n in ("conv_w_pw1", "conv_w_pw2", "gla_w_in", "gla_w_out", "ffn_w_in", "ffn_w_out"):
        arrived = [_split_wait(f"scatter_wait_{n}_{l}", scattering[n, l], (dx,)) for l in range(weights[n].shape[0])]
        mine = sum_slots_layers("sum_" + n, arrived)
        handle, token = swap_start("swap_start_" + n, mine, last)
        swapping.append((n, handle))
        last = (token,)

    small_all = _split_wait("small_grads_wait", small_handle, last)
    small_sum = sum_slots("sum_small_grads", small_all)
    small_tot = dict(zip(small_names, _flat_unpack(small_sum, full_shapes)))
    for n in small_sharded:
        cols = weights[n].shape[-1]
        small_tot[n] = lax.dynamic_slice_in_dim(small_tot[n], my_chip * cols, cols, axis=small_tot[n].ndim - 1)

    dmod_all = _flat_unpack(small_all, full_shapes, lead=(N_DEVICES,))[small_names.index("b_ada")]
    dmod_cols = lax.dynamic_slice_in_dim(
        dmod_all.reshape(N_DEVICES, depth, N_CHIPS, ada_cols), my_chip, 1, axis=2)[:, :, 0, :]
    dmod16 = jnp.pad(dmod_cols.reshape(N_DEVICES, depth * ada_cols), ((0, 8), (0, 0))).astype(BF16)
    g_ada = [mm_tn(f"ada_wgrad_{i}", c16, dmod16[:, i * ada_cols:(i + 1) * ada_cols],
                   Wt(None, "plain", 0, D, ada_cols), out_dtype=F32) for i in range(depth)]
    g_ada = jnp.stack(g_ada).reshape(depth * D, ada_cols)

    results = {}
    out = adamw("adamw_w_ada", flat2(w_ada), [g_ada], flat2(m_w_ada), flat2(v_w_ada))
    results["w_ada"] = [t.reshape(w_ada.shape) for t in out]
    w_small = _flat_pack([weights[n] for n in small_names])
    out_small = adamw("adamw_small", w_small, [_flat_pack([small_tot[n] for n in small_names])],
                      _flat_pack([mom_m[n] for n in small_names]), _flat_pack([mom_v[n] for n in small_names]))
    for n, handle in swapping:
        w = weights[n]
        mine, theirs = _split_wait("swap_wait_" + n, handle, (out[0],), with_src=True)
        res = adamw("adamw_" + n, flat2(w), [mine, theirs], flat2(mom_m[n]), flat2(mom_v[n]))
        results[n] = [t.reshape(w.shape) for t in res]
    out = out_small
    shapes = [weights[n].shape for n in small_names]
    unpacked = [_flat_unpack(t, shapes) for t in out]
    for q, n in enumerate(small_names):
        results[n] = [unpacked[r][q] for r in range(4)]

    outs = [loss, grad_x]
    for r in range(4):
        outs += [results[n][r] for n in order]
    return tuple(outs)
```

```python
import jax
import jax.numpy as jnp
from jax import lax
from jax.experimental import pallas as pl
from jax.experimental.pallas import tpu as pltpu

F32, BF16 = jnp.float32, jnp.bfloat16
MESH = pl.DeviceIdType.MESH
HBM = pl.BlockSpec(memory_space=pl.ANY)

EPS = 1e-6
VMEM_LIMIT_BYTES = 48 * 1024 * 1024
N_CHIPS = 4
N_DEVICES = 8
GLA_HEADS = 4
GLA_CHUNK = 64
GLA_TAU = 16.0
CONV_HALO = 32
ADAM_LR, ADAM_B1, ADAM_B2, ADAM_EPS, ADAM_WD, ADAM_STEP = 0.001, 0.9, 0.999, 1e-08, 0.01, 10


def _pc(body, **kw):
    return pl.pallas_call(body, **kw)


def _params(*sem):
    return pltpu.CompilerParams(dimension_semantics=sem, vmem_limit_bytes=VMEM_LIMIT_BYTES)


def _pick(n, cands):
    for c in cands:
        if c <= n and n % c == 0:
            return c
    return n


def _fold8(z):
    r, w = z.shape
    return z.reshape(r // 8, 8, w).sum(axis=0)


def _sigmoid(x):
    return 1.0 / (1.0 + jnp.exp(-x))


def _exchange(name, xs, out_shape, masks, src_of, dst_of, local=True):
    n_in = len(xs)

    def body(*refs):
        x_refs, o_ref = refs[:n_in], refs[n_in]
        send_sems, recv_sems, local_sems = refs[n_in + 1:]
        x, y, c = lax.axis_index("x"), lax.axis_index("y"), lax.axis_index("c")
        me = (x, y, c)
        peers = [(1 - x if a else x, 1 - y if b else y, 1 - c if d else c) for a, b, d in masks]
        started = []
        if local:
            for q, (s, t) in enumerate(zip(src_of(x_refs, me), dst_of(o_ref, me))):
                cp = pltpu.make_async_copy(s, t, local_sems.at[q])
                cp.start()
                started.append(cp)
        sends = []
        for k, peer in enumerate(peers):
            for q, (s, t) in enumerate(zip(src_of(x_refs, peer), dst_of(o_ref, me))):
                cp = pltpu.make_async_remote_copy(
                    src_ref=s, dst_ref=t, send_sem=send_sems.at[k, q], recv_sem=recv_sems.at[k, q],
                    device_id=peer, device_id_type=MESH)
                cp.start()
                sends.append(cp)
        for k, peer in enumerate(peers):
            for q, (s, t) in enumerate(zip(src_of(x_refs, me), dst_of(o_ref, peer))):
                pltpu.make_async_remote_copy(
                    src_ref=s, dst_ref=t, send_sem=send_sems.at[k, q], recv_sem=recv_sems.at[k, q],
                    device_id=peer, device_id_type=MESH).wait_recv()
        for cp in sends:
            cp.wait_send()
        for cp in started:
            cp.wait()

    n_q = n_in if len(xs) > 1 else 1
    return _pc(
        body, name=name, out_shape=out_shape,
        in_specs=[HBM] * n_in, out_specs=HBM,
        scratch_shapes=[pltpu.SemaphoreType.DMA((len(masks), n_q)),
                        pltpu.SemaphoreType.DMA((len(masks), n_q)),
                        pltpu.SemaphoreType.DMA((n_q,))],
    )(*xs)


_CHIP_MASKS = [(1, 0, 0), (0, 1, 0), (1, 1, 0)]
_ALL_MASKS = [(a, b, d) for a in (0, 1) for b in (0, 1) for d in (0, 1) if (a, b, d) != (0, 0, 0)]


def _chip(p):
    return 2 * p[0] + p[1]


def _dev(p):
    return 4 * p[0] + 2 * p[1] + p[2]


def allgather_devices(name, v):
    r, c = v.shape
    return _exchange(
        name, [v], jax.ShapeDtypeStruct((N_DEVICES, r, c), v.dtype), _ALL_MASKS,
        lambda xr, peer: [xr[0]], lambda o, src: [o.at[_dev(src)]])


_HBM = pl.BlockSpec(memory_space=pltpu.HBM)
_SEM = pl.BlockSpec(memory_space=pltpu.SEMAPHORE)
_N_PEER_CHIPS = len(_CHIP_MASKS)


def _peer_chips():
    x, y, c = lax.axis_index("x"), lax.axis_index("y"), lax.axis_index("c")
    return (x, y), [((1 - x if a else x, 1 - y if b else y), c) for a, b, _ in _CHIP_MASKS]


def _chip_copies(src_ref, land_ref, send_sems, recv_sems, whole_src, incoming):
    me, peers = _peer_chips()
    out = []
    for k, (chip, c) in enumerate(peers):
        out.append(pltpu.make_async_remote_copy(
            src_ref=src_ref if whole_src else src_ref.at[_chip(chip)],
            dst_ref=land_ref.at[_chip(chip) if incoming else _chip(me)],
            send_sem=send_sems.at[k], recv_sem=recv_sems.at[k], device_id=(*chip, c), device_id_type=MESH))
    return out


def _core_copies(src_ref, land_ref, send_sems, recv_sems, incoming):
    x, y, c = lax.axis_index("x"), lax.axis_index("y"), lax.axis_index("c")
    return [pltpu.make_async_remote_copy(
        src_ref=src_ref, dst_ref=land_ref, send_sem=send_sems.at[0], recv_sem=recv_sems.at[0],
        device_id=(x, y, 1 - c), device_id_type=MESH)]


def _split_start(name, src, land, n_copies, copies, after):
    n_after = len(after)

    def body(*refs):
        src_ref, land_ref = refs[0], refs[1]
        send_sems, recv_sems, _, _, token = refs[2 + n_after:]
        for send in copies(src_ref, land_ref, send_sems, recv_sems, False):
            send.start()
        token[...] = jnp.zeros(token.shape, token.dtype)

    send_sems, recv_sems, src_thru, land_thru, token = _pc(
        body, name=name,
        out_shape=(pltpu.SemaphoreType.DMA((n_copies,)), pltpu.SemaphoreType.DMA((n_copies,)),
                   pltpu.HBM(src.shape, src.dtype), pltpu.HBM(land.shape, land.dtype),
                   jax.ShapeDtypeStruct((8, 128), F32)),
        in_specs=[_HBM, _HBM] + [HBM] * n_after,
        out_specs=(_SEM, _SEM, _HBM, _HBM, pl.BlockSpec(memory_space=pltpu.VMEM)),
        input_output_aliases={0: 2, 1: 3},
        compiler_params=pltpu.CompilerParams(has_side_effects=pltpu.SideEffectType.DATAFLOW_SIDE_EFFECTING),
    )(pltpu.with_memory_space_constraint(src, pltpu.HBM), pltpu.with_memory_space_constraint(land, pltpu.HBM),
      *after)
    return (send_sems, recv_sems, src_thru, land_thru, copies), token


def _split_wait(name, handle, after, with_src=False):
    send_sems, recv_sems, src_thru, land_thru, copies = handle
    n_after = len(after)

    def body(src_ref, land_ref, send_sems, recv_sems, *rest):
        for send in copies(src_ref, land_ref, send_sems, recv_sems, False):
            send.wait_send()
        for recv in copies(src_ref, land_ref, send_sems, recv_sems, True):
            recv.wait_recv()

    src, land = _pc(
        body, name=name,
        out_shape=(pltpu.HBM(src_thru.shape, src_thru.dtype), pltpu.HBM(land_thru.shape, land_thru.dtype)),
        in_specs=[_HBM, _HBM, _SEM, _SEM] + [HBM] * n_after, out_specs=(_HBM, _HBM),
        input_output_aliases={0: 0, 1: 1},
        compiler_params=pltpu.CompilerParams(has_side_effects=pltpu.SideEffectType.DATAFLOW_SIDE_EFFECTING),
    )(src_thru, land_thru, send_sems, recv_sems, *after)
    return (src, land) if with_src else land


def chips_start(name, src, own, my_chip, whole_src, after=()):
    r, c = own.shape
    land = lax.dynamic_update_slice(lax.empty((N_CHIPS, r, c), src.dtype), own[None], (my_chip, 0, 0))

    def copies(src_ref, land_ref, send_sems, recv_sems, incoming):
        return _chip_copies(src_ref, land_ref, send_sems, recv_sems, whole_src, incoming)

    return _split_start(name, src, land, _N_PEER_CHIPS, copies, after)


def _device_copies(src_ref, land_ref, send_sems, recv_sems, incoming):
    x, y, c = lax.axis_index("x"), lax.axis_index("y"), lax.axis_index("c")
    out = []
    for k, (a, b, d) in enumerate(_ALL_MASKS):
        peer = (1 - x if a else x, 1 - y if b else y, 1 - c if d else c)
        out.append(pltpu.make_async_remote_copy(
            src_ref=src_ref, dst_ref=land_ref.at[_dev(peer) if incoming else _dev((x, y, c))],
            send_sem=send_sems.at[k], recv_sem=recv_sems.at[k], device_id=peer, device_id_type=MESH))
    return out


def devices_start(name, v, my_dev, after=()):
    land = lax.dynamic_update_slice(lax.empty((N_DEVICES,) + v.shape, v.dtype), v[None], (my_dev, 0, 0))
    return _split_start(name, v, land, len(_ALL_MASKS), _device_copies, after)


def swap_start(name, a, after=()):
    return _split_start(name, a, lax.empty(a.shape, a.dtype), 1, _core_copies, after)


class Wt:
    def __init__(self, arr, kind, layer, rows, cols, tn=None):
        self.arr, self.kind, self.l, self.R, self.C, self.tn = arr, kind, layer, rows, cols, tn
        self.K = 4 * rows if kind == "row" else rows
        self.N = 4 * cols if kind == "col" else cols

    def spec(self, tk, tn):
        l, R, C = self.l, self.R, self.C
        if self.kind == "plain":
            off = l * (R // tk)
            return (tk, tn), lambda kb, jb: (off + kb, jb)
        if self.kind == "col":
            assert tn == self.tn
            per, off = C // tn, l * (R // tk)
            return (None, tk, tn), lambda kb, jb: (2 * (jb % 2) + (jb // 2) // per, off + kb, (jb // 2) % per)
        per = R // tk
        return (None, tk, tn), lambda kb, jb: (kb // per, l * per + kb % per, jb)

    def tile_k(self, cands):
        return _pick(self.R, cands)

    def tile_n(self, cands):
        return self.tn if self.kind == "col" else _pick(self.C, cands)


_TM = (1024, 512, 256, 128, 64, 32, 16, 8)
_TK = (2048, 1408, 1024, 896, 512, 256, 128, 64, 32, 16)
_TN = (1024, 896, 768, 512, 384, 256, 128)
_TR = (1024, 1408, 512, 256, 128, 64, 32, 16)


def _accumulate(part, acc_ref, k, nk, finish):
    if nk == 1:
        finish(part)
        return

    @pl.when(k == 0)
    def _():
        acc_ref[...] = part

    @pl.when(k > 0)
    def _():
        acc_ref[...] += part

    @pl.when(k == nk - 1)
    def _():
        finish(acc_ref[...])


def mm_nn(name, a, w, *, bias=None, out_dtype=F32):
    M, K = a.shape
    assert K == w.K
    N = w.N
    tm, tk, tn = _pick(M, _TM), w.tile_k(_TK), w.tile_n(_TN)
    nk = K // tk
    wblock, wmap = w.spec(tk, tn)
    in_specs = [pl.BlockSpec((tm, tk), lambda i, j, k: (i, k)),
                pl.BlockSpec(wblock, lambda i, j, k: wmap(k, j))]
    args = [a, w.arr]
    if bias is not None:
        in_specs.append(pl.BlockSpec((1, tn), lambda i, j, k: (0, j)))
        args.append(bias)

    def body(*refs):
        a_ref, b_ref = refs[0], refs[1]
        bias_ref = refs[2] if bias is not None else None
        o_ref = refs[3] if bias is not None else refs[2]
        acc_ref = refs[-1] if nk > 1 else None
        part = jnp.dot(a_ref[...].astype(BF16), b_ref[...].astype(BF16), preferred_element_type=F32)

        def finish(acc):
            if bias_ref is not None:
                acc = acc + bias_ref[...]
            o_ref[...] = acc.astype(o_ref.dtype)

        _accumulate(part, acc_ref, pl.program_id(2), nk, finish)

    return _pc(
        body, name=name, out_shape=jax.ShapeDtypeStruct((M, N), out_dtype),
        grid=(M // tm, N // tn, nk), in_specs=in_specs,
        out_specs=pl.BlockSpec((tm, tn), lambda i, j, k: (i, j)),
        scratch_shapes=[pltpu.VMEM((tm, tn), F32)] if nk > 1 else [],
        compiler_params=_params("parallel", "parallel", "arbitrary"),
    )(*args)


def mm_nt(name, a, w, *, out_dtype=F32):
    M, N = a.shape
    assert N == w.N
    K = w.K
    tm, tj, tn = _pick(M, _TM), w.tile_k(_TR), w.tile_n(_TK)
    wblock, wmap = w.spec(tj, tn)
    nb = 2 if w.kind == "col" else 1
    nn = N // (nb * tn)

    def body(a_ref, *refs):
        b_refs, o_ref, scr = refs[:nb], refs[nb], refs[nb + 1:]
        part = None
        for q, b_ref in enumerate(b_refs):
            p = lax.dot_general(a_ref[:, q * tn:(q + 1) * tn].astype(BF16), b_ref[...].astype(BF16),
                                (((1,), (1,)), ((), ())), preferred_element_type=F32)
            part = p if part is None else part + p

        def finish(acc):
            o_ref[...] = acc.astype(o_ref.dtype)

        _accumulate(part, scr[0] if nn > 1 else None, pl.program_id(2), nn, finish)

    def b_spec(q):
        return pl.BlockSpec(wblock, lambda i, j, n: wmap(j, nb * n + q))

    return _pc(
        body, name=name, out_shape=jax.ShapeDtypeStruct((M, K), out_dtype),
        grid=(M // tm, K // tj, nn),
        in_specs=[pl.BlockSpec((tm, nb * tn), lambda i, j, n: (i, n))] + [b_spec(q) for q in range(nb)],
        out_specs=pl.BlockSpec((tm, tj), lambda i, j, n: (i, j)),
        scratch_shapes=[pltpu.VMEM((tm, tj), F32)] if nn > 1 else [],
        compiler_params=_params("parallel", "parallel", "arbitrary"),
    )(a, *([w.arr] * nb))


def mm_tn(name, a, b, w, *, out_dtype=BF16):
    T, K = a.shape
    N = b.shape[1]
    assert (K, N) == (w.K, w.N) and w.l == 0
    tm = w.tile_k(_TR)
    tn = w.tile_n(_TN)
    tk = _pick(T, (2048, 1024, 512, 256, 128, 64, 32, 16))
    nk = T // tk
    oblock, omap = w.spec(tm, tn)
    shape = (w.R, w.C) if w.kind == "plain" else (N_CHIPS, w.R, w.C)

    def body(a_ref, b_ref, o_ref, *scr):
        part = lax.dot_general(a_ref[...].astype(BF16), b_ref[...].astype(BF16),
                               (((0,), (0,)), ((), ())), preferred_element_type=F32)

        def finish(acc):
            o_ref[...] = acc.astype(o_ref.dtype)

        _accumulate(part, scr[0] if nk > 1 else None, pl.program_id(2), nk, finish)

    return _pc(
        body, name=name, out_shape=jax.ShapeDtypeStruct(shape, out_dtype),
        grid=(K // tm, N // tn, nk),
        in_specs=[pl.BlockSpec((tk, tm), lambda i, j, k: (k, i)),
                  pl.BlockSpec((tk, tn), lambda i, j, k: (k, j))],
        out_specs=pl.BlockSpec(oblock, lambda i, j, k: omap(i, j)),
        scratch_shapes=[pltpu.VMEM((tm, tn), F32)] if nk > 1 else [],
        compiler_params=_params("parallel", "parallel", "arbitrary"),
    )(a, b)


_TM_FUSED = (512, 256, 128, 64, 32, 16, 8)


def mm_nn_post(name, a, w, x, wvec, *, bias=None, nxt=None):
    M, K = a.shape
    assert K == w.K
    N = w.N
    tm, tk = _pick(M, _TM_FUSED), w.tile_k(_TK)
    rows = _pick(tm, (64, 32, 16, 8))
    nk = K // tk
    wblock, wmap = w.spec(tk, N)
    in_specs = [pl.BlockSpec((tm, tk), lambda i, k: (i, k)),
                pl.BlockSpec(wblock, lambda i, k: wmap(k, 0)),
                pl.BlockSpec((tm, N), lambda i, k: (i, 0)),
                pl.BlockSpec((1, N), lambda i, k: (0, 0))]
    args = [a, w.arr, x, wvec]
    for extra in ([bias] if bias is not None else []) + list(nxt or ()):
        in_specs.append(pl.BlockSpec((1, N), lambda i, k: (0, 0)))
        args.append(extra)
    n_out = 3 if nxt else 2

    def body(*refs):
        a_ref, b_ref, x_ref, wv_ref = refs[:4]
        bias_ref = refs[4] if bias is not None else None
        gain_ref, shift_ref = refs[-n_out - 3:-n_out - 1] if nxt else (None, None)
        y_ref, xn_ref = refs[-n_out - 1:-n_out + 1]
        acc_ref = refs[-1]
        part = jnp.dot(a_ref[...].astype(BF16), b_ref[...].astype(BF16), preferred_element_type=F32)
        k = pl.program_id(1)

        @pl.when(k == 0)
        def _():
            acc_ref[...] = part

        @pl.when(k > 0)
        def _():
            acc_ref[...] += part

        @pl.when(k == nk - 1)
        def _():
            for r0 in range(0, tm, rows):
                rr = pl.ds(r0, rows)
                yv = acc_ref[rr, :]
                if bias_ref is not None:
                    yv = yv + bias_ref[...]
                y_ref[rr, :] = yv.astype(BF16)
                xn = x_ref[rr, :] + yv * _rms(yv) * wv_ref[...]
                xn_ref[rr, :] = xn
                if nxt:
                    refs[-2][rr, :] = (xn * _rms(xn) * gain_ref[...] + shift_ref[...]).astype(BF16)

    spec = pl.BlockSpec((tm, N), lambda i, k: (i, 0))
    out_shape = [jax.ShapeDtypeStruct((M, N), BF16), jax.ShapeDtypeStruct((M, N), F32)]
    if nxt:
        out_shape.append(jax.ShapeDtypeStruct((M, N), BF16))
    return _pc(
        body, name=name, out_shape=out_shape,
        grid=(M // tm, nk), in_specs=in_specs, out_specs=[spec] * n_out,
        scratch_shapes=[pltpu.VMEM((tm, N), F32)],
        compiler_params=_params("parallel", "arbitrary"),
    )(*args)


def gated_in_fwd(name, h, w, unit, *, bias=None):
    M, K = h.shape
    assert w.kind == "col" and K == w.K == w.R
    tn, F = w.tn, w.N // 2
    tm = _pick(M, _TM_FUSED)
    wblock, wmap = w.spec(K, tn)
    in_specs = [pl.BlockSpec((tm, K), lambda g, i: (i, 0)),
                pl.BlockSpec(wblock, lambda g, i: wmap(0, 2 * g)),
                pl.BlockSpec(wblock, lambda g, i: wmap(0, 2 * g + 1))]
    args = [h, w.arr, w.arr]
    if bias is not None:
        in_specs.append(pl.BlockSpec((1, 2 * tn), lambda g, i: (0, g)))
        args.append(bias)

    def body(h_ref, w1_ref, w2_ref, *refs):
        u_ref, act_ref = refs[-2:]
        hv = h_ref[...].astype(BF16)
        first = jnp.dot(hv, w1_ref[...].astype(BF16), preferred_element_type=F32)
        second = jnp.dot(hv, w2_ref[...].astype(BF16), preferred_element_type=F32)
        if bias is not None:
            first, second = first + refs[0][:, :tn], second + refs[0][:, tn:]
        u_ref[:, :tn] = first.astype(BF16)
        u_ref[:, tn:] = second.astype(BF16)
        if unit == "swiglu":
            act_ref[...] = (first * _sigmoid(first) * second).astype(act_ref.dtype)
        else:
            act_ref[...] = (first * _sigmoid(second)).astype(act_ref.dtype)

    return _pc(
        body, name=name,
        out_shape=[jax.ShapeDtypeStruct((M, 2 * F), BF16),
                   jax.ShapeDtypeStruct((M, F), BF16 if unit == "swiglu" else F32)],
        grid=(F // tn, M // tm), in_specs=in_specs,
        out_specs=[pl.BlockSpec((tm, 2 * tn), lambda g, i: (i, g)),
                   pl.BlockSpec((tm, tn), lambda g, i: (i, g))],
        compiler_params=_params("parallel", "parallel"),
    )(*args)


def ffn_out_dgrad(name, dy, w, u):
    M, N = dy.shape
    assert w.kind == "row" and N == w.N == w.C
    tj, F = w.R, w.K
    tm = _pick(M, _TM_FUSED)
    wblock, wmap = w.spec(tj, N)

    def body(dy_ref, b_ref, u_ref, du_ref):
        d = lax.dot_general(dy_ref[...].astype(BF16), b_ref[...].astype(BF16),
                            (((1,), (1,)), ((), ())), preferred_element_type=F32)
        gate, up = u_ref[:, :tj].astype(F32), u_ref[:, tj:].astype(F32)
        sg = _sigmoid(gate)
        du_ref[:, :tj] = (d * up * (sg * (1.0 + gate * (1.0 - sg)))).astype(BF16)
        du_ref[:, tj:] = (d * gate * sg).astype(BF16)

    return _pc(
        body, name=name, out_shape=jax.ShapeDtypeStruct((M, 2 * F), BF16),
        grid=(F // tj, M // tm),
        in_specs=[pl.BlockSpec((tm, N), lambda j, i: (i, 0)),
                  pl.BlockSpec(wblock, lambda j, i: wmap(j, 0)),
                  pl.BlockSpec((tm, 2 * tj), lambda j, i: (i, j))],
        out_specs=pl.BlockSpec((tm, 2 * tj), lambda j, i: (i, j)),
        compiler_params=_params("parallel", "parallel"),
    )(dy, w.arr, u)


def _rows(name, body, rows_in, vecs_in, rows_out, accs_out, tm=256, deps=()):
    T = rows_in[0].shape[0]
    tm = _pick(T, (tm, 128, 64, 32, 16, 8))
    n_r, n_v, n_o, n_d = len(rows_in), len(vecs_in), len(rows_out), len(deps)

    def kern(*refs):
        r_refs, v_refs = refs[:n_r], refs[n_r:n_r + n_v]
        refs = refs[n_r + n_v + n_d:]
        o_refs, a_refs = refs[:n_o], refs[n_o:]

        @pl.when(pl.program_id(0) == 0)
        def _():
            for a_ref in a_refs:
                a_ref[...] = jnp.zeros(a_ref.shape, F32)

        body(r_refs, v_refs, o_refs, a_refs)

    in_specs = [pl.BlockSpec((tm, a.shape[1]), lambda i: (i, 0)) for a in rows_in]
    in_specs += [pl.BlockSpec(v.shape, lambda i: (0, 0)) for v in vecs_in]
    in_specs += [HBM] * n_d
    out_shape = [jax.ShapeDtypeStruct((T, w), dt) for w, dt in rows_out]
    out_shape += [jax.ShapeDtypeStruct((8, w), F32) for w in accs_out]
    out_specs = [pl.BlockSpec((tm, w), lambda i: (i, 0)) for w, _ in rows_out]
    out_specs += [pl.BlockSpec((8, w), lambda i: (0, 0)) for w in accs_out]
    return _pc(kern, name=name, out_shape=out_shape, grid=(T // tm,), in_specs=in_specs,
               out_specs=out_specs, compiler_params=_params("arbitrary"))(*rows_in, *vecs_in, *deps)


def _rms(x):
    return lax.rsqrt(jnp.mean(x * x, axis=-1, keepdims=True) + EPS)


def prenorm_fwd(name, x, gain, shift, deps=()):
    def body(r, v, o, a):
        xv = r[0][...]
        o[0][...] = (xv * _rms(xv) * v[0][...] + v[1][...]).astype(BF16)
    return _rows(name, body, [x], [gain, shift], [(x.shape[1], BF16)], [], deps=deps)[0]


def postnorm_bwd(name, dxo, y, post_g, gate):
    def body(r, v, o, a):
        d, yv = r[0][...], r[1][...].astype(F32)
        pg, gt = v[0][...], v[1][...]
        ry = _rms(yv)
        yn = yv * ry
        t = d * yn
        dyn = d * (gt * pg)
        dy = ry * (dyn - yn * jnp.mean(dyn * yn, axis=-1, keepdims=True))
        o[0][...] = dy.astype(BF16)
        a[0][...] += _fold8(t)
        a[1][...] += _fold8(dy)
    D = y.shape[1]
    return _rows(name, body, [dxo, y], [post_g, gate], [(D, BF16)], [D, D])


def prenorm_bwd(name, dh, x, dxo, pre_g, scale, deps=()):
    def body(r, v, o, a):
        dhv, xv, d = r[0][...].astype(F32), r[1][...], r[2][...]
        pg, sc1 = v[0][...], 1.0 + v[1][...]
        rx = _rms(xv)
        xn = xv * rx
        t = dhv * xn
        dxn = dhv * (pg * sc1)
        o[0][...] = d + rx * (dxn - xn * jnp.mean(dxn * xn, axis=-1, keepdims=True))
        a[0][...] += _fold8(dhv)
        a[1][...] += _fold8(t)
    D = x.shape[1]
    return _rows(name, body, [dh, x, dxo], [pre_g, scale], [(D, F32)], [D, D], deps=deps)


def norm_chain_bwd(name, dh, x, dxo, pre_g, scale, y, post_g, gate, deps=()):
    def body(r, v, o, a):
        dhv, xv, d, yv = r[0][...].astype(F32), r[1][...], r[2][...], r[3][...].astype(F32)
        pg, sc1, pg2, gt = v[0][...], 1.0 + v[1][...], v[2][...], v[3][...]
        rx = _rms(xv)
        xn = xv * rx
        t = dhv * xn
        dxn = dhv * (pg * sc1)
        dx = d + rx * (dxn - xn * jnp.mean(dxn * xn, axis=-1, keepdims=True))
        o[0][...] = dx
        a[0][...] += _fold8(dhv)
        a[1][...] += _fold8(t)
        ry = _rms(yv)
        yn = yv * ry
        t2 = dx * yn
        dyn = dx * (gt * pg2)
        dy = ry * (dyn - yn * jnp.mean(dyn * yn, axis=-1, keepdims=True))
        o[1][...] = dy.astype(BF16)
        a[2][...] += _fold8(t2)
        a[3][...] += _fold8(dy)
    D = x.shape[1]
    return _rows(name, body, [dh, x, dxo, y], [pre_g, scale, post_g, gate], [(D, F32), (D, BF16)], [D] * 4,
                 deps=deps)


def loss_bwd(name, y, target):
    D = y.shape[1]

    def body(r, v, o, a):
        e = r[0][...] - r[1][...]
        o[0][...] = e * (1.0 / D)
        a[0][...] += _fold8(e * e * (0.5 / D))
    return _rows(name, body, [y, target], [], [(D, F32)], [D])


def ln_silu_fwd(name, v_, g, b):
    def body(r, v, o, a):
        x = r[0][...]
        xc = x - jnp.mean(x, axis=-1, keepdims=True)
        ln = xc * lax.rsqrt(jnp.mean(xc * xc, axis=-1, keepdims=True) + EPS) * v[0][...] + v[1][...]
        o[0][...] = (ln * _sigmoid(ln)).astype(BF16)
    return _rows(name, body, [v_], [g, b], [(v_.shape[1], BF16)], [])[0]


def ln_silu_bwd(name, ds, v_, g, b):
    def body(r, v, o, a):
        dsv, x = r[0][...], r[1][...]
        xc = x - jnp.mean(x, axis=-1, keepdims=True)
        rstd = lax.rsqrt(jnp.mean(xc * xc, axis=-1, keepdims=True) + EPS)
        xh = xc * rstd
        ln = xh * v[0][...] + v[1][...]
        sg = _sigmoid(ln)
        dln = dsv * (sg * (1.0 + ln * (1.0 - sg)))
        dxh = dln * v[0][...]
        o[0][...] = rstd * (dxh - jnp.mean(dxh, axis=-1, keepdims=True)
                            - xh * jnp.mean(dxh * xh, axis=-1, keepdims=True))
        a[0][...] += _fold8(dln * xh)
        a[1][...] += _fold8(dln)
    D = v_.shape[1]
    return _rows(name, body, [ds, v_], [g, b], [(D, F32)], [D, D])


def _pairs(name, body, u, others, out_w, out_dtype, tn, n_acc=0, tm=256):
    T, F2 = u.shape
    F = F2 // 2
    tm = _pick(T, (tm, 128, 64, 32, 16, 8))

    def kern(*refs):
        u_ref, o_refs = refs[0], refs[1:1 + len(others)]
        out_ref, acc_refs = refs[1 + len(others)], refs[2 + len(others):]

        @pl.when(pl.program_id(1) == 0)
        def _():
            for a_ref in acc_refs:
                a_ref[...] = jnp.zeros(a_ref.shape, F32)

        body(u_ref, o_refs, out_ref, acc_refs)

    out_shape = [jax.ShapeDtypeStruct((T, out_w * F), out_dtype)]
    out_shape += [jax.ShapeDtypeStruct((8, F2), F32)] * n_acc
    out_specs = [pl.BlockSpec((tm, out_w * tn), lambda g, i: (i, g))]
    out_specs += [pl.BlockSpec((8, 2 * tn), lambda g, i: (0, g))] * n_acc
    return _pc(kern, name=name, out_shape=out_shape, grid=(F // tn, T // tm),
               in_specs=[pl.BlockSpec((tm, 2 * tn), lambda g, i: (i, g))]
               + [pl.BlockSpec((tm, tn), lambda g, i: (i, g))] * len(others),
               out_specs=out_specs, compiler_params=_params("parallel", "arbitrary"))(u, *others)


def glu_bwd(name, u, dglu, tn):
    def body(u_ref, o, out, acc):
        a, g = u_ref[:, :tn].astype(F32), u_ref[:, tn:].astype(F32)
        d = o[0][...]
        sg = _sigmoid(g)
        da, dg = d * sg, d * a * (sg * (1.0 - sg))
        out[:, :tn] = da.astype(BF16)
        out[:, tn:] = dg.astype(BF16)
        acc[0][:, :tn] += _fold8(da)
        acc[0][:, tn:] += _fold8(dg)
    return _pairs(name, body, u, [dglu], 2, BF16, tn, n_acc=1)


_CONV_ROWS, _CONV_LANES = 64, 128
_SUBLANES = 8


def _fill_shifts(ext, sh, n):
    for r in range(1, _SUBLANES):
        sh[r - 1, pl.ds(0, n), :] = ext[pl.ds(r, n), :]


def _tap(ext, sh, o, r0, rows, ln):
    q, r = divmod(o, _SUBLANES)
    if r == 0:
        return ext[pl.ds(r0 + o, rows), ln]
    return sh[r - 1, pl.ds(r0 + _SUBLANES * q, rows), ln]


def dwconv_fwd(name, x, w, b):
    T, D = x.shape
    width = w.shape[0]
    tt, cb = _pick(T, (256, 128, 64)), _pick(D, (256, 128))
    off = CONV_HALO - (width - 1)
    rows = min(_CONV_ROWS, tt)
    n_sh = tt + CONV_HALO - _SUBLANES

    def body(cur_ref, prev_ref, w_ref, b_ref, o_ref, ext, sh):
        t = pl.program_id(1)
        tail = prev_ref[pl.ds(tt - CONV_HALO, CONV_HALO), :]
        ext[pl.ds(0, CONV_HALO), :] = jnp.where(t > 0, tail, 0.0)
        ext[pl.ds(CONV_HALO, tt), :] = cur_ref[...]
        _fill_shifts(ext, sh, n_sh)
        for l0 in range(0, cb, _CONV_LANES):
            ln = pl.ds(l0, _CONV_LANES)
            for r0 in range(0, tt, rows):
                acc = jnp.broadcast_to(b_ref[:, ln], (rows, _CONV_LANES))
                for k in range(width):
                    acc = acc + _tap(ext, sh, off + k, r0, rows, ln) * w_ref[pl.ds(k, 1), ln]
                o_ref[pl.ds(r0, rows), ln] = acc

    return _pc(
        body, name=name, out_shape=jax.ShapeDtypeStruct((T, D), F32), grid=(D // cb, T // tt),
        in_specs=[pl.BlockSpec((tt, cb), lambda c, t: (t, c)),
                  pl.BlockSpec((tt, cb), lambda c, t: (jnp.maximum(t - 1, 0), c)),
                  pl.BlockSpec((width, cb), lambda c, t: (0, c)),
                  pl.BlockSpec((1, cb), lambda c, t: (0, c))],
        out_specs=pl.BlockSpec((tt, cb), lambda c, t: (t, c)),
        scratch_shapes=[pltpu.VMEM((tt + CONV_HALO, cb), F32), pltpu.VMEM((_SUBLANES - 1, tt + CONV_HALO, cb), F32)],
        compiler_params=_params("parallel", "arbitrary"),
    )(x, x, w, b)


def dwconv_bwd(name, dv, x, w):
    T, D = x.shape
    width = w.shape[0]
    tt, cb = _pick(T, (256, 128, 64)), _pick(D, (256, 128))
    off = CONV_HALO - (width - 1)
    rows = min(_CONV_ROWS, tt)
    nt = T // tt
    n_sh = tt + CONV_HALO - _SUBLANES

    def body(dv_ref, dvn_ref, x_ref, xp_ref, w_ref, dx_ref, dw_ref, db_ref, ext_d, ext_x, sh_d, sh_x):
        t = pl.program_id(1)

        @pl.when(t == 0)
        def _():
            dw_ref[...] = jnp.zeros(dw_ref.shape, F32)
            db_ref[...] = jnp.zeros(db_ref.shape, F32)

        ext_d[pl.ds(0, tt), :] = dv_ref[...]
        ext_d[pl.ds(tt, CONV_HALO), :] = jnp.where(t < nt - 1, dvn_ref[pl.ds(0, CONV_HALO), :], 0.0)
        ext_x[pl.ds(0, CONV_HALO), :] = jnp.where(t > 0, xp_ref[pl.ds(tt - CONV_HALO, CONV_HALO), :], 0.0)
        ext_x[pl.ds(CONV_HALO, tt), :] = x_ref[...]
        db_ref[...] += _fold8(dv_ref[...])
        _fill_shifts(ext_d, sh_d, n_sh)
        _fill_shifts(ext_x, sh_x, n_sh)
        for l0 in range(0, cb, _CONV_LANES):
            ln = pl.ds(l0, _CONV_LANES)
            for r0 in range(0, tt, rows):
                acc = jnp.zeros((rows, _CONV_LANES), F32)
                for k in range(width):
                    acc = acc + _tap(ext_d, sh_d, (width - 1) - k, r0, rows, ln) * w_ref[pl.ds(k, 1), ln]
                dx_ref[pl.ds(r0, rows), ln] = acc
            sums = [jnp.zeros((8, _CONV_LANES), F32)] * width
            for r0 in range(0, tt, rows):
                dvb = ext_d[pl.ds(r0, rows), ln]
                sums = [sums[k] + _fold8(dvb * _tap(ext_x, sh_x, off + k, r0, rows, ln)) for k in range(width)]
            for k in range(width):
                dw_ref[pl.ds(8 * k, 8), ln] += sums[k]

    return _pc(
        body, name=name,
        out_shape=[jax.ShapeDtypeStruct((T, D), F32), jax.ShapeDtypeStruct((8 * width, D), F32),
                   jax.ShapeDtypeStruct((8, D), F32)],
        grid=(D // cb, nt),
        in_specs=[pl.BlockSpec((tt, cb), lambda c, t: (t, c)),
                  pl.BlockSpec((tt, cb), lambda c, t: (jnp.minimum(t + 1, nt - 1), c)),
                  pl.BlockSpec((tt, cb), lambda c, t: (t, c)),
                  pl.BlockSpec((tt, cb), lambda c, t: (jnp.maximum(t - 1, 0), c)),
                  pl.BlockSpec((width, cb), lambda c, t: (0, c))],
        out_specs=[pl.BlockSpec((tt, cb), lambda c, t: (t, c)),
                   pl.BlockSpec((8 * width, cb), lambda c, t: (0, c)),
                   pl.BlockSpec((8, cb), lambda c, t: (0, c))],
        scratch_shapes=[pltpu.VMEM((tt + CONV_HALO, cb), F32), pltpu.VMEM((tt + CONV_HALO, cb), F32),
                        pltpu.VMEM((_SUBLANES - 1, tt + CONV_HALO, cb), F32),
                        pltpu.VMEM((_SUBLANES - 1, tt + CONV_HALO, cb), F32)],
        compiler_params=_params("parallel", "arbitrary"),
    )(dv, dv, x, x, w)


def _gla_dims(proj, wgu):
    DK = wgu.shape[1]
    DV = (proj.shape[1] - 128 - 2 * DK) // 2
    return DK, DV, DK // GLA_HEADS, DV // GLA_HEADS


def _gla_decay(a_ref, w_ref, bg_ref):
    C = GLA_CHUNK
    z = jnp.dot(a_ref[...].astype(BF16), w_ref[...].astype(BF16), preferred_element_type=F32) + bg_ref[...]
    g = (jnp.minimum(z, 0.0) - jnp.log(1.0 + jnp.exp(-jnp.abs(z)))) * (1.0 / GLA_TAU)
    row = lax.broadcasted_iota(jnp.int32, (C, C), 0)
    col = lax.broadcasted_iota(jnp.int32, (C, C), 1)
    bc = jnp.dot((row >= col).astype(F32), g, precision=lax.Precision.HIGHEST, preferred_element_type=F32)
    last = lax.broadcasted_iota(jnp.int32, bc.shape, 0) == C - 1
    b_last = jnp.sum(jnp.where(last, bc, 0.0), axis=0, keepdims=True)
    return z, bc, b_last


def _gla_in_specs(wgu, DK, DV, cidx):
    C = GLA_CHUNK
    return [pl.BlockSpec((C, DK), lambda c: (cidx(c), 0)),
            pl.BlockSpec((C, DK), lambda c: (cidx(c), 1)),
            pl.BlockSpec((C, DV), lambda c: (cidx(c), (2 * DK) // DV)),
            pl.BlockSpec((C, 128), lambda c: (cidx(c), (2 * DK + 2 * DV) // 128)),
            pl.BlockSpec(wgu.shape, lambda c: (0, 0)),
            pl.BlockSpec((1, DK), lambda c: (0, 0))]


def _dot_nt(a, b):
    return lax.dot_general(a.astype(BF16), b.astype(BF16), (((1,), (1,)), ((), ())), preferred_element_type=F32)


def _dot_tn(a, b):
    return lax.dot_general(a.astype(BF16), b.astype(BF16), (((0,), (0,)), ((), ())), preferred_element_type=F32)


def _dot(a, b):
    return jnp.dot(a.astype(BF16), b.astype(BF16), preferred_element_type=F32)


def gla_fwd(name, proj, wgu, bg):
    T = proj.shape[0]
    C, H = GLA_CHUNK, GLA_HEADS
    DK, DV, dk, dv = _gla_dims(proj, wgu)
    nc = T // C
    scale = dk ** -0.5

    def body(q_ref, k_ref, v_ref, a_ref, w_ref, bg_ref, o_ref, s_ref, st):
        @pl.when(pl.program_id(0) == 0)
        def _():
            st[...] = jnp.zeros(st.shape, F32)

        _, bc, b_last = _gla_decay(a_ref, w_ref, bg_ref)
        k = k_ref[...]
        qe = (q_ref[...] * scale * jnp.exp(bc)).astype(BF16)
        ke = (k * jnp.exp(-bc)).astype(BF16)
        kd = (k * jnp.exp(b_last - bc)).astype(BF16)
        el = jnp.exp(b_last)
        row = lax.broadcasted_iota(jnp.int32, (C, C), 0)
        col = lax.broadcasted_iota(jnp.int32, (C, C), 1)
        for h in range(H):
            kk, vv = slice(h * dk, (h + 1) * dk), slice(h * dv, (h + 1) * dv)
            sp = st[h]
            s_ref[h] = sp.astype(BF16)
            v = v_ref[:, vv]
            att = jnp.where(row >= col, _dot_nt(qe[:, kk], ke[:, kk]), 0.0)
            o_ref[:, vv] = _dot_nt(qe[:, kk], sp) + _dot(att, v)
            st[h] = sp * el[:, kk] + _dot_tn(v, kd[:, kk])

    return _pc(
        body, name=name,
        out_shape=[jax.ShapeDtypeStruct((T, DV), F32), jax.ShapeDtypeStruct((H, nc, dv, dk), BF16)],
        grid=(nc,), in_specs=_gla_in_specs(wgu, DK, DV, lambda c: c),
        out_specs=[pl.BlockSpec((C, DV), lambda c: (c, 0)),
                   pl.BlockSpec((H, None, dv, dk), lambda c: (0, c, 0, 0))],
        scratch_shapes=[pltpu.VMEM((H, dv, dk), F32)],
        compiler_params=_params("arbitrary"),
    )(proj, proj, proj, proj, wgu, bg)


def gla_bwd(name, proj, wgu, bg, states, do, d_r):
    T = proj.shape[0]
    C, H = GLA_CHUNK, GLA_HEADS
    DK, DV, dk, dv = _gla_dims(proj, wgu)
    nc = T // C
    scale = dk ** -0.5
    rev = lambda c: nc - 1 - c

    def body(q_ref, k_ref, v_ref, a_ref, w_ref, bg_ref, s_ref, do_ref, dr_ref,
             dp_ref, dz_ref, dbg_ref, dst, db_scr):
        @pl.when(pl.program_id(0) == 0)
        def _():
            dst[...] = jnp.zeros(dst.shape, F32)
            dbg_ref[...] = jnp.zeros(dbg_ref.shape, F32)

        z, bc, b_last = _gla_decay(a_ref, w_ref, bg_ref)
        k = k_ref[...]
        eb, enb, ed, el = jnp.exp(bc), jnp.exp(-bc), jnp.exp(b_last - bc), jnp.exp(b_last)
        qe, ke, kd = q_ref[...] * scale * eb, k * enb, k * ed
        row = lax.broadcasted_iota(jnp.int32, (C, C), 0)
        col = lax.broadcasted_iota(jnp.int32, (C, C), 1)
        keep = row >= col
        last = lax.broadcasted_iota(jnp.int32, (C, dk), 0) == C - 1
        for h in range(H):
            kk, vv = slice(h * dk, (h + 1) * dk), slice(h * dv, (h + 1) * dv)
            qe_h, ke_h, kd_h, el_h = qe[:, kk], ke[:, kk], kd[:, kk], el[:, kk]
            v, d_o = v_ref[:, vv], do_ref[:, vv]
            sp = s_ref[h].astype(F32)
            ds_ = dst[h]
            att = jnp.where(keep, _dot_nt(qe_h, ke_h), 0.0)
            datt = jnp.where(keep, _dot_nt(d_o, v), 0.0)
            dqe = _dot(d_o, sp) + _dot(datt, ke_h)
            dke = _dot_tn(datt, qe_h)
            dp_ref[:, 2 * DK + h * dv:2 * DK + (h + 1) * dv] = (_dot_tn(att, d_o) + _dot_nt(kd_h, ds_)).astype(BF16)
            dkd = _dot(v, ds_)
            dst[h] = ds_ * el_h + _dot_tn(d_o, qe_h)
            dp_ref[:, kk] = (dqe * scale * eb[:, kk]).astype(BF16)
            dp_ref[:, DK + h * dk:DK + (h + 1) * dk] = (dke * enb[:, kk] + dkd * ed[:, kk]).astype(BF16)
            d_el = jnp.sum(sp * ds_, axis=0, keepdims=True)
            db_last = jnp.sum(dkd * kd_h, axis=0, keepdims=True) + d_el * el_h
            db_scr[:, kk] = dqe * qe_h - dke * ke_h - dkd * kd_h + jnp.where(last, db_last, 0.0)
        dg = jnp.dot((row <= col).astype(F32), db_scr[...], precision=lax.Precision.HIGHEST,
                     preferred_element_type=F32)
        dz = dg * (1.0 / GLA_TAU) * _sigmoid(-z)
        dz_ref[...] = dz.astype(BF16)
        dbg_ref[...] += _fold8(dz)
        dp_ref[:, 2 * DK + DV:2 * DK + 2 * DV] = dr_ref[...]
        dp_ref[:, 2 * DK + 2 * DV:] = _dot_nt(dz, w_ref[...]).astype(BF16)

    return _pc(
        body, name=name,
        out_shape=[jax.ShapeDtypeStruct((T, proj.shape[1]), BF16), jax.ShapeDtypeStruct((T, DK), BF16),
                   jax.ShapeDtypeStruct((8, DK), F32)],
        grid=(nc,),
        in_specs=_gla_in_specs(wgu, DK, DV, rev)
        + [pl.BlockSpec((H, None, dv, dk), lambda c: (0, rev(c), 0, 0)),
           pl.BlockSpec((C, DV), lambda c: (rev(c), 0)),
           pl.BlockSpec((C, DV), lambda c: (rev(c), 0))],
        out_specs=[pl.BlockSpec((C, proj.shape[1]), lambda c: (rev(c), 0)),
                   pl.BlockSpec((C, DK), lambda c: (rev(c), 0)),
                   pl.BlockSpec((8, DK), lambda c: (0, 0))],
        scratch_shapes=[pltpu.VMEM((H, dv, dk), F32), pltpu.VMEM((C, DK), F32)],
        compiler_params=_params("arbitrary"),
    )(proj, proj, proj, proj, wgu, bg, states, do, d_r)


def gla_out_fwd(name, o, proj, norm_g, r_block0):
    T, DV = o.shape
    dv = DV // GLA_HEADS
    tm = _pick(T, (512, 256, 128, 64))

    def body(o_ref, r_ref, g_ref, out_ref):
        ov, rv = o_ref[...], r_ref[...]
        out_ref[...] = (ov * _rms(ov) * g_ref[...] * (rv * _sigmoid(rv))).astype(BF16)

    return _pc(
        body, name=name, out_shape=jax.ShapeDtypeStruct((T, DV), BF16), grid=(GLA_HEADS, T // tm),
        in_specs=[pl.BlockSpec((tm, dv), lambda h, i: (i, h)),
                  pl.BlockSpec((tm, dv), lambda h, i: (i, r_block0 + h)),
                  pl.BlockSpec((1, dv), lambda h, i: (0, h))],
        out_specs=pl.BlockSpec((tm, dv), lambda h, i: (i, h)),
        compiler_params=_params("parallel", "parallel"),
    )(o, proj, norm_g)


def gla_out_bwd(name, dog, o, proj, norm_g, r_block0):
    T, DV = o.shape
    dv = DV // GLA_HEADS
    tm = _pick(T, (512, 256, 128, 64))

    def body(d_ref, o_ref, r_ref, g_ref, do_ref, dr_ref, dg_ref):
        @pl.when(pl.program_id(1) == 0)
        def _():
            dg_ref[...] = jnp.zeros(dg_ref.shape, F32)

        d, ov, rv, g = d_ref[...], o_ref[...], r_ref[...], g_ref[...]
        ro = _rms(ov)
        oh = ov * ro
        sg = _sigmoid(rv)
        dn = d * (rv * sg)
        dr_ref[...] = (d * (oh * g) * (sg * (1.0 + rv * (1.0 - sg)))).astype(BF16)
        doh = dn * g
        do_ref[...] = ro * (doh - oh * jnp.mean(doh * oh, axis=-1, keepdims=True))
        dg_ref[...] += _fold8(dn * oh)

    return _pc(
        body, name=name,
        out_shape=[jax.ShapeDtypeStruct((T, DV), F32), jax.ShapeDtypeStruct((T, DV), BF16),
                   jax.ShapeDtypeStruct((8, DV), F32)],
        grid=(GLA_HEADS, T // tm),
        in_specs=[pl.BlockSpec((tm, dv), lambda h, i: (i, h)),
                  pl.BlockSpec((tm, dv), lambda h, i: (i, h)),
                  pl.BlockSpec((tm, dv), lambda h, i: (i, r_block0 + h)),
                  pl.BlockSpec((1, dv), lambda h, i: (0, h))],
        out_specs=[pl.BlockSpec((tm, dv), lambda h, i: (i, h)),
                   pl.BlockSpec((tm, dv), lambda h, i: (i, h)),
                   pl.BlockSpec((8, dv), lambda h, i: (0, h))],
        compiler_params=_params("parallel", "arbitrary"),
    )(dog, o, proj, norm_g)


def _ew_rows(r, c):
    return _pick(r, tuple(t for t in (512, 256, 128, 64, 32, 16, 8) if t * c <= 256 * 1024) or (8,))


def sum_slots_layers(name, lands):
    nl = len(lands)
    n, r, c = lands[0].shape
    tr = _ew_rows(r, c)
    nb = r // tr

    def body(*refs):
        o_ref = refs[nl]
        for l in range(nl):
            @pl.when(pl.program_id(0) == l)
            def _(p_ref=refs[l]):
                acc = p_ref[0].astype(F32)
                for s in range(1, n):
                    acc = acc + p_ref[s].astype(F32)
                o_ref[...] = acc.astype(o_ref.dtype)

    def in_map(l):
        return lambda q, i: (0, jnp.where(q == l, i, 0), 0)

    return _pc(body, name=name, out_shape=jax.ShapeDtypeStruct((nl * r, c), BF16), grid=(nl, nb),
               in_specs=[pl.BlockSpec((n, tr, c), in_map(l)) for l in range(nl)],
               out_specs=pl.BlockSpec((tr, c), lambda q, i: (q * nb + i, 0)),
               compiler_params=_params("arbitrary", "arbitrary"))(*lands)


def sum_slots(name, p):
    n, r, c = p.shape
    tr = _ew_rows(r, c)

    def body(p_ref, o_ref):
        acc = p_ref[0].astype(F32)
        for s in range(1, n):
            acc = acc + p_ref[s].astype(F32)
        o_ref[...] = acc

    return _pc(body, name=name, out_shape=jax.ShapeDtypeStruct((r, c), F32), grid=(r // tr,),
               in_specs=[pl.BlockSpec((n, tr, c), lambda i: (0, i, 0))],
               out_specs=pl.BlockSpec((tr, c), lambda i: (i, 0)),
               compiler_params=_params("parallel"))(p)


def adamw(name, w, gs, m, v, deps=()):
    r, c = w.shape
    tr = _ew_rows(r, c)
    n_g = len(gs)
    m_corr = 1.0 / (1.0 - ADAM_B1 ** ADAM_STEP)
    v_corr = 1.0 / (1.0 - ADAM_B2 ** ADAM_STEP)

    def body(*refs):
        w_ref, g_refs, m_ref, v_ref = refs[0], refs[1:1 + n_g], refs[1 + n_g], refs[2 + n_g]
        g_out, d_out, m_out, v_out = refs[-4:]
        g = g_refs[0][...].astype(F32)
        if n_g == 2:
            g = g + g_refs[1][...].astype(F32)
        mn = ADAM_B1 * m_ref[...] + (1.0 - ADAM_B1) * g
        vn = ADAM_B2 * v_ref[...] + (1.0 - ADAM_B2) * (g * g)
        g_out[...] = g
        m_out[...] = mn
        v_out[...] = vn
        d_out[...] = -ADAM_LR * ((mn * m_corr) / (jnp.sqrt(vn * v_corr) + ADAM_EPS) + ADAM_WD * w_ref[...])

    spec = pl.BlockSpec((tr, c), lambda i: (i, 0))
    return _pc(body, name=name, out_shape=[jax.ShapeDtypeStruct((r, c), F32)] * 4, grid=(r // tr,),
               in_specs=[spec] * (3 + n_g) + [HBM] * len(deps), out_specs=[spec] * 4,
               compiler_params=_params("parallel"))(w, *gs, m, v, *deps)


def _flat_pack(arrs):
    flat = jnp.concatenate([a.reshape(-1).astype(F32) for a in arrs])
    n = flat.shape[0]
    pad = (-n) % 1024
    return jnp.pad(flat, (0, pad)).reshape(-1, 128)


def _flat_unpack(packed, shapes, lead=()):
    flat = packed.reshape(lead + (-1,))
    out, pos = [], 0
    for s in shapes:
        n = 1
        for d in s:
            n *= d
        out.append(flat[..., pos:pos + n].reshape(lead + tuple(s)))
        pos += n
    return out


def _interleave_vec(b, tn):
    f = b.shape[-1] // 2
    return b.reshape(2, f // tn, tn).transpose(1, 0, 2).reshape(1, 2 * f)


def _deinterleave_vec(b, tn):
    f = b.shape[-1] // 2
    return b.reshape(f // tn, 2, tn).transpose(1, 0, 2).reshape(2 * f)


def kernel(x, c, w_ada, b_ada, pre_mix_g, post_mix_g, pre_ffn_g, post_ffn_g, conv_w_pw1, conv_b_pw1, conv_w_dw, conv_b_dw, conv_ln_g, conv_ln_b, conv_w_pw2, conv_b_pw2, gla_w_in, gla_w_gate_up, gla_b_gate, gla_norm_g, gla_w_out, ffn_w_in, ffn_w_out, loss_target, m_w_ada, m_b_ada, m_pre_mix_g, m_post_mix_g, m_pre_ffn_g, m_post_ffn_g, m_conv_w_pw1, m_conv_b_pw1, m_conv_w_dw, m_conv_b_dw, m_conv_ln_g, m_conv_ln_b, m_conv_w_pw2, m_conv_b_pw2, m_gla_w_in, m_gla_w_gate_up, m_gla_b_gate, m_gla_norm_g, m_gla_w_out, m_ffn_w_in, m_ffn_w_out, v_w_ada, v_b_ada, v_pre_mix_g, v_post_mix_g, v_pre_ffn_g, v_post_ffn_g, v_conv_w_pw1, v_conv_b_pw1, v_conv_w_dw, v_conv_b_dw, v_conv_ln_g, v_conv_ln_b, v_conv_w_pw2, v_conv_b_pw2, v_gla_w_in, v_gla_w_gate_up, v_gla_b_gate, v_gla_norm_g, v_gla_w_out, v_ffn_w_in, v_ffn_w_out):
    weights = dict(w_ada=w_ada, b_ada=b_ada, pre_mix_g=pre_mix_g, post_mix_g=post_mix_g, pre_ffn_g=pre_ffn_g, post_ffn_g=post_ffn_g, conv_w_pw1=conv_w_pw1, conv_b_pw1=conv_b_pw1, conv_w_dw=conv_w_dw, conv_b_dw=conv_b_dw, conv_ln_g=conv_ln_g, conv_ln_b=conv_ln_b, conv_w_pw2=conv_w_pw2, conv_b_pw2=conv_b_pw2, gla_w_in=gla_w_in, gla_w_gate_up=gla_w_gate_up, gla_b_gate=gla_b_gate, gla_norm_g=gla_norm_g, gla_w_out=gla_w_out, ffn_w_in=ffn_w_in, ffn_w_out=ffn_w_out)
    mom_m = dict(w_ada=m_w_ada, b_ada=m_b_ada, pre_mix_g=m_pre_mix_g, post_mix_g=m_post_mix_g, pre_ffn_g=m_pre_ffn_g, post_ffn_g=m_post_ffn_g, conv_w_pw1=m_conv_w_pw1, conv_b_pw1=m_conv_b_pw1, conv_w_dw=m_conv_w_dw, conv_b_dw=m_conv_b_dw, conv_ln_g=m_conv_ln_g, conv_ln_b=m_conv_ln_b, conv_w_pw2=m_conv_w_pw2, conv_b_pw2=m_conv_b_pw2, gla_w_in=m_gla_w_in, gla_w_gate_up=m_gla_w_gate_up, gla_b_gate=m_gla_b_gate, gla_norm_g=m_gla_norm_g, gla_w_out=m_gla_w_out, ffn_w_in=m_ffn_w_in, ffn_w_out=m_ffn_w_out)
    mom_v = dict(w_ada=v_w_ada, b_ada=v_b_ada, pre_mix_g=v_pre_mix_g, post_mix_g=v_post_mix_g, pre_ffn_g=v_pre_ffn_g, post_ffn_g=v_post_ffn_g, conv_w_pw1=v_conv_w_pw1, conv_b_pw1=v_conv_b_pw1, conv_w_dw=v_conv_w_dw, conv_b_dw=v_conv_b_dw, conv_ln_g=v_conv_ln_g, conv_ln_b=v_conv_ln_b, conv_w_pw2=v_conv_w_pw2, conv_b_pw2=v_conv_b_pw2, gla_w_in=v_gla_w_in, gla_w_gate_up=v_gla_w_gate_up, gla_b_gate=v_gla_b_gate, gla_norm_g=v_gla_norm_g, gla_w_out=v_gla_w_out, ffn_w_in=v_ffn_w_in, ffn_w_out=v_ffn_w_out)
    order = list(weights)

    ax, ay, ac = lax.axis_index("x"), lax.axis_index("y"), lax.axis_index("c")
    my_chip, my_dev = 2 * ax + ay, 4 * ax + 2 * ay + ac

    x = x[0]
    target = loss_target[0]
    T, D = x.shape
    depth = w_ada.shape[0]
    n_conv, n_gla = conv_w_pw1.shape[0], gla_w_in.shape[0]
    width = conv_w_dw.shape[1]
    F = ffn_w_out.shape[1] * N_CHIPS
    DK = gla_w_gate_up.shape[2] * N_CHIPS
    rank = gla_w_gate_up.shape[1]
    gla_cols = 2 * DK + 2 * D + rank
    P = 2 * DK + 2 * D + 128
    dvh = D // GLA_HEADS
    tn_pw1 = _pick(conv_w_pw1.shape[2], (1024, 512, 256, 128))
    tn_ffn = ffn_w_out.shape[1]

    small_sharded = ["conv_w_dw", "gla_w_gate_up", "gla_b_gate", "gla_norm_g"]
    packed = _flat_pack([weights[n] for n in small_sharded])
    got = allgather_devices("gather_small", packed)[0::2]
    parts = _flat_unpack(got, [weights[n].shape for n in small_sharded], lead=(N_CHIPS,))
    w_dw_full, wgu_full, bgate_full, normg_full = [
        jnp.concatenate([p[s] for s in range(N_CHIPS)], axis=-1) for p in parts]
    w_dw_full = w_dw_full
    wgu_pad = jnp.pad(wgu_full, ((0, 0), (0, 128 - rank), (0, 0)))

    c_act = c * _sigmoid(c)
    c_all = allgather_devices("gather_c", jnp.pad(c_act, ((0, 7), (0, 0))))[:, 0, :]
    c16 = jnp.pad(c_all, ((0, 8), (0, 0))).astype(BF16)
    ada_cols = w_ada.shape[2]
    w_ada2 = w_ada.reshape(depth * D, ada_cols)
    mod_cols = [mm_nn(f"ada_fwd_{i}", c16, Wt(w_ada2, "plain", i, D, ada_cols)) for i in range(depth)]
    mod_cols = jnp.stack(mod_cols).reshape(depth * 16, ada_cols)
    mod_all = allgather_devices("gather_mod", mod_cols)[0::2]
    mod_all = mod_all.reshape(N_CHIPS, depth, 16, ada_cols).transpose(1, 2, 0, 3).reshape(depth, 16, 6 * D)
    mod = lax.dynamic_index_in_dim(mod_all, my_dev, axis=1, keepdims=False) + b_ada
    mod = mod.reshape(depth, 6, 1, D)

    gin_cols = gla_w_in.shape[2]

    def sublayer_weights(g):
        i = g // 2
        if g % 2:
            return [("ffn_w_in", i), ("ffn_w_out", i)]
        return [("conv_w_pw1", i // 2), ("conv_w_pw2", i // 2)] if i % 2 == 0 else [("gla_w_in", i // 2), ("gla_w_out", i // 2)]

    issued = []

    def start_gathers(g, after):
        handles, tokens = {}, []
        for n, l in sublayer_weights(g):
            shard = weights[n][l].astype(BF16)
            handles[n], token = chips_start(f"gather_start_{n}_{l}", shard, shard, my_chip, True,
                                            tuple(after) + tuple(issued[-1:]))
            issued.append(token)
            tokens.append(token)
        return handles, tokens

    def wait_gathers(g, handles, after):
        out = []
        for n, l in sublayer_weights(g):
            got = _split_wait(f"gather_wait_{n}_{l}", handles[n], after)
            r, cc = got.shape[1:]
            if n == "gla_w_in":
                full = got.transpose(1, 0, 2).reshape(D, N_CHIPS * gin_cols)
                out.append(Wt(jnp.pad(full, ((0, 0), (0, P - gla_cols))), "plain", 0, D, P))
            elif n in ("conv_w_pw1", "ffn_w_in"):
                out.append(Wt(got, "col", 0, r, cc, tn_pw1 if n == "conv_w_pw1" else tn_ffn))
            else:
                out.append(Wt(got, "row", 0, r, cc))
        return out

    n_sub = 2 * depth
    in_flight = {g: start_gathers(g, (mod,)) for g in range(min(3, n_sub))}

    def enter_sublayer(g, xs):
        if g + 2 >= n_sub:
            return ()
        if g + 2 not in in_flight:
            in_flight[g + 2] = start_gathers(g + 2, (xs,))
        return in_flight[g + 2][1]

    w_pw1, w_pw2, w_gin, w_gout = [None] * n_conv, [None] * n_conv, [None] * n_gla, [None] * n_gla
    w_fin, w_fout = [None] * depth, [None] * depth
    row = lambda a: a.reshape(1, -1)
    saved = []
    xs = x
    for i in range(depth):
        j = i // 2
        sh1, sc1, gt1, sh2, sc2, gt2 = [mod[i, q] for q in range(6)]
        s = dict(x_in=xs)
        deps = tuple(enter_sublayer(2 * i, xs))
        if i == 0:
            deps += tuple(in_flight[0][1] + in_flight[1][1])
            h = prenorm_fwd(f"pre_mix_{i}", xs, row(pre_mix_g[i]) * (1.0 + sc1), sh1)
        s["h_mix"] = h
        if i % 2 == 0:
            w_pw1[j], w_pw2[j] = wait_gathers(2 * i, in_flight[2 * i][0], (h,) + deps)
        else:
            w_gin[j], w_gout[j] = wait_gathers(2 * i, in_flight[2 * i][0], (h,) + deps)
        nxt = (row(pre_ffn_g[i]) * (1.0 + sc2), sh2)
        if i % 2 == 0:
            u, glu = gated_in_fwd(f"pw1_{i}", h, w_pw1[j], "glu", bias=_interleave_vec(row(conv_b_pw1[j]), tn_pw1))
            v = dwconv_fwd(f"dwconv_{i}", glu, w_dw_full[j], row(conv_b_dw[j]))
            sl = ln_silu_fwd(f"ln_silu_{i}", v, row(conv_ln_g[j]), row(conv_ln_b[j]))
            y, x_mid, h = mm_nn_post(f"pw2_{i}", sl, w_pw2[j], xs, gt1 * row(post_mix_g[i]), bias=row(conv_b_pw2[j]),
                                     nxt=nxt)
            s.update(u=u, glu=glu, v=v, s=sl)
        else:
            proj = mm_nn(f"gla_in_{i}", h, w_gin[j])
            o, states = gla_fwd(f"gla_{i}", proj, wgu_pad[j], row(bgate_full[j]))
            og = gla_out_fwd(f"gla_out_{i}", o, proj, row(normg_full[j]), (2 * DK + D) // dvh)
            y, x_mid, h = mm_nn_post(f"gla_wout_{i}", og, w_gout[j], xs, gt1 * row(post_mix_g[i]), nxt=nxt)
            s.update(proj=proj, o=o, states=states, og=og)
        s["y_mix"] = y
        xs = x_mid
        s["x_mid"] = xs
        deps = tuple(enter_sublayer(2 * i + 1, xs))
        w_fin[i], w_fout[i] = wait_gathers(2 * i + 1, in_flight[2 * i + 1][0], (h,) + deps)
        u, a = gated_in_fwd(f"ffn_in_{i}", h, w_fin[i], "swiglu")
        s.update(h_ffn=h, u_ffn=u, a_ffn=a)
        if i + 1 < depth:
            y, xs, h = mm_nn_post(f"ffn_out_{i}", a, w_fout[i], xs, gt2 * row(post_ffn_g[i]),
                                  nxt=(row(pre_mix_g[i + 1]) * (1.0 + mod[i + 1, 1]), mod[i + 1, 0]))
        else:
            y, xs = mm_nn_post(f"ffn_out_{i}", a, w_fout[i], xs, gt2 * row(post_ffn_g[i]))
        s["y_ffn"] = y
        saved.append(s)

    dx, loss_acc = loss_bwd("loss", xs, target)
    loss = lax.psum(jnp.sum(loss_acc), ("x", "y", "c"))

    fold = lambda a: jnp.sum(a, axis=0)
    small_g = {n: [None] * weights[n].shape[0] for n in order if n not in
               ("w_ada", "conv_w_pw1", "conv_w_pw2", "gla_w_in", "gla_w_out", "ffn_w_in", "ffn_w_out")}
    grad_w = lambda wt: Wt(None, wt.kind, 0, wt.R, wt.C, wt.tn)
    scattering = {}

    def start_scatter(n, l, p):
        own = lax.dynamic_index_in_dim(p, my_chip, axis=0, keepdims=False)
        scattering[n, l], token = chips_start(f"scatter_start_{n}_{l}", p, own, my_chip, False)
        return token

    for i in reversed(range(depth)):
        j = i // 2
        s = saved[i]
        sh1, sc1, gt1, sh2, sc2, gt2 = [mod[i, q] for q in range(6)]
        if i == depth - 1:
            dy, t_post, _ = postnorm_bwd(f"post_ffn_bwd_{i}", dx, s["y_ffn"], row(post_ffn_g[i]), gt2)
        else:
            dy, t_post = carried
        d_gt2 = fold(t_post) * post_ffn_g[i]
        small_g["post_ffn_g"][i] = fold(t_post) * gt2[0]
        du = ffn_out_dgrad(f"ffn_out_dgrad_{i}", dy, w_fout[i], s["u_ffn"])
        t_out = start_scatter("ffn_w_out", i, mm_tn(f"ffn_out_wgrad_{i}", s["a_ffn"], dy, grad_w(w_fout[i])))
        t_in = start_scatter("ffn_w_in", i, mm_tn(f"ffn_in_wgrad_{i}", s["h_ffn"], du, grad_w(w_fin[i])))
        dh = mm_nt(f"ffn_in_dgrad_{i}", du, w_fin[i], out_dtype=BF16)
        dx, dy, d_sh2, t_pre, t_post, dy_sum = norm_chain_bwd(
            f"pre_ffn_bwd_{i}", dh, s["x_mid"], dx, row(pre_ffn_g[i]), sc2,
            s["y_mix"], row(post_mix_g[i]), gt1, deps=(t_out, t_in))
        d_sh2, d_sc2 = fold(d_sh2), fold(t_pre) * pre_ffn_g[i]
        small_g["pre_ffn_g"][i] = fold(t_pre) * (1.0 + sc2[0])
        d_gt1 = fold(t_post) * post_mix_g[i]
        small_g["post_mix_g"][i] = fold(t_post) * gt1[0]
        if i % 2 == 0:
            small_g["conv_b_pw2"][j] = fold(dy_sum)
            dsl = mm_nt(f"pw2_dgrad_{i}", dy, w_pw2[j])
            t_out = start_scatter("conv_w_pw2", j, mm_tn(f"pw2_wgrad_{i}", s["s"], dy, grad_w(w_pw2[j])))
            dv, d_lg, d_lb = ln_silu_bwd(f"ln_silu_bwd_{i}", dsl, s["v"], row(conv_ln_g[j]), row(conv_ln_b[j]))
            small_g["conv_ln_g"][j], small_g["conv_ln_b"][j] = fold(d_lg), fold(d_lb)
            dglu, d_wdw, d_bdw = dwconv_bwd(f"dwconv_bwd_{i}", dv, s["glu"], w_dw_full[j])
            small_g["conv_w_dw"][j] = d_wdw.reshape(width, 8, D).sum(axis=1)
            small_g["conv_b_dw"][j] = fold(d_bdw)
            du, du_sum = glu_bwd(f"glu_bwd_{i}", s["u"], dglu, tn_pw1)
            small_g["conv_b_pw1"][j] = _deinterleave_vec(fold(du_sum), tn_pw1)
            t_in = start_scatter("conv_w_pw1", j, mm_tn(f"pw1_wgrad_{i}", s["h_mix"], du, grad_w(w_pw1[j])))
            dh = mm_nt(f"pw1_dgrad_{i}", du, w_pw1[j], out_dtype=BF16)
        else:
            dog = mm_nt(f"gla_wout_dgrad_{i}", dy, w_gout[j])
            t_out = start_scatter("gla_w_out", j, mm_tn(f"gla_wout_wgrad_{i}", s["og"], dy, grad_w(w_gout[j])))
            d_o, d_r, d_ng = gla_out_bwd(f"gla_out_bwd_{i}", dog, s["o"], s["proj"], row(normg_full[j]),
                                         (2 * DK + D) // dvh)
            small_g["gla_norm_g"][j] = fold(d_ng)
            dproj, dz, d_bg = gla_bwd(f"gla_bwd_{i}", s["proj"], wgu_pad[j], row(bgate_full[j]),
                                      s["states"], d_o, d_r)
            small_g["gla_b_gate"][j] = fold(d_bg)
            a_low = s["proj"][:, 2 * DK + 2 * D:].astype(BF16)
            d_wgu = mm_tn(f"gla_gate_wgrad_{i}", a_low, dz, Wt(None, "plain", 0, 128, DK), out_dtype=F32)
            small_g["gla_w_gate_up"][j] = d_wgu[:rank]
            g_in = mm_tn(f"gla_in_wgrad_{i}", s["h_mix"], dproj, Wt(None, "plain", 0, D, P))
            t_in = start_scatter("gla_w_in", j, g_in[:, :gla_cols].reshape(D, N_CHIPS, gin_cols).transpose(1, 0, 2))
            dh = mm_nt(f"gla_in_dgrad_{i}", dproj, w_gin[j], out_dtype=BF16)
        if i > 0:
            dx, dy_b, d_sh1, t_pre, t_post_b, _ = norm_chain_bwd(
                f"pre_mix_bwd_{i}", dh, s["x_in"], dx, row(pre_mix_g[i]), sc1,
                saved[i - 1]["y_ffn"], row(post_ffn_g[i - 1]), mod[i - 1, 5], deps=(t_out, t_in))
            carried = (dy_b, t_post_b)
        else:
            dx, d_sh1, t_pre = prenorm_bwd(f"pre_mix_bwd_{i}", dh, s["x_in"], dx, row(pre_mix_g[i]), sc1,
                                           deps=(t_out, t_in))
        d_sh1, d_sc1 = fold(d_sh1), fold(t_pre) * pre_mix_g[i]
        small_g["pre_mix_g"][i] = fold(t_pre) * (1.0 + sc1[0])
        small_g["b_ada"][i] = jnp.concatenate([d_sh1, d_sc1, d_gt1, d_sh2, d_sc2, d_gt2])
    grad_x = dx[None]

    small_names = list(small_g)
    local_small = [jnp.stack(small_g[n]) for n in small_names]
    full_shapes = [a.shape for a in local_small]
    small_handle, small_token = devices_start("small_grads_start", _flat_pack(local_small), my_dev)

    def flat2(a):
        return a.reshape(-1, a.shape[-1])

    swapping, last = [], (small_token,)
    for n in ("conv_w_pw1", "conv_w_pw2", "gla_w_in", "gla_w_out", "ffn_w_in", "ffn_w_out"):
        arrived = [_split_wait(f"scatter_wait_{n}_{l}", scattering[n, l], (dx,)) for l in range(weights[n].shape[0])]
        mine = sum_slots_layers("sum_" + n, arrived)
        handle, token = swap_start("swap_start_" + n, mine, last)
        swapping.append((n, handle))
        last = (token,)

    small_all = _split_wait("small_grads_wait", small_handle, last)
    small_sum = sum_slots("sum_small_grads", small_all)
    small_tot = dict(zip(small_names, _flat_unpack(small_sum, full_shapes)))
    for n in small_sharded:
        cols = weights[n].shape[-1]
        small_tot[n] = lax.dynamic_slice_in_dim(small_tot[n], my_chip * cols, cols, axis=small_tot[n].ndim - 1)

    dmod_all = _flat_unpack(small_all, full_shapes, lead=(N_DEVICES,))[small_names.index("b_ada")]
    dmod_cols = lax.dynamic_slice_in_dim(
        dmod_all.reshape(N_DEVICES, depth, N_CHIPS, ada_cols), my_chip, 1, axis=2)[:, :, 0, :]
    dmod16 = jnp.pad(dmod_cols.reshape(N_DEVICES, depth * ada_cols), ((0, 8), (0, 0))).astype(BF16)
    g_ada = [mm_tn(f"ada_wgrad_{i}", c16, dmod16[:, i * ada_cols:(i + 1) * ada_cols],
                   Wt(None, "plain", 0, D, ada_cols), out_dtype=F32) for i in range(depth)]
    g_ada = jnp.stack(g_ada).reshape(depth * D, ada_cols)

    results = {}
    out = adamw("adamw_w_ada", flat2(w_ada), [g_ada], flat2(m_w_ada), flat2(v_w_ada))
    results["w_ada"] = [t.reshape(w_ada.shape) for t in out]
    w_small = _flat_pack([weights[n] for n in small_names])
    out_small = adamw("adamw_small", w_small, [_flat_pack([small_tot[n] for n in small_names])],
                      _flat_pack([mom_m[n] for n in small_names]), _flat_pack([mom_v[n] for n in small_names]))
    for n, handle in swapping:
        w = weights[n]
        mine, theirs = _split_wait("swap_wait_" + n, handle, (out[0],), with_src=True)
        res = adamw("adamw_" + n, flat2(w), [mine, theirs], flat2(mom_m[n]), flat2(mom_v[n]))
        results[n] = [t.reshape(w.shape) for t in res]
    out = out_small
    shapes = [weights[n].shape for n in small_names]
    unpacked = [_flat_unpack(t, shapes) for t in out]
    for q, n in enumerate(small_names):
        results[n] = [unpacked[r][q] for r in range(4)]

    outs = [loss, grad_x]
    for r in range(4):
        outs += [results[n][r] for n in order]
    return tuple(outs)
```

```python
import jax
import jax.numpy as jnp
from jax import lax
from jax.experimental import pallas as pl
from jax.experimental.pallas import tpu as pltpu

F32, BF16 = jnp.float32, jnp.bfloat16
MESH = pl.DeviceIdType.MESH
HBM = pl.BlockSpec(memory_space=pl.ANY)

EPS = 1e-6
VMEM_LIMIT_BYTES = 48 * 1024 * 1024
N_CHIPS = 4
N_DEVICES = 8
GLA_HEADS = 4
GLA_CHUNK = 64
GLA_TAU = 16.0
CONV_HALO = 32
ADAM_LR, ADAM_B1, ADAM_B2, ADAM_EPS, ADAM_WD, ADAM_STEP = 0.001, 0.9, 0.999, 1e-08, 0.01, 10


def _pc(body, **kw):
    return pl.pallas_call(body, **kw)


def _params(*sem):
    return pltpu.CompilerParams(dimension_semantics=sem, vmem_limit_bytes=VMEM_LIMIT_BYTES)


def _pick(n, cands):
    for c in cands:
        if c <= n and n % c == 0:
            return c
    return n


def _fold8(z):
    r, w = z.shape
    return z.reshape(r // 8, 8, w).sum(axis=0)


def _sigmoid(x):
    return 1.0 / (1.0 + jnp.exp(-x))


def _exchange(name, xs, out_shape, masks, src_of, dst_of, local=True):
    n_in = len(xs)

    def body(*refs):
        x_refs, o_ref = refs[:n_in], refs[n_in]
        send_sems, recv_sems, local_sems = refs[n_in + 1:]
        x, y, c = lax.axis_index("x"), lax.axis_index("y"), lax.axis_index("c")
        me = (x, y, c)
        peers = [(1 - x if a else x, 1 - y if b else y, 1 - c if d else c) for a, b, d in masks]
        started = []
        if local:
            for q, (s, t) in enumerate(zip(src_of(x_refs, me), dst_of(o_ref, me))):
                cp = pltpu.make_async_copy(s, t, local_sems.at[q])
                cp.start()
                started.append(cp)
        sends = []
        for k, peer in enumerate(peers):
            for q, (s, t) in enumerate(zip(src_of(x_refs, peer), dst_of(o_ref, me))):
                cp = pltpu.make_async_remote_copy(
                    src_ref=s, dst_ref=t, send_sem=send_sems.at[k, q], recv_sem=recv_sems.at[k, q],
                    device_id=peer, device_id_type=MESH)
                cp.start()
                sends.append(cp)
        for k, peer in enumerate(peers):
            for q, (s, t) in enumerate(zip(src_of(x_refs, me), dst_of(o_ref, peer))):
                pltpu.make_async_remote_copy(
                    src_ref=s, dst_ref=t, send_sem=send_sems.at[k, q], recv_sem=recv_sems.at[k, q],
                    device_id=peer, device_id_type=MESH).wait_recv()
        for cp in sends:
            cp.wait_send()
        for cp in started:
            cp.wait()

    n_q = n_in if len(xs) > 1 else 1
    return _pc(
        body, name=name, out_shape=out_shape,
        in_specs=[HBM] * n_in, out_specs=HBM,
        scratch_shapes=[pltpu.SemaphoreType.DMA((len(masks), n_q)),
                        pltpu.SemaphoreType.DMA((len(masks), n_q)),
                        pltpu.SemaphoreType.DMA((n_q,))],
    )(*xs)


_CHIP_MASKS = [(1, 0, 0), (0, 1, 0), (1, 1, 0)]
_ALL_MASKS = [(a, b, d) for a in (0, 1) for b in (0, 1) for d in (0, 1) if (a, b, d) != (0, 0, 0)]


def _chip(p):
    return 2 * p[0] + p[1]


def _dev(p):
    return 4 * p[0] + 2 * p[1] + p[2]


def allgather_devices(name, v):
    r, c = v.shape
    return _exchange(
        name, [v], jax.ShapeDtypeStruct((N_DEVICES, r, c), v.dtype), _ALL_MASKS,
        lambda xr, peer: [xr[0]], lambda o, src: [o.at[_dev(src)]])


_HBM = pl.BlockSpec(memory_space=pltpu.HBM)
_SEM = pl.BlockSpec(memory_space=pltpu.SEMAPHORE)
_N_PEER_CHIPS = len(_CHIP_MASKS)


def _peer_chips():
    x, y, c = lax.axis_index("x"), lax.axis_index("y"), lax.axis_index("c")
    return (x, y), [((1 - x if a else x, 1 - y if b else y), c) for a, b, _ in _CHIP_MASKS]


def _chip_copies(src_ref, land_ref, send_sems, recv_sems, whole_src, incoming):
    me, peers = _peer_chips()
    out = []
    for k, (chip, c) in enumerate(peers):
        out.append(pltpu.make_async_remote_copy(
            src_ref=src_ref if whole_src else src_ref.at[_chip(chip)],
            dst_ref=land_ref.at[_chip(chip) if incoming else _chip(me)],
            send_sem=send_sems.at[k], recv_sem=recv_sems.at[k], device_id=(*chip, c), device_id_type=MESH))
    return out


def _core_copies(src_ref, land_ref, send_sems, recv_sems, incoming):
    x, y, c = lax.axis_index("x"), lax.axis_index("y"), lax.axis_index("c")
    return [pltpu.make_async_remote_copy(
        src_ref=src_ref, dst_ref=land_ref, send_sem=send_sems.at[0], recv_sem=recv_sems.at[0],
        device_id=(x, y, 1 - c), device_id_type=MESH)]


def _split_start(name, src, land, n_copies, copies, after):
    n_after = len(after)

    def body(*refs):
        src_ref, land_ref = refs[0], refs[1]
        send_sems, recv_sems, _, _, token = refs[2 + n_after:]
        for send in copies(src_ref, land_ref, send_sems, recv_sems, False):
            send.start()
        token[...] = jnp.zeros(token.shape, token.dtype)

    send_sems, recv_sems, src_thru, land_thru, token = _pc(
        body, name=name,
        out_shape=(pltpu.SemaphoreType.DMA((n_copies,)), pltpu.SemaphoreType.DMA((n_copies,)),
                   pltpu.HBM(src.shape, src.dtype), pltpu.HBM(land.shape, land.dtype),
                   jax.ShapeDtypeStruct((8, 128), F32)),
        in_specs=[_HBM, _HBM] + [HBM] * n_after,
        out_specs=(_SEM, _SEM, _HBM, _HBM, pl.BlockSpec(memory_space=pltpu.VMEM)),
        input_output_aliases={0: 2, 1: 3},
        compiler_params=pltpu.CompilerParams(has_side_effects=pltpu.SideEffectType.DATAFLOW_SIDE_EFFECTING),
    )(pltpu.with_memory_space_constraint(src, pltpu.HBM), pltpu.with_memory_space_constraint(land, pltpu.HBM),
      *after)
    return (send_sems, recv_sems, src_thru, land_thru, copies), token


def _split_wait(name, handle, after, with_src=False):
    send_sems, recv_sems, src_thru, land_thru, copies = handle
    n_after = len(after)

    def body(src_ref, land_ref, send_sems, recv_sems, *rest):
        for send in copies(src_ref, land_ref, send_sems, recv_sems, False):
            send.wait_send()
        for recv in copies(src_ref, land_ref, send_sems, recv_sems, True):
            recv.wait_recv()

    src, land = _pc(
        body, name=name,
        out_shape=(pltpu.HBM(src_thru.shape, src_thru.dtype), pltpu.HBM(land_thru.shape, land_thru.dtype)),
        in_specs=[_HBM, _HBM, _SEM, _SEM] + [HBM] * n_after, out_specs=(_HBM, _HBM),
        input_output_aliases={0: 0, 1: 1},
        compiler_params=pltpu.CompilerParams(has_side_effects=pltpu.SideEffectType.DATAFLOW_SIDE_EFFECTING),
    )(src_thru, land_thru, send_sems, recv_sems, *after)
    return (src, land) if with_src else land


def chips_start(name, src, own, my_chip, whole_src, after=()):
    r, c = own.shape
    land = lax.dynamic_update_slice(lax.empty((N_CHIPS, r, c), src.dtype), own[None], (my_chip, 0, 0))

    def copies(src_ref, land_ref, send_sems, recv_sems, incoming):
        return _chip_copies(src_ref, land_ref, send_sems, recv_sems, whole_src, incoming)

    return _split_start(name, src, land, _N_PEER_CHIPS, copies, after)


def _device_copies(src_ref, land_ref, send_sems, recv_sems, incoming):
    x, y, c = lax.axis_index("x"), lax.axis_index("y"), lax.axis_index("c")
    out = []
    for k, (a, b, d) in enumerate(_ALL_MASKS):
        peer = (1 - x if a else x, 1 - y if b else y, 1 - c if d else c)
        out.append(pltpu.make_async_remote_copy(
            src_ref=src_ref, dst_ref=land_ref.at[_dev(peer) if incoming else _dev((x, y, c))],
            send_sem=send_sems.at[k], recv_sem=recv_sems.at[k], device_id=peer, device_id_type=MESH))
    return out


def devices_start(name, v, my_dev, after=()):
    land = lax.dynamic_update_slice(lax.empty((N_DEVICES,) + v.shape, v.dtype), v[None], (my_dev, 0, 0))
    return _split_start(name, v, land, len(_ALL_MASKS), _device_copies, after)


def swap_start(name, a, after=()):
    return _split_start(name, a, lax.empty(a.shape, a.dtype), 1, _core_copies, after)


class Wt:
    def __init__(self, arr, kind, layer, rows, cols, tn=None):
        self.arr, self.kind, self.l, self.R, self.C, self.tn = arr, kind, layer, rows, cols, tn
        self.K = 4 * rows if kind == "row" else rows
        self.N = 4 * cols if kind == "col" else cols

    def spec(self, tk, tn):
        l, R, C = self.l, self.R, self.C
        if self.kind == "plain":
            off = l * (R // tk)
            return (tk, tn), lambda kb, jb: (off + kb, jb)
        if self.kind == "col":
            assert tn == self.tn
            per, off = C // tn, l * (R // tk)
            return (None, tk, tn), lambda kb, jb: (2 * (jb % 2) + (jb // 2) // per, off + kb, (jb // 2) % per)
        per = R // tk
        return (None, tk, tn), lambda kb, jb: (kb // per, l * per + kb % per, jb)

    def tile_k(self, cands):
        return _pick(self.R, cands)

    def tile_n(self, cands):
        return self.tn if self.kind == "col" else _pick(self.C, cands)


_TM = (1024, 512, 256, 128, 64, 32, 16, 8)
_TK = (2048, 1408, 1024, 896, 512, 256, 128, 64, 32, 16)
_TN = (1024, 896, 768, 512, 384, 256, 128)
_TR = (1024, 1408, 512, 256, 128, 64, 32, 16)


def _accumulate(part, acc_ref, k, nk, finish):
    if nk == 1:
        finish(part)
        return

    @pl.when(k == 0)
    def _():
        acc_ref[...] = part

    @pl.when(k > 0)
    def _():
        acc_ref[...] += part

    @pl.when(k == nk - 1)
    def _():
        finish(acc_ref[...])


def mm_nn(name, a, w, *, bias=None, out_dtype=F32):
    M, K = a.shape
    assert K == w.K
    N = w.N
    tm, tk, tn = _pick(M, _TM), w.tile_k(_TK), w.tile_n(_TN)
    nk = K // tk
    wblock, wmap = w.spec(tk, tn)
    in_specs = [pl.BlockSpec((tm, tk), lambda i, j, k: (i, k)),
                pl.BlockSpec(wblock, lambda i, j, k: wmap(k, j))]
    args = [a, w.arr]
    if bias is not None:
        in_specs.append(pl.BlockSpec((1, tn), lambda i, j, k: (0, j)))
        args.append(bias)

    def body(*refs):
        a_ref, b_ref = refs[0], refs[1]
        bias_ref = refs[2] if bias is not None else None
        o_ref = refs[3] if bias is not None else refs[2]
        acc_ref = refs[-1] if nk > 1 else None
        part = jnp.dot(a_ref[...].astype(BF16), b_ref[...].astype(BF16), preferred_element_type=F32)

        def finish(acc):
            if bias_ref is not None:
                acc = acc + bias_ref[...]
            o_ref[...] = acc.astype(o_ref.dtype)

        _accumulate(part, acc_ref, pl.program_id(2), nk, finish)

    return _pc(
        body, name=name, out_shape=jax.ShapeDtypeStruct((M, N), out_dtype),
        grid=(M // tm, N // tn, nk), in_specs=in_specs,
        out_specs=pl.BlockSpec((tm, tn), lambda i, j, k: (i, j)),
        scratch_shapes=[pltpu.VMEM((tm, tn), F32)] if nk > 1 else [],
        compiler_params=_params("parallel", "parallel", "arbitrary"),
    )(*args)


def mm_nt(name, a, w, *, out_dtype=F32, deps=()):
    M, N = a.shape
    assert N == w.N
    K = w.K
    tm, tj, tn = _pick(M, _TM), w.tile_k(_TR), w.tile_n(_TK)
    wblock, wmap = w.spec(tj, tn)
    nb = 2 if w.kind == "col" else 1
    nn = N // (nb * tn)

    def body(a_ref, *refs):
        b_refs, o_ref, scr = refs[:nb], refs[nb + len(deps)], refs[nb + len(deps) + 1:]
        part = None
        for q, b_ref in enumerate(b_refs):
            p = lax.dot_general(a_ref[:, q * tn:(q + 1) * tn].astype(BF16), b_ref[...].astype(BF16),
                                (((1,), (1,)), ((), ())), preferred_element_type=F32)
            part = p if part is None else part + p

        def finish(acc):
            o_ref[...] = acc.astype(o_ref.dtype)

        _accumulate(part, scr[0] if nn > 1 else None, pl.program_id(2), nn, finish)

    def b_spec(q):
        return pl.BlockSpec(wblock, lambda i, j, n: wmap(j, nb * n + q))

    return _pc(
        body, name=name, out_shape=jax.ShapeDtypeStruct((M, K), out_dtype),
        grid=(M // tm, K // tj, nn),
        in_specs=[pl.BlockSpec((tm, nb * tn), lambda i, j, n: (i, n))] + [b_spec(q) for q in range(nb)]
        + [HBM] * len(deps),
        out_specs=pl.BlockSpec((tm, tj), lambda i, j, n: (i, j)),
        scratch_shapes=[pltpu.VMEM((tm, tj), F32)] if nn > 1 else [],
        compiler_params=_params("parallel", "parallel", "arbitrary"),
    )(a, *([w.arr] * nb), *deps)


def mm_tn(name, a, b, w, *, out_dtype=BF16):
    T, K = a.shape
    N = b.shape[1]
    assert (K, N) == (w.K, w.N) and w.l == 0
    tm = w.tile_k(_TR)
    tn = w.tile_n(_TN)
    tk = _pick(T, (2048, 1024, 512, 256, 128, 64, 32, 16))
    nk = T // tk
    oblock, omap = w.spec(tm, tn)
    shape = (w.R, w.C) if w.kind == "plain" else (N_CHIPS, w.R, w.C)

    def body(a_ref, b_ref, o_ref, *scr):
        part = lax.dot_general(a_ref[...].astype(BF16), b_ref[...].astype(BF16),
                               (((0,), (0,)), ((), ())), preferred_element_type=F32)

        def finish(acc):
            o_ref[...] = acc.astype(o_ref.dtype)

        _accumulate(part, scr[0] if nk > 1 else None, pl.program_id(2), nk, finish)

    return _pc(
        body, name=name, out_shape=jax.ShapeDtypeStruct(shape, out_dtype),
        grid=(K // tm, N // tn, nk),
        in_specs=[pl.BlockSpec((tk, tm), lambda i, j, k: (k, i)),
                  pl.BlockSpec((tk, tn), lambda i, j, k: (k, j))],
        out_specs=pl.BlockSpec(oblock, lambda i, j, k: omap(i, j)),
        scratch_shapes=[pltpu.VMEM((tm, tn), F32)] if nk > 1 else [],
        compiler_params=_params("parallel", "parallel", "arbitrary"),
    )(a, b)


_TM_FUSED = (512, 256, 128, 64, 32, 16, 8)


def mm_nn_post(name, a, w, x, wvec, *, bias=None, nxt=None):
    M, K = a.shape
    assert K == w.K
    N = w.N
    tm, tk = _pick(M, _TM_FUSED), w.tile_k(_TK)
    rows = _pick(tm, (64, 32, 16, 8))
    nk = K // tk
    wblock, wmap = w.spec(tk, N)
    in_specs = [pl.BlockSpec((tm, tk), lambda i, k: (i, k)),
                pl.BlockSpec(wblock, lambda i, k: wmap(k, 0)),
                pl.BlockSpec((tm, N), lambda i, k: (i, 0)),
                pl.BlockSpec((1, N), lambda i, k: (0, 0))]
    args = [a, w.arr, x, wvec]
    for extra in ([bias] if bias is not None else []) + list(nxt or ()):
        in_specs.append(pl.BlockSpec((1, N), lambda i, k: (0, 0)))
        args.append(extra)
    n_out = 3 if nxt else 2

    def body(*refs):
        a_ref, b_ref, x_ref, wv_ref = refs[:4]
        bias_ref = refs[4] if bias is not None else None
        gain_ref, shift_ref = refs[-n_out - 3:-n_out - 1] if nxt else (None, None)
        y_ref, xn_ref = refs[-n_out - 1:-n_out + 1]
        acc_ref = refs[-1]
        part = jnp.dot(a_ref[...].astype(BF16), b_ref[...].astype(BF16), preferred_element_type=F32)
        k = pl.program_id(1)

        @pl.when(k == 0)
        def _():
            acc_ref[...] = part

        @pl.when(k > 0)
        def _():
            acc_ref[...] += part

        @pl.when(k == nk - 1)
        def _():
            for r0 in range(0, tm, rows):
                rr = pl.ds(r0, rows)
                yv = acc_ref[rr, :]
                if bias_ref is not None:
                    yv = yv + bias_ref[...]
                y_ref[rr, :] = yv.astype(BF16)
                xn = x_ref[rr, :] + yv * _rms(yv) * wv_ref[...]
                xn_ref[rr, :] = xn
                if nxt:
                    refs[-2][rr, :] = (xn * _rms(xn) * gain_ref[...] + shift_ref[...]).astype(BF16)

    spec = pl.BlockSpec((tm, N), lambda i, k: (i, 0))
    out_shape = [jax.ShapeDtypeStruct((M, N), BF16), jax.ShapeDtypeStruct((M, N), F32)]
    if nxt:
        out_shape.append(jax.ShapeDtypeStruct((M, N), BF16))
    return _pc(
        body, name=name, out_shape=out_shape,
        grid=(M // tm, nk), in_specs=in_specs, out_specs=[spec] * n_out,
        scratch_shapes=[pltpu.VMEM((tm, N), F32)],
        compiler_params=_params("parallel", "arbitrary"),
    )(*args)


def gated_in_fwd(name, h, w, unit, *, bias=None):
    M, K = h.shape
    assert w.kind == "col" and K == w.K == w.R
    tn, F = w.tn, w.N // 2
    tm = _pick(M, _TM_FUSED)
    wblock, wmap = w.spec(K, tn)
    in_specs = [pl.BlockSpec((tm, K), lambda g, i: (i, 0)),
                pl.BlockSpec(wblock, lambda g, i: wmap(0, 2 * g)),
                pl.BlockSpec(wblock, lambda g, i: wmap(0, 2 * g + 1))]
    args = [h, w.arr, w.arr]
    if bias is not None:
        in_specs.append(pl.BlockSpec((1, 2 * tn), lambda g, i: (0, g)))
        args.append(bias)

    def body(h_ref, w1_ref, w2_ref, *refs):
        u_ref, act_ref = refs[-2:]
        hv = h_ref[...].astype(BF16)
        first = jnp.dot(hv, w1_ref[...].astype(BF16), preferred_element_type=F32)
        second = jnp.dot(hv, w2_ref[...].astype(BF16), preferred_element_type=F32)
        if bias is not None:
            first, second = first + refs[0][:, :tn], second + refs[0][:, tn:]
        u_ref[:, :tn] = first.astype(BF16)
        u_ref[:, tn:] = second.astype(BF16)
        if unit == "swiglu":
            act_ref[...] = (first * _sigmoid(first) * second).astype(act_ref.dtype)
        else:
            act_ref[...] = (first * _sigmoid(second)).astype(act_ref.dtype)

    return _pc(
        body, name=name,
        out_shape=[jax.ShapeDtypeStruct((M, 2 * F), BF16),
                   jax.ShapeDtypeStruct((M, F), BF16 if unit == "swiglu" else F32)],
        grid=(F // tn, M // tm), in_specs=in_specs,
        out_specs=[pl.BlockSpec((tm, 2 * tn), lambda g, i: (i, g)),
                   pl.BlockSpec((tm, tn), lambda g, i: (i, g))],
        compiler_params=_params("parallel", "parallel"),
    )(*args)


def ffn_out_dgrad(name, dy, w, u):
    M, N = dy.shape
    assert w.kind == "row" and N == w.N == w.C
    tj, F = w.R, w.K
    tm = _pick(M, _TM_FUSED)
    wblock, wmap = w.spec(tj, N)

    def body(dy_ref, b_ref, u_ref, du_ref):
        d = lax.dot_general(dy_ref[...].astype(BF16), b_ref[...].astype(BF16),
                            (((1,), (1,)), ((), ())), preferred_element_type=F32)
        gate, up = u_ref[:, :tj].astype(F32), u_ref[:, tj:].astype(F32)
        sg = _sigmoid(gate)
        du_ref[:, :tj] = (d * up * (sg * (1.0 + gate * (1.0 - sg)))).astype(BF16)
        du_ref[:, tj:] = (d * gate * sg).astype(BF16)

    return _pc(
        body, name=name, out_shape=jax.ShapeDtypeStruct((M, 2 * F), BF16),
        grid=(F // tj, M // tm),
        in_specs=[pl.BlockSpec((tm, N), lambda j, i: (i, 0)),
                  pl.BlockSpec(wblock, lambda j, i: wmap(j, 0)),
                  pl.BlockSpec((tm, 2 * tj), lambda j, i: (i, j))],
        out_specs=pl.BlockSpec((tm, 2 * tj), lambda j, i: (i, j)),
        compiler_params=_params("parallel", "parallel"),
    )(dy, w.arr, u)


def _rows(name, body, rows_in, vecs_in, rows_out, accs_out, tm=256, deps=()):
    T = rows_in[0].shape[0]
    tm = _pick(T, (tm, 128, 64, 32, 16, 8))
    n_r, n_v, n_o, n_d = len(rows_in), len(vecs_in), len(rows_out), len(deps)

    def kern(*refs):
        r_refs, v_refs = refs[:n_r], refs[n_r:n_r + n_v]
        refs = refs[n_r + n_v + n_d:]
        o_refs, a_refs = refs[:n_o], refs[n_o:]

        @pl.when(pl.program_id(0) == 0)
        def _():
            for a_ref in a_refs:
                a_ref[...] = jnp.zeros(a_ref.shape, F32)

        body(r_refs, v_refs, o_refs, a_refs)

    in_specs = [pl.BlockSpec((tm, a.shape[1]), lambda i: (i, 0)) for a in rows_in]
    in_specs += [pl.BlockSpec(v.shape, lambda i: (0, 0)) for v in vecs_in]
    in_specs += [HBM] * n_d
    out_shape = [jax.ShapeDtypeStruct((T, w), dt) for w, dt in rows_out]
    out_shape += [jax.ShapeDtypeStruct((8, w), F32) for w in accs_out]
    out_specs = [pl.BlockSpec((tm, w), lambda i: (i, 0)) for w, _ in rows_out]
    out_specs += [pl.BlockSpec((8, w), lambda i: (0, 0)) for w in accs_out]
    return _pc(kern, name=name, out_shape=out_shape, grid=(T // tm,), in_specs=in_specs,
               out_specs=out_specs, compiler_params=_params("arbitrary"))(*rows_in, *vecs_in, *deps)


def _rms(x):
    return lax.rsqrt(jnp.mean(x * x, axis=-1, keepdims=True) + EPS)


def prenorm_fwd(name, x, gain, shift, deps=()):
    def body(r, v, o, a):
        xv = r[0][...]
        o[0][...] = (xv * _rms(xv) * v[0][...] + v[1][...]).astype(BF16)
    return _rows(name, body, [x], [gain, shift], [(x.shape[1], BF16)], [], deps=deps)[0]


def postnorm_bwd(name, dxo, y, post_g, gate):
    def body(r, v, o, a):
        d, yv = r[0][...], r[1][...].astype(F32)
        pg, gt = v[0][...], v[1][...]
        ry = _rms(yv)
        yn = yv * ry
        t = d * yn
        dyn = d * (gt * pg)
        dy = ry * (dyn - yn * jnp.mean(dyn * yn, axis=-1, keepdims=True))
        o[0][...] = dy.astype(BF16)
        a[0][...] += _fold8(t)
        a[1][...] += _fold8(dy)
    D = y.shape[1]
    return _rows(name, body, [dxo, y], [post_g, gate], [(D, BF16)], [D, D])


def prenorm_bwd(name, dh, x, dxo, pre_g, scale, deps=()):
    def body(r, v, o, a):
        dhv, xv, d = r[0][...].astype(F32), r[1][...], r[2][...]
        pg, sc1 = v[0][...], 1.0 + v[1][...]
        rx = _rms(xv)
        xn = xv * rx
        t = dhv * xn
        dxn = dhv * (pg * sc1)
        o[0][...] = d + rx * (dxn - xn * jnp.mean(dxn * xn, axis=-1, keepdims=True))
        a[0][...] += _fold8(dhv)
        a[1][...] += _fold8(t)
    D = x.shape[1]
    return _rows(name, body, [dh, x, dxo], [pre_g, scale], [(D, F32)], [D, D], deps=deps)


def norm_chain_bwd(name, dh, x, dxo, pre_g, scale, y, post_g, gate, deps=()):
    def body(r, v, o, a):
        dhv, xv, d, yv = r[0][...].astype(F32), r[1][...], r[2][...], r[3][...].astype(F32)
        pg, sc1, pg2, gt = v[0][...], 1.0 + v[1][...], v[2][...], v[3][...]
        rx = _rms(xv)
        xn = xv * rx
        t = dhv * xn
        dxn = dhv * (pg * sc1)
        dx = d + rx * (dxn - xn * jnp.mean(dxn * xn, axis=-1, keepdims=True))
        o[0][...] = dx
        a[0][...] += _fold8(dhv)
        a[1][...] += _fold8(t)
        ry = _rms(yv)
        yn = yv * ry
        t2 = dx * yn
        dyn = dx * (gt * pg2)
        dy = ry * (dyn - yn * jnp.mean(dyn * yn, axis=-1, keepdims=True))
        o[1][...] = dy.astype(BF16)
        a[2][...] += _fold8(t2)
        a[3][...] += _fold8(dy)
    D = x.shape[1]
    return _rows(name, body, [dh, x, dxo, y], [pre_g, scale, post_g, gate], [(D, F32), (D, BF16)], [D] * 4,
                 deps=deps)


def loss_bwd(name, y, target):
    D = y.shape[1]

    def body(r, v, o, a):
        e = r[0][...] - r[1][...]
        o[0][...] = e * (1.0 / D)
        a[0][...] += _fold8(e * e * (0.5 / D))
    return _rows(name, body, [y, target], [], [(D, F32)], [D])


def ln_silu_fwd(name, v_, g, b):
    def body(r, v, o, a):
        x = r[0][...]
        xc = x - jnp.mean(x, axis=-1, keepdims=True)
        ln = xc * lax.rsqrt(jnp.mean(xc * xc, axis=-1, keepdims=True) + EPS) * v[0][...] + v[1][...]
        o[0][...] = (ln * _sigmoid(ln)).astype(BF16)
    return _rows(name, body, [v_], [g, b], [(v_.shape[1], BF16)], [])[0]


def ln_silu_bwd(name, ds, v_, g, b):
    def body(r, v, o, a):
        dsv, x = r[0][...], r[1][...]
        xc = x - jnp.mean(x, axis=-1, keepdims=True)
        rstd = lax.rsqrt(jnp.mean(xc * xc, axis=-1, keepdims=True) + EPS)
        xh = xc * rstd
        ln = xh * v[0][...] + v[1][...]
        sg = _sigmoid(ln)
        dln = dsv * (sg * (1.0 + ln * (1.0 - sg)))
        dxh = dln * v[0][...]
        o[0][...] = rstd * (dxh - jnp.mean(dxh, axis=-1, keepdims=True)
                            - xh * jnp.mean(dxh * xh, axis=-1, keepdims=True))
        a[0][...] += _fold8(dln * xh)
        a[1][...] += _fold8(dln)
    D = v_.shape[1]
    return _rows(name, body, [ds, v_], [g, b], [(D, F32)], [D, D])


def _pairs(name, body, u, others, out_w, out_dtype, tn, n_acc=0, tm=256):
    T, F2 = u.shape
    F = F2 // 2
    tm = _pick(T, (tm, 128, 64, 32, 16, 8))

    def kern(*refs):
        u_ref, o_refs = refs[0], refs[1:1 + len(others)]
        out_ref, acc_refs = refs[1 + len(others)], refs[2 + len(others):]

        @pl.when(pl.program_id(1) == 0)
        def _():
            for a_ref in acc_refs:
                a_ref[...] = jnp.zeros(a_ref.shape, F32)

        body(u_ref, o_refs, out_ref, acc_refs)

    out_shape = [jax.ShapeDtypeStruct((T, out_w * F), out_dtype)]
    out_shape += [jax.ShapeDtypeStruct((8, F2), F32)] * n_acc
    out_specs = [pl.BlockSpec((tm, out_w * tn), lambda g, i: (i, g))]
    out_specs += [pl.BlockSpec((8, 2 * tn), lambda g, i: (0, g))] * n_acc
    return _pc(kern, name=name, out_shape=out_shape, grid=(F // tn, T // tm),
               in_specs=[pl.BlockSpec((tm, 2 * tn), lambda g, i: (i, g))]
               + [pl.BlockSpec((tm, tn), lambda g, i: (i, g))] * len(others),
               out_specs=out_specs, compiler_params=_params("parallel", "arbitrary"))(u, *others)


def glu_bwd(name, u, dglu, tn):
    def body(u_ref, o, out, acc):
        a, g = u_ref[:, :tn].astype(F32), u_ref[:, tn:].astype(F32)
        d = o[0][...]
        sg = _sigmoid(g)
        da, dg = d * sg, d * a * (sg * (1.0 - sg))
        out[:, :tn] = da.astype(BF16)
        out[:, tn:] = dg.astype(BF16)
        acc[0][:, :tn] += _fold8(da)
        acc[0][:, tn:] += _fold8(dg)
    return _pairs(name, body, u, [dglu], 2, BF16, tn, n_acc=1)


_CONV_ROWS, _CONV_LANES = 64, 128
_SUBLANES = 8


def _fill_shifts(ext, sh, n):
    for r in range(1, _SUBLANES):
        sh[r - 1, pl.ds(0, n), :] = ext[pl.ds(r, n), :]


def _tap(ext, sh, o, r0, rows, ln):
    q, r = divmod(o, _SUBLANES)
    if r == 0:
        return ext[pl.ds(r0 + o, rows), ln]
    return sh[r - 1, pl.ds(r0 + _SUBLANES * q, rows), ln]


def dwconv_fwd(name, x, w, b):
    T, D = x.shape
    width = w.shape[0]
    tt, cb = _pick(T, (256, 128, 64)), _pick(D, (256, 128))
    off = CONV_HALO - (width - 1)
    rows = min(_CONV_ROWS, tt)
    n_sh = tt + CONV_HALO - _SUBLANES

    def body(cur_ref, prev_ref, w_ref, b_ref, o_ref, ext, sh):
        t = pl.program_id(1)
        tail = prev_ref[pl.ds(tt - CONV_HALO, CONV_HALO), :]
        ext[pl.ds(0, CONV_HALO), :] = jnp.where(t > 0, tail, 0.0)
        ext[pl.ds(CONV_HALO, tt), :] = cur_ref[...]
        _fill_shifts(ext, sh, n_sh)
        for l0 in range(0, cb, _CONV_LANES):
            ln = pl.ds(l0, _CONV_LANES)
            for r0 in range(0, tt, rows):
                acc = jnp.broadcast_to(b_ref[:, ln], (rows, _CONV_LANES))
                for k in range(width):
                    acc = acc + _tap(ext, sh, off + k, r0, rows, ln) * w_ref[pl.ds(k, 1), ln]
                o_ref[pl.ds(r0, rows), ln] = acc

    return _pc(
        body, name=name, out_shape=jax.ShapeDtypeStruct((T, D), F32), grid=(D // cb, T // tt),
        in_specs=[pl.BlockSpec((tt, cb), lambda c, t: (t, c)),
                  pl.BlockSpec((tt, cb), lambda c, t: (jnp.maximum(t - 1, 0), c)),
                  pl.BlockSpec((width, cb), lambda c, t: (0, c)),
                  pl.BlockSpec((1, cb), lambda c, t: (0, c))],
        out_specs=pl.BlockSpec((tt, cb), lambda c, t: (t, c)),
        scratch_shapes=[pltpu.VMEM((tt + CONV_HALO, cb), F32), pltpu.VMEM((_SUBLANES - 1, tt + CONV_HALO, cb), F32)],
        compiler_params=_params("parallel", "arbitrary"),
    )(x, x, w, b)


def dwconv_bwd(name, dv, x, w):
    T, D = x.shape
    width = w.shape[0]
    tt, cb = _pick(T, (256, 128, 64)), _pick(D, (256, 128))
    off = CONV_HALO - (width - 1)
    rows = min(_CONV_ROWS, tt)
    nt = T // tt
    n_sh = tt + CONV_HALO - _SUBLANES

    def body(dv_ref, dvn_ref, x_ref, xp_ref, w_ref, dx_ref, dw_ref, db_ref, ext_d, ext_x, sh_d, sh_x):
        t = pl.program_id(1)

        @pl.when(t == 0)
        def _():
            dw_ref[...] = jnp.zeros(dw_ref.shape, F32)
            db_ref[...] = jnp.zeros(db_ref.shape, F32)

        ext_d[pl.ds(0, tt), :] = dv_ref[...]
        ext_d[pl.ds(tt, CONV_HALO), :] = jnp.where(t < nt - 1, dvn_ref[pl.ds(0, CONV_HALO), :], 0.0)
        ext_x[pl.ds(0, CONV_HALO), :] = jnp.where(t > 0, xp_ref[pl.ds(tt - CONV_HALO, CONV_HALO), :], 0.0)
        ext_x[pl.ds(CONV_HALO, tt), :] = x_ref[...]
        db_ref[...] += _fold8(dv_ref[...])
        _fill_shifts(ext_d, sh_d, n_sh)
        _fill_shifts(ext_x, sh_x, n_sh)
        for l0 in range(0, cb, _CONV_LANES):
            ln = pl.ds(l0, _CONV_LANES)
            for r0 in range(0, tt, rows):
                acc = jnp.zeros((rows, _CONV_LANES), F32)
                for k in range(width):
                    acc = acc + _tap(ext_d, sh_d, (width - 1) - k, r0, rows, ln) * w_ref[pl.ds(k, 1), ln]
                dx_ref[pl.ds(r0, rows), ln] = acc
            sums = [jnp.zeros((8, _CONV_LANES), F32)] * width
            for r0 in range(0, tt, rows):
                dvb = ext_d[pl.ds(r0, rows), ln]
                sums = [sums[k] + _fold8(dvb * _tap(ext_x, sh_x, off + k, r0, rows, ln)) for k in range(width)]
            for k in range(width):
                dw_ref[pl.ds(8 * k, 8), ln] += sums[k]

    return _pc(
        body, name=name,
        out_shape=[jax.ShapeDtypeStruct((T, D), F32), jax.ShapeDtypeStruct((8 * width, D), F32),
                   jax.ShapeDtypeStruct((8, D), F32)],
        grid=(D // cb, nt),
        in_specs=[pl.BlockSpec((tt, cb), lambda c, t: (t, c)),
                  pl.BlockSpec((tt, cb), lambda c, t: (jnp.minimum(t + 1, nt - 1), c)),
                  pl.BlockSpec((tt, cb), lambda c, t: (t, c)),
                  pl.BlockSpec((tt, cb), lambda c, t: (jnp.maximum(t - 1, 0), c)),
                  pl.BlockSpec((width, cb), lambda c, t: (0, c))],
        out_specs=[pl.BlockSpec((tt, cb), lambda c, t: (t, c)),
                   pl.BlockSpec((8 * width, cb), lambda c, t: (0, c)),
                   pl.BlockSpec((8, cb), lambda c, t: (0, c))],
        scratch_shapes=[pltpu.VMEM((tt + CONV_HALO, cb), F32), pltpu.VMEM((tt + CONV_HALO, cb), F32),
                        pltpu.VMEM((_SUBLANES - 1, tt + CONV_HALO, cb), F32),
                        pltpu.VMEM((_SUBLANES - 1, tt + CONV_HALO, cb), F32)],
        compiler_params=_params("parallel", "arbitrary"),
    )(dv, dv, x, x, w)


def _gla_dims(proj, wgu):
    DK = wgu.shape[1]
    DV = (proj.shape[1] - 128 - 2 * DK) // 2
    return DK, DV, DK // GLA_HEADS, DV // GLA_HEADS


def _gla_decay(a_ref, w_ref, bg_ref):
    C = GLA_CHUNK
    z = jnp.dot(a_ref[...].astype(BF16), w_ref[...].astype(BF16), preferred_element_type=F32) + bg_ref[...]
    g = (jnp.minimum(z, 0.0) - jnp.log(1.0 + jnp.exp(-jnp.abs(z)))) * (1.0 / GLA_TAU)
    row = lax.broadcasted_iota(jnp.int32, (C, C), 0)
    col = lax.broadcasted_iota(jnp.int32, (C, C), 1)
    bc = jnp.dot((row >= col).astype(F32), g, precision=lax.Precision.HIGHEST, preferred_element_type=F32)
    last = lax.broadcasted_iota(jnp.int32, bc.shape, 0) == C - 1
    b_last = jnp.sum(jnp.where(last, bc, 0.0), axis=0, keepdims=True)
    return z, bc, b_last


def _gla_in_specs(wgu, DK, DV, cidx):
    C = GLA_CHUNK
    return [pl.BlockSpec((C, DK), lambda c: (cidx(c), 0)),
            pl.BlockSpec((C, DK), lambda c: (cidx(c), 1)),
            pl.BlockSpec((C, DV), lambda c: (cidx(c), (2 * DK) // DV)),
            pl.BlockSpec((C, 128), lambda c: (cidx(c), (2 * DK + 2 * DV) // 128)),
            pl.BlockSpec(wgu.shape, lambda c: (0, 0)),
            pl.BlockSpec((1, DK), lambda c: (0, 0))]


def _dot_nt(a, b):
    return lax.dot_general(a.astype(BF16), b.astype(BF16), (((1,), (1,)), ((), ())), preferred_element_type=F32)


def _dot_tn(a, b):
    return lax.dot_general(a.astype(BF16), b.astype(BF16), (((0,), (0,)), ((), ())), preferred_element_type=F32)


def _dot(a, b):
    return jnp.dot(a.astype(BF16), b.astype(BF16), preferred_element_type=F32)


def gla_fwd(name, proj, wgu, bg):
    T = proj.shape[0]
    C, H = GLA_CHUNK, GLA_HEADS
    DK, DV, dk, dv = _gla_dims(proj, wgu)
    nc = T // C
    scale = dk ** -0.5

    def body(q_ref, k_ref, v_ref, a_ref, w_ref, bg_ref, o_ref, s_ref, st):
        @pl.when(pl.program_id(0) == 0)
        def _():
            st[...] = jnp.zeros(st.shape, F32)

        _, bc, b_last = _gla_decay(a_ref, w_ref, bg_ref)
        k = k_ref[...]
        qe = (q_ref[...] * scale * jnp.exp(bc)).astype(BF16)
        ke = (k * jnp.exp(-bc)).astype(BF16)
        kd = (k * jnp.exp(b_last - bc)).astype(BF16)
        el = jnp.exp(b_last)
        row = lax.broadcasted_iota(jnp.int32, (C, C), 0)
        col = lax.broadcasted_iota(jnp.int32, (C, C), 1)
        for h in range(H):
            kk, vv = slice(h * dk, (h + 1) * dk), slice(h * dv, (h + 1) * dv)
            sp = st[h]
            s_ref[h] = sp.astype(BF16)
            v = v_ref[:, vv]
            att = jnp.where(row >= col, _dot_nt(qe[:, kk], ke[:, kk]), 0.0)
            o_ref[:, vv] = _dot_nt(qe[:, kk], sp) + _dot(att, v)
            st[h] = sp * el[:, kk] + _dot_tn(v, kd[:, kk])

    return _pc(
        body, name=name,
        out_shape=[jax.ShapeDtypeStruct((T, DV), F32), jax.ShapeDtypeStruct((H, nc, dv, dk), BF16)],
        grid=(nc,), in_specs=_gla_in_specs(wgu, DK, DV, lambda c: c),
        out_specs=[pl.BlockSpec((C, DV), lambda c: (c, 0)),
                   pl.BlockSpec((H, None, dv, dk), lambda c: (0, c, 0, 0))],
        scratch_shapes=[pltpu.VMEM((H, dv, dk), F32)],
        compiler_params=_params("arbitrary"),
    )(proj, proj, proj, proj, wgu, bg)


def gla_bwd(name, proj, wgu, bg, states, do, d_r):
    T = proj.shape[0]
    C, H = GLA_CHUNK, GLA_HEADS
    DK, DV, dk, dv = _gla_dims(proj, wgu)
    nc = T // C
    scale = dk ** -0.5
    rev = lambda c: nc - 1 - c

    def body(q_ref, k_ref, v_ref, a_ref, w_ref, bg_ref, s_ref, do_ref, dr_ref,
             dp_ref, dz_ref, dbg_ref, dst, db_scr):
        @pl.when(pl.program_id(0) == 0)
        def _():
            dst[...] = jnp.zeros(dst.shape, F32)
            dbg_ref[...] = jnp.zeros(dbg_ref.shape, F32)

        z, bc, b_last = _gla_decay(a_ref, w_ref, bg_ref)
        k = k_ref[...]
        eb, enb, ed, el = jnp.exp(bc), jnp.exp(-bc), jnp.exp(b_last - bc), jnp.exp(b_last)
        qe, ke, kd = q_ref[...] * scale * eb, k * enb, k * ed
        row = lax.broadcasted_iota(jnp.int32, (C, C), 0)
        col = lax.broadcasted_iota(jnp.int32, (C, C), 1)
        keep = row >= col
        last = lax.broadcasted_iota(jnp.int32, (C, dk), 0) == C - 1
        for h in range(H):
            kk, vv = slice(h * dk, (h + 1) * dk), slice(h * dv, (h + 1) * dv)
            qe_h, ke_h, kd_h, el_h = qe[:, kk], ke[:, kk], kd[:, kk], el[:, kk]
            v, d_o = v_ref[:, vv], do_ref[:, vv]
            sp = s_ref[h].astype(F32)
            ds_ = dst[h]
            att = jnp.where(keep, _dot_nt(qe_h, ke_h), 0.0)
            datt = jnp.where(keep, _dot_nt(d_o, v), 0.0)
            dqe = _dot(d_o, sp) + _dot(datt, ke_h)
            dke = _dot_tn(datt, qe_h)
            dp_ref[:, 2 * DK + h * dv:2 * DK + (h + 1) * dv] = (_dot_tn(att, d_o) + _dot_nt(kd_h, ds_)).astype(BF16)
            dkd = _dot(v, ds_)
            dst[h] = ds_ * el_h + _dot_tn(d_o, qe_h)
            dp_ref[:, kk] = (dqe * scale * eb[:, kk]).astype(BF16)
            dp_ref[:, DK + h * dk:DK + (h + 1) * dk] = (dke * enb[:, kk] + dkd * ed[:, kk]).astype(BF16)
            d_el = jnp.sum(sp * ds_, axis=0, keepdims=True)
            db_last = jnp.sum(dkd * kd_h, axis=0, keepdims=True) + d_el * el_h
            db_scr[:, kk] = dqe * qe_h - dke * ke_h - dkd * kd_h + jnp.where(last, db_last, 0.0)
        dg = jnp.dot((row <= col).astype(F32), db_scr[...], precision=lax.Precision.HIGHEST,
                     preferred_element_type=F32)
        dz = dg * (1.0 / GLA_TAU) * _sigmoid(-z)
        dz_ref[...] = dz.astype(BF16)
        dbg_ref[...] += _fold8(dz)
        dp_ref[:, 2 * DK + DV:2 * DK + 2 * DV] = dr_ref[...]
        dp_ref[:, 2 * DK + 2 * DV:] = _dot_nt(dz, w_ref[...]).astype(BF16)

    return _pc(
        body, name=name,
        out_shape=[jax.ShapeDtypeStruct((T, proj.shape[1]), BF16), jax.ShapeDtypeStruct((T, DK), BF16),
                   jax.ShapeDtypeStruct((8, DK), F32)],
        grid=(nc,),
        in_specs=_gla_in_specs(wgu, DK, DV, rev)
        + [pl.BlockSpec((H, None, dv, dk), lambda c: (0, rev(c), 0, 0)),
           pl.BlockSpec((C, DV), lambda c: (rev(c), 0)),
           pl.BlockSpec((C, DV), lambda c: (rev(c), 0))],
        out_specs=[pl.BlockSpec((C, proj.shape[1]), lambda c: (rev(c), 0)),
                   pl.BlockSpec((C, DK), lambda c: (rev(c), 0)),
                   pl.BlockSpec((8, DK), lambda c: (0, 0))],
        scratch_shapes=[pltpu.VMEM((H, dv, dk), F32), pltpu.VMEM((C, DK), F32)],
        compiler_params=_params("arbitrary"),
    )(proj, proj, proj, proj, wgu, bg, states, do, d_r)


def gla_out_fwd(name, o, proj, norm_g, r_block0):
    T, DV = o.shape
    dv = DV // GLA_HEADS
    tm = _pick(T, (512, 256, 128, 64))

    def body(o_ref, r_ref, g_ref, out_ref):
        ov, rv = o_ref[...], r_ref[...]
        out_ref[...] = (ov * _rms(ov) * g_ref[...] * (rv * _sigmoid(rv))).astype(BF16)

    return _pc(
        body, name=name, out_shape=jax.ShapeDtypeStruct((T, DV), BF16), grid=(GLA_HEADS, T // tm),
        in_specs=[pl.BlockSpec((tm, dv), lambda h, i: (i, h)),
                  pl.BlockSpec((tm, dv), lambda h, i: (i, r_block0 + h)),
                  pl.BlockSpec((1, dv), lambda h, i: (0, h))],
        out_specs=pl.BlockSpec((tm, dv), lambda h, i: (i, h)),
        compiler_params=_params("parallel", "parallel"),
    )(o, proj, norm_g)


def gla_out_bwd(name, dog, o, proj, norm_g, r_block0):
    T, DV = o.shape
    dv = DV // GLA_HEADS
    tm = _pick(T, (512, 256, 128, 64))

    def body(d_ref, o_ref, r_ref, g_ref, do_ref, dr_ref, dg_ref):
        @pl.when(pl.program_id(1) == 0)
        def _():
            dg_ref[...] = jnp.zeros(dg_ref.shape, F32)

        d, ov, rv, g = d_ref[...], o_ref[...], r_ref[...], g_ref[...]
        ro = _rms(ov)
        oh = ov * ro
        sg = _sigmoid(rv)
        dn = d * (rv * sg)
        dr_ref[...] = (d * (oh * g) * (sg * (1.0 + rv * (1.0 - sg)))).astype(BF16)
        doh = dn * g
        do_ref[...] = ro * (doh - oh * jnp.mean(doh * oh, axis=-1, keepdims=True))
        dg_ref[...] += _fold8(dn * oh)

    return _pc(
        body, name=name,
        out_shape=[jax.ShapeDtypeStruct((T, DV), F32), jax.ShapeDtypeStruct((T, DV), BF16),
                   jax.ShapeDtypeStruct((8, DV), F32)],
        grid=(GLA_HEADS, T // tm),
        in_specs=[pl.BlockSpec((tm, dv), lambda h, i: (i, h)),
                  pl.BlockSpec((tm, dv), lambda h, i: (i, h)),
                  pl.BlockSpec((tm, dv), lambda h, i: (i, r_block0 + h)),
                  pl.BlockSpec((1, dv), lambda h, i: (0, h))],
        out_specs=[pl.BlockSpec((tm, dv), lambda h, i: (i, h)),
                   pl.BlockSpec((tm, dv), lambda h, i: (i, h)),
                   pl.BlockSpec((8, dv), lambda h, i: (0, h))],
        compiler_params=_params("parallel", "arbitrary"),
    )(dog, o, proj, norm_g)


def _ew_rows(r, c):
    return _pick(r, tuple(t for t in (512, 256, 128, 64, 32, 16, 8) if t * c <= 256 * 1024) or (8,))


def sum_slots_layers(name, lands):
    nl = len(lands)
    n, r, c = lands[0].shape
    tr = _ew_rows(r, c)
    nb = r // tr

    def body(*refs):
        o_ref = refs[nl]
        for l in range(nl):
            @pl.when(pl.program_id(0) == l)
            def _(p_ref=refs[l]):
                acc = p_ref[0].astype(F32)
                for s in range(1, n):
                    acc = acc + p_ref[s].astype(F32)
                o_ref[...] = acc.astype(o_ref.dtype)

    def in_map(l):
        return lambda q, i: (0, jnp.where(q == l, i, 0), 0)

    return _pc(body, name=name, out_shape=jax.ShapeDtypeStruct((nl * r, c), BF16), grid=(nl, nb),
               in_specs=[pl.BlockSpec((n, tr, c), in_map(l)) for l in range(nl)],
               out_specs=pl.BlockSpec((tr, c), lambda q, i: (q * nb + i, 0)),
               compiler_params=_params("arbitrary", "arbitrary"))(*lands)


def sum_slots(name, p):
    n, r, c = p.shape
    tr = _ew_rows(r, c)

    def body(p_ref, o_ref):
        acc = p_ref[0].astype(F32)
        for s in range(1, n):
            acc = acc + p_ref[s].astype(F32)
        o_ref[...] = acc

    return _pc(body, name=name, out_shape=jax.ShapeDtypeStruct((r, c), F32), grid=(r // tr,),
               in_specs=[pl.BlockSpec((n, tr, c), lambda i: (0, i, 0))],
               out_specs=pl.BlockSpec((tr, c), lambda i: (i, 0)),
               compiler_params=_params("parallel"))(p)


def adamw(name, w, gs, m, v, deps=()):
    r, c = w.shape
    tr = _ew_rows(r, c)
    n_g = len(gs)
    m_corr = 1.0 / (1.0 - ADAM_B1 ** ADAM_STEP)
    v_corr = 1.0 / (1.0 - ADAM_B2 ** ADAM_STEP)

    def body(*refs):
        w_ref, g_refs, m_ref, v_ref = refs[0], refs[1:1 + n_g], refs[1 + n_g], refs[2 + n_g]
        g_out, d_out, m_out, v_out = refs[-4:]
        g = g_refs[0][...].astype(F32)
        if n_g == 2:
            g = g + g_refs[1][...].astype(F32)
        mn = ADAM_B1 * m_ref[...] + (1.0 - ADAM_B1) * g
        vn = ADAM_B2 * v_ref[...] + (1.0 - ADAM_B2) * (g * g)
        g_out[...] = g
        m_out[...] = mn
        v_out[...] = vn
        d_out[...] = -ADAM_LR * ((mn * m_corr) / (jnp.sqrt(vn * v_corr) + ADAM_EPS) + ADAM_WD * w_ref[...])

    spec = pl.BlockSpec((tr, c), lambda i: (i, 0))
    return _pc(body, name=name, out_shape=[jax.ShapeDtypeStruct((r, c), F32)] * 4, grid=(r // tr,),
               in_specs=[spec] * (3 + n_g) + [HBM] * len(deps), out_specs=[spec] * 4,
               compiler_params=_params("parallel"))(w, *gs, m, v, *deps)


def _flat_pack(arrs):
    flat = jnp.concatenate([a.reshape(-1).astype(F32) for a in arrs])
    n = flat.shape[0]
    pad = (-n) % 1024
    return jnp.pad(flat, (0, pad)).reshape(-1, 128)


def _flat_unpack(packed, shapes, lead=()):
    flat = packed.reshape(lead + (-1,))
    out, pos = [], 0
    for s in shapes:
        n = 1
        for d in s:
            n *= d
        out.append(flat[..., pos:pos + n].reshape(lead + tuple(s)))
        pos += n
    return out


def _interleave_vec(b, tn):
    f = b.shape[-1] // 2
    return b.reshape(2, f // tn, tn).transpose(1, 0, 2).reshape(1, 2 * f)


def _deinterleave_vec(b, tn):
    f = b.shape[-1] // 2
    return b.reshape(f // tn, 2, tn).transpose(1, 0, 2).reshape(2 * f)


def kernel(x, c, w_ada, b_ada, pre_mix_g, post_mix_g, pre_ffn_g, post_ffn_g, conv_w_pw1, conv_b_pw1, conv_w_dw, conv_b_dw, conv_ln_g, conv_ln_b, conv_w_pw2, conv_b_pw2, gla_w_in, gla_w_gate_up, gla_b_gate, gla_norm_g, gla_w_out, ffn_w_in, ffn_w_out, loss_target, m_w_ada, m_b_ada, m_pre_mix_g, m_post_mix_g, m_pre_ffn_g, m_post_ffn_g, m_conv_w_pw1, m_conv_b_pw1, m_conv_w_dw, m_conv_b_dw, m_conv_ln_g, m_conv_ln_b, m_conv_w_pw2, m_conv_b_pw2, m_gla_w_in, m_gla_w_gate_up, m_gla_b_gate, m_gla_norm_g, m_gla_w_out, m_ffn_w_in, m_ffn_w_out, v_w_ada, v_b_ada, v_pre_mix_g, v_post_mix_g, v_pre_ffn_g, v_post_ffn_g, v_conv_w_pw1, v_conv_b_pw1, v_conv_w_dw, v_conv_b_dw, v_conv_ln_g, v_conv_ln_b, v_conv_w_pw2, v_conv_b_pw2, v_gla_w_in, v_gla_w_gate_up, v_gla_b_gate, v_gla_norm_g, v_gla_w_out, v_ffn_w_in, v_ffn_w_out):
    weights = dict(w_ada=w_ada, b_ada=b_ada, pre_mix_g=pre_mix_g, post_mix_g=post_mix_g, pre_ffn_g=pre_ffn_g, post_ffn_g=post_ffn_g, conv_w_pw1=conv_w_pw1, conv_b_pw1=conv_b_pw1, conv_w_dw=conv_w_dw, conv_b_dw=conv_b_dw, conv_ln_g=conv_ln_g, conv_ln_b=conv_ln_b, conv_w_pw2=conv_w_pw2, conv_b_pw2=conv_b_pw2, gla_w_in=gla_w_in, gla_w_gate_up=gla_w_gate_up, gla_b_gate=gla_b_gate, gla_norm_g=gla_norm_g, gla_w_out=gla_w_out, ffn_w_in=ffn_w_in, ffn_w_out=ffn_w_out)
    mom_m = dict(w_ada=m_w_ada, b_ada=m_b_ada, pre_mix_g=m_pre_mix_g, post_mix_g=m_post_mix_g, pre_ffn_g=m_pre_ffn_g, post_ffn_g=m_post_ffn_g, conv_w_pw1=m_conv_w_pw1, conv_b_pw1=m_conv_b_pw1, conv_w_dw=m_conv_w_dw, conv_b_dw=m_conv_b_dw, conv_ln_g=m_conv_ln_g, conv_ln_b=m_conv_ln_b, conv_w_pw2=m_conv_w_pw2, conv_b_pw2=m_conv_b_pw2, gla_w_in=m_gla_w_in, gla_w_gate_up=m_gla_w_gate_up, gla_b_gate=m_gla_b_gate, gla_norm_g=m_gla_norm_g, gla_w_out=m_gla_w_out, ffn_w_in=m_ffn_w_in, ffn_w_out=m_ffn_w_out)
    mom_v = dict(w_ada=v_w_ada, b_ada=v_b_ada, pre_mix_g=v_pre_mix_g, post_mix_g=v_post_mix_g, pre_ffn_g=v_pre_ffn_g, post_ffn_g=v_post_ffn_g, conv_w_pw1=v_conv_w_pw1, conv_b_pw1=v_conv_b_pw1, conv_w_dw=v_conv_w_dw, conv_b_dw=v_conv_b_dw, conv_ln_g=v_conv_ln_g, conv_ln_b=v_conv_ln_b, conv_w_pw2=v_conv_w_pw2, conv_b_pw2=v_conv_b_pw2, gla_w_in=v_gla_w_in, gla_w_gate_up=v_gla_w_gate_up, gla_b_gate=v_gla_b_gate, gla_norm_g=v_gla_norm_g, gla_w_out=v_gla_w_out, ffn_w_in=v_ffn_w_in, ffn_w_out=v_ffn_w_out)
    order = list(weights)

    ax, ay, ac = lax.axis_index("x"), lax.axis_index("y"), lax.axis_index("c")
    my_chip, my_dev = 2 * ax + ay, 4 * ax + 2 * ay + ac

    x = x[0]
    target = loss_target[0]
    T, D = x.shape
    depth = w_ada.shape[0]
    n_conv, n_gla = conv_w_pw1.shape[0], gla_w_in.shape[0]
    width = conv_w_dw.shape[1]
    F = ffn_w_out.shape[1] * N_CHIPS
    DK = gla_w_gate_up.shape[2] * N_CHIPS
    rank = gla_w_gate_up.shape[1]
    gla_cols = 2 * DK + 2 * D + rank
    P = 2 * DK + 2 * D + 128
    dvh = D // GLA_HEADS
    tn_pw1 = _pick(conv_w_pw1.shape[2], (1024, 512, 256, 128))
    tn_ffn = ffn_w_out.shape[1]

    small_sharded = ["conv_w_dw", "gla_w_gate_up", "gla_b_gate", "gla_norm_g"]
    packed = _flat_pack([weights[n] for n in small_sharded])
    got = allgather_devices("gather_small", packed)[0::2]
    parts = _flat_unpack(got, [weights[n].shape for n in small_sharded], lead=(N_CHIPS,))
    w_dw_full, wgu_full, bgate_full, normg_full = [
        jnp.concatenate([p[s] for s in range(N_CHIPS)], axis=-1) for p in parts]
    w_dw_full = w_dw_full
    wgu_pad = jnp.pad(wgu_full, ((0, 0), (0, 128 - rank), (0, 0)))

    c_act = c * _sigmoid(c)
    c_all = allgather_devices("gather_c", jnp.pad(c_act, ((0, 7), (0, 0))))[:, 0, :]
    c16 = jnp.pad(c_all, ((0, 8), (0, 0))).astype(BF16)
    ada_cols = w_ada.shape[2]
    w_ada2 = w_ada.reshape(depth * D, ada_cols)
    mod_cols = [mm_nn(f"ada_fwd_{i}", c16, Wt(w_ada2, "plain", i, D, ada_cols)) for i in range(depth)]
    mod_cols = jnp.stack(mod_cols).reshape(depth * 16, ada_cols)
    mod_all = allgather_devices("gather_mod", mod_cols)[0::2]
    mod_all = mod_all.reshape(N_CHIPS, depth, 16, ada_cols).transpose(1, 2, 0, 3).reshape(depth, 16, 6 * D)
    mod = lax.dynamic_index_in_dim(mod_all, my_dev, axis=1, keepdims=False) + b_ada
    mod = mod.reshape(depth, 6, 1, D)

    gin_cols = gla_w_in.shape[2]

    def sublayer_weights(g):
        i = g // 2
        if g % 2:
            return [("ffn_w_in", i), ("ffn_w_out", i)]
        return [("conv_w_pw1", i // 2), ("conv_w_pw2", i // 2)] if i % 2 == 0 else [("gla_w_in", i // 2), ("gla_w_out", i // 2)]

    issued = []

    def start_gathers(g, after):
        handles, tokens = {}, []
        for n, l in sublayer_weights(g):
            shard = weights[n][l].astype(BF16)
            handles[n], token = chips_start(f"gather_start_{n}_{l}", shard, shard, my_chip, True,
                                            tuple(after) + tuple(issued[-1:]))
            issued.append(token)
            tokens.append(token)
        return handles, tokens

    def wait_gathers(g, handles, after):
        out = []
        for n, l in sublayer_weights(g):
            got = _split_wait(f"gather_wait_{n}_{l}", handles[n], after)
            r, cc = got.shape[1:]
            if n == "gla_w_in":
                full = got.transpose(1, 0, 2).reshape(D, N_CHIPS * gin_cols)
                out.append(Wt(jnp.pad(full, ((0, 0), (0, P - gla_cols))), "plain", 0, D, P))
            elif n in ("conv_w_pw1", "ffn_w_in"):
                out.append(Wt(got, "col", 0, r, cc, tn_pw1 if n == "conv_w_pw1" else tn_ffn))
            else:
                out.append(Wt(got, "row", 0, r, cc))
        return out

    n_sub = 2 * depth
    in_flight = {g: start_gathers(g, (mod,)) for g in range(min(3, n_sub))}

    def enter_sublayer(g, xs):
        if g + 2 >= n_sub:
            return ()
        if g + 2 not in in_flight:
            in_flight[g + 2] = start_gathers(g + 2, (xs,))
        return in_flight[g + 2][1]

    w_pw1, w_pw2, w_gin, w_gout = [None] * n_conv, [None] * n_conv, [None] * n_gla, [None] * n_gla
    w_fin, w_fout = [None] * depth, [None] * depth
    row = lambda a: a.reshape(1, -1)
    saved = []
    xs = x
    for i in range(depth):
        j = i // 2
        sh1, sc1, gt1, sh2, sc2, gt2 = [mod[i, q] for q in range(6)]
        s = dict(x_in=xs)
        deps = tuple(enter_sublayer(2 * i, xs))
        if i == 0:
            deps += tuple(in_flight[0][1] + in_flight[1][1])
            h = prenorm_fwd(f"pre_mix_{i}", xs, row(pre_mix_g[i]) * (1.0 + sc1), sh1)
        s["h_mix"] = h
        if i % 2 == 0:
            w_pw1[j], w_pw2[j] = wait_gathers(2 * i, in_flight[2 * i][0], (h,) + deps)
        else:
            w_gin[j], w_gout[j] = wait_gathers(2 * i, in_flight[2 * i][0], (h,) + deps)
        nxt = (row(pre_ffn_g[i]) * (1.0 + sc2), sh2)
        if i % 2 == 0:
            u, glu = gated_in_fwd(f"pw1_{i}", h, w_pw1[j], "glu", bias=_interleave_vec(row(conv_b_pw1[j]), tn_pw1))
            v = dwconv_fwd(f"dwconv_{i}", glu, w_dw_full[j], row(conv_b_dw[j]))
            sl = ln_silu_fwd(f"ln_silu_{i}", v, row(conv_ln_g[j]), row(conv_ln_b[j]))
            y, x_mid, h = mm_nn_post(f"pw2_{i}", sl, w_pw2[j], xs, gt1 * row(post_mix_g[i]), bias=row(conv_b_pw2[j]),
                                     nxt=nxt)
            s.update(u=u, glu=glu, v=v, s=sl)
        else:
            proj = mm_nn(f"gla_in_{i}", h, w_gin[j])
            o, states = gla_fwd(f"gla_{i}", proj, wgu_pad[j], row(bgate_full[j]))
            og = gla_out_fwd(f"gla_out_{i}", o, proj, row(normg_full[j]), (2 * DK + D) // dvh)
            y, x_mid, h = mm_nn_post(f"gla_wout_{i}", og, w_gout[j], xs, gt1 * row(post_mix_g[i]), nxt=nxt)
            s.update(proj=proj, o=o, states=states, og=og)
        s["y_mix"] = y
        xs = x_mid
        s["x_mid"] = xs
        deps = tuple(enter_sublayer(2 * i + 1, xs))
        w_fin[i], w_fout[i] = wait_gathers(2 * i + 1, in_flight[2 * i + 1][0], (h,) + deps)
        u, a = gated_in_fwd(f"ffn_in_{i}", h, w_fin[i], "swiglu")
        s.update(h_ffn=h, u_ffn=u, a_ffn=a)
        if i + 1 < depth:
            y, xs, h = mm_nn_post(f"ffn_out_{i}", a, w_fout[i], xs, gt2 * row(post_ffn_g[i]),
                                  nxt=(row(pre_mix_g[i + 1]) * (1.0 + mod[i + 1, 1]), mod[i + 1, 0]))
        else:
            y, xs = mm_nn_post(f"ffn_out_{i}", a, w_fout[i], xs, gt2 * row(post_ffn_g[i]))
        s["y_ffn"] = y
        saved.append(s)

    dx, loss_acc = loss_bwd("loss", xs, target)
    loss = lax.psum(jnp.sum(loss_acc), ("x", "y", "c"))

    fold = lambda a: jnp.sum(a, axis=0)
    small_g = {n: [None] * weights[n].shape[0] for n in order if n not in
               ("w_ada", "conv_w_pw1", "conv_w_pw2", "gla_w_in", "gla_w_out", "ffn_w_in", "ffn_w_out")}
    grad_w = lambda wt: Wt(None, wt.kind, 0, wt.R, wt.C, wt.tn)
    scattering = {}

    def start_scatter(n, l, p):
        own = lax.dynamic_index_in_dim(p, my_chip, axis=0, keepdims=False)
        scattering[n, l], token = chips_start(f"scatter_start_{n}_{l}", p, own, my_chip, False)
        return token

    for i in reversed(range(depth)):
        j = i // 2
        s = saved[i]
        sh1, sc1, gt1, sh2, sc2, gt2 = [mod[i, q] for q in range(6)]
        if i == depth - 1:
            dy, t_post, _ = postnorm_bwd(f"post_ffn_bwd_{i}", dx, s["y_ffn"], row(post_ffn_g[i]), gt2)
        else:
            dy, t_post = carried
        d_gt2 = fold(t_post) * post_ffn_g[i]
        small_g["post_ffn_g"][i] = fold(t_post) * gt2[0]
        du = ffn_out_dgrad(f"ffn_out_dgrad_{i}", dy, w_fout[i], s["u_ffn"])
        t_out = start_scatter("ffn_w_out", i, mm_tn(f"ffn_out_wgrad_{i}", s["a_ffn"], dy, grad_w(w_fout[i])))
        t_in = start_scatter("ffn_w_in", i, mm_tn(f"ffn_in_wgrad_{i}", s["h_ffn"], du, grad_w(w_fin[i])))
        dh = mm_nt(f"ffn_in_dgrad_{i}", du, w_fin[i], out_dtype=BF16, deps=(t_out, t_in))
        dx, dy, d_sh2, t_pre, t_post, dy_sum = norm_chain_bwd(
            f"pre_ffn_bwd_{i}", dh, s["x_mid"], dx, row(pre_ffn_g[i]), sc2,
            s["y_mix"], row(post_mix_g[i]), gt1, deps=(t_out, t_in))
        d_sh2, d_sc2 = fold(d_sh2), fold(t_pre) * pre_ffn_g[i]
        small_g["pre_ffn_g"][i] = fold(t_pre) * (1.0 + sc2[0])
        d_gt1 = fold(t_post) * post_mix_g[i]
        small_g["post_mix_g"][i] = fold(t_post) * gt1[0]
        if i % 2 == 0:
            small_g["conv_b_pw2"][j] = fold(dy_sum)
            dsl = mm_nt(f"pw2_dgrad_{i}", dy, w_pw2[j])
            t_out = start_scatter("conv_w_pw2", j, mm_tn(f"pw2_wgrad_{i}", s["s"], dy, grad_w(w_pw2[j])))
            dv, d_lg, d_lb = ln_silu_bwd(f"ln_silu_bwd_{i}", dsl, s["v"], row(conv_ln_g[j]), row(conv_ln_b[j]))
            small_g["conv_ln_g"][j], small_g["conv_ln_b"][j] = fold(d_lg), fold(d_lb)
            dglu, d_wdw, d_bdw = dwconv_bwd(f"dwconv_bwd_{i}", dv, s["glu"], w_dw_full[j])
            small_g["conv_w_dw"][j] = d_wdw.reshape(width, 8, D).sum(axis=1)
            small_g["conv_b_dw"][j] = fold(d_bdw)
            du, du_sum = glu_bwd(f"glu_bwd_{i}", s["u"], dglu, tn_pw1)
            small_g["conv_b_pw1"][j] = _deinterleave_vec(fold(du_sum), tn_pw1)
            t_in = start_scatter("conv_w_pw1", j, mm_tn(f"pw1_wgrad_{i}", s["h_mix"], du, grad_w(w_pw1[j])))
            dh = mm_nt(f"pw1_dgrad_{i}", du, w_pw1[j], out_dtype=BF16, deps=(t_out, t_in))
        else:
            dog = mm_nt(f"gla_wout_dgrad_{i}", dy, w_gout[j])
            t_out = start_scatter("gla_w_out", j, mm_tn(f"gla_wout_wgrad_{i}", s["og"], dy, grad_w(w_gout[j])))
            d_o, d_r, d_ng = gla_out_bwd(f"gla_out_bwd_{i}", dog, s["o"], s["proj"], row(normg_full[j]),
                                         (2 * DK + D) // dvh)
            small_g["gla_norm_g"][j] = fold(d_ng)
            dproj, dz, d_bg = gla_bwd(f"gla_bwd_{i}", s["proj"], wgu_pad[j], row(bgate_full[j]),
                                      s["states"], d_o, d_r)
            small_g["gla_b_gate"][j] = fold(d_bg)
            a_low = s["proj"][:, 2 * DK + 2 * D:].astype(BF16)
            d_wgu = mm_tn(f"gla_gate_wgrad_{i}", a_low, dz, Wt(None, "plain", 0, 128, DK), out_dtype=F32)
            small_g["gla_w_gate_up"][j] = d_wgu[:rank]
            g_in = mm_tn(f"gla_in_wgrad_{i}", s["h_mix"], dproj, Wt(None, "plain", 0, D, P))
            t_in = start_scatter("gla_w_in", j, g_in[:, :gla_cols].reshape(D, N_CHIPS, gin_cols).transpose(1, 0, 2))
            dh = mm_nt(f"gla_in_dgrad_{i}", dproj, w_gin[j], out_dtype=BF16, deps=(t_out, t_in))
        if i > 0:
            dx, dy_b, d_sh1, t_pre, t_post_b, _ = norm_chain_bwd(
                f"pre_mix_bwd_{i}", dh, s["x_in"], dx, row(pre_mix_g[i]), sc1,
                saved[i - 1]["y_ffn"], row(post_ffn_g[i - 1]), mod[i - 1, 5], deps=(t_out, t_in))
            carried = (dy_b, t_post_b)
        else:
            dx, d_sh1, t_pre = prenorm_bwd(f"pre_mix_bwd_{i}", dh, s["x_in"], dx, row(pre_mix_g[i]), sc1,
                                           deps=(t_out, t_in))
        d_sh1, d_sc1 = fold(d_sh1), fold(t_pre) * pre_mix_g[i]
        small_g["pre_mix_g"][i] = fold(t_pre) * (1.0 + sc1[0])
        small_g["b_ada"][i] = jnp.concatenate([d_sh1, d_sc1, d_gt1, d_sh2, d_sc2, d_gt2])
    grad_x = dx[None]

    small_names = list(small_g)
    local_small = [jnp.stack(small_g[n]) for n in small_names]
    full_shapes = [a.shape for a in local_small]
    small_handle, small_token = devices_start("small_grads_start", _flat_pack(local_small), my_dev)

    def flat2(a):
        return a.reshape(-1, a.shape[-1])

    swapping, last = [], (small_token,)
    for n in ("conv_w_pw1", "conv_w_pw2", "gla_w_in", "gla_w_out", "ffn_w_in", "ffn_w_out"):
        arrived = [_split_wait(f"scatter_wait_{n}_{l}", scattering[n, l], (dx,)) for l in range(weights[n].shape[0])]
        mine = sum_slots_layers("sum_" + n, arrived)
        handle, token = swap_start("swap_start_" + n, mine, last)
        swapping.append((n, handle))
        last = (token,)

    small_all = _split_wait("small_grads_wait", small_handle, last)
    small_sum = sum_slots("sum_small_grads", small_all)
    small_tot = dict(zip(small_names, _flat_unpack(small_sum, full_shapes)))
    for n in small_sharded:
        cols = weights[n].shape[-1]
        small_tot[n] = lax.dynamic_slice_in_dim(small_tot[n], my_chip * cols, cols, axis=small_tot[n].ndim - 1)

    dmod_all = _flat_unpack(small_all, full_shapes, lead=(N_DEVICES,))[small_names.index("b_ada")]
    dmod_cols = lax.dynamic_slice_in_dim(
        dmod_all.reshape(N_DEVICES, depth, N_CHIPS, ada_cols), my_chip, 1, axis=2)[:, :, 0, :]
    dmod16 = jnp.pad(dmod_cols.reshape(N_DEVICES, depth * ada_cols), ((0, 8), (0, 0))).astype(BF16)
    g_ada = [mm_tn(f"ada_wgrad_{i}", c16, dmod16[:, i * ada_cols:(i + 1) * ada_cols],
                   Wt(None, "plain", 0, D, ada_cols), out_dtype=F32) for i in range(depth)]
    g_ada = jnp.stack(g_ada).reshape(depth * D, ada_cols)

    results = {}
    out = adamw("adamw_w_ada", flat2(w_ada), [g_ada], flat2(m_w_ada), flat2(v_w_ada))
    results["w_ada"] = [t.reshape(w_ada.shape) for t in out]
    w_small = _flat_pack([weights[n] for n in small_names])
    out_small = adamw("adamw_small", w_small, [_flat_pack([small_tot[n] for n in small_names])],
                      _flat_pack([mom_m[n] for n in small_names]), _flat_pack([mom_v[n] for n in small_names]))
    for n, handle in swapping:
        w = weights[n]
        mine, theirs = _split_wait("swap_wait_" + n, handle, (out[0],), with_src=True)
        res = adamw("adamw_" + n, flat2(w), [mine, theirs], flat2(mom_m[n]), flat2(mom_v[n]))
        results[n] = [t.reshape(w.shape) for t in res]
    out = out_small
    shapes = [weights[n].shape for n in small_names]
    unpacked = [_flat_unpack(t, shapes) for t in out]
    for q, n in enumerate(small_names):
        results[n] = [unpacked[r][q] for r in range(4)]

    outs = [loss, grad_x]
    for r in range(4):
        outs += [results[n][r] for n in order]
    return tuple(outs)
```

```python
import jax
import jax.numpy as jnp
from jax import lax
from jax.experimental import pallas as pl
from jax.experimental.pallas import tpu as pltpu

F32, BF16 = jnp.float32, jnp.bfloat16
MESH = pl.DeviceIdType.MESH
HBM = pl.BlockSpec(memory_space=pl.ANY)

EPS = 1e-6
VMEM_LIMIT_BYTES = 48 * 1024 * 1024
N_CHIPS = 4
N_DEVICES = 8
GLA_HEADS = 4
GLA_CHUNK = 128
GLA_TAU = 16.0
CONV_HALO = 32
ADAM_LR, ADAM_B1, ADAM_B2, ADAM_EPS, ADAM_WD, ADAM_STEP = 0.001, 0.9, 0.999, 1e-08, 0.01, 10


def _pc(body, **kw):
    return pl.pallas_call(body, **kw)


def _params(*sem):
    return pltpu.CompilerParams(dimension_semantics=sem, vmem_limit_bytes=VMEM_LIMIT_BYTES)


def _pick(n, cands):
    for c in cands:
        if c <= n and n % c == 0:
            return c
    return n


def _fold8(z):
    r, w = z.shape
    return z.reshape(r // 8, 8, w).sum(axis=0)


def _sigmoid(x):
    return 1.0 / (1.0 + jnp.exp(-x))


def _exchange(name, xs, out_shape, masks, src_of, dst_of, local=True):
    n_in = len(xs)

    def body(*refs):
        x_refs, o_ref = refs[:n_in], refs[n_in]
        send_sems, recv_sems, local_sems = refs[n_in + 1:]
        x, y, c = lax.axis_index("x"), lax.axis_index("y"), lax.axis_index("c")
        me = (x, y, c)
        peers = [(1 - x if a else x, 1 - y if b else y, 1 - c if d else c) for a, b, d in masks]
        started = []
        if local:
            for q, (s, t) in enumerate(zip(src_of(x_refs, me), dst_of(o_ref, me))):
                cp = pltpu.make_async_copy(s, t, local_sems.at[q])
                cp.start()
                started.append(cp)
        sends = []
        for k, peer in enumerate(peers):
            for q, (s, t) in enumerate(zip(src_of(x_refs, peer), dst_of(o_ref, me))):
                cp = pltpu.make_async_remote_copy(
                    src_ref=s, dst_ref=t, send_sem=send_sems.at[k, q], recv_sem=recv_sems.at[k, q],
                    device_id=peer, device_id_type=MESH)
                cp.start()
                sends.append(cp)
        for k, peer in enumerate(peers):
            for q, (s, t) in enumerate(zip(src_of(x_refs, me), dst_of(o_ref, peer))):
                pltpu.make_async_remote_copy(
                    src_ref=s, dst_ref=t, send_sem=send_sems.at[k, q], recv_sem=recv_sems.at[k, q],
                    device_id=peer, device_id_type=MESH).wait_recv()
        for cp in sends:
            cp.wait_send()
        for cp in started:
            cp.wait()

    n_q = n_in if len(xs) > 1 else 1
    return _pc(
        body, name=name, out_shape=out_shape,
        in_specs=[HBM] * n_in, out_specs=HBM,
        scratch_shapes=[pltpu.SemaphoreType.DMA((len(masks), n_q)),
                        pltpu.SemaphoreType.DMA((len(masks), n_q)),
                        pltpu.SemaphoreType.DMA((n_q,))],
    )(*xs)


_CHIP_MASKS = [(1, 0, 0), (0, 1, 0), (1, 1, 0)]
_ALL_MASKS = [(a, b, d) for a in (0, 1) for b in (0, 1) for d in (0, 1) if (a, b, d) != (0, 0, 0)]


def _chip(p):
    return 2 * p[0] + p[1]


def _dev(p):
    return 4 * p[0] + 2 * p[1] + p[2]


def allgather_devices(name, v):
    r, c = v.shape
    return _exchange(
        name, [v], jax.ShapeDtypeStruct((N_DEVICES, r, c), v.dtype), _ALL_MASKS,
        lambda xr, peer: [xr[0]], lambda o, src: [o.at[_dev(src)]])


_HBM = pl.BlockSpec(memory_space=pltpu.HBM)
_SEM = pl.BlockSpec(memory_space=pltpu.SEMAPHORE)
_N_PEER_CHIPS = len(_CHIP_MASKS)


def _peer_chips():
    x, y, c = lax.axis_index("x"), lax.axis_index("y"), lax.axis_index("c")
    return (x, y), [((1 - x if a else x, 1 - y if b else y), c) for a, b, _ in _CHIP_MASKS]


def _chip_copies(src_ref, land_ref, send_sems, recv_sems, whole_src, incoming):
    me, peers = _peer_chips()
    out = []
    for k, (chip, c) in enumerate(peers):
        out.append(pltpu.make_async_remote_copy(
            src_ref=src_ref if whole_src else src_ref.at[_chip(chip)],
            dst_ref=land_ref.at[_chip(chip) if incoming else _chip(me)],
            send_sem=send_sems.at[k], recv_sem=recv_sems.at[k], device_id=(*chip, c), device_id_type=MESH))
    return out


def _core_copies(src_ref, land_ref, send_sems, recv_sems, incoming):
    x, y, c = lax.axis_index("x"), lax.axis_index("y"), lax.axis_index("c")
    return [pltpu.make_async_remote_copy(
        src_ref=src_ref, dst_ref=land_ref, send_sem=send_sems.at[0], recv_sem=recv_sems.at[0],
        device_id=(x, y, 1 - c), device_id_type=MESH)]


def _split_start(name, src, land, n_copies, copies, after):
    n_after = len(after)

    def body(*refs):
        src_ref, land_ref = refs[0], refs[1]
        send_sems, recv_sems, _, _, token = refs[2 + n_after:]
        for send in copies(src_ref, land_ref, send_sems, recv_sems, False):
            send.start()
        token[...] = jnp.zeros(token.shape, token.dtype)

    send_sems, recv_sems, src_thru, land_thru, token = _pc(
        body, name=name,
        out_shape=(pltpu.SemaphoreType.DMA((n_copies,)), pltpu.SemaphoreType.DMA((n_copies,)),
                   pltpu.HBM(src.shape, src.dtype), pltpu.HBM(land.shape, land.dtype),
                   jax.ShapeDtypeStruct((8, 128), F32)),
        in_specs=[_HBM, _HBM] + [HBM] * n_after,
        out_specs=(_SEM, _SEM, _HBM, _HBM, pl.BlockSpec(memory_space=pltpu.VMEM)),
        input_output_aliases={0: 2, 1: 3},
        compiler_params=pltpu.CompilerParams(has_side_effects=pltpu.SideEffectType.DATAFLOW_SIDE_EFFECTING),
    )(pltpu.with_memory_space_constraint(src, pltpu.HBM), pltpu.with_memory_space_constraint(land, pltpu.HBM),
      *after)
    return (send_sems, recv_sems, src_thru, land_thru, copies), token


def _split_wait(name, handle, after, with_src=False):
    send_sems, recv_sems, src_thru, land_thru, copies = handle
    n_after = len(after)

    def body(src_ref, land_ref, send_sems, recv_sems, *rest):
        for send in copies(src_ref, land_ref, send_sems, recv_sems, False):
            send.wait_send()
        for recv in copies(src_ref, land_ref, send_sems, recv_sems, True):
            recv.wait_recv()

    src, land = _pc(
        body, name=name,
        out_shape=(pltpu.HBM(src_thru.shape, src_thru.dtype), pltpu.HBM(land_thru.shape, land_thru.dtype)),
        in_specs=[_HBM, _HBM, _SEM, _SEM] + [HBM] * n_after, out_specs=(_HBM, _HBM),
        input_output_aliases={0: 0, 1: 1},
        compiler_params=pltpu.CompilerParams(has_side_effects=pltpu.SideEffectType.DATAFLOW_SIDE_EFFECTING),
    )(src_thru, land_thru, send_sems, recv_sems, *after)
    return (src, land) if with_src else land


def chips_start(name, src, own, my_chip, whole_src, after=()):
    r, c = own.shape
    land = lax.dynamic_update_slice(lax.empty((N_CHIPS, r, c), src.dtype), own[None], (my_chip, 0, 0))

    def copies(src_ref, land_ref, send_sems, recv_sems, incoming):
        return _chip_copies(src_ref, land_ref, send_sems, recv_sems, whole_src, incoming)

    return _split_start(name, src, land, _N_PEER_CHIPS, copies, after)


def _device_copies(src_ref, land_ref, send_sems, recv_sems, incoming):
    x, y, c = lax.axis_index("x"), lax.axis_index("y"), lax.axis_index("c")
    out = []
    for k, (a, b, d) in enumerate(_ALL_MASKS):
        peer = (1 - x if a else x, 1 - y if b else y, 1 - c if d else c)
        out.append(pltpu.make_async_remote_copy(
            src_ref=src_ref, dst_ref=land_ref.at[_dev(peer) if incoming else _dev((x, y, c))],
            send_sem=send_sems.at[k], recv_sem=recv_sems.at[k], device_id=peer, device_id_type=MESH))
    return out


def devices_start(name, v, my_dev, after=()):
    land = lax.dynamic_update_slice(lax.empty((N_DEVICES,) + v.shape, v.dtype), v[None], (my_dev, 0, 0))
    return _split_start(name, v, land, len(_ALL_MASKS), _device_copies, after)


def swap_start(name, a, after=()):
    return _split_start(name, a, lax.empty(a.shape, a.dtype), 1, _core_copies, after)


class Wt:
    def __init__(self, arr, kind, layer, rows, cols, tn=None):
        self.arr, self.kind, self.l, self.R, self.C, self.tn = arr, kind, layer, rows, cols, tn
        self.K = 4 * rows if kind == "row" else rows
        self.N = 4 * cols if kind == "col" else cols

    def spec(self, tk, tn):
        l, R, C = self.l, self.R, self.C
        if self.kind == "plain":
            off = l * (R // tk)
            return (tk, tn), lambda kb, jb: (off + kb, jb)
        if self.kind == "col":
            assert tn == self.tn
            per, off = C // tn, l * (R // tk)
            return (None, tk, tn), lambda kb, jb: (2 * (jb % 2) + (jb // 2) // per, off + kb, (jb // 2) % per)
        per = R // tk
        return (None, tk, tn), lambda kb, jb: (kb // per, l * per + kb % per, jb)

    def tile_k(self, cands):
        return _pick(self.R, cands)

    def tile_n(self, cands):
        return self.tn if self.kind == "col" else _pick(self.C, cands)


_TM = (1024, 512, 256, 128, 64, 32, 16, 8)
_TK = (2048, 1408, 1024, 896, 512, 256, 128, 64, 32, 16)
_TN = (1024, 896, 768, 512, 384, 256, 128)
_TR = (1024, 1408, 512, 256, 128, 64, 32, 16)


def _accumulate(part, acc_ref, k, nk, finish):
    if nk == 1:
        finish(part)
        return

    @pl.when(k == 0)
    def _():
        acc_ref[...] = part

    @pl.when(k > 0)
    def _():
        acc_ref[...] += part

    @pl.when(k == nk - 1)
    def _():
        finish(acc_ref[...])


def mm_nn(name, a, w, *, bias=None, out_dtype=F32):
    M, K = a.shape
    assert K == w.K
    N = w.N
    tm, tk, tn = _pick(M, _TM), w.tile_k(_TK), w.tile_n(_TN)
    nk = K // tk
    wblock, wmap = w.spec(tk, tn)
    in_specs = [pl.BlockSpec((tm, tk), lambda i, j, k: (i, k)),
                pl.BlockSpec(wblock, lambda i, j, k: wmap(k, j))]
    args = [a, w.arr]
    if bias is not None:
        in_specs.append(pl.BlockSpec((1, tn), lambda i, j, k: (0, j)))
        args.append(bias)

    def body(*refs):
        a_ref, b_ref = refs[0], refs[1]
        bias_ref = refs[2] if bias is not None else None
        o_ref = refs[3] if bias is not None else refs[2]
        acc_ref = refs[-1] if nk > 1 else None
        part = jnp.dot(a_ref[...].astype(BF16), b_ref[...].astype(BF16), preferred_element_type=F32)

        def finish(acc):
            if bias_ref is not None:
                acc = acc + bias_ref[...]
            o_ref[...] = acc.astype(o_ref.dtype)

        _accumulate(part, acc_ref, pl.program_id(2), nk, finish)

    return _pc(
        body, name=name, out_shape=jax.ShapeDtypeStruct((M, N), out_dtype),
        grid=(M // tm, N // tn, nk), in_specs=in_specs,
        out_specs=pl.BlockSpec((tm, tn), lambda i, j, k: (i, j)),
        scratch_shapes=[pltpu.VMEM((tm, tn), F32)] if nk > 1 else [],
        compiler_params=_params("parallel", "parallel", "arbitrary"),
    )(*args)


def mm_nt(name, a, w, *, out_dtype=F32, deps=()):
    M, N = a.shape
    assert N == w.N
    K = w.K
    tm, tj, tn = _pick(M, _TM), w.tile_k(_TR), w.tile_n(_TK)
    wblock, wmap = w.spec(tj, tn)
    nb = 2 if w.kind == "col" else 1
    nn = N // (nb * tn)

    def body(a_ref, *refs):
        b_refs, o_ref, scr = refs[:nb], refs[nb + len(deps)], refs[nb + len(deps) + 1:]
        part = None
        for q, b_ref in enumerate(b_refs):
            p = lax.dot_general(a_ref[:, q * tn:(q + 1) * tn].astype(BF16), b_ref[...].astype(BF16),
                                (((1,), (1,)), ((), ())), preferred_element_type=F32)
            part = p if part is None else part + p

        def finish(acc):
            o_ref[...] = acc.astype(o_ref.dtype)

        _accumulate(part, scr[0] if nn > 1 else None, pl.program_id(2), nn, finish)

    def b_spec(q):
        return pl.BlockSpec(wblock, lambda i, j, n: wmap(j, nb * n + q))

    return _pc(
        body, name=name, out_shape=jax.ShapeDtypeStruct((M, K), out_dtype),
        grid=(M // tm, K // tj, nn),
        in_specs=[pl.BlockSpec((tm, nb * tn), lambda i, j, n: (i, n))] + [b_spec(q) for q in range(nb)]
        + [HBM] * len(deps),
        out_specs=pl.BlockSpec((tm, tj), lambda i, j, n: (i, j)),
        scratch_shapes=[pltpu.VMEM((tm, tj), F32)] if nn > 1 else [],
        compiler_params=_params("parallel", "parallel", "arbitrary"),
    )(a, *([w.arr] * nb), *deps)


def mm_tn(name, a, b, w, *, out_dtype=BF16):
    T, K = a.shape
    N = b.shape[1]
    assert (K, N) == (w.K, w.N) and w.l == 0
    tm = w.tile_k(_TR)
    tn = w.tile_n(_TN)
    tk = _pick(T, (2048, 1024, 512, 256, 128, 64, 32, 16))
    nk = T // tk
    oblock, omap = w.spec(tm, tn)
    shape = (w.R, w.C) if w.kind == "plain" else (N_CHIPS, w.R, w.C)

    def body(a_ref, b_ref, o_ref, *scr):
        part = lax.dot_general(a_ref[...].astype(BF16), b_ref[...].astype(BF16),
                               (((0,), (0,)), ((), ())), preferred_element_type=F32)

        def finish(acc):
            o_ref[...] = acc.astype(o_ref.dtype)

        _accumulate(part, scr[0] if nk > 1 else None, pl.program_id(2), nk, finish)

    return _pc(
        body, name=name, out_shape=jax.ShapeDtypeStruct(shape, out_dtype),
        grid=(K // tm, N // tn, nk),
        in_specs=[pl.BlockSpec((tk, tm), lambda i, j, k: (k, i)),
                  pl.BlockSpec((tk, tn), lambda i, j, k: (k, j))],
        out_specs=pl.BlockSpec(oblock, lambda i, j, k: omap(i, j)),
        scratch_shapes=[pltpu.VMEM((tm, tn), F32)] if nk > 1 else [],
        compiler_params=_params("parallel", "parallel", "arbitrary"),
    )(a, b)


_TM_FUSED = (512, 256, 128, 64, 32, 16, 8)


def mm_nn_post(name, a, w, x, wvec, *, bias=None, nxt=None):
    M, K = a.shape
    assert K == w.K
    N = w.N
    tm, tk = _pick(M, _TM_FUSED), w.tile_k(_TK)
    rows = _pick(tm, (64, 32, 16, 8))
    nk = K // tk
    wblock, wmap = w.spec(tk, N)
    in_specs = [pl.BlockSpec((tm, tk), lambda i, k: (i, k)),
                pl.BlockSpec(wblock, lambda i, k: wmap(k, 0)),
                pl.BlockSpec((tm, N), lambda i, k: (i, 0)),
                pl.BlockSpec((1, N), lambda i, k: (0, 0))]
    args = [a, w.arr, x, wvec]
    for extra in ([bias] if bias is not None else []) + list(nxt or ()):
        in_specs.append(pl.BlockSpec((1, N), lambda i, k: (0, 0)))
        args.append(extra)
    n_out = 3 if nxt else 2

    def body(*refs):
        a_ref, b_ref, x_ref, wv_ref = refs[:4]
        bias_ref = refs[4] if bias is not None else None
        gain_ref, shift_ref = refs[-n_out - 3:-n_out - 1] if nxt else (None, None)
        y_ref, xn_ref = refs[-n_out - 1:-n_out + 1]
        acc_ref = refs[-1]
        part = jnp.dot(a_ref[...].astype(BF16), b_ref[...].astype(BF16), preferred_element_type=F32)
        k = pl.program_id(1)

        @pl.when(k == 0)
        def _():
            acc_ref[...] = part

        @pl.when(k > 0)
        def _():
            acc_ref[...] += part

        @pl.when(k == nk - 1)
        def _():
            for r0 in range(0, tm, rows):
                rr = pl.ds(r0, rows)
                yv = acc_ref[rr, :]
                if bias_ref is not None:
                    yv = yv + bias_ref[...]
                y_ref[rr, :] = yv.astype(BF16)
                xn = x_ref[rr, :] + yv * _rms(yv) * wv_ref[...]
                xn_ref[rr, :] = xn
                if nxt:
                    refs[-2][rr, :] = (xn * _rms(xn) * gain_ref[...] + shift_ref[...]).astype(BF16)

    spec = pl.BlockSpec((tm, N), lambda i, k: (i, 0))
    out_shape = [jax.ShapeDtypeStruct((M, N), BF16), jax.ShapeDtypeStruct((M, N), F32)]
    if nxt:
        out_shape.append(jax.ShapeDtypeStruct((M, N), BF16))
    return _pc(
        body, name=name, out_shape=out_shape,
        grid=(M // tm, nk), in_specs=in_specs, out_specs=[spec] * n_out,
        scratch_shapes=[pltpu.VMEM((tm, N), F32)],
        compiler_params=_params("parallel", "arbitrary"),
    )(*args)


def gated_in_fwd(name, h, w, unit, *, bias=None):
    M, K = h.shape
    assert w.kind == "col" and K == w.K == w.R
    tn, F = w.tn, w.N // 2
    tm = _pick(M, _TM_FUSED)
    wblock, wmap = w.spec(K, tn)
    in_specs = [pl.BlockSpec((tm, K), lambda g, i: (i, 0)),
                pl.BlockSpec(wblock, lambda g, i: wmap(0, 2 * g)),
                pl.BlockSpec(wblock, lambda g, i: wmap(0, 2 * g + 1))]
    args = [h, w.arr, w.arr]
    if bias is not None:
        in_specs.append(pl.BlockSpec((1, 2 * tn), lambda g, i: (0, g)))
        args.append(bias)

    def body(h_ref, w1_ref, w2_ref, *refs):
        u_ref, act_ref = refs[-2:]
        hv = h_ref[...].astype(BF16)
        first = jnp.dot(hv, w1_ref[...].astype(BF16), preferred_element_type=F32)
        second = jnp.dot(hv, w2_ref[...].astype(BF16), preferred_element_type=F32)
        if bias is not None:
            first, second = first + refs[0][:, :tn], second + refs[0][:, tn:]
        u_ref[:, :tn] = first.astype(BF16)
        u_ref[:, tn:] = second.astype(BF16)
        if unit == "swiglu":
            act_ref[...] = (first * _sigmoid(first) * second).astype(act_ref.dtype)
        else:
            act_ref[...] = (first * _sigmoid(second)).astype(act_ref.dtype)

    return _pc(
        body, name=name,
        out_shape=[jax.ShapeDtypeStruct((M, 2 * F), BF16),
                   jax.ShapeDtypeStruct((M, F), BF16 if unit == "swiglu" else F32)],
        grid=(F // tn, M // tm), in_specs=in_specs,
        out_specs=[pl.BlockSpec((tm, 2 * tn), lambda g, i: (i, g)),
                   pl.BlockSpec((tm, tn), lambda g, i: (i, g))],
        compiler_params=_params("parallel", "parallel"),
    )(*args)


def ffn_out_dgrad(name, dy, w, u):
    M, N = dy.shape
    assert w.kind == "row" and N == w.N == w.C
    tj, F = w.R, w.K
    tm = _pick(M, _TM_FUSED)
    wblock, wmap = w.spec(tj, N)

    def body(dy_ref, b_ref, u_ref, du_ref):
        d = lax.dot_general(dy_ref[...].astype(BF16), b_ref[...].astype(BF16),
                            (((1,), (1,)), ((), ())), preferred_element_type=F32)
        gate, up = u_ref[:, :tj].astype(F32), u_ref[:, tj:].astype(F32)
        sg = _sigmoid(gate)
        du_ref[:, :tj] = (d * up * (sg * (1.0 + gate * (1.0 - sg)))).astype(BF16)
        du_ref[:, tj:] = (d * gate * sg).astype(BF16)

    return _pc(
        body, name=name, out_shape=jax.ShapeDtypeStruct((M, 2 * F), BF16),
        grid=(F // tj, M // tm),
        in_specs=[pl.BlockSpec((tm, N), lambda j, i: (i, 0)),
                  pl.BlockSpec(wblock, lambda j, i: wmap(j, 0)),
                  pl.BlockSpec((tm, 2 * tj), lambda j, i: (i, j))],
        out_specs=pl.BlockSpec((tm, 2 * tj), lambda j, i: (i, j)),
        compiler_params=_params("parallel", "parallel"),
    )(dy, w.arr, u)


def _rows(name, body, rows_in, vecs_in, rows_out, accs_out, tm=256, deps=()):
    T = rows_in[0].shape[0]
    tm = _pick(T, (tm, 128, 64, 32, 16, 8))
    n_r, n_v, n_o, n_d = len(rows_in), len(vecs_in), len(rows_out), len(deps)

    def kern(*refs):
        r_refs, v_refs = refs[:n_r], refs[n_r:n_r + n_v]
        refs = refs[n_r + n_v + n_d:]
        o_refs, a_refs = refs[:n_o], refs[n_o:]

        @pl.when(pl.program_id(0) == 0)
        def _():
            for a_ref in a_refs:
                a_ref[...] = jnp.zeros(a_ref.shape, F32)

        body(r_refs, v_refs, o_refs, a_refs)

    in_specs = [pl.BlockSpec((tm, a.shape[1]), lambda i: (i, 0)) for a in rows_in]
    in_specs += [pl.BlockSpec(v.shape, lambda i: (0, 0)) for v in vecs_in]
    in_specs += [HBM] * n_d
    out_shape = [jax.ShapeDtypeStruct((T, w), dt) for w, dt in rows_out]
    out_shape += [jax.ShapeDtypeStruct((8, w), F32) for w in accs_out]
    out_specs = [pl.BlockSpec((tm, w), lambda i: (i, 0)) for w, _ in rows_out]
    out_specs += [pl.BlockSpec((8, w), lambda i: (0, 0)) for w in accs_out]
    return _pc(kern, name=name, out_shape=out_shape, grid=(T // tm,), in_specs=in_specs,
               out_specs=out_specs, compiler_params=_params("arbitrary"))(*rows_in, *vecs_in, *deps)


def _rms(x):
    return lax.rsqrt(jnp.mean(x * x, axis=-1, keepdims=True) + EPS)


def prenorm_fwd(name, x, gain, shift, deps=()):
    def body(r, v, o, a):
        xv = r[0][...]
        o[0][...] = (xv * _rms(xv) * v[0][...] + v[1][...]).astype(BF16)
    return _rows(name, body, [x], [gain, shift], [(x.shape[1], BF16)], [], deps=deps)[0]


def postnorm_bwd(name, dxo, y, post_g, gate):
    def body(r, v, o, a):
        d, yv = r[0][...], r[1][...].astype(F32)
        pg, gt = v[0][...], v[1][...]
        ry = _rms(yv)
        yn = yv * ry
        t = d * yn
        dyn = d * (gt * pg)
        dy = ry * (dyn - yn * jnp.mean(dyn * yn, axis=-1, keepdims=True))
        o[0][...] = dy.astype(BF16)
        a[0][...] += _fold8(t)
        a[1][...] += _fold8(dy)
    D = y.shape[1]
    return _rows(name, body, [dxo, y], [post_g, gate], [(D, BF16)], [D, D])


def prenorm_bwd(name, dh, x, dxo, pre_g, scale, deps=()):
    def body(r, v, o, a):
        dhv, xv, d = r[0][...].astype(F32), r[1][...], r[2][...]
        pg, sc1 = v[0][...], 1.0 + v[1][...]
        rx = _rms(xv)
        xn = xv * rx
        t = dhv * xn
        dxn = dhv * (pg * sc1)
        o[0][...] = d + rx * (dxn - xn * jnp.mean(dxn * xn, axis=-1, keepdims=True))
        a[0][...] += _fold8(dhv)
        a[1][...] += _fold8(t)
    D = x.shape[1]
    return _rows(name, body, [dh, x, dxo], [pre_g, scale], [(D, F32)], [D, D], deps=deps)


def norm_chain_bwd(name, dh, x, dxo, pre_g, scale, y, post_g, gate, deps=()):
    def body(r, v, o, a):
        dhv, xv, d, yv = r[0][...].astype(F32), r[1][...], r[2][...], r[3][...].astype(F32)
        pg, sc1, pg2, gt = v[0][...], 1.0 + v[1][...], v[2][...], v[3][...]
        rx = _rms(xv)
        xn = xv * rx
        t = dhv * xn
        dxn = dhv * (pg * sc1)
        dx = d + rx * (dxn - xn * jnp.mean(dxn * xn, axis=-1, keepdims=True))
        o[0][...] = dx
        a[0][...] += _fold8(dhv)
        a[1][...] += _fold8(t)
        ry = _rms(yv)
        yn = yv * ry
        t2 = dx * yn
        dyn = dx * (gt * pg2)
        dy = ry * (dyn - yn * jnp.mean(dyn * yn, axis=-1, keepdims=True))
        o[1][...] = dy.astype(BF16)
        a[2][...] += _fold8(t2)
        a[3][...] += _fold8(dy)
    D = x.shape[1]
    return _rows(name, body, [dh, x, dxo, y], [pre_g, scale, post_g, gate], [(D, F32), (D, BF16)], [D] * 4,
                 deps=deps)


def loss_bwd(name, y, target):
    D = y.shape[1]

    def body(r, v, o, a):
        e = r[0][...] - r[1][...]
        o[0][...] = e * (1.0 / D)
        a[0][...] += _fold8(e * e * (0.5 / D))
    return _rows(name, body, [y, target], [], [(D, F32)], [D])


def ln_silu_fwd(name, v_, g, b):
    def body(r, v, o, a):
        x = r[0][...]
        xc = x - jnp.mean(x, axis=-1, keepdims=True)
        ln = xc * lax.rsqrt(jnp.mean(xc * xc, axis=-1, keepdims=True) + EPS) * v[0][...] + v[1][...]
        o[0][...] = (ln * _sigmoid(ln)).astype(BF16)
    return _rows(name, body, [v_], [g, b], [(v_.shape[1], BF16)], [])[0]


def ln_silu_bwd(name, ds, v_, g, b):
    def body(r, v, o, a):
        dsv, x = r[0][...], r[1][...]
        xc = x - jnp.mean(x, axis=-1, keepdims=True)
        rstd = lax.rsqrt(jnp.mean(xc * xc, axis=-1, keepdims=True) + EPS)
        xh = xc * rstd
        ln = xh * v[0][...] + v[1][...]
        sg = _sigmoid(ln)
        dln = dsv * (sg * (1.0 + ln * (1.0 - sg)))
        dxh = dln * v[0][...]
        o[0][...] = rstd * (dxh - jnp.mean(dxh, axis=-1, keepdims=True)
                            - xh * jnp.mean(dxh * xh, axis=-1, keepdims=True))
        a[0][...] += _fold8(dln * xh)
        a[1][...] += _fold8(dln)
    D = v_.shape[1]
    return _rows(name, body, [ds, v_], [g, b], [(D, F32)], [D, D])


def _pairs(name, body, u, others, out_w, out_dtype, tn, n_acc=0, tm=256):
    T, F2 = u.shape
    F = F2 // 2
    tm = _pick(T, (tm, 128, 64, 32, 16, 8))

    def kern(*refs):
        u_ref, o_refs = refs[0], refs[1:1 + len(others)]
        out_ref, acc_refs = refs[1 + len(others)], refs[2 + len(others):]

        @pl.when(pl.program_id(1) == 0)
        def _():
            for a_ref in acc_refs:
                a_ref[...] = jnp.zeros(a_ref.shape, F32)

        body(u_ref, o_refs, out_ref, acc_refs)

    out_shape = [jax.ShapeDtypeStruct((T, out_w * F), out_dtype)]
    out_shape += [jax.ShapeDtypeStruct((8, F2), F32)] * n_acc
    out_specs = [pl.BlockSpec((tm, out_w * tn), lambda g, i: (i, g))]
    out_specs += [pl.BlockSpec((8, 2 * tn), lambda g, i: (0, g))] * n_acc
    return _pc(kern, name=name, out_shape=out_shape, grid=(F // tn, T // tm),
               in_specs=[pl.BlockSpec((tm, 2 * tn), lambda g, i: (i, g))]
               + [pl.BlockSpec((tm, tn), lambda g, i: (i, g))] * len(others),
               out_specs=out_specs, compiler_params=_params("parallel", "arbitrary"))(u, *others)


def glu_bwd(name, u, dglu, tn):
    def body(u_ref, o, out, acc):
        a, g = u_ref[:, :tn].astype(F32), u_ref[:, tn:].astype(F32)
        d = o[0][...]
        sg = _sigmoid(g)
        da, dg = d * sg, d * a * (sg * (1.0 - sg))
        out[:, :tn] = da.astype(BF16)
        out[:, tn:] = dg.astype(BF16)
        acc[0][:, :tn] += _fold8(da)
        acc[0][:, tn:] += _fold8(dg)
    return _pairs(name, body, u, [dglu], 2, BF16, tn, n_acc=1)


_CONV_ROWS, _CONV_LANES = 64, 128
_SUBLANES = 8


def _fill_shifts(ext, sh, n):
    for r in range(1, _SUBLANES):
        sh[r - 1, pl.ds(0, n), :] = ext[pl.ds(r, n), :]


def _tap(ext, sh, o, r0, rows, ln):
    q, r = divmod(o, _SUBLANES)
    if r == 0:
        return ext[pl.ds(r0 + o, rows), ln]
    return sh[r - 1, pl.ds(r0 + _SUBLANES * q, rows), ln]


def dwconv_fwd(name, x, w, b):
    T, D = x.shape
    width = w.shape[0]
    tt, cb = _pick(T, (256, 128, 64)), _pick(D, (256, 128))
    off = CONV_HALO - (width - 1)
    rows = min(_CONV_ROWS, tt)
    n_sh = tt + CONV_HALO - _SUBLANES

    def body(cur_ref, prev_ref, w_ref, b_ref, o_ref, ext, sh):
        t = pl.program_id(1)
        tail = prev_ref[pl.ds(tt - CONV_HALO, CONV_HALO), :]
        ext[pl.ds(0, CONV_HALO), :] = jnp.where(t > 0, tail, 0.0)
        ext[pl.ds(CONV_HALO, tt), :] = cur_ref[...]
        _fill_shifts(ext, sh, n_sh)
        for l0 in range(0, cb, _CONV_LANES):
            ln = pl.ds(l0, _CONV_LANES)
            for r0 in range(0, tt, rows):
                acc = jnp.broadcast_to(b_ref[:, ln], (rows, _CONV_LANES))
                for k in range(width):
                    acc = acc + _tap(ext, sh, off + k, r0, rows, ln) * w_ref[pl.ds(k, 1), ln]
                o_ref[pl.ds(r0, rows), ln] = acc

    return _pc(
        body, name=name, out_shape=jax.ShapeDtypeStruct((T, D), F32), grid=(D // cb, T // tt),
        in_specs=[pl.BlockSpec((tt, cb), lambda c, t: (t, c)),
                  pl.BlockSpec((tt, cb), lambda c, t: (jnp.maximum(t - 1, 0), c)),
                  pl.BlockSpec((width, cb), lambda c, t: (0, c)),
                  pl.BlockSpec((1, cb), lambda c, t: (0, c))],
        out_specs=pl.BlockSpec((tt, cb), lambda c, t: (t, c)),
        scratch_shapes=[pltpu.VMEM((tt + CONV_HALO, cb), F32), pltpu.VMEM((_SUBLANES - 1, tt + CONV_HALO, cb), F32)],
        compiler_params=_params("parallel", "arbitrary"),
    )(x, x, w, b)


def dwconv_bwd(name, dv, x, w):
    T, D = x.shape
    width = w.shape[0]
    tt, cb = _pick(T, (256, 128, 64)), _pick(D, (256, 128))
    off = CONV_HALO - (width - 1)
    rows = min(_CONV_ROWS, tt)
    nt = T // tt
    n_sh = tt + CONV_HALO - _SUBLANES

    def body(dv_ref, dvn_ref, x_ref, xp_ref, w_ref, dx_ref, dw_ref, db_ref, ext_d, ext_x, sh_d, sh_x):
        t = pl.program_id(1)

        @pl.when(t == 0)
        def _():
            dw_ref[...] = jnp.zeros(dw_ref.shape, F32)
            db_ref[...] = jnp.zeros(db_ref.shape, F32)

        ext_d[pl.ds(0, tt), :] = dv_ref[...]
        ext_d[pl.ds(tt, CONV_HALO), :] = jnp.where(t < nt - 1, dvn_ref[pl.ds(0, CONV_HALO), :], 0.0)
        ext_x[pl.ds(0, CONV_HALO), :] = jnp.where(t > 0, xp_ref[pl.ds(tt - CONV_HALO, CONV_HALO), :], 0.0)
        ext_x[pl.ds(CONV_HALO, tt), :] = x_ref[...]
        db_ref[...] += _fold8(dv_ref[...])
        _fill_shifts(ext_d, sh_d, n_sh)
        _fill_shifts(ext_x, sh_x, n_sh)
        for l0 in range(0, cb, _CONV_LANES):
            ln = pl.ds(l0, _CONV_LANES)
            for r0 in range(0, tt, rows):
                acc = jnp.zeros((rows, _CONV_LANES), F32)
                for k in range(width):
                    acc = acc + _tap(ext_d, sh_d, (width - 1) - k, r0, rows, ln) * w_ref[pl.ds(k, 1), ln]
                dx_ref[pl.ds(r0, rows), ln] = acc
            sums = [jnp.zeros((8, _CONV_LANES), F32)] * width
            for r0 in range(0, tt, rows):
                dvb = ext_d[pl.ds(r0, rows), ln]
                sums = [sums[k] + _fold8(dvb * _tap(ext_x, sh_x, off + k, r0, rows, ln)) for k in range(width)]
            for k in range(width):
                dw_ref[pl.ds(8 * k, 8), ln] += sums[k]

    return _pc(
        body, name=name,
        out_shape=[jax.ShapeDtypeStruct((T, D), F32), jax.ShapeDtypeStruct((8 * width, D), F32),
                   jax.ShapeDtypeStruct((8, D), F32)],
        grid=(D // cb, nt),
        in_specs=[pl.BlockSpec((tt, cb), lambda c, t: (t, c)),
                  pl.BlockSpec((tt, cb), lambda c, t: (jnp.minimum(t + 1, nt - 1), c)),
                  pl.BlockSpec((tt, cb), lambda c, t: (t, c)),
                  pl.BlockSpec((tt, cb), lambda c, t: (jnp.maximum(t - 1, 0), c)),
                  pl.BlockSpec((width, cb), lambda c, t: (0, c))],
        out_specs=[pl.BlockSpec((tt, cb), lambda c, t: (t, c)),
                   pl.BlockSpec((8 * width, cb), lambda c, t: (0, c)),
                   pl.BlockSpec((8, cb), lambda c, t: (0, c))],
        scratch_shapes=[pltpu.VMEM((tt + CONV_HALO, cb), F32), pltpu.VMEM((tt + CONV_HALO, cb), F32),
                        pltpu.VMEM((_SUBLANES - 1, tt + CONV_HALO, cb), F32),
                        pltpu.VMEM((_SUBLANES - 1, tt + CONV_HALO, cb), F32)],
        compiler_params=_params("parallel", "arbitrary"),
    )(dv, dv, x, x, w)


def _gla_dims(proj, wgu):
    DK = wgu.shape[1]
    DV = (proj.shape[1] - 128 - 2 * DK) // 2
    return DK, DV, DK // GLA_HEADS, DV // GLA_HEADS


def _gla_decay(a_ref, w_ref, bg_ref):
    C = GLA_CHUNK
    z = jnp.dot(a_ref[...].astype(BF16), w_ref[...].astype(BF16), preferred_element_type=F32) + bg_ref[...]
    g = (jnp.minimum(z, 0.0) - jnp.log(1.0 + jnp.exp(-jnp.abs(z)))) * (1.0 / GLA_TAU)
    row = lax.broadcasted_iota(jnp.int32, (C, C), 0)
    col = lax.broadcasted_iota(jnp.int32, (C, C), 1)
    bc = jnp.dot((row >= col).astype(F32), g, precision=lax.Precision.HIGHEST, preferred_element_type=F32)
    last = lax.broadcasted_iota(jnp.int32, bc.shape, 0) == C - 1
    b_last = jnp.sum(jnp.where(last, bc, 0.0), axis=0, keepdims=True)
    return z, bc, b_last


def _gla_in_specs(wgu, DK, DV, cidx):
    C = GLA_CHUNK
    return [pl.BlockSpec((C, DK), lambda c: (cidx(c), 0)),
            pl.BlockSpec((C, DK), lambda c: (cidx(c), 1)),
            pl.BlockSpec((C, DV), lambda c: (cidx(c), (2 * DK) // DV)),
            pl.BlockSpec((C, 128), lambda c: (cidx(c), (2 * DK + 2 * DV) // 128)),
            pl.BlockSpec(wgu.shape, lambda c: (0, 0)),
            pl.BlockSpec((1, DK), lambda c: (0, 0))]


def _dot_nt(a, b):
    return lax.dot_general(a.astype(BF16), b.astype(BF16), (((1,), (1,)), ((), ())), preferred_element_type=F32)


def _dot_tn(a, b):
    return lax.dot_general(a.astype(BF16), b.astype(BF16), (((0,), (0,)), ((), ())), preferred_element_type=F32)


def _dot(a, b):
    return jnp.dot(a.astype(BF16), b.astype(BF16), preferred_element_type=F32)


def gla_fwd(name, proj, wgu, bg):
    T = proj.shape[0]
    C, H = GLA_CHUNK, GLA_HEADS
    DK, DV, dk, dv = _gla_dims(proj, wgu)
    nc = T // C
    scale = dk ** -0.5

    def body(q_ref, k_ref, v_ref, a_ref, w_ref, bg_ref, o_ref, s_ref, st):
        @pl.when(pl.program_id(0) == 0)
        def _():
            st[...] = jnp.zeros(st.shape, F32)

        _, bc, b_last = _gla_decay(a_ref, w_ref, bg_ref)
        k = k_ref[...]
        qe = (q_ref[...] * scale * jnp.exp(bc)).astype(BF16)
        ke = (k * jnp.exp(-bc)).astype(BF16)
        kd = (k * jnp.exp(b_last - bc)).astype(BF16)
        el = jnp.exp(b_last)
        row = lax.broadcasted_iota(jnp.int32, (C, C), 0)
        col = lax.broadcasted_iota(jnp.int32, (C, C), 1)
        for h in range(H):
            kk, vv = slice(h * dk, (h + 1) * dk), slice(h * dv, (h + 1) * dv)
            sp = st[h]
            s_ref[h] = sp.astype(BF16)
            v = v_ref[:, vv]
            att = jnp.where(row >= col, _dot_nt(qe[:, kk], ke[:, kk]), 0.0)
            o_ref[:, vv] = _dot_nt(qe[:, kk], sp) + _dot(att, v)
            st[h] = sp * el[:, kk] + _dot_tn(v, kd[:, kk])

    return _pc(
        body, name=name,
        out_shape=[jax.ShapeDtypeStruct((T, DV), F32), jax.ShapeDtypeStruct((H, nc, dv, dk), BF16)],
        grid=(nc,), in_specs=_gla_in_specs(wgu, DK, DV, lambda c: c),
        out_specs=[pl.BlockSpec((C, DV), lambda c: (c, 0)),
                   pl.BlockSpec((H, None, dv, dk), lambda c: (0, c, 0, 0))],
        scratch_shapes=[pltpu.VMEM((H, dv, dk), F32)],
        compiler_params=_params("arbitrary"),
    )(proj, proj, proj, proj, wgu, bg)


def gla_bwd(name, proj, wgu, bg, states, do, d_r):
    T = proj.shape[0]
    C, H = GLA_CHUNK, GLA_HEADS
    DK, DV, dk, dv = _gla_dims(proj, wgu)
    nc = T // C
    scale = dk ** -0.5
    rev = lambda c: nc - 1 - c

    def body(q_ref, k_ref, v_ref, a_ref, w_ref, bg_ref, s_ref, do_ref, dr_ref,
             dp_ref, dz_ref, dbg_ref, dst, db_scr):
        @pl.when(pl.program_id(0) == 0)
        def _():
            dst[...] = jnp.zeros(dst.shape, F32)
            dbg_ref[...] = jnp.zeros(dbg_ref.shape, F32)

        z, bc, b_last = _gla_decay(a_ref, w_ref, bg_ref)
        k = k_ref[...]
        eb, enb, ed, el = jnp.exp(bc), jnp.exp(-bc), jnp.exp(b_last - bc), jnp.exp(b_last)
        qe, ke, kd = q_ref[...] * scale * eb, k * enb, k * ed
        row = lax.broadcasted_iota(jnp.int32, (C, C), 0)
        col = lax.broadcasted_iota(jnp.int32, (C, C), 1)
        keep = row >= col
        last = lax.broadcasted_iota(jnp.int32, (C, dk), 0) == C - 1
        for h in range(H):
            kk, vv = slice(h * dk, (h + 1) * dk), slice(h * dv, (h + 1) * dv)
            qe_h, ke_h, kd_h, el_h = qe[:, kk], ke[:, kk], kd[:, kk], el[:, kk]
            v, d_o = v_ref[:, vv], do_ref[:, vv]
            sp = s_ref[h].astype(F32)
            ds_ = dst[h]
            att = jnp.where(keep, _dot_nt(qe_h, ke_h), 0.0)
            datt = jnp.where(keep, _dot_nt(d_o, v), 0.0)
            dqe = _dot(d_o, sp) + _dot(datt, ke_h)
            dke = _dot_tn(datt, qe_h)
            dp_ref[:, 2 * DK + h * dv:2 * DK + (h + 1) * dv] = (_dot_tn(att, d_o) + _dot_nt(kd_h, ds_)).astype(BF16)
            dkd = _dot(v, ds_)
            dst[h] = ds_ * el_h + _dot_tn(d_o, qe_h)
            dp_ref[:, kk] = (dqe * scale * eb[:, kk]).astype(BF16)
            dp_ref[:, DK + h * dk:DK + (h + 1) * dk] = (dke * enb[:, kk] + dkd * ed[:, kk]).astype(BF16)
            d_el = jnp.sum(sp * ds_, axis=0, keepdims=True)
            db_last = jnp.sum(dkd * kd_h, axis=0, keepdims=True) + d_el * el_h
            db_scr[:, kk] = dqe * qe_h - dke * ke_h - dkd * kd_h + jnp.where(last, db_last, 0.0)
        dg = jnp.dot((row <= col).astype(F32), db_scr[...], precision=lax.Precision.HIGHEST,
                     preferred_element_type=F32)
        dz = dg * (1.0 / GLA_TAU) * _sigmoid(-z)
        dz_ref[...] = dz.astype(BF16)
        dbg_ref[...] += _fold8(dz)
        dp_ref[:, 2 * DK + DV:2 * DK + 2 * DV] = dr_ref[...]
        dp_ref[:, 2 * DK + 2 * DV:] = _dot_nt(dz, w_ref[...]).astype(BF16)

    return _pc(
        body, name=name,
        out_shape=[jax.ShapeDtypeStruct((T, proj.shape[1]), BF16), jax.ShapeDtypeStruct((T, DK), BF16),
                   jax.ShapeDtypeStruct((8, DK), F32)],
        grid=(nc,),
        in_specs=_gla_in_specs(wgu, DK, DV, rev)
        + [pl.BlockSpec((H, None, dv, dk), lambda c: (0, rev(c), 0, 0)),
           pl.BlockSpec((C, DV), lambda c: (rev(c), 0)),
           pl.BlockSpec((C, DV), lambda c: (rev(c), 0))],
        out_specs=[pl.BlockSpec((C, proj.shape[1]), lambda c: (rev(c), 0)),
                   pl.BlockSpec((C, DK), lambda c: (rev(c), 0)),
                   pl.BlockSpec((8, DK), lambda c: (0, 0))],
        scratch_shapes=[pltpu.VMEM((H, dv, dk), F32), pltpu.VMEM((C, DK), F32)],
        compiler_params=_params("arbitrary"),
    )(proj, proj, proj, proj, wgu, bg, states, do, d_r)


def gla_out_fwd(name, o, proj, norm_g, r_block0):
    T, DV = o.shape
    dv = DV // GLA_HEADS
    tm = _pick(T, (512, 256, 128, 64))

    def body(o_ref, r_ref, g_ref, out_ref):
        ov, rv = o_ref[...], r_ref[...]
        out_ref[...] = (ov * _rms(ov) * g_ref[...] * (rv * _sigmoid(rv))).astype(BF16)

    return _pc(
        body, name=name, out_shape=jax.ShapeDtypeStruct((T, DV), BF16), grid=(GLA_HEADS, T // tm),
        in_specs=[pl.BlockSpec((tm, dv), lambda h, i: (i, h)),
                  pl.BlockSpec((tm, dv), lambda h, i: (i, r_block0 + h)),
                  pl.BlockSpec((1, dv), lambda h, i: (0, h))],
        out_specs=pl.BlockSpec((tm, dv), lambda h, i: (i, h)),
        compiler_params=_params("parallel", "parallel"),
    )(o, proj, norm_g)


def gla_out_bwd(name, dog, o, proj, norm_g, r_block0):
    T, DV = o.shape
    dv = DV // GLA_HEADS
    tm = _pick(T, (512, 256, 128, 64))

    def body(d_ref, o_ref, r_ref, g_ref, do_ref, dr_ref, dg_ref):
        @pl.when(pl.program_id(1) == 0)
        def _():
            dg_ref[...] = jnp.zeros(dg_ref.shape, F32)

        d, ov, rv, g = d_ref[...], o_ref[...], r_ref[...], g_ref[...]
        ro = _rms(ov)
        oh = ov * ro
        sg = _sigmoid(rv)
        dn = d * (rv * sg)
        dr_ref[...] = (d * (oh * g) * (sg * (1.0 + rv * (1.0 - sg)))).astype(BF16)
        doh = dn * g
        do_ref[...] = ro * (doh - oh * jnp.mean(doh * oh, axis=-1, keepdims=True))
        dg_ref[...] += _fold8(dn * oh)

    return _pc(
        body, name=name,
        out_shape=[jax.ShapeDtypeStruct((T, DV), F32), jax.ShapeDtypeStruct((T, DV), BF16),
                   jax.ShapeDtypeStruct((8, DV), F32)],
        grid=(GLA_HEADS, T // tm),
        in_specs=[pl.BlockSpec((tm, dv), lambda h, i: (i, h)),
                  pl.BlockSpec((tm, dv), lambda h, i: (i, h)),
                  pl.BlockSpec((tm, dv), lambda h, i: (i, r_block0 + h)),
                  pl.BlockSpec((1, dv), lambda h, i: (0, h))],
        out_specs=[pl.BlockSpec((tm, dv), lambda h, i: (i, h)),
                   pl.BlockSpec((tm, dv), lambda h, i: (i, h)),
                   pl.BlockSpec((8, dv), lambda h, i: (0, h))],
        compiler_params=_params("parallel", "arbitrary"),
    )(dog, o, proj, norm_g)


def _ew_rows(r, c):
    return _pick(r, tuple(t for t in (512, 256, 128, 64, 32, 16, 8) if t * c <= 256 * 1024) or (8,))


def sum_slots_layers(name, lands):
    nl = len(lands)
    n, r, c = lands[0].shape
    tr = _ew_rows(r, c)
    nb = r // tr

    def body(*refs):
        o_ref = refs[nl]
        for l in range(nl):
            @pl.when(pl.program_id(0) == l)
            def _(p_ref=refs[l]):
                acc = p_ref[0].astype(F32)
                for s in range(1, n):
                    acc = acc + p_ref[s].astype(F32)
                o_ref[...] = acc.astype(o_ref.dtype)

    def in_map(l):
        return lambda q, i: (0, jnp.where(q == l, i, 0), 0)

    return _pc(body, name=name, out_shape=jax.ShapeDtypeStruct((nl * r, c), BF16), grid=(nl, nb),
               in_specs=[pl.BlockSpec((n, tr, c), in_map(l)) for l in range(nl)],
               out_specs=pl.BlockSpec((tr, c), lambda q, i: (q * nb + i, 0)),
               compiler_params=_params("arbitrary", "arbitrary"))(*lands)


def sum_slots(name, p):
    n, r, c = p.shape
    tr = _ew_rows(r, c)

    def body(p_ref, o_ref):
        acc = p_ref[0].astype(F32)
        for s in range(1, n):
            acc = acc + p_ref[s].astype(F32)
        o_ref[...] = acc

    return _pc(body, name=name, out_shape=jax.ShapeDtypeStruct((r, c), F32), grid=(r // tr,),
               in_specs=[pl.BlockSpec((n, tr, c), lambda i: (0, i, 0))],
               out_specs=pl.BlockSpec((tr, c), lambda i: (i, 0)),
               compiler_params=_params("parallel"))(p)


def adamw(name, w, gs, m, v, deps=()):
    r, c = w.shape
    tr = _ew_rows(r, c)
    n_g = len(gs)
    m_corr = 1.0 / (1.0 - ADAM_B1 ** ADAM_STEP)
    v_corr = 1.0 / (1.0 - ADAM_B2 ** ADAM_STEP)

    def body(*refs):
        w_ref, g_refs, m_ref, v_ref = refs[0], refs[1:1 + n_g], refs[1 + n_g], refs[2 + n_g]
        g_out, d_out, m_out, v_out = refs[-4:]
        g = g_refs[0][...].astype(F32)
        if n_g == 2:
            g = g + g_refs[1][...].astype(F32)
        mn = ADAM_B1 * m_ref[...] + (1.0 - ADAM_B1) * g
        vn = ADAM_B2 * v_ref[...] + (1.0 - ADAM_B2) * (g * g)
        g_out[...] = g
        m_out[...] = mn
        v_out[...] = vn
        d_out[...] = -ADAM_LR * ((mn * m_corr) / (jnp.sqrt(vn * v_corr) + ADAM_EPS) + ADAM_WD * w_ref[...])

    spec = pl.BlockSpec((tr, c), lambda i: (i, 0))
    return _pc(body, name=name, out_shape=[jax.ShapeDtypeStruct((r, c), F32)] * 4, grid=(r // tr,),
               in_specs=[spec] * (3 + n_g) + [HBM] * len(deps), out_specs=[spec] * 4,
               compiler_params=_params("parallel"))(w, *gs, m, v, *deps)


def _flat_pack(arrs):
    flat = jnp.concatenate([a.reshape(-1).astype(F32) for a in arrs])
    n = flat.shape[0]
    pad = (-n) % 1024
    return jnp.pad(flat, (0, pad)).reshape(-1, 128)


def _flat_unpack(packed, shapes, lead=()):
    flat = packed.reshape(lead + (-1,))
    out, pos = [], 0
    for s in shapes:
        n = 1
        for d in s:
            n *= d
        out.append(flat[..., pos:pos + n].reshape(lead + tuple(s)))
        pos += n
    return out


def _interleave_vec(b, tn):
    f = b.shape[-1] // 2
    return b.reshape(2, f // tn, tn).transpose(1, 0, 2).reshape(1, 2 * f)


def _deinterleave_vec(b, tn):
    f = b.shape[-1] // 2
    return b.reshape(f // tn, 2, tn).transpose(1, 0, 2).reshape(2 * f)


def kernel(x, c, w_ada, b_ada, pre_mix_g, post_mix_g, pre_ffn_g, post_ffn_g, conv_w_pw1, conv_b_pw1, conv_w_dw, conv_b_dw, conv_ln_g, conv_ln_b, conv_w_pw2, conv_b_pw2, gla_w_in, gla_w_gate_up, gla_b_gate, gla_norm_g, gla_w_out, ffn_w_in, ffn_w_out, loss_target, m_w_ada, m_b_ada, m_pre_mix_g, m_post_mix_g, m_pre_ffn_g, m_post_ffn_g, m_conv_w_pw1, m_conv_b_pw1, m_conv_w_dw, m_conv_b_dw, m_conv_ln_g, m_conv_ln_b, m_conv_w_pw2, m_conv_b_pw2, m_gla_w_in, m_gla_w_gate_up, m_gla_b_gate, m_gla_norm_g, m_gla_w_out, m_ffn_w_in, m_ffn_w_out, v_w_ada, v_b_ada, v_pre_mix_g, v_post_mix_g, v_pre_ffn_g, v_post_ffn_g, v_conv_w_pw1, v_conv_b_pw1, v_conv_w_dw, v_conv_b_dw, v_conv_ln_g, v_conv_ln_b, v_conv_w_pw2, v_conv_b_pw2, v_gla_w_in, v_gla_w_gate_up, v_gla_b_gate, v_gla_norm_g, v_gla_w_out, v_ffn_w_in, v_ffn_w_out):
    weights = dict(w_ada=w_ada, b_ada=b_ada, pre_mix_g=pre_mix_g, post_mix_g=post_mix_g, pre_ffn_g=pre_ffn_g, post_ffn_g=post_ffn_g, conv_w_pw1=conv_w_pw1, conv_b_pw1=conv_b_pw1, conv_w_dw=conv_w_dw, conv_b_dw=conv_b_dw, conv_ln_g=conv_ln_g, conv_ln_b=conv_ln_b, conv_w_pw2=conv_w_pw2, conv_b_pw2=conv_b_pw2, gla_w_in=gla_w_in, gla_w_gate_up=gla_w_gate_up, gla_b_gate=gla_b_gate, gla_norm_g=gla_norm_g, gla_w_out=gla_w_out, ffn_w_in=ffn_w_in, ffn_w_out=ffn_w_out)
    mom_m = dict(w_ada=m_w_ada, b_ada=m_b_ada, pre_mix_g=m_pre_mix_g, post_mix_g=m_post_mix_g, pre_ffn_g=m_pre_ffn_g, post_ffn_g=m_post_ffn_g, conv_w_pw1=m_conv_w_pw1, conv_b_pw1=m_conv_b_pw1, conv_w_dw=m_conv_w_dw, conv_b_dw=m_conv_b_dw, conv_ln_g=m_conv_ln_g, conv_ln_b=m_conv_ln_b, conv_w_pw2=m_conv_w_pw2, conv_b_pw2=m_conv_b_pw2, gla_w_in=m_gla_w_in, gla_w_gate_up=m_gla_w_gate_up, gla_b_gate=m_gla_b_gate, gla_norm_g=m_gla_norm_g, gla_w_out=m_gla_w_out, ffn_w_in=m_ffn_w_in, ffn_w_out=m_ffn_w_out)
    mom_v = dict(w_ada=v_w_ada, b_ada=v_b_ada, pre_mix_g=v_pre_mix_g, post_mix_g=v_post_mix_g, pre_ffn_g=v_pre_ffn_g, post_ffn_g=v_post_ffn_g, conv_w_pw1=v_conv_w_pw1, conv_b_pw1=v_conv_b_pw1, conv_w_dw=v_conv_w_dw, conv_b_dw=v_conv_b_dw, conv_ln_g=v_conv_ln_g, conv_ln_b=v_conv_ln_b, conv_w_pw2=v_conv_w_pw2, conv_b_pw2=v_conv_b_pw2, gla_w_in=v_gla_w_in, gla_w_gate_up=v_gla_w_gate_up, gla_b_gate=v_gla_b_gate, gla_norm_g=v_gla_norm_g, gla_w_out=v_gla_w_out, ffn_w_in=v_ffn_w_in, ffn_w_out=v_ffn_w_out)
    order = list(weights)

    ax, ay, ac = lax.axis_index("x"), lax.axis_index("y"), lax.axis_index("c")
    my_chip, my_dev = 2 * ax + ay, 4 * ax + 2 * ay + ac

    x = x[0]
    target = loss_target[0]
    T, D = x.shape
    depth = w_ada.shape[0]
    n_conv, n_gla = conv_w_pw1.shape[0], gla_w_in.shape[0]
    width = conv_w_dw.shape[1]
    F = ffn_w_out.shape[1] * N_CHIPS
    DK = gla_w_gate_up.shape[2] * N_CHIPS
    rank = gla_w_gate_up.shape[1]
    gla_cols = 2 * DK + 2 * D + rank
    P = 2 * DK + 2 * D + 128
    dvh = D // GLA_HEADS
    tn_pw1 = _pick(conv_w_pw1.shape[2], (1024, 512, 256, 128))
    tn_ffn = ffn_w_out.shape[1]

    small_sharded = ["conv_w_dw", "gla_w_gate_up", "gla_b_gate", "gla_norm_g"]
    packed = _flat_pack([weights[n] for n in small_sharded])
    got = allgather_devices("gather_small", packed)[0::2]
    parts = _flat_unpack(got, [weights[n].shape for n in small_sharded], lead=(N_CHIPS,))
    w_dw_full, wgu_full, bgate_full, normg_full = [
        jnp.concatenate([p[s] for s in range(N_CHIPS)], axis=-1) for p in parts]
    w_dw_full = w_dw_full
    wgu_pad = jnp.pad(wgu_full, ((0, 0), (0, 128 - rank), (0, 0)))

    c_act = c * _sigmoid(c)
    c_all = allgather_devices("gather_c", jnp.pad(c_act, ((0, 7), (0, 0))))[:, 0, :]
    c16 = jnp.pad(c_all, ((0, 8), (0, 0))).astype(BF16)
    ada_cols = w_ada.shape[2]
    w_ada2 = w_ada.reshape(depth * D, ada_cols)
    mod_cols = [mm_nn(f"ada_fwd_{i}", c16, Wt(w_ada2, "plain", i, D, ada_cols)) for i in range(depth)]
    mod_cols = jnp.stack(mod_cols).reshape(depth * 16, ada_cols)
    mod_all = allgather_devices("gather_mod", mod_cols)[0::2]
    mod_all = mod_all.reshape(N_CHIPS, depth, 16, ada_cols).transpose(1, 2, 0, 3).reshape(depth, 16, 6 * D)
    mod = lax.dynamic_index_in_dim(mod_all, my_dev, axis=1, keepdims=False) + b_ada
    mod = mod.reshape(depth, 6, 1, D)

    gin_cols = gla_w_in.shape[2]

    def sublayer_weights(g):
        i = g // 2
        if g % 2:
            return [("ffn_w_in", i), ("ffn_w_out", i)]
        return [("conv_w_pw1", i // 2), ("conv_w_pw2", i // 2)] if i % 2 == 0 else [("gla_w_in", i // 2), ("gla_w_out", i // 2)]

    issued = []

    def start_gathers(g, after):
        handles, tokens = {}, []
        for n, l in sublayer_weights(g):
            shard = weights[n][l].astype(BF16)
            handles[n], token = chips_start(f"gather_start_{n}_{l}", shard, shard, my_chip, True,
                                            tuple(after) + tuple(issued[-1:]))
            issued.append(token)
            tokens.append(token)
        return handles, tokens

    def wait_gathers(g, handles, after):
        out = []
        for n, l in sublayer_weights(g):
            got = _split_wait(f"gather_wait_{n}_{l}", handles[n], after)
            r, cc = got.shape[1:]
            if n == "gla_w_in":
                full = got.transpose(1, 0, 2).reshape(D, N_CHIPS * gin_cols)
                out.append(Wt(jnp.pad(full, ((0, 0), (0, P - gla_cols))), "plain", 0, D, P))
            elif n in ("conv_w_pw1", "ffn_w_in"):
                out.append(Wt(got, "col", 0, r, cc, tn_pw1 if n == "conv_w_pw1" else tn_ffn))
            else:
                out.append(Wt(got, "row", 0, r, cc))
        return out

    n_sub = 2 * depth
    in_flight = {g: start_gathers(g, (mod,)) for g in range(min(3, n_sub))}

    def enter_sublayer(g, xs):
        if g + 2 >= n_sub:
            return ()
        if g + 2 not in in_flight:
            in_flight[g + 2] = start_gathers(g + 2, (xs,))
        return in_flight[g + 2][1]

    w_pw1, w_pw2, w_gin, w_gout = [None] * n_conv, [None] * n_conv, [None] * n_gla, [None] * n_gla
    w_fin, w_fout = [None] * depth, [None] * depth
    row = lambda a: a.reshape(1, -1)
    saved = []
    xs = x
    for i in range(depth):
        j = i // 2
        sh1, sc1, gt1, sh2, sc2, gt2 = [mod[i, q] for q in range(6)]
        s = dict(x_in=xs)
        deps = tuple(enter_sublayer(2 * i, xs))
        if i == 0:
            deps += tuple(in_flight[0][1] + in_flight[1][1])
            h = prenorm_fwd(f"pre_mix_{i}", xs, row(pre_mix_g[i]) * (1.0 + sc1), sh1)
        s["h_mix"] = h
        if i % 2 == 0:
            w_pw1[j], w_pw2[j] = wait_gathers(2 * i, in_flight[2 * i][0], (h,) + deps)
        else:
            w_gin[j], w_gout[j] = wait_gathers(2 * i, in_flight[2 * i][0], (h,) + deps)
        nxt = (row(pre_ffn_g[i]) * (1.0 + sc2), sh2)
        if i % 2 == 0:
            u, glu = gated_in_fwd(f"pw1_{i}", h, w_pw1[j], "glu", bias=_interleave_vec(row(conv_b_pw1[j]), tn_pw1))
            v = dwconv_fwd(f"dwconv_{i}", glu, w_dw_full[j], row(conv_b_dw[j]))
            sl = ln_silu_fwd(f"ln_silu_{i}", v, row(conv_ln_g[j]), row(conv_ln_b[j]))
            y, x_mid, h = mm_nn_post(f"pw2_{i}", sl, w_pw2[j], xs, gt1 * row(post_mix_g[i]), bias=row(conv_b_pw2[j]),
                                     nxt=nxt)
            s.update(u=u, glu=glu, v=v, s=sl)
        else:
            proj = mm_nn(f"gla_in_{i}", h, w_gin[j])
            o, states = gla_fwd(f"gla_{i}", proj, wgu_pad[j], row(bgate_full[j]))
            og = gla_out_fwd(f"gla_out_{i}", o, proj, row(normg_full[j]), (2 * DK + D) // dvh)
            y, x_mid, h = mm_nn_post(f"gla_wout_{i}", og, w_gout[j], xs, gt1 * row(post_mix_g[i]), nxt=nxt)
            s.update(proj=proj, o=o, states=states, og=og)
        s["y_mix"] = y
        xs = x_mid
        s["x_mid"] = xs
        deps = tuple(enter_sublayer(2 * i + 1, xs))
        w_fin[i], w_fout[i] = wait_gathers(2 * i + 1, in_flight[2 * i + 1][0], (h,) + deps)
        u, a = gated_in_fwd(f"ffn_in_{i}", h, w_fin[i], "swiglu")
        s.update(h_ffn=h, u_ffn=u, a_ffn=a)
        if i + 1 < depth:
            y, xs, h = mm_nn_post(f"ffn_out_{i}", a, w_fout[i], xs, gt2 * row(post_ffn_g[i]),
                                  nxt=(row(pre_mix_g[i + 1]) * (1.0 + mod[i + 1, 1]), mod[i + 1, 0]))
        else:
            y, xs = mm_nn_post(f"ffn_out_{i}", a, w_fout[i], xs, gt2 * row(post_ffn_g[i]))
        s["y_ffn"] = y
        saved.append(s)

    dx, loss_acc = loss_bwd("loss", xs, target)
    loss = lax.psum(jnp.sum(loss_acc), ("x", "y", "c"))

    fold = lambda a: jnp.sum(a, axis=0)
    small_g = {n: [None] * weights[n].shape[0] for n in order if n not in
               ("w_ada", "conv_w_pw1", "conv_w_pw2", "gla_w_in", "gla_w_out", "ffn_w_in", "ffn_w_out")}
    grad_w = lambda wt: Wt(None, wt.kind, 0, wt.R, wt.C, wt.tn)
    scattering = {}

    def start_scatter(n, l, p):
        own = lax.dynamic_index_in_dim(p, my_chip, axis=0, keepdims=False)
        scattering[n, l], token = chips_start(f"scatter_start_{n}_{l}", p, own, my_chip, False)
        return token

    for i in reversed(range(depth)):
        j = i // 2
        s = saved[i]
        sh1, sc1, gt1, sh2, sc2, gt2 = [mod[i, q] for q in range(6)]
        if i == depth - 1:
            dy, t_post, _ = postnorm_bwd(f"post_ffn_bwd_{i}", dx, s["y_ffn"], row(post_ffn_g[i]), gt2)
        else:
            dy, t_post = carried
        d_gt2 = fold(t_post) * post_ffn_g[i]
        small_g["post_ffn_g"][i] = fold(t_post) * gt2[0]
        du = ffn_out_dgrad(f"ffn_out_dgrad_{i}", dy, w_fout[i], s["u_ffn"])
        t_out = start_scatter("ffn_w_out", i, mm_tn(f"ffn_out_wgrad_{i}", s["a_ffn"], dy, grad_w(w_fout[i])))
        t_in = start_scatter("ffn_w_in", i, mm_tn(f"ffn_in_wgrad_{i}", s["h_ffn"], du, grad_w(w_fin[i])))
        dh = mm_nt(f"ffn_in_dgrad_{i}", du, w_fin[i], out_dtype=BF16, deps=(t_out, t_in))
        dx, dy, d_sh2, t_pre, t_post, dy_sum = norm_chain_bwd(
            f"pre_ffn_bwd_{i}", dh, s["x_mid"], dx, row(pre_ffn_g[i]), sc2,
            s["y_mix"], row(post_mix_g[i]), gt1, deps=(t_out, t_in))
        d_sh2, d_sc2 = fold(d_sh2), fold(t_pre) * pre_ffn_g[i]
        small_g["pre_ffn_g"][i] = fold(t_pre) * (1.0 + sc2[0])
        d_gt1 = fold(t_post) * post_mix_g[i]
        small_g["post_mix_g"][i] = fold(t_post) * gt1[0]
        if i % 2 == 0:
            small_g["conv_b_pw2"][j] = fold(dy_sum)
            dsl = mm_nt(f"pw2_dgrad_{i}", dy, w_pw2[j])
            t_out = start_scatter("conv_w_pw2", j, mm_tn(f"pw2_wgrad_{i}", s["s"], dy, grad_w(w_pw2[j])))
            dv, d_lg, d_lb = ln_silu_bwd(f"ln_silu_bwd_{i}", dsl, s["v"], row(conv_ln_g[j]), row(conv_ln_b[j]))
            small_g["conv_ln_g"][j], small_g["conv_ln_b"][j] = fold(d_lg), fold(d_lb)
            dglu, d_wdw, d_bdw = dwconv_bwd(f"dwconv_bwd_{i}", dv, s["glu"], w_dw_full[j])
            small_g["conv_w_dw"][j] = d_wdw.reshape(width, 8, D).sum(axis=1)
            small_g["conv_b_dw"][j] = fold(d_bdw)
            du, du_sum = glu_bwd(f"glu_bwd_{i}", s["u"], dglu, tn_pw1)
            small_g["conv_b_pw1"][j] = _deinterleave_vec(fold(du_sum), tn_pw1)
            t_in = start_scatter("conv_w_pw1", j, mm_tn(f"pw1_wgrad_{i}", s["h_mix"], du, grad_w(w_pw1[j])))
            dh = mm_nt(f"pw1_dgrad_{i}", du, w_pw1[j], out_dtype=BF16, deps=(t_out, t_in))
        else:
            dog = mm_nt(f"gla_wout_dgrad_{i}", dy, w_gout[j])
            t_out = start_scatter("gla_w_out", j, mm_tn(f"gla_wout_wgrad_{i}", s["og"], dy, grad_w(w_gout[j])))
            d_o, d_r, d_ng = gla_out_bwd(f"gla_out_bwd_{i}", dog, s["o"], s["proj"], row(normg_full[j]),
                                         (2 * DK + D) // dvh)
            small_g["gla_norm_g"][j] = fold(d_ng)
            dproj, dz, d_bg = gla_bwd(f"gla_bwd_{i}", s["proj"], wgu_pad[j], row(bgate_full[j]),
                                      s["states"], d_o, d_r)
            small_g["gla_b_gate"][j] = fold(d_bg)
            a_low = s["proj"][:, 2 * DK + 2 * D:].astype(BF16)
            d_wgu = mm_tn(f"gla_gate_wgrad_{i}", a_low, dz, Wt(None, "plain", 0, 128, DK), out_dtype=F32)
            small_g["gla_w_gate_up"][j] = d_wgu[:rank]
            g_in = mm_tn(f"gla_in_wgrad_{i}", s["h_mix"], dproj, Wt(None, "plain", 0, D, P))
            t_in = start_scatter("gla_w_in", j, g_in[:, :gla_cols].reshape(D, N_CHIPS, gin_cols).transpose(1, 0, 2))
            dh = mm_nt(f"gla_in_dgrad_{i}", dproj, w_gin[j], out_dtype=BF16, deps=(t_out, t_in))
        if i > 0:
            dx, dy_b, d_sh1, t_pre, t_post_b, _ = norm_chain_bwd(
                f"pre_mix_bwd_{i}", dh, s["x_in"], dx, row(pre_mix_g[i]), sc1,
                saved[i - 1]["y_ffn"], row(post_ffn_g[i - 1]), mod[i - 1, 5], deps=(t_out, t_in))
            carried = (dy_b, t_post_b)
        else:
            dx, d_sh1, t_pre = prenorm_bwd(f"pre_mix_bwd_{i}", dh, s["x_in"], dx, row(pre_mix_g[i]), sc1,
                                           deps=(t_out, t_in))
        d_sh1, d_sc1 = fold(d_sh1), fold(t_pre) * pre_mix_g[i]
        small_g["pre_mix_g"][i] = fold(t_pre) * (1.0 + sc1[0])
        small_g["b_ada"][i] = jnp.concatenate([d_sh1, d_sc1, d_gt1, d_sh2, d_sc2, d_gt2])
    grad_x = dx[None]

    small_names = list(small_g)
    local_small = [jnp.stack(small_g[n]) for n in small_names]
    full_shapes = [a.shape for a in local_small]
    small_handle, small_token = devices_start("small_grads_start", _flat_pack(local_small), my_dev)

    def flat2(a):
        return a.reshape(-1, a.shape[-1])

    swapping, last = [], (small_token,)
    for n in ("conv_w_pw1", "conv_w_pw2", "gla_w_in", "gla_w_out", "ffn_w_in", "ffn_w_out"):
        arrived = [_split_wait(f"scatter_wait_{n}_{l}", scattering[n, l], (dx,)) for l in range(weights[n].shape[0])]
        mine = sum_slots_layers("sum_" + n, arrived)
        handle, token = swap_start("swap_start_" + n, mine, last)
        swapping.append((n, handle))
        last = (token,)

    small_all = _split_wait("small_grads_wait", small_handle, last)
    small_sum = sum_slots("sum_small_grads", small_all)
    small_tot = dict(zip(small_names, _flat_unpack(small_sum, full_shapes)))
    for n in small_sharded:
        cols = weights[n].shape[-1]
        small_tot[n] = lax.dynamic_slice_in_dim(small_tot[n], my_chip * cols, cols, axis=small_tot[n].ndim - 1)

    dmod_all = _flat_unpack(small_all, full_shapes, lead=(N_DEVICES,))[small_names.index("b_ada")]
    dmod_cols = lax.dynamic_slice_in_dim(
        dmod_all.reshape(N_DEVICES, depth, N_CHIPS, ada_cols), my_chip, 1, axis=2)[:, :, 0, :]
    dmod16 = jnp.pad(dmod_cols.reshape(N_DEVICES, depth * ada_cols), ((0, 8), (0, 0))).astype(BF16)
    g_ada = [mm_tn(f"ada_wgrad_{i}", c16, dmod16[:, i * ada_cols:(i + 1) * ada_cols],
                   Wt(None, "plain", 0, D, ada_cols), out_dtype=F32) for i in range(depth)]
    g_ada = jnp.stack(g_ada).reshape(depth * D, ada_cols)

    results = {}
    out = adamw("adamw_w_ada", flat2(w_ada), [g_ada], flat2(m_w_ada), flat2(v_w_ada))
    results["w_ada"] = [t.reshape(w_ada.shape) for t in out]
    w_small = _flat_pack([weights[n] for n in small_names])
    out_small = adamw("adamw_small", w_small, [_flat_pack([small_tot[n] for n in small_names])],
                      _flat_pack([mom_m[n] for n in small_names]), _flat_pack([mom_v[n] for n in small_names]))
    for n, handle in swapping:
        w = weights[n]
        mine, theirs = _split_wait("swap_wait_" + n, handle, (out[0],), with_src=True)
        res = adamw("adamw_" + n, flat2(w), [mine, theirs], flat2(mom_m[n]), flat2(mom_v[n]))
        results[n] = [t.reshape(w.shape) for t in res]
    out = out_small
    shapes = [weights[n].shape for n in small_names]
    unpacked = [_flat_unpack(t, shapes) for t in out]
    for q, n in enumerate(small_names):
        results[n] = [unpacked[r][q] for r in range(4)]

    outs = [loss, grad_x]
    for r in range(4):
        outs += [results[n][r] for n in order]
    return tuple(outs)
```
